```python
import functools
import jax, jax.numpy as jnp
from jax import lax
import numpy as np

D_MODEL = 1024
BATCH = 4
SEQ = 4096
DEPTH = 2
DEC_BATCH = 32
DEC_SEQ = 1
PAST_LEN = 16384
PAGE_SIZE = 128

N_AB_LAYERS = (DEPTH + 1) // 2
N_C_LAYERS = DEPTH // 2
FOX_HEADS = 4
FOX_DIM = 128
FOX_W = FOX_HEADS * FOX_DIM
Q_BLOCK = 128
GDN_HEADS = 4
GDN_DK = 128
GDN_DV = 128
GDN_QK_W = GDN_HEADS * GDN_DK
GDN_V_W = GDN_HEADS * GDN_DV
GDN_CONV_DIM = 2 * GDN_QK_W + GDN_V_W
CONV_W = 4
GDN_CHUNK = 64
AB_SPLITS = (FOX_W, 2 * FOX_W, 3 * FOX_W, 3 * FOX_W + FOX_HEADS,
             3 * FOX_W + FOX_HEADS + GDN_CONV_DIM,
             3 * FOX_W + FOX_HEADS + GDN_CONV_DIM + GDN_V_W,
             3 * FOX_W + FOX_HEADS + GDN_CONV_DIM + GDN_V_W + GDN_HEADS)
AB_IN = 3 * FOX_W + FOX_HEADS + GDN_CONV_DIM + GDN_V_W + 2 * GDN_HEADS
AB_MIX_W = FOX_W + GDN_V_W
RET_HEADS = 8
RET_DK = 128
RET_DV = 256
RET_QK_W = RET_HEADS * RET_DK
RET_V_W = RET_HEADS * RET_DV
RET_CHUNK = 128
C_IN = 2 * RET_QK_W + 2 * RET_V_W
ROPE_BASE = 10000.0
D_FF = 2816
EPS = 1e-6
POOL_NUM = 5
POOL_DEN = 4

kernel_name = "fox_gdn_retention_macaron_step"


def rmsnorm(x, g):
    xf = x.astype(jnp.float32)
    y = xf * lax.rsqrt(jnp.mean(xf * xf, axis=-1, keepdims=True) + EPS)
    return (y * g.astype(jnp.float32)).astype(x.dtype)


def l2norm(x):
    xf = x.astype(jnp.float32)
    return xf * lax.rsqrt(jnp.sum(xf * xf, axis=-1, keepdims=True) + EPS)


def head_layernorm(x):
    xf = x.astype(jnp.float32)
    mu = jnp.mean(xf, axis=-1, keepdims=True)
    xc = xf - mu
    return xc * lax.rsqrt(jnp.mean(xc * xc, axis=-1, keepdims=True) + EPS)


def swiglu(x, w_gu, w_down):
    a, b = jnp.split(x @ w_gu, 2, axis=-1)
    return (jax.nn.silu(a) * b) @ w_down


def rope(x, pos):
    half = x.shape[-1] // 2
    freqs = ROPE_BASE ** (-jnp.arange(half, dtype=jnp.float32) / half)
    ang = pos[:, None] * freqs[None, :]
    cos = jnp.cos(ang)[None, :, None, :]
    sin = jnp.sin(ang)[None, :, None, :]
    xf = x.astype(jnp.float32)
    x1, x2 = xf[..., :half], xf[..., half:]
    return jnp.concatenate([x1 * cos - x2 * sin, x1 * sin + x2 * cos], axis=-1)


def causal_conv(u, prev, w):
    L = u.shape[1]
    full = jnp.concatenate([prev.astype(u.dtype), u], axis=1)
    out = full[:, 0:L] * w[0]
    for i in range(1, CONV_W):
        out = out + full[:, i:i + L] * w[i]
    return out, full[:, L:]


def run_chunks(step, S0, xs, chunk):
    L = xs[0].shape[1]
    C = chunk if L % chunk == 0 else L
    N = L // C
    split = lambda t: jnp.moveaxis(t.reshape(t.shape[0], N, C, *t.shape[2:]), 1, 0)
    S, o = lax.scan(step, S0, tuple(split(t) for t in xs))
    o = jnp.moveaxis(o, 0, 1)
    return o.reshape(o.shape[0], L, *o.shape[3:]), S


def fox_prompt(q, k, v, logf):
    Bn, L, H, Dh = q.shape
    nb = L // Q_BLOCK
    c = jnp.cumsum(logf, axis=1)
    cT = c.transpose(0, 2, 1)
    kpos = jnp.arange(L)
    qb = q.reshape(Bn, nb, Q_BLOCK, H, Dh).transpose(1, 0, 2, 3, 4)
    cb = c.reshape(Bn, nb, Q_BLOCK, H).transpose(1, 0, 3, 2)
    scale = FOX_DIM ** -0.5

    def block(args):
        qi, ci, bi = args
        s = jnp.einsum('bqhd,bkhd->bhqk', qi, k).astype(jnp.float32) * scale
        s = s + ci[..., :, None] - cT[:, :, None, :]
        qpos = bi * Q_BLOCK + jnp.arange(Q_BLOCK)
        s = jnp.where(kpos[None, :] <= qpos[:, None], s, -jnp.inf)
        p = jax.nn.softmax(s, axis=-1).astype(v.dtype)
        return jnp.einsum('bhqk,bkhd->bqhd', p, v)

    o = lax.map(block, (qb, cb, jnp.arange(nb)))
    return o.transpose(1, 0, 2, 3, 4).reshape(Bn, L, H, Dh)


def fox_sample(q, k, v, logf, k_cache, v_cache, lf_cache, layer, page_table):
    Bn, L, H, Dh = q.shape
    P = page_table.shape[1] * PAGE_SIZE
    kp = k_cache[layer, page_table].reshape(Bn, P, H, Dh)
    vp = v_cache[layer, page_table].reshape(Bn, P, H, Dh)
    lp = lf_cache[layer, page_table].reshape(Bn, P, H).astype(jnp.float32)
    rc = lax.cumsum(lp, axis=1, reverse=True) - lp
    cn = jnp.cumsum(logf, axis=1).transpose(0, 2, 1)
    scale = FOX_DIM ** -0.5
    s_past = (jnp.einsum('bqhd,bkhd->bhqk', q, kp).astype(jnp.float32) * scale
              + cn[..., :, None] + rc.transpose(0, 2, 1)[:, :, None, :])
    s_new = (jnp.einsum('bqhd,bkhd->bhqk', q, k).astype(jnp.float32) * scale
             + cn[..., :, None] - cn[..., None, :])
    i = jnp.arange(L)
    s_new = jnp.where(i[None, :] <= i[:, None], s_new, -jnp.inf)
    p = jax.nn.softmax(jnp.concatenate([s_past, s_new], axis=-1), axis=-1).astype(v.dtype)
    return (jnp.einsum('bhqk,bkhd->bqhd', p[..., :P], vp)
            + jnp.einsum('bhqk,bkhd->bqhd', p[..., P:], v))


def gdn_chunk(S, xs):
    q, k, v, g, beta = xs
    q, k, v = (t.transpose(0, 2, 1, 3) for t in (q, k, v))
    g, beta = g.transpose(0, 2, 1), beta.transpose(0, 2, 1)
    C = q.shape[2]
    i = jnp.arange(C)
    tril = i[:, None] >= i[None, :]
    strict = i[:, None] > i[None, :]
    gc = jnp.cumsum(g, axis=-1)
    gam = jnp.exp(jnp.where(tril, gc[..., :, None] - gc[..., None, :], -jnp.inf))
    kb = k * beta[..., None]
    A = jnp.where(strict, jnp.einsum('bhid,bhjd->bhij', kb, k) * gam, 0.0)
    M = A + jnp.eye(C, dtype=jnp.float32)
    rhs = jnp.concatenate([v * beta[..., None], kb * jnp.exp(gc)[..., None]], axis=-1)
    X = lax.linalg.triangular_solve(M, rhs, left_side=True, lower=True, unit_diagonal=True)
    u, w = X[..., :GDN_DV], X[..., GDN_DV:]
    v_new = u - jnp.einsum('bhck,bhkv->bhcv', w, S)
    attn = jnp.einsum('bhid,bhjd->bhij', q, k) * gam
    o = (jnp.einsum('bhck,bhkv->bhcv', q * jnp.exp(gc)[..., None], S)
         + jnp.einsum('bhij,bhjv->bhiv', attn, v_new))
    gl = gc[..., -1:]
    S_new = (S * jnp.exp(gl)[..., None]
             + jnp.einsum('bhck,bhcv->bhkv', k * jnp.exp(gl - gc)[..., None], v_new))
    return S_new, o.transpose(0, 2, 1, 3)


def gdn_mix(conv_in, z, a, b, conv_prev, S0, conv_w, A_log, dt_bias, norm_g):
    Bn, L, _ = conv_in.shape
    c, conv_new = causal_conv(conv_in, conv_prev, conv_w)
    c = jax.nn.silu(c)
    q, k, v = jnp.split(c, (GDN_QK_W, 2 * GDN_QK_W), axis=-1)
    q = l2norm(q.reshape(Bn, L, GDN_HEADS, GDN_DK)) * GDN_DK ** -0.5
    k = l2norm(k.reshape(Bn, L, GDN_HEADS, GDN_DK))
    v = v.reshape(Bn, L, GDN_HEADS, GDN_DV).astype(jnp.float32)
    g = -jnp.exp(A_log.astype(jnp.float32)) * jax.nn.softplus(a.astype(jnp.float32) + dt_bias.astype(jnp.float32))
    beta = jax.nn.sigmoid(b.astype(jnp.float32))
    o, S_new = run_chunks(gdn_chunk, S0.astype(jnp.float32), (q, k, v, g, beta), GDN_CHUNK)
    o = rmsnorm(o, norm_g) * jax.nn.silu(z.reshape(Bn, L, GDN_HEADS, GDN_DV).astype(jnp.float32))
    return o.reshape(Bn, L, GDN_V_W), conv_new, S_new


def ab_mixer(u, layer, past, w_in, w_out, b_f, conv_w, A_log, dt_bias, norm_g):
    Bn, L, _ = u.shape
    fq, fk, fv, ff, gconv, gz, ga, gb = jnp.split(u @ w_in, AB_SPLITS, axis=-1)
    heads = lambda t: t.reshape(Bn, L, FOX_HEADS, FOX_DIM)
    fq, fk, fv = heads(fq), heads(fk), heads(fv)
    logf = jax.nn.log_sigmoid((ff + b_f).astype(jnp.float32))
    if past is None:
        fo = fox_prompt(fq, fk, fv, logf)
        conv_prev = jnp.zeros((Bn, CONV_W - 1, GDN_CONV_DIM), u.dtype)
        S0 = jnp.zeros((Bn, GDN_HEADS, GDN_DK, GDN_DV), jnp.float32)
    else:
        k_cache, v_cache, lf_cache, page_table, conv_prev, S0 = past
        fo = fox_sample(fq, fk, fv, logf, k_cache, v_cache, lf_cache, layer, page_table)
    go, conv_new, S_new = gdn_mix(gconv, gz, ga, gb, conv_prev, S0, conv_w, A_log, dt_bias, norm_g)
    y = jnp.concatenate([fo.reshape(Bn, L, FOX_W), go.astype(fo.dtype)], axis=-1) @ w_out
    return y, (fk, fv, logf, conv_new, S_new)


def ret_chunk(S, xs, lg):
    q, k, v = (t.transpose(0, 2, 1, 3) for t in xs)
    C = q.shape[2]
    i = jnp.arange(C, dtype=jnp.float32)
    diff = i[:, None] - i[None, :]
    dmat = jnp.exp(jnp.where(diff[None] >= 0, diff[None] * lg[:, None, None], -jnp.inf))
    inner = jnp.einsum('bhij,bhjv->bhiv', jnp.einsum('bhid,bhjd->bhij', q, k) * dmat[None], v)
    cross = jnp.einsum('bhck,bhkv->bhcv', q * jnp.exp(lg[:, None] * (i[None, :] + 1.0))[None, :, :, None], S)
    S_new = (S * jnp.exp(lg * C)[None, :, None, None]
             + jnp.einsum('bhck,bhcv->bhkv', k * jnp.exp(lg[:, None] * (C - 1.0 - i[None, :]))[None, :, :, None], v))
    return S_new, (inner + cross).transpose(0, 2, 1, 3)


def ret_mixer(u, pos, S0, w_in, w_out, gnorm):
    Bn, L, _ = u.shape
    q, k, v, gate = jnp.split(u @ w_in, (RET_QK_W, 2 * RET_QK_W, 2 * RET_QK_W + RET_V_W), axis=-1)
    q = rope(q.reshape(Bn, L, RET_HEADS, RET_DK), pos)
    k = rope(k.reshape(Bn, L, RET_HEADS, RET_DK), pos) * RET_DK ** -0.5
    v = v.reshape(Bn, L, RET_HEADS, RET_DV).astype(jnp.float32)
    if S0 is None:
        S0 = jnp.zeros((Bn, RET_HEADS, RET_DK, RET_DV), jnp.float32)
    lg = jnp.log(1.0 - jnp.exp2(-5.0 - jnp.arange(RET_HEADS, dtype=jnp.float32)))
    o, S_new = run_chunks(functools.partial(ret_chunk, lg=lg), S0.astype(jnp.float32), (q, k, v), RET_CHUNK)
    o = head_layernorm(o).reshape(Bn, L, RET_V_W) * gnorm
    y = (jax.nn.silu(gate) * o.astype(gate.dtype)) @ w_out
    return y, S_new


def setup_inputs(seed: int = 0) -> dict:
    key = jax.random.key(seed)
    ks = iter(jax.random.split(key, 32))
    f32 = jnp.float32
    nrm = lambda shape, scale: jax.random.normal(next(ks), shape, f32) * scale
    n_pages = PAST_LEN // PAGE_SIZE
    n_pool = DEC_BATCH * n_pages * POOL_NUM // POOL_DEN
    x_prompt = nrm((BATCH, SEQ, D_MODEL), 1.0)
    x_sample = nrm((DEC_BATCH, DEC_SEQ, D_MODEL), 1.0)
    cache_fox_k = nrm((N_AB_LAYERS, n_pool, PAGE_SIZE, FOX_HEADS, FOX_DIM), 1.0)
    cache_fox_v = nrm((N_AB_LAYERS, n_pool, PAGE_SIZE, FOX_HEADS, FOX_DIM), 1.0)
    cache_fox_logf = jax.nn.log_sigmoid(nrm((N_AB_LAYERS, n_pool, PAGE_SIZE, FOX_HEADS), 1.0) + 9.0)
    page_table = jax.random.permutation(next(ks), n_pool)[:DEC_BATCH * n_pages].reshape(DEC_BATCH, n_pages).astype(jnp.int32)
    state_gdn_conv = nrm((N_AB_LAYERS, DEC_BATCH, CONV_W - 1, GDN_CONV_DIM), 1.0)
    state_gdn_S = nrm((N_AB_LAYERS, DEC_BATCH, GDN_HEADS, GDN_DK, GDN_DV), 0.1)
    state_ret_S = nrm((N_C_LAYERS, DEC_BATCH, RET_HEADS, RET_DK, RET_DV), 0.5)
    norm_g = 1.0 + nrm((DEPTH, 3, D_MODEL), 0.02)
    final_norm_g = 1.0 + nrm((D_MODEL,), 0.02)
    ffn_w_gu = nrm((DEPTH, 2, D_MODEL, 2 * D_FF), D_MODEL ** -0.5)
    ffn_w_down = nrm((DEPTH, 2, D_FF, D_MODEL), D_FF ** -0.5)
    ab_w_in = nrm((N_AB_LAYERS, D_MODEL, AB_IN), D_MODEL ** -0.5)
    ab_w_out = nrm((N_AB_LAYERS, AB_MIX_W, D_MODEL), AB_MIX_W ** -0.5)
    fox_b_f = 5.0 + nrm((N_AB_LAYERS, FOX_HEADS), 0.5)
    gdn_conv_w = nrm((N_AB_LAYERS, CONV_W, GDN_CONV_DIM), CONV_W ** -0.5)
    gdn_A_log = jnp.log(jax.random.uniform(next(ks), (N_AB_LAYERS, GDN_HEADS), f32, 1.0, 16.0))
    dt = jnp.exp(jax.random.uniform(next(ks), (N_AB_LAYERS, GDN_HEADS), f32, np.log(1e-3), np.log(1e-1)))
    gdn_dt_bias = dt + jnp.log(-jnp.expm1(-dt))
    gdn_norm_g = 1.0 + nrm((N_AB_LAYERS, GDN_DV), 0.02)
    c_w_in = nrm((N_C_LAYERS, D_MODEL, C_IN), D_MODEL ** -0.5)
    c_w_out = nrm((N_C_LAYERS, RET_V_W, D_MODEL), RET_V_W ** -0.5)
    ret_norm_g = 1.0 + nrm((N_C_LAYERS, RET_V_W), 0.02)
    return {"x_prompt": x_prompt, "x_sample": x_sample,
            "cache_fox_k": cache_fox_k, "cache_fox_v": cache_fox_v, "cache_fox_logf": cache_fox_logf,
            "page_table": page_table, "state_gdn_conv": state_gdn_conv, "state_gdn_S": state_gdn_S,
            "state_ret_S": state_ret_S, "norm_g": norm_g, "final_norm_g": final_norm_g,
            "ffn_w_gu": ffn_w_gu, "ffn_w_down": ffn_w_down, "ab_w_in": ab_w_in, "ab_w_out": ab_w_out,
            "fox_b_f": fox_b_f, "gdn_conv_w": gdn_conv_w, "gdn_A_log": gdn_A_log, "gdn_dt_bias": gdn_dt_bias,
            "gdn_norm_g": gdn_norm_g, "c_w_in": c_w_in, "c_w_out": c_w_out, "ret_norm_g": ret_norm_g}


def reference(x_prompt, x_sample, cache_fox_k, cache_fox_v, cache_fox_logf, page_table,
              state_gdn_conv, state_gdn_S, state_ret_S, norm_g, final_norm_g, ffn_w_gu, ffn_w_down,
              ab_w_in, ab_w_out, fox_b_f, gdn_conv_w, gdn_A_log, gdn_dt_bias, gdn_norm_g,
              c_w_in, c_w_out, ret_norm_g):
    xp, xs = x_prompt, x_sample
    past_len = page_table.shape[1] * PAGE_SIZE
    pos_p = jnp.arange(xp.shape[1], dtype=jnp.float32)
    pos_s = past_len + jnp.arange(xs.shape[1], dtype=jnp.float32)
    fkp, fvp, flp, fks, fvs, fls = [], [], [], [], [], []
    gcp, gsp, gcs, gss = [], [], [], []
    rsp, rss = [], []
    for li in range(DEPTH):
        g_ffa, g_mix, g_ffb = norm_g[li, 0], norm_g[li, 1], norm_g[li, 2]
        xp = xp + 0.5 * swiglu(rmsnorm(xp, g_ffa), ffn_w_gu[li, 0], ffn_w_down[li, 0])
        xs = xs + 0.5 * swiglu(rmsnorm(xs, g_ffa), ffn_w_gu[li, 0], ffn_w_down[li, 0])
        j = li // 2
        if li % 2 == 0:
            params = (ab_w_in[j], ab_w_out[j], fox_b_f[j], gdn_conv_w[j], gdn_A_log[j], gdn_dt_bias[j], gdn_norm_g[j])
            mp, (k_p, v_p, lf_p, conv_p, S_p) = ab_mixer(rmsnorm(xp, g_mix), j, None, *params)
            past = (cache_fox_k, cache_fox_v, cache_fox_logf, page_table, state_gdn_conv[j], state_gdn_S[j])
            ms, (k_s, v_s, lf_s, conv_s, S_s) = ab_mixer(rmsnorm(xs, g_mix), j, past, *params)
            fkp.append(k_p); fvp.append(v_p); flp.append(lf_p)
            fks.append(k_s); fvs.append(v_s); fls.append(lf_s)
            gcp.append(conv_p); gsp.append(S_p); gcs.append(conv_s); gss.append(S_s)
        else:
            mp, R_p = ret_mixer(rmsnorm(xp, g_mix), pos_p, None, c_w_in[j], c_w_out[j], ret_norm_g[j])
            ms, R_s = ret_mixer(rmsnorm(xs, g_mix), pos_s, state_ret_S[j], c_w_in[j], c_w_out[j], ret_norm_g[j])
            rsp.append(R_p); rss.append(R_s)
        xp = xp + mp.astype(xp.dtype)
        xs = xs + ms.astype(xs.dtype)
        xp = xp + 0.5 * swiglu(rmsnorm(xp, g_ffb), ffn_w_gu[li, 1], ffn_w_down[li, 1])
        xs = xs + 0.5 * swiglu(rmsnorm(xs, g_ffb), ffn_w_gu[li, 1], ffn_w_down[li, 1])
    y_prompt = rmsnorm(xp, final_norm_g)
    y_sample = rmsnorm(xs, final_norm_g)
    return (y_prompt, y_sample,
            jnp.stack(fkp), jnp.stack(fvp), jnp.stack(flp),
            jnp.stack(fks), jnp.stack(fvs), jnp.stack(fls),
            jnp.stack(gcp), jnp.stack(gsp), jnp.stack(gcs), jnp.stack(gss),
            jnp.stack(rsp), jnp.stack(rss))
```

```python
import functools
import math

import jax
import jax.numpy as jnp
from jax import lax
from jax.experimental import pallas as pl
from jax.experimental.pallas import tpu as pltpu

F32 = jnp.float32
BF16 = jnp.bfloat16
EPS = 1e-6
ROPE_BASE = 10000.0
NEG = -1e30
LANES = 128
GDN_CHUNK = 64
VMEM_LIMIT = 56 * 1024 * 1024
HI = lax.Precision.HIGHEST


def _cparams(n_axes):
    return pltpu.CompilerParams(dimension_semantics=("arbitrary",) * n_axes,
                                vmem_limit_bytes=VMEM_LIMIT)


def _dot(a, b, precision=None):
    return jnp.dot(a, b, preferred_element_type=F32, precision=precision)


def _dot_nt(a, b, precision=None):
    return lax.dot_general(a, b, (((1,), (1,)), ((), ())), preferred_element_type=F32, precision=precision)


def _dot_tn(a, b, precision=None):
    return lax.dot_general(a, b, (((0,), (0,)), ((), ())), preferred_element_type=F32, precision=precision)


def _rms_rows(x, g):
    return x * lax.rsqrt(jnp.mean(x * x, axis=-1, keepdims=True) + EPS) * g


def _ffn_body(x_ref, g_ref, wa_ref, wb_ref, wd_ref, o_ref, xn_ref, acc_ref):
    f = pl.program_id(1)

    @pl.when(f == 0)
    def _():
        xn_ref[...] = _rms_rows(x_ref[...], g_ref[...]).astype(BF16)
        acc_ref[...] = jnp.zeros_like(acc_ref)

    xn = xn_ref[...]
    a = _dot(xn, wa_ref[...])
    b = _dot(xn, wb_ref[...])
    h = (jax.nn.silu(a) * b).astype(BF16)
    acc_ref[...] += _dot(h, wd_ref[...])

    @pl.when(f == pl.num_programs(1) - 1)
    def _():
        o_ref[...] = x_ref[...] + 0.5 * acc_ref[...]


def _ffn(x, g, w_gu, w_down):
    M, D = x.shape
    F = w_down.shape[0]
    tm = min(M, 512)
    tf = F // 2 if (F // 2) % LANES == 0 else F
    nf = F // tf
    return pl.pallas_call(
        _ffn_body,
        out_shape=jax.ShapeDtypeStruct((M, D), F32),
        grid=(M // tm, nf),
        in_specs=[
            pl.BlockSpec((tm, D), lambda m, f: (m, 0)),
            pl.BlockSpec((1, D), lambda m, f: (0, 0)),
            pl.BlockSpec((D, tf), lambda m, f: (0, f)),
            pl.BlockSpec((D, tf), lambda m, f: (0, nf + f)),
            pl.BlockSpec((tf, D), lambda m, f: (f, 0)),
        ],
        out_specs=pl.BlockSpec((tm, D), lambda m, f: (m, 0)),
        scratch_shapes=[pltpu.VMEM((tm, D), BF16), pltpu.VMEM((tm, D), F32)],
        compiler_params=_cparams(2),
        name="ffn",
    )(x, g.reshape(1, D), w_gu, w_gu, w_down)


def _proj_body(*refs, n_out, chunk):
    x_ref, g_ref = refs[:2]
    w_refs = refs[2:2 + n_out]
    o_refs = refs[2 + n_out:]
    xn = _rms_rows(x_ref[...], g_ref[...]).astype(BF16)
    for w_ref, o_ref in zip(w_refs, o_refs):
        n = w_ref.shape[1]
        for c0 in range(0, n, chunk):
            c1 = min(n, c0 + chunk)
            o_ref[:, c0:c1] = _dot(xn, w_ref[:, c0:c1])


def _rms_proj(x, g, ws):
    M, D = x.shape
    tm = min(M, 256)
    n_out = len(ws)
    return pl.pallas_call(
        functools.partial(_proj_body, n_out=n_out, chunk=512),
        out_shape=[jax.ShapeDtypeStruct((M, w.shape[1]), F32) for w in ws],
        grid=(M // tm,),
        in_specs=[pl.BlockSpec((tm, D), lambda m: (m, 0)), pl.BlockSpec((1, D), lambda m: (0, 0))]
        + [pl.BlockSpec(w.shape, lambda m: (0, 0)) for w in ws],
        out_specs=[pl.BlockSpec((tm, w.shape[1]), lambda m: (m, 0)) for w in ws],
        compiler_params=_cparams(1),
        name="rms_proj",
    )(x, g.reshape(1, D), *ws)


def _oproj_body(*refs, n_in):
    x_ref = refs[0]
    a_refs = refs[1:1 + n_in]
    w_refs = refs[1 + n_in:1 + 2 * n_in]
    o_ref = refs[1 + 2 * n_in]
    acc = x_ref[...]
    for a_ref, w_ref in zip(a_refs, w_refs):
        acc = acc + _dot(a_ref[...].astype(BF16), w_ref[...])
    o_ref[...] = acc


def _out_proj(x, acts, ws):
    M, D = x.shape
    tm = min(M, 512)
    n_in = len(acts)
    return pl.pallas_call(
        functools.partial(_oproj_body, n_in=n_in),
        out_shape=jax.ShapeDtypeStruct((M, D), F32),
        grid=(M // tm,),
        in_specs=[pl.BlockSpec((tm, D), lambda m: (m, 0))]
        + [pl.BlockSpec((tm, a.shape[1]), lambda m: (m, 0)) for a in acts]
        + [pl.BlockSpec(w.shape, lambda m: (0, 0)) for w in ws],
        out_specs=pl.BlockSpec((tm, D), lambda m: (m, 0)),
        compiler_params=_cparams(1),
        name="out_proj",
    )(x, *acts, *ws)


def _final_norm_body(x_ref, g_ref, o_ref):
    o_ref[...] = _rms_rows(x_ref[...], g_ref[...])


def _final_norm(x, g):
    M, D = x.shape
    tm = min(M, 512)
    return pl.pallas_call(
        _final_norm_body,
        out_shape=jax.ShapeDtypeStruct((M, D), F32),
        grid=(M // tm,),
        in_specs=[pl.BlockSpec((tm, D), lambda m: (m, 0)), pl.BlockSpec((1, D), lambda m: (0, 0))],
        out_specs=pl.BlockSpec((tm, D), lambda m: (m, 0)),
        compiler_params=_cparams(1),
        name="final_norm",
    )(x, g.reshape(1, D))


def _gates_body(s_ref, p_ref, o_ref, *, nh):
    s = s_ref[...] + p_ref[0:1, :]
    lane = lax.broadcasted_iota(jnp.int32, s.shape, 1)
    logf = jax.nn.log_sigmoid(s)
    g = -jnp.exp(p_ref[1:2, :]) * jax.nn.softplus(s)
    beta = jax.nn.sigmoid(s)
    o_ref[...] = jnp.where(lane < nh, logf, jnp.where(lane < 2 * nh, g, jnp.where(lane < 3 * nh, beta, 0.0)))


def _gates(small, b_f, dt_bias, a_log):
    M = small.shape[0]
    nh = b_f.shape[0]
    tm = min(M, 2048)
    pad = jnp.zeros((LANES - 3 * nh,), F32)
    p = jnp.zeros((8, LANES), F32)
    p = p.at[0].set(jnp.concatenate([b_f, dt_bias, jnp.zeros((nh,), F32), pad]))
    p = p.at[1].set(jnp.concatenate([jnp.zeros((nh,), F32), a_log, jnp.zeros((nh,), F32), pad]))
    return pl.pallas_call(
        functools.partial(_gates_body, nh=nh),
        out_shape=jax.ShapeDtypeStruct((M, LANES), F32),
        grid=(M // tm,),
        in_specs=[pl.BlockSpec((tm, LANES), lambda m: (m, 0)), pl.BlockSpec((8, LANES), lambda m: (0, 0))],
        out_specs=pl.BlockSpec((tm, LANES), lambda m: (m, 0)),
        compiler_params=_cparams(1),
        name="gates",
    )(small, p)


def _cumsum_body(x_ref, o_ref):
    L = x_ref.shape[1]
    r = lax.broadcasted_iota(jnp.int32, (LANES, LANES), 0)
    c = lax.broadcasted_iota(jnp.int32, (LANES, LANES), 1)
    upper = (r <= c).astype(F32)
    carry = jnp.zeros((x_ref.shape[0], 1), F32)
    for j in range(L // LANES):
        cs = _dot(x_ref[:, j * LANES:(j + 1) * LANES], upper, HI) + carry
        o_ref[:, j * LANES:(j + 1) * LANES] = cs
        carry = cs[:, LANES - 1:LANES]


def _cumsum_rows(x, seg):
    R, M = x.shape
    return pl.pallas_call(
        _cumsum_body,
        out_shape=jax.ShapeDtypeStruct((R, M), F32),
        grid=(M // seg,),
        in_specs=[pl.BlockSpec((R, seg), lambda b: (0, b))],
        out_specs=pl.BlockSpec((R, seg), lambda b: (0, b)),
        compiler_params=_cparams(1),
        name="cumsum_rows",
    )(x)


def _fox_body(q_ref, k_ref, v_ref, c_ref, o_ref, m_ref, l_ref, acc_ref, *, t, scale):
    i = pl.program_id(2)
    q = q_ref[...].astype(BF16)
    q0 = pl.multiple_of(i * t, t)
    r = c_ref[:, pl.ds(q0, t)][:, t - 1:t]
    m_ref[...] = jnp.full_like(m_ref, NEG)
    l_ref[...] = jnp.zeros_like(l_ref)
    acc_ref[...] = jnp.zeros_like(acc_ref)

    def step(j, masked):
        k0 = pl.multiple_of(j * t, t)
        k = k_ref[pl.ds(k0, t), :].astype(BF16)
        v = v_ref[pl.ds(k0, t), :].astype(BF16)
        s = _dot_nt(q, k) * scale + (r - c_ref[:, pl.ds(k0, t)])
        if masked:
            row = lax.broadcasted_iota(jnp.int32, (t, t), 0)
            col = lax.broadcasted_iota(jnp.int32, (t, t), 1)
            s = jnp.where(col <= row, s, NEG)
        m_prev = m_ref[...]
        m_new = jnp.maximum(m_prev, jnp.max(s, axis=-1, keepdims=True))
        p = jnp.exp(s - m_new)
        alpha = jnp.exp(m_prev - m_new)
        l_ref[...] = alpha * l_ref[...] + jnp.sum(p, axis=-1, keepdims=True)
        acc_ref[...] = alpha * acc_ref[...] + _dot(p.astype(BF16), v)
        m_ref[...] = m_new

    def loop_body(j, carry):
        step(j, False)
        return carry

    lax.fori_loop(0, i, loop_body, 0)
    step(i, True)
    o_ref[...] = acc_ref[...] / l_ref[...]


def _fox_prompt(q, k, v, c_rows, B, L, H):
    M, W = q.shape
    Dh = W // H
    t = min(L, 512)
    nq = L // t
    kernel = functools.partial(_fox_body, t=t, scale=Dh ** -0.5)
    return pl.pallas_call(
        kernel,
        out_shape=jax.ShapeDtypeStruct((M, W), F32),
        grid=(B, H, nq),
        in_specs=[
            pl.BlockSpec((t, Dh), lambda b, h, i: (b * nq + i, h)),
            pl.BlockSpec((L, Dh), lambda b, h, i: (b, h)),
            pl.BlockSpec((L, Dh), lambda b, h, i: (b, h)),
            pl.BlockSpec((None, 1, L), lambda b, h, i: (b * H + h, 0, 0)),
        ],
        out_specs=pl.BlockSpec((t, Dh), lambda b, h, i: (b * nq + i, h)),
        scratch_shapes=[pltpu.VMEM((t, 1), F32), pltpu.VMEM((t, 1), F32), pltpu.VMEM((t, Dh), F32)],
        compiler_params=_cparams(3),
        name="fox_prompt",
    )(q, k, v, c_rows)


def _unit_lower_inverse(a, n):
    ri = lax.broadcasted_iota(jnp.int32, (n, n), 0)
    ci = lax.broadcasted_iota(jnp.int32, (n, n), 1)
    eye = (ri == ci).astype(F32)
    ad = jnp.where((ri >> 4) == (ci >> 4), a, 0.0)
    t = eye - ad
    pw = ad
    for _ in range(3):
        pw = _dot(pw, pw, HI)
        t = t + _dot(t, pw, HI)
    size = 16
    while size < GDN_CHUNK:
        sh = size.bit_length() - 1
        off = ((ri >> (sh + 1)) == (ci >> (sh + 1))) & ((ri >> sh) == (ci >> sh) + 1)
        t = t - _dot(t, _dot(jnp.where(off, a, 0.0), t, HI), HI)
        size *= 2
    return t


def _gdn_body(cin_ref, z_ref, gc_ref, gr_ref, cw_ref, ng_ref, o_ref, s_out_ref, s_ref, tail_ref, *, nh, dk, dv):
    j = pl.program_id(1)
    T = cin_ref.shape[0]
    C = GDN_CHUNK

    @pl.when(j == 0)
    def _():
        s_ref[...] = jnp.zeros_like(s_ref)
        tail_ref[...] = jnp.zeros_like(tail_ref)

    u = cin_ref[...]
    full = jnp.concatenate([tail_ref[...], u], axis=0)
    tail_ref[...] = u[T - 8:T]
    w = cw_ref[...]
    kw = w.shape[0]
    conv = full[8 - kw + 1:8 - kw + 1 + T] * w[0:1]
    for i in range(1, kw):
        conv = conv + full[8 - kw + 1 + i:8 - kw + 1 + i + T] * w[i:i + 1]
    conv = jax.nn.silu(conv)

    ri = lax.broadcasted_iota(jnp.int32, (T, T), 0)
    ci = lax.broadcasted_iota(jnp.int32, (T, T), 1)
    same = (ri // C) == (ci // C)
    tril = same & (ri >= ci)
    strict = same & (ri > ci)
    gcol_all = _dot(tril.astype(F32), gc_ref[...], HI)
    grow_all = _dot(gr_ref[...], (same & (ri <= ci)).astype(F32), HI)
    rowi = lax.broadcasted_iota(jnp.int32, (T, 1), 0)

    for h in range(nh):
        q = conv[:, h * dk:(h + 1) * dk]
        k = conv[:, nh * dk + h * dk:nh * dk + (h + 1) * dk]
        v = conv[:, 2 * nh * dk + h * dv:2 * nh * dk + (h + 1) * dv]
        q = q * lax.rsqrt(jnp.sum(q * q, axis=-1, keepdims=True) + EPS) * dk ** -0.5
        k = k * lax.rsqrt(jnp.sum(k * k, axis=-1, keepdims=True) + EPS)
        beta = gc_ref[:, 2 * nh + h:2 * nh + h + 1]
        gcc = gcol_all[:, nh + h:nh + h + 1]
        gcr = grow_all[nh + h:nh + h + 1, :]
        gam = jnp.exp(jnp.where(tril, gcc - gcr, NEG))
        kb = k * beta
        kb16 = kb.astype(BF16)
        k16 = k.astype(BF16)
        a = jnp.where(strict, _dot_nt(kb16, k16) * gam, 0.0)
        tinv = _unit_lower_inverse(a, T)
        rhs = jnp.concatenate([v * beta, kb * jnp.exp(gcc)], axis=-1)
        x = _dot(tinv, rhs, HI)
        uu, ww = x[:, :dv], x[:, dv:]
        attn = (_dot_nt(q.astype(BF16), k16) * gam).astype(BF16)
        qg = (q * jnp.exp(gcc)).astype(BF16)
        gl = gcr[:, C - 1:C]
        for cidx in range(1, T // C):
            gl = jnp.where(rowi >= cidx * C, gcr[:, (cidx + 1) * C - 1:(cidx + 1) * C], gl)
        kd = (k * jnp.exp(gl - gcc)).astype(BF16)
        egl = jnp.exp(gl)
        S = s_ref[h]
        outs = []
        for cidx in range(T // C):
            sl = slice(cidx * C, (cidx + 1) * C)
            v_new = uu[sl] - _dot(ww[sl].astype(BF16), S.astype(BF16))
            o = _dot(qg[sl], S.astype(BF16)) + _dot(attn[sl, cidx * C:(cidx + 1) * C], v_new.astype(BF16))
            S = S * egl[cidx * C:cidx * C + 1] + _dot_tn(kd[sl], v_new.astype(BF16))
            outs.append(o)
        s_ref[h] = S
        o = jnp.concatenate(outs, axis=0)
        o = _rms_rows(o, ng_ref[...])
        o_ref[:, h * dv:(h + 1) * dv] = o * jax.nn.silu(z_ref[:, h * dv:(h + 1) * dv])

    @pl.when(j == pl.num_programs(1) - 1)
    def _():
        s_out_ref[...] = s_ref[...]


def _gdn_prompt(conv_in, z, gates_col, gates_row, conv_w, norm_g, B, L, nh, dk, dv):
    M, CD = conv_in.shape
    T = 2 * GDN_CHUNK
    nb = L // T
    kernel = functools.partial(_gdn_body, nh=nh, dk=dk, dv=dv)
    return pl.pallas_call(
        kernel,
        out_shape=[jax.ShapeDtypeStruct((M, nh * dv), F32), jax.ShapeDtypeStruct((B, nh, dk, dv), F32)],
        grid=(B, nb),
        in_specs=[
            pl.BlockSpec((T, CD), lambda b, j: (b * nb + j, 0)),
            pl.BlockSpec((T, nh * dv), lambda b, j: (b * nb + j, 0)),
            pl.BlockSpec((T, LANES), lambda b, j: (b * nb + j, 0)),
            pl.BlockSpec((16, T), lambda b, j: (0, b * nb + j)),
            pl.BlockSpec(conv_w.shape, lambda b, j: (0, 0)),
            pl.BlockSpec((1, dv), lambda b, j: (0, 0)),
        ],
        out_specs=[
            pl.BlockSpec((T, nh * dv), lambda b, j: (b * nb + j, 0)),
            pl.BlockSpec((None, nh, dk, dv), lambda b, j: (b, 0, 0, 0)),
        ],
        scratch_shapes=[pltpu.VMEM((nh, dk, dv), F32), pltpu.VMEM((8, CD), F32)],
        compiler_params=_cparams(2),
        name="gdn_prompt",
    )(conv_in, z, gates_col, gates_row, conv_w, norm_g.reshape(1, dv))


def _rope(x, cos2, sin2, nh, dk):
    outs = []
    for h in range(nh):
        xh = x[:, h * dk:(h + 1) * dk]
        outs.append(xh * cos2 + pltpu.roll(xh, dk // 2, 1) * sin2)
    return outs


def _rope_tables(pos, freqs):
    ang = pos * freqs
    cos, sin = jnp.cos(ang), jnp.sin(ang)
    return jnp.concatenate([cos, cos], axis=-1), jnp.concatenate([-sin, sin], axis=-1)


def _head_layernorm(o):
    mu = jnp.mean(o, axis=-1, keepdims=True)
    oc = o - mu
    return oc * lax.rsqrt(jnp.mean(oc * oc, axis=-1, keepdims=True) + EPS)


def _log_gamma(h):
    return math.log(1.0 - 2.0 ** (-5.0 - h))


def _ret_body(q_ref, k_ref, v_ref, gate_ref, fr_ref, gn_ref, o_ref, s_out_ref, s_ref, *, nh, dk, dv):
    j = pl.program_id(1)
    C = q_ref.shape[0]

    @pl.when(j == 0)
    def _():
        s_ref[...] = jnp.zeros_like(s_ref)

    rowi = lax.broadcasted_iota(jnp.int32, (C, 1), 0)
    pos = (j * C + rowi).astype(F32)
    cos2, sin2 = _rope_tables(pos, fr_ref[...])
    qs = _rope(q_ref[...], cos2, sin2, nh, dk)
    ks = _rope(k_ref[...], cos2, sin2, nh, dk)
    ri = lax.broadcasted_iota(jnp.int32, (C, C), 0)
    ci = lax.broadcasted_iota(jnp.int32, (C, C), 1)
    diff = (ri - ci).astype(F32)
    rowf = rowi.astype(F32)
    for h in range(nh):
        lg = _log_gamma(h)
        q = qs[h]
        k = ks[h] * dk ** -0.5
        v = v_ref[:, h * dv:(h + 1) * dv].astype(BF16)
        dmat = jnp.exp(jnp.where(ri >= ci, diff * lg, NEG))
        q16 = q.astype(BF16)
        inner = _dot((_dot_nt(q16, k.astype(BF16)) * dmat).astype(BF16), v)
        S = s_ref[h]
        cross = _dot((q * jnp.exp(lg * (rowf + 1.0))).astype(BF16), S.astype(BF16))
        s_ref[h] = S * math.exp(lg * C) + _dot_tn((k * jnp.exp(lg * (C - 1.0 - rowf))).astype(BF16), v)
        o = _head_layernorm(inner + cross) * gn_ref[:, h * dv:(h + 1) * dv]
        o_ref[:, h * dv:(h + 1) * dv] = jax.nn.silu(gate_ref[:, h * dv:(h + 1) * dv]) * o

    @pl.when(j == pl.num_programs(1) - 1)
    def _():
        s_out_ref[...] = s_ref[...]


def _ret_prompt(q, k, v, gate, freqs, gnorm, B, L, nh, dk, dv):
    M = q.shape[0]
    C = 128
    nb = L // C
    kernel = functools.partial(_ret_body, nh=nh, dk=dk, dv=dv)
    return pl.pallas_call(
        kernel,
        out_shape=[jax.ShapeDtypeStruct((M, nh * dv), F32), jax.ShapeDtypeStruct((B, nh, dk, dv), F32)],
        grid=(B, nb),
        in_specs=[
            pl.BlockSpec((C, nh * dk), lambda b, j: (b * nb + j, 0)),
            pl.BlockSpec((C, nh * dk), lambda b, j: (b * nb + j, 0)),
            pl.BlockSpec((C, nh * dv), lambda b, j: (b * nb + j, 0)),
            pl.BlockSpec((C, nh * dv), lambda b, j: (b * nb + j, 0)),
            pl.BlockSpec((1, dk // 2), lambda b, j: (0, 0)),
            pl.BlockSpec((1, nh * dv), lambda b, j: (0, 0)),
        ],
        out_specs=[
            pl.BlockSpec((C, nh * dv), lambda b, j: (b * nb + j, 0)),
            pl.BlockSpec((None, nh, dk, dv), lambda b, j: (b, 0, 0, 0)),
        ],
        scratch_shapes=[pltpu.VMEM((nh, dk, dv), F32)],
        compiler_params=_cparams(2),
        name="ret_prompt",
    )(q, k, v, gate, freqs, gnorm.reshape(1, nh * dv))


def _page_sums_body(x_ref, o_ref, *, nh):
    n = x_ref.shape[1]
    fi = lax.broadcasted_iota(jnp.int32, (n, n), 0)
    fo = lax.broadcasted_iota(jnp.int32, (n, n), 1)
    same_head = (fi % nh) == (fo % nh)
    x = x_ref[...]
    o_ref[:, :n] = _dot(x, (same_head & (fi > fo)).astype(F32), HI)
    o_ref[:, n:] = _dot(x, same_head.astype(F32), HI)


def _page_sums(lf_pages, nh):
    P, n = lf_pages.shape
    tp = 512 if P % 512 == 0 else P
    return pl.pallas_call(
        functools.partial(_page_sums_body, nh=nh),
        out_shape=jax.ShapeDtypeStruct((P, 2 * n), F32),
        grid=(P // tp,),
        in_specs=[pl.BlockSpec((tp, n), lambda p: (p, 0))],
        out_specs=pl.BlockSpec((tp, 2 * n), lambda p: (p, 0)),
        compiler_params=_cparams(1),
        name="page_sums",
    )(lf_pages)


def _paged_body(pt_ref, *refs, G, nh, scale):
    k_refs = refs[:G]
    v_refs = refs[G:2 * G]
    w_refs = refs[2 * G:3 * G]
    q_ref, kn_ref, vn_ref, lfn_ref, o_ref, m_ref, l_ref, acc_ref, carry_ref = refs[3 * G:]
    s_idx = pl.program_id(1)
    n = k_refs[0].shape[0]
    dh = q_ref.shape[1]

    @pl.when(s_idx == 0)
    def _():
        m_ref[...] = jnp.full_like(m_ref, NEG)
        l_ref[...] = jnp.zeros_like(l_ref)
        acc_ref[...] = jnp.zeros_like(acc_ref)
        carry_ref[...] = jnp.zeros_like(carry_ref)

    q = q_ref[...]
    q8 = jnp.concatenate([q, jnp.zeros((8 - nh, dh), F32)], axis=0).astype(BF16)
    lfn = jnp.concatenate([lfn_ref[...], jnp.zeros((8 - nh, 1), F32)], axis=0)
    hrow = lax.broadcasted_iota(jnp.int32, (8, n), 0)
    hcol = lax.broadcasted_iota(jnp.int32, (8, n), 1) % nh
    match = hrow == hcol
    for i in reversed(range(G)):
        kp = k_refs[i][...].astype(BF16)
        vp = v_refs[i][...].astype(BF16)
        wt = w_refs[i][...]
        carry = carry_ref[...]
        bias = wt[:, :n] + carry
        carry_ref[...] = carry + wt[:, n:]
        s = _dot_nt(q8, kp) * scale + bias + lfn
        s = jnp.where(match, s, NEG)
        m_prev = m_ref[...]
        m_new = jnp.maximum(m_prev, jnp.max(s, axis=-1, keepdims=True))
        p = jnp.exp(s - m_new)
        alpha = jnp.exp(m_prev - m_new)
        l_ref[...] = alpha * l_ref[...] + jnp.sum(p, axis=-1, keepdims=True)
        acc_ref[...] = alpha * acc_ref[...] + _dot(p.astype(BF16), vp)
        m_ref[...] = m_new

    @pl.when(s_idx == pl.num_programs(1) - 1)
    def _():
        qf = q.astype(BF16).astype(F32)
        kn = kn_ref[...].astype(BF16).astype(F32)
        s_new = jnp.sum(qf * kn, axis=-1, keepdims=True) * scale
        m_prev = m_ref[0:nh, :]
        m_new = jnp.maximum(m_prev, s_new)
        alpha = jnp.exp(m_prev - m_new)
        p_new = jnp.exp(s_new - m_new)
        l_fin = alpha * l_ref[0:nh, :] + p_new
        vn = vn_ref[...].astype(BF16).astype(F32)
        num = alpha * acc_ref[0:nh, :] + p_new.astype(BF16).astype(F32) * vn
        o_ref[...] = num / l_fin


def _fox_sample(q, k_new, v_new, lf_new, k_pages, v_pages, page_sums, page_table, nh):
    Bn, _, dh = q.shape
    n_pages = page_table.shape[1]
    n = k_pages.shape[1]
    G = 8 if n_pages % 8 == 0 else 1
    ns = n_pages // G

    def page_map(i):
        return lambda b, s, pt: (pt[b, (ns - 1 - s) * G + i], 0, 0)

    per_b = lambda b, s, pt: (b, 0, 0)
    grid_spec = pltpu.PrefetchScalarGridSpec(
        num_scalar_prefetch=1,
        grid=(Bn, ns),
        in_specs=[pl.BlockSpec((None, n, dh), page_map(i)) for i in range(G)]
        + [pl.BlockSpec((None, n, dh), page_map(i)) for i in range(G)]
        + [pl.BlockSpec((None, 1, 2 * n), page_map(i)) for i in range(G)]
        + [pl.BlockSpec((None, nh, dh), per_b), pl.BlockSpec((None, nh, dh), per_b),
           pl.BlockSpec((None, nh, dh), per_b), pl.BlockSpec((None, nh, 1), per_b)],
        out_specs=pl.BlockSpec((None, nh, dh), per_b),
        scratch_shapes=[pltpu.VMEM((8, 1), F32), pltpu.VMEM((8, 1), F32), pltpu.VMEM((8, dh), F32),
                        pltpu.VMEM((1, n), F32)],
    )
    return pl.pallas_call(
        functools.partial(_paged_body, G=G, nh=nh, scale=dh ** -0.5),
        out_shape=jax.ShapeDtypeStruct((Bn, nh, dh), F32),
        grid_spec=grid_spec,
        compiler_params=_cparams(2),
        name="fox_sample",
    )(page_table, *([k_pages] * G), *([v_pages] * G), *([page_sums] * G), q, k_new, v_new, lf_new)


def _pad8(row):
    return jnp.concatenate([row, jnp.zeros((8 - row.shape[0], row.shape[1]), row.dtype)], axis=0)


def _gdn_sample_body(cin_ref, prev_ref, z_ref, gt_ref, cw_ref, ng_ref, s0_ref, o_ref, cnew_ref, s_out_ref, *, nh, dk, dv):
    u = cin_ref[...]
    prev = prev_ref[...]
    w = cw_ref[...]
    kw = w.shape[0]
    conv = u * w[kw - 1:kw]
    for i in range(kw - 1):
        conv = conv + prev[i:i + 1] * w[i:i + 1]
    cnew_ref[0:kw - 2, :] = prev[1:kw - 1]
    cnew_ref[kw - 2:kw - 1, :] = u
    conv = jax.nn.silu(conv)
    gt = gt_ref[...]
    for h in range(nh):
        q = conv[:, h * dk:(h + 1) * dk]
        k = conv[:, nh * dk + h * dk:nh * dk + (h + 1) * dk]
        v = conv[:, 2 * nh * dk + h * dv:2 * nh * dk + (h + 1) * dv]
        q = q * lax.rsqrt(jnp.sum(q * q, axis=-1, keepdims=True) + EPS) * dk ** -0.5
        k = k * lax.rsqrt(jnp.sum(k * k, axis=-1, keepdims=True) + EPS)
        g = gt[:, nh + h:nh + h + 1]
        beta = gt[:, 2 * nh + h:2 * nh + h + 1]
        eg = jnp.exp(g)
        S = s0_ref[h]
        lhs = _pad8(jnp.concatenate([k * beta * eg, q * eg], axis=0)).astype(BF16)
        r = _dot(lhs, S.astype(BF16))
        v_new = v * beta - r[0:1]
        qk = jnp.sum(q.astype(BF16).astype(F32) * k.astype(BF16).astype(F32), axis=-1, keepdims=True)
        vn16 = v_new.astype(BF16)
        o = r[1:2] + qk.astype(BF16).astype(F32) * vn16.astype(F32)
        s_out_ref[h] = S * eg + _dot_tn(_pad8(k).astype(BF16), _pad8(vn16))
        o = _rms_rows(o, ng_ref[...])
        o_ref[:, h * dv:(h + 1) * dv] = o * jax.nn.silu(z_ref[:, h * dv:(h + 1) * dv])


def _gdn_sample(conv_in, conv_prev, z, gates_col, conv_w, norm_g, S0, nh, dk, dv):
    Bn, _, CD = conv_in.shape
    kw = conv_w.shape[0]
    per_b3 = lambda b: (b, 0, 0)
    kernel = functools.partial(_gdn_sample_body, nh=nh, dk=dk, dv=dv)
    return pl.pallas_call(
        kernel,
        out_shape=[jax.ShapeDtypeStruct((Bn, 1, nh * dv), F32), jax.ShapeDtypeStruct((Bn, kw - 1, CD), F32),
                   jax.ShapeDtypeStruct((Bn, nh, dk, dv), F32)],
        grid=(Bn,),
        in_specs=[
            pl.BlockSpec((None, 1, CD), per_b3),
            pl.BlockSpec((None, kw - 1, CD), per_b3),
            pl.BlockSpec((None, 1, nh * dv), per_b3),
            pl.BlockSpec((None, 1, LANES), per_b3),
            pl.BlockSpec(conv_w.shape, lambda b: (0, 0)),
            pl.BlockSpec((1, dv), lambda b: (0, 0)),
            pl.BlockSpec((None, nh, dk, dv), lambda b: (b, 0, 0, 0)),
        ],
        out_specs=[
            pl.BlockSpec((None, 1, nh * dv), per_b3),
            pl.BlockSpec((None, kw - 1, CD), per_b3),
            pl.BlockSpec((None, nh, dk, dv), lambda b: (b, 0, 0, 0)),
        ],
        compiler_params=_cparams(1),
        name="gdn_sample",
    )(conv_in, conv_prev, z, gates_col, conv_w, norm_g.reshape(1, dv), S0)


def _ret_sample_body(q_ref, k_ref, v_ref, gate_ref, fr_ref, gn_ref, s0_ref, o_ref, s_out_ref, *, nh, dk, dv, pos):
    cos2, sin2 = _rope_tables(jnp.full((1, 1), pos, F32), fr_ref[...])
    qs = _rope(q_ref[...], cos2, sin2, nh, dk)
    ks = _rope(k_ref[...], cos2, sin2, nh, dk)
    for h in range(nh):
        lg = _log_gamma(h)
        q = qs[h]
        k = ks[h] * dk ** -0.5
        v16 = v_ref[:, h * dv:(h + 1) * dv].astype(BF16)
        S = s0_ref[h]
        qk = jnp.sum(q.astype(BF16).astype(F32) * k.astype(BF16).astype(F32), axis=-1, keepdims=True)
        inner = qk.astype(BF16).astype(F32) * v16.astype(F32)
        cross = _dot(_pad8(q * math.exp(lg)).astype(BF16), S.astype(BF16))[0:1]
        s_out_ref[h] = S * math.exp(lg) + _dot_tn(_pad8(k).astype(BF16), _pad8(v16))
        o = _head_layernorm(inner + cross) * gn_ref[:, h * dv:(h + 1) * dv]
        o_ref[:, h * dv:(h + 1) * dv] = jax.nn.silu(gate_ref[:, h * dv:(h + 1) * dv]) * o


def _ret_sample(q, k, v, gate, freqs, gnorm, S0, pos, nh, dk, dv):
    Bn = q.shape[0]
    per_b3 = lambda b: (b, 0, 0)
    kernel = functools.partial(_ret_sample_body, nh=nh, dk=dk, dv=dv, pos=float(pos))
    return pl.pallas_call(
        kernel,
        out_shape=[jax.ShapeDtypeStruct((Bn, 1, nh * dv), F32), jax.ShapeDtypeStruct((Bn, nh, dk, dv), F32)],
        grid=(Bn,),
        in_specs=[
            pl.BlockSpec((None, 1, nh * dk), per_b3),
            pl.BlockSpec((None, 1, nh * dk), per_b3),
            pl.BlockSpec((None, 1, nh * dv), per_b3),
            pl.BlockSpec((None, 1, nh * dv), per_b3),
            pl.BlockSpec((1, dk // 2), lambda b: (0, 0)),
            pl.BlockSpec((1, nh * dv), lambda b: (0, 0)),
            pl.BlockSpec((None, nh, dk, dv), lambda b: (b, 0, 0, 0)),
        ],
        out_specs=[
            pl.BlockSpec((None, 1, nh * dv), per_b3),
            pl.BlockSpec((None, nh, dk, dv), lambda b: (b, 0, 0, 0)),
        ],
        compiler_params=_cparams(1),
        name="ret_sample",
    )(q, k, v, gate, freqs, gnorm.reshape(1, nh * dv), S0)


def kernel(x_prompt, x_sample, cache_fox_k, cache_fox_v, cache_fox_logf, page_table, state_gdn_conv, state_gdn_S, state_ret_S, norm_g, final_norm_g, ffn_w_gu, ffn_w_down, ab_w_in, ab_w_out, fox_b_f, gdn_conv_w, gdn_A_log, gdn_dt_bias, gdn_norm_g, c_w_in, c_w_out, ret_norm_g):
    B, L, D = x_prompt.shape
    Bn, Ls, _ = x_sample.shape
    assert Ls == 1, "the sample group decodes one token per sequence"
    depth = norm_g.shape[0]
    _, n_pool, page, fh, fd = cache_fox_k.shape
    fw = fh * fd
    _, _, gh, gdk, gdv = state_gdn_S.shape
    conv_dim = gdn_conv_w.shape[2]
    kw = gdn_conv_w.shape[1]
    _, _, rh, rdk, rdv = state_ret_S.shape
    n_pages = page_table.shape[1]
    assert fh == gh and 3 * fh <= 16

    xp = x_prompt.reshape(B * L, D)
    xs = x_sample.reshape(Bn, D)
    w_gu16 = ffn_w_gu.astype(BF16)
    w_down16 = ffn_w_down.astype(BF16)
    freqs = (ROPE_BASE ** (-jnp.arange(rdk // 2, dtype=F32) / (rdk // 2))).reshape(1, rdk // 2)

    fkp, fvp, flp, fks, fvs, fls = [], [], [], [], [], []
    gcp, gsp, gcs, gss = [], [], [], []
    rsp, rss = [], []
    for li in range(depth):
        xp = _ffn(xp, norm_g[li, 0], w_gu16[li, 0], w_down16[li, 0])
        xs = _ffn(xs, norm_g[li, 0], w_gu16[li, 0], w_down16[li, 0])
        j = li // 2
        if li % 2 == 0:
            w_in = ab_w_in[j]
            o0 = 3 * fw
            small_w = jnp.concatenate(
                [w_in[:, o0:o0 + fh], w_in[:, o0 + fh + conv_dim + gh * gdv:], jnp.zeros((D, LANES - 3 * fh), F32)], axis=1)
            ws = [w_in[:, :fw], w_in[:, fw:2 * fw], w_in[:, 2 * fw:3 * fw], w_in[:, o0 + fh:o0 + fh + conv_dim],
                  w_in[:, o0 + fh + conv_dim:o0 + fh + conv_dim + gh * gdv], small_w]
            ws = [w.astype(BF16) for w in ws]
            w_out16 = ab_w_out[j].astype(BF16)
            w_outs = [w_out16[:fw], w_out16[fw:]]

            q, k, v, cin, z, small = _rms_proj(xp, norm_g[li, 1], ws)
            gates = _gates(small, fox_b_f[j], gdn_dt_bias[j], gdn_A_log[j])
            gates_row = gates[:, :16].T
            c_rows = _cumsum_rows(gates_row, L)[:fh].reshape(fh, B, L).transpose(1, 0, 2).reshape(B * fh, 1, L)
            fo = _fox_prompt(q, k, v, c_rows, B, L, fh)
            go, S_p = _gdn_prompt(cin, z, gates, gates_row, gdn_conv_w[j], gdn_norm_g[j], B, L, gh, gdk, gdv)
            xp = _out_proj(xp, [fo, go], w_outs)
            fkp.append(k.reshape(B, L, fh, fd))
            fvp.append(v.reshape(B, L, fh, fd))
            flp.append(gates[:, :fh].reshape(B, L, fh))
            gcp.append(cin.reshape(B, L, conv_dim)[:, L - (kw - 1):])
            gsp.append(S_p)

            q, k, v, cin, z, small = _rms_proj(xs, norm_g[li, 1], ws)
            gates = _gates(small, fox_b_f[j], gdn_dt_bias[j], gdn_A_log[j])
            sums = _page_sums(cache_fox_logf[j].reshape(n_pool, page * fh), fh).reshape(n_pool, 1, 2 * page * fh)
            fo = _fox_sample(q.reshape(Bn, fh, fd), k.reshape(Bn, fh, fd), v.reshape(Bn, fh, fd),
                             gates[:, :fh].reshape(Bn, fh, 1),
                             cache_fox_k[j].reshape(n_pool, page * fh, fd), cache_fox_v[j].reshape(n_pool, page * fh, fd),
                             sums, page_table, fh)
            go, conv_s, S_s = _gdn_sample(cin.reshape(Bn, 1, conv_dim), state_gdn_conv[j], z.reshape(Bn, 1, gh * gdv),
                                          gates.reshape(Bn, 1, LANES), gdn_conv_w[j], gdn_norm_g[j], state_gdn_S[j],
                                          gh, gdk, gdv)
            xs = _out_proj(xs, [fo.reshape(Bn, fw), go.reshape(Bn, gh * gdv)], w_outs)
            fks.append(k.reshape(Bn, 1, fh, fd))
            fvs.append(v.reshape(Bn, 1, fh, fd))
            fls.append(gates[:, :fh].reshape(Bn, 1, fh))
            gcs.append(conv_s)
            gss.append(S_s)
        else:
            w_in = c_w_in[j].astype(BF16)
            qk_w, v_w = rh * rdk, rh * rdv
            ws = [w_in[:, :qk_w], w_in[:, qk_w:2 * qk_w], w_in[:, 2 * qk_w:2 * qk_w + v_w], w_in[:, 2 * qk_w + v_w:]]
            w_out16 = c_w_out[j].astype(BF16)

            q, k, v, gate = _rms_proj(xp, norm_g[li, 1], ws)
            y, R_p = _ret_prompt(q, k, v, gate, freqs, ret_norm_g[j], B, L, rh, rdk, rdv)
            xp = _out_proj(xp, [y], [w_out16])
            rsp.append(R_p)

            q, k, v, gate = _rms_proj(xs, norm_g[li, 1], ws)
            y, R_s = _ret_sample(q.reshape(Bn, 1, qk_w), k.reshape(Bn, 1, qk_w), v.reshape(Bn, 1, v_w),
                                 gate.reshape(Bn, 1, v_w), freqs, ret_norm_g[j], state_ret_S[j],
                                 n_pages * page, rh, rdk, rdv)
            xs = _out_proj(xs, [y.reshape(Bn, v_w)], [w_out16])
            rss.append(R_s)
        xp = _ffn(xp, norm_g[li, 2], w_gu16[li, 1], w_down16[li, 1])
        xs = _ffn(xs, norm_g[li, 2], w_gu16[li, 1], w_down16[li, 1])
    y_prompt = _final_norm(xp, final_norm_g).reshape(B, L, D)
    y_sample = _final_norm(xs, final_norm_g).reshape(Bn, 1, D)
    return (y_prompt, y_sample,
            jnp.stack(fkp), jnp.stack(fvp), jnp.stack(flp),
            jnp.stack(fks), jnp.stack(fvs), jnp.stack(fls),
            jnp.stack(gcp), jnp.stack(gsp), jnp.stack(gcs), jnp.stack(gss),
            jnp.stack(rsp), jnp.stack(rss))
```

```python
import functools
import math

import jax
import jax.numpy as jnp
from jax import lax
from jax.experimental import pallas as pl
from jax.experimental.pallas import tpu as pltpu

F32 = jnp.float32
BF16 = jnp.bfloat16
EPS = 1e-6
ROPE_BASE = 10000.0
NEG = -1e30
LANES = 128
GDN_CHUNK = 64
VMEM_LIMIT = 56 * 1024 * 1024
HI = lax.Precision.HIGHEST


def _cparams(n_axes):
    return pltpu.CompilerParams(dimension_semantics=("arbitrary",) * n_axes,
                                vmem_limit_bytes=VMEM_LIMIT)


def _dot(a, b, precision=None):
    return jnp.dot(a, b, preferred_element_type=F32, precision=precision)


def _dot_nt(a, b, precision=None):
    return lax.dot_general(a, b, (((1,), (1,)), ((), ())), preferred_element_type=F32, precision=precision)


def _dot_tn(a, b, precision=None):
    return lax.dot_general(a, b, (((0,), (0,)), ((), ())), preferred_element_type=F32, precision=precision)


def _rms_rows(x, g):
    return x * lax.rsqrt(jnp.mean(x * x, axis=-1, keepdims=True) + EPS) * g


MXU_DEPTH = 256


def _ffn_body(*refs, F, chunk, final_norm):
    if final_norm:
        x_ref, g_ref, wgu_ref, wd_ref, gf_ref, o_ref = refs
    else:
        x_ref, g_ref, wgu_ref, wd_ref, o_ref = refs
    x = x_ref[...]
    xn = _rms_rows(x, g_ref[...]).astype(BF16)
    acc = None
    for c0 in range(0, F, chunk):
        c1 = min(F, c0 + chunk)
        a = _dot(xn, wgu_ref[:, c0:c1])
        b = _dot(xn, wgu_ref[:, F + c0:F + c1])
        h = (jax.nn.silu(a) * b).astype(BF16)
        d = _dot(h, wd_ref[c0:c1, :])
        acc = d if acc is None else acc + d
    y = x + 0.5 * acc
    o_ref[...] = _rms_rows(y, gf_ref[...]) if final_norm else y


def _ffn(x, g, w_gu, w_down, final_g=None):
    M, D = x.shape
    F = w_down.shape[0]
    tm = min(M, 512)
    final_norm = final_g is not None
    const = lambda m: (0, 0)
    in_specs = [pl.BlockSpec((tm, D), lambda m: (m, 0)), pl.BlockSpec((1, D), const),
                pl.BlockSpec((D, 2 * F), const), pl.BlockSpec((F, D), const)]
    args = [x, g.reshape(1, D), w_gu, w_down]
    if final_norm:
        in_specs.append(pl.BlockSpec((1, D), const))
        args.append(final_g.reshape(1, D))
    return pl.pallas_call(
        functools.partial(_ffn_body, F=F, chunk=2 * MXU_DEPTH, final_norm=final_norm),
        out_shape=jax.ShapeDtypeStruct((M, D), F32),
        grid=(M // tm,),
        in_specs=in_specs,
        out_specs=pl.BlockSpec((tm, D), lambda m: (m, 0)),
        compiler_params=_cparams(1),
        name="ffn",
    )(*args)


def _proj_body(*refs, n_out, chunk):
    x_ref, g_ref = refs[:2]
    w_refs = refs[2:2 + n_out]
    o_refs = refs[2 + n_out:]
    xn = _rms_rows(x_ref[...], g_ref[...]).astype(BF16)
    for w_ref, o_ref in zip(w_refs, o_refs):
        n = w_ref.shape[1]
        for c0 in range(0, n, chunk):
            c1 = min(n, c0 + chunk)
            o_ref[:, c0:c1] = _dot(xn, w_ref[:, c0:c1])


def _rms_proj(x, g, ws):
    M, D = x.shape
    tm = min(M, 256)
    n_out = len(ws)
    return pl.pallas_call(
        functools.partial(_proj_body, n_out=n_out, chunk=512),
        out_shape=[jax.ShapeDtypeStruct((M, w.shape[1]), F32) for w in ws],
        grid=(M // tm,),
        in_specs=[pl.BlockSpec((tm, D), lambda m: (m, 0)), pl.BlockSpec((1, D), lambda m: (0, 0))]
        + [pl.BlockSpec(w.shape, lambda m: (0, 0)) for w in ws],
        out_specs=[pl.BlockSpec((tm, w.shape[1]), lambda m: (m, 0)) for w in ws],
        compiler_params=_cparams(1),
        name="rms_proj",
    )(x, g.reshape(1, D), *ws)


def _oproj_body(*refs, n_in):
    x_ref = refs[0]
    a_refs = refs[1:1 + n_in]
    w_refs = refs[1 + n_in:1 + 2 * n_in]
    o_ref = refs[1 + 2 * n_in]
    acc = x_ref[...]
    for a_ref, w_ref in zip(a_refs, w_refs):
        acc = acc + _dot(a_ref[...].astype(BF16), w_ref[...])
    o_ref[...] = acc


def _out_proj(x, acts, ws):
    M, D = x.shape
    tm = min(M, 512)
    n_in = len(acts)
    return pl.pallas_call(
        functools.partial(_oproj_body, n_in=n_in),
        out_shape=jax.ShapeDtypeStruct((M, D), F32),
        grid=(M // tm,),
        in_specs=[pl.BlockSpec((tm, D), lambda m: (m, 0))]
        + [pl.BlockSpec((tm, a.shape[1]), lambda m: (m, 0)) for a in acts]
        + [pl.BlockSpec(w.shape, lambda m: (0, 0)) for w in ws],
        out_specs=pl.BlockSpec((tm, D), lambda m: (m, 0)),
        compiler_params=_cparams(1),
        name="out_proj",
    )(x, *acts, *ws)


def _gates_body(s_ref, p_ref, o_ref, *, nh):
    s = s_ref[...] + p_ref[0:1, :]
    lane = lax.broadcasted_iota(jnp.int32, s.shape, 1)
    logf = jax.nn.log_sigmoid(s)
    g = -jnp.exp(p_ref[1:2, :]) * jax.nn.softplus(s)
    beta = jax.nn.sigmoid(s)
    o_ref[...] = jnp.where(lane < nh, logf, jnp.where(lane < 2 * nh, g, jnp.where(lane < 3 * nh, beta, 0.0)))


def _gates(small, b_f, dt_bias, a_log):
    M = small.shape[0]
    nh = b_f.shape[0]
    tm = min(M, 2048)
    pad = jnp.zeros((LANES - 3 * nh,), F32)
    p = jnp.zeros((8, LANES), F32)
    p = p.at[0].set(jnp.concatenate([b_f, dt_bias, jnp.zeros((nh,), F32), pad]))
    p = p.at[1].set(jnp.concatenate([jnp.zeros((nh,), F32), a_log, jnp.zeros((nh,), F32), pad]))
    return pl.pallas_call(
        functools.partial(_gates_body, nh=nh),
        out_shape=jax.ShapeDtypeStruct((M, LANES), F32),
        grid=(M // tm,),
        in_specs=[pl.BlockSpec((tm, LANES), lambda m: (m, 0)), pl.BlockSpec((8, LANES), lambda m: (0, 0))],
        out_specs=pl.BlockSpec((tm, LANES), lambda m: (m, 0)),
        compiler_params=_cparams(1),
        name="gates",
    )(small, p)


def _cumsum_body(x_ref, o_ref):
    L = x_ref.shape[1]
    r = lax.broadcasted_iota(jnp.int32, (LANES, LANES), 0)
    c = lax.broadcasted_iota(jnp.int32, (LANES, LANES), 1)
    upper = (r <= c).astype(F32)
    carry = jnp.zeros((x_ref.shape[0], 1), F32)
    for j in range(L // LANES):
        cs = _dot(x_ref[:, j * LANES:(j + 1) * LANES], upper, HI) + carry
        o_ref[:, j * LANES:(j + 1) * LANES] = cs
        carry = cs[:, LANES - 1:LANES]


def _cumsum_rows(x, seg):
    R, M = x.shape
    return pl.pallas_call(
        _cumsum_body,
        out_shape=jax.ShapeDtypeStruct((R, M), F32),
        grid=(M // seg,),
        in_specs=[pl.BlockSpec((R, seg), lambda b: (0, b))],
        out_specs=pl.BlockSpec((R, seg), lambda b: (0, b)),
        compiler_params=_cparams(1),
        name="cumsum_rows",
    )(x)


def _fox_body(q_ref, k_ref, v_ref, c_ref, o_ref, vt_ref, cb_ref, acc_ref, m_ref, *, t, scale):
    i = pl.program_id(2)
    L, dh = k_ref.shape
    log2e = math.log2(math.e)

    @pl.when(i == 0)
    def _():
        for jj in range(L // t):
            vt_ref[0:dh, jj * t:(jj + 1) * t] = v_ref[jj * t:(jj + 1) * t, :].T.astype(BF16)
        vt_ref[dh:, :] = jnp.ones((vt_ref.shape[0] - dh, L), BF16)
        for jj in range(L // LANES):
            row = c_ref[:, jj * LANES:(jj + 1) * LANES] * log2e
            cb_ref[jj * LANES:(jj + 1) * LANES, :] = jnp.broadcast_to(row, (LANES, LANES)).T

    q = q_ref[...].astype(BF16)
    q0 = pl.multiple_of(i * t, t)
    r2 = c_ref[:, pl.ds(q0, t)][:, t - 1:t] * log2e
    m_ref[...] = jnp.full_like(m_ref, NEG)
    acc_ref[...] = jnp.zeros_like(acc_ref)

    def step(j, masked):
        k0 = pl.multiple_of(j * t, t)
        k = k_ref[pl.ds(k0, t), :].astype(BF16)
        bias = r2 - cb_ref[pl.ds(k0, t), :]
        st = _dot_nt(k, q) * (scale * log2e) + jnp.concatenate([bias] * (t // LANES), axis=1)
        if masked:
            key = lax.broadcasted_iota(jnp.int32, (t, t), 0)
            qry = lax.broadcasted_iota(jnp.int32, (t, t), 1)
            st = jnp.where(key <= qry, st, NEG)
        m_prev = m_ref[...]
        m_new = jnp.maximum(m_prev, jnp.max(st, axis=0, keepdims=True))
        p = jnp.exp2(st - m_new).astype(BF16)
        alpha = jnp.exp2(m_prev - m_new)
        acc_ref[...] = acc_ref[...] * alpha + _dot(vt_ref[:, pl.ds(k0, t)], p)
        m_ref[...] = m_new

    def loop_body(j, carry):
        step(j, False)
        return carry

    lax.fori_loop(0, i, loop_body, 0)
    step(i, True)
    acc = acc_ref[...]
    o_ref[...] = (acc[0:dh] / acc[dh:dh + 1]).T


def _fox_prompt(q, k, v, c_rows, B, L, H):
    M, W = q.shape
    Dh = W // H
    t = min(L, 512)
    nq = L // t
    kernel = functools.partial(_fox_body, t=t, scale=Dh ** -0.5)
    return pl.pallas_call(
        kernel,
        out_shape=jax.ShapeDtypeStruct((M, W), F32),
        grid=(B, H, nq),
        in_specs=[
            pl.BlockSpec((t, Dh), lambda b, h, i: (b * nq + i, h)),
            pl.BlockSpec((L, Dh), lambda b, h, i: (b, h)),
            pl.BlockSpec((L, Dh), lambda b, h, i: (b, h)),
            pl.BlockSpec((None, 1, L), lambda b, h, i: (b * H + h, 0, 0)),
        ],
        out_specs=pl.BlockSpec((t, Dh), lambda b, h, i: (b * nq + i, h)),
        scratch_shapes=[pltpu.VMEM((Dh + 16, L), BF16), pltpu.VMEM((L, LANES), F32),
                        pltpu.VMEM((Dh + 16, t), F32), pltpu.VMEM((1, t), F32)],
        compiler_params=_cparams(3),
        name="fox_prompt",
    )(q, k, v, c_rows)


def _split2(x):
    hi = x.astype(BF16)
    return hi, (x - hi.astype(F32)).astype(BF16)


def _split3(x):
    h1 = x.astype(BF16)
    r1 = x - h1.astype(F32)
    h2 = r1.astype(BF16)
    return h1, h2, (r1 - h2.astype(F32)).astype(BF16)


def _dot3(a, b):
    (ah, al), (bh, bl) = a, b
    return _dot(ah, bh) + (_dot(ah, bl) + _dot(al, bh))


def _unit_lower_inverses(a_list, n):
    ri = lax.broadcasted_iota(jnp.int32, (n, n), 0)
    ci = lax.broadcasted_iota(jnp.int32, (n, n), 1)
    eye = (ri == ci).astype(F32)
    nm = len(a_list)
    diag16 = (ri >> 4) == (ci >> 4)
    pw = [jnp.where(diag16, a, 0.0) for a in a_list]
    t = [eye - p for p in pw]
    for _ in range(3):
        pw16 = [p.astype(BF16) for p in pw]
        pw = [_dot(p, p) for p in pw16]
        pn16 = [p.astype(BF16) for p in pw]
        t = [t[m] + _dot(t[m].astype(BF16), pn16[m]) for m in range(nm)]
    size = 16
    while size < GDN_CHUNK:
        sh = size.bit_length() - 1
        off = ((ri >> (sh + 1)) == (ci >> (sh + 1))) & ((ri >> sh) == (ci >> sh) + 1)
        t16 = [x.astype(BF16) for x in t]
        mid = [_dot(jnp.where(off, a_list[m], 0.0).astype(BF16), t16[m]) for m in range(nm)]
        t = [t[m] - _dot(t16[m], mid[m].astype(BF16)) for m in range(nm)]
        size *= 2
    ts = [_split2(x) for x in t]
    res = [(eye - t[m]) - _dot3(_split2(a_list[m]), ts[m]) for m in range(nm)]
    return [t[m] + _dot(ts[m][0], res[m].astype(BF16)) for m in range(nm)]


def _gdn_body(cin_ref, z_ref, gc_ref, gr_ref, cw_ref, ng_ref, o_ref, s_out_ref, s_ref, tail_ref, *, nh, dk, dv):
    j = pl.program_id(1)
    T = cin_ref.shape[0]
    C = GDN_CHUNK

    @pl.when(j == 0)
    def _():
        s_ref[...] = jnp.zeros_like(s_ref)
        tail_ref[...] = jnp.zeros_like(tail_ref)

    u = cin_ref[...]
    full = jnp.concatenate([tail_ref[...], u], axis=0)
    tail_ref[...] = u[T - 8:T]
    w = cw_ref[...]
    kw = w.shape[0]
    conv = full[8 - kw + 1:8 - kw + 1 + T] * w[0:1]
    for i in range(1, kw):
        conv = conv + full[8 - kw + 1 + i:8 - kw + 1 + i + T] * w[i:i + 1]
    conv = jax.nn.silu(conv)

    ri = lax.broadcasted_iota(jnp.int32, (T, T), 0)
    ci = lax.broadcasted_iota(jnp.int32, (T, T), 1)
    same = (ri // C) == (ci // C)
    tril = same & (ri >= ci)
    strict = same & (ri > ci)
    tril16 = tril.astype(BF16)
    triu16 = (same & (ri <= ci)).astype(BF16)
    gcol_all = sum(_dot(tril16, part) for part in _split3(gc_ref[...]))
    grow_all = sum(_dot(part, triu16) for part in _split3(gr_ref[...]))
    rowi = lax.broadcasted_iota(jnp.int32, (T, 1), 0)
    nc = T // C

    heads = range(nh)
    q, k, beta, gcc, gcr, gam, kb, k16 = [], [], [], [], [], [], [], []
    for h in heads:
        qh = conv[:, h * dk:(h + 1) * dk]
        kh = conv[:, nh * dk + h * dk:nh * dk + (h + 1) * dk]
        q.append(qh * lax.rsqrt(jnp.sum(qh * qh, axis=-1, keepdims=True) + EPS) * dk ** -0.5)
        k.append(kh * lax.rsqrt(jnp.sum(kh * kh, axis=-1, keepdims=True) + EPS))
        beta.append(gc_ref[:, 2 * nh + h:2 * nh + h + 1])
        gcc.append(gcol_all[:, nh + h:nh + h + 1])
        gcr.append(grow_all[nh + h:nh + h + 1, :])
        gam.append(jnp.exp(jnp.where(tril, gcc[h] - gcr[h], NEG)))
        kb.append(k[h] * beta[h])
        k16.append(k[h].astype(BF16))
    a = [jnp.where(strict, _dot_nt(kb[h].astype(BF16), k16[h]) * gam[h], 0.0) for h in heads]
    tinv = _unit_lower_inverses(a, T)
    x = []
    for h in heads:
        v = conv[:, 2 * nh * dk + h * dv:2 * nh * dk + (h + 1) * dv]
        rhs = jnp.concatenate([v * beta[h], kb[h] * jnp.exp(gcc[h])], axis=-1)
        x.append(_dot3(_split2(tinv[h]), _split2(rhs)))
    attn, qg, kd, egl = [], [], [], []
    for h in heads:
        attn.append((_dot_nt(q[h].astype(BF16), k16[h]) * gam[h]).astype(BF16))
        qg.append((q[h] * jnp.exp(gcc[h])).astype(BF16))
        gl = gcr[h][:, C - 1:C]
        for cidx in range(1, nc):
            gl = jnp.where(rowi >= cidx * C, gcr[h][:, (cidx + 1) * C - 1:(cidx + 1) * C], gl)
        kd.append((k[h] * jnp.exp(gl - gcc[h])).astype(BF16))
        egl.append(jnp.exp(gl))
    S = [s_ref[h] for h in heads]
    outs = [[] for _ in heads]
    for cidx in range(nc):
        sl = slice(cidx * C, (cidx + 1) * C)
        S16 = [s.astype(BF16) for s in S]
        vn16 = [(x[h][sl, :dv] - _dot(x[h][sl, dv:].astype(BF16), S16[h])).astype(BF16) for h in heads]
        for h in heads:
            outs[h].append(_dot(qg[h][sl], S16[h]) + _dot(attn[h][sl, cidx * C:(cidx + 1) * C], vn16[h]))
        S = [S[h] * egl[h][cidx * C:cidx * C + 1] + _dot_tn(kd[h][sl], vn16[h]) for h in heads]
    for h in heads:
        s_ref[h] = S[h]
        o = _rms_rows(jnp.concatenate(outs[h], axis=0), ng_ref[...])
        o_ref[:, h * dv:(h + 1) * dv] = o * jax.nn.silu(z_ref[:, h * dv:(h + 1) * dv])

    @pl.when(j == pl.num_programs(1) - 1)
    def _():
        s_out_ref[...] = s_ref[...]


def _gdn_prompt(conv_in, z, gates_col, gates_row, conv_w, norm_g, B, L, nh, dk, dv):
    M, CD = conv_in.shape
    T = 2 * GDN_CHUNK
    nb = L // T
    kernel = functools.partial(_gdn_body, nh=nh, dk=dk, dv=dv)
    return pl.pallas_call(
        kernel,
        out_shape=[jax.ShapeDtypeStruct((M, nh * dv), F32), jax.ShapeDtypeStruct((B, nh, dk, dv), F32)],
        grid=(B, nb),
        in_specs=[
            pl.BlockSpec((T, CD), lambda b, j: (b * nb + j, 0)),
            pl.BlockSpec((T, nh * dv), lambda b, j: (b * nb + j, 0)),
            pl.BlockSpec((T, LANES), lambda b, j: (b * nb + j, 0)),
            pl.BlockSpec((16, T), lambda b, j: (0, b * nb + j)),
            pl.BlockSpec(conv_w.shape, lambda b, j: (0, 0)),
            pl.BlockSpec((1, dv), lambda b, j: (0, 0)),
        ],
        out_specs=[
            pl.BlockSpec((T, nh * dv), lambda b, j: (b * nb + j, 0)),
            pl.BlockSpec((None, nh, dk, dv), lambda b, j: (b, 0, 0, 0)),
        ],
        scratch_shapes=[pltpu.VMEM((nh, dk, dv), F32), pltpu.VMEM((8, CD), F32)],
        compiler_params=_cparams(2),
        name="gdn_prompt",
    )(conv_in, z, gates_col, gates_row, conv_w, norm_g.reshape(1, dv))


def _rope(x, cos2, sin2, nh, dk):
    outs = []
    for h in range(nh):
        xh = x[:, h * dk:(h + 1) * dk]
        outs.append(xh * cos2 + pltpu.roll(xh, dk // 2, 1) * sin2)
    return outs


def _rope_tables(pos, freqs):
    ang = pos * freqs
    cos, sin = jnp.cos(ang), jnp.sin(ang)
    return jnp.concatenate([cos, cos], axis=-1), jnp.concatenate([-sin, sin], axis=-1)


def _head_layernorm(o):
    mu = jnp.mean(o, axis=-1, keepdims=True)
    oc = o - mu
    return oc * lax.rsqrt(jnp.mean(oc * oc, axis=-1, keepdims=True) + EPS)


def _log_gamma(h):
    return math.log(1.0 - 2.0 ** (-5.0 - h))


def _ret_body(q_ref, k_ref, v_ref, gate_ref, fr_ref, gn_ref, o_ref, s_out_ref, s_ref, *, nh, dk, dv):
    j = pl.program_id(1)
    C = q_ref.shape[0]

    @pl.when(j == 0)
    def _():
        s_ref[...] = jnp.zeros_like(s_ref)

    rowi = lax.broadcasted_iota(jnp.int32, (C, 1), 0)
    pos = (j * C + rowi).astype(F32)
    cos2, sin2 = _rope_tables(pos, fr_ref[...])
    qs = _rope(q_ref[...], cos2, sin2, nh, dk)
    ks = _rope(k_ref[...], cos2, sin2, nh, dk)
    ri = lax.broadcasted_iota(jnp.int32, (C, C), 0)
    ci = lax.broadcasted_iota(jnp.int32, (C, C), 1)
    diff = (ri - ci).astype(F32)
    rowf = rowi.astype(F32)
    for h in range(nh):
        lg = _log_gamma(h)
        q = qs[h]
        k = ks[h] * dk ** -0.5
        v = v_ref[:, h * dv:(h + 1) * dv].astype(BF16)
        dmat = jnp.exp(jnp.where(ri >= ci, diff * lg, NEG))
        q16 = q.astype(BF16)
        inner = _dot((_dot_nt(q16, k.astype(BF16)) * dmat).astype(BF16), v)
        S = s_ref[h]
        cross = _dot((q * jnp.exp(lg * (rowf + 1.0))).astype(BF16), S.astype(BF16))
        s_ref[h] = S * math.exp(lg * C) + _dot_tn((k * jnp.exp(lg * (C - 1.0 - rowf))).astype(BF16), v)
        o = _head_layernorm(inner + cross) * gn_ref[:, h * dv:(h + 1) * dv]
        o_ref[:, h * dv:(h + 1) * dv] = jax.nn.silu(gate_ref[:, h * dv:(h + 1) * dv]) * o

    @pl.when(j == pl.num_programs(1) - 1)
    def _():
        s_out_ref[...] = s_ref[...]


def _ret_prompt(q, k, v, gate, freqs, gnorm, B, L, nh, dk, dv):
    M = q.shape[0]
    C = 128
    nb = L // C
    kernel = functools.partial(_ret_body, nh=nh, dk=dk, dv=dv)
    return pl.pallas_call(
        kernel,
        out_shape=[jax.ShapeDtypeStruct((M, nh * dv), F32), jax.ShapeDtypeStruct((B, nh, dk, dv), F32)],
        grid=(B, nb),
        in_specs=[
            pl.BlockSpec((C, nh * dk), lambda b, j: (b * nb + j, 0)),
            pl.BlockSpec((C, nh * dk), lambda b, j: (b * nb + j, 0)),
            pl.BlockSpec((C, nh * dv), lambda b, j: (b * nb + j, 0)),
            pl.BlockSpec((C, nh * dv), lambda b, j: (b * nb + j, 0)),
            pl.BlockSpec((1, dk // 2), lambda b, j: (0, 0)),
            pl.BlockSpec((1, nh * dv), lambda b, j: (0, 0)),
        ],
        out_specs=[
            pl.BlockSpec((C, nh * dv), lambda b, j: (b * nb + j, 0)),
            pl.BlockSpec((None, nh, dk, dv), lambda b, j: (b, 0, 0, 0)),
        ],
        scratch_shapes=[pltpu.VMEM((nh, dk, dv), F32)],
        compiler_params=_cparams(2),
        name="ret_prompt",
    )(q, k, v, gate, freqs, gnorm.reshape(1, nh * dv))


def _page_sums_body(x_ref, o_ref, *, nh):
    n = x_ref.shape[1]
    fi = lax.broadcasted_iota(jnp.int32, (n, n), 0)
    fo = lax.broadcasted_iota(jnp.int32, (n, n), 1)
    same_head = (fi % nh) == (fo % nh)
    x = x_ref[...]
    o_ref[:, :n] = _dot(x, (same_head & (fi > fo)).astype(F32), HI)
    o_ref[:, n:] = _dot(x, same_head.astype(F32), HI)


def _page_sums(lf_pages, nh):
    P, n = lf_pages.shape
    tp = 512 if P % 512 == 0 else P
    return pl.pallas_call(
        functools.partial(_page_sums_body, nh=nh),
        out_shape=jax.ShapeDtypeStruct((P, 2 * n), F32),
        grid=(P // tp,),
        in_specs=[pl.BlockSpec((tp, n), lambda p: (p, 0))],
        out_specs=pl.BlockSpec((tp, 2 * n), lambda p: (p, 0)),
        compiler_params=_cparams(1),
        name="page_sums",
    )(lf_pages)


def _paged_body(pt_ref, *refs, G, nh, scale):
    k_refs = refs[:G]
    v_refs = refs[G:2 * G]
    w_refs = refs[2 * G:3 * G]
    q_ref, kn_ref, vn_ref, lfn_ref, o_ref, m_ref, l_ref, acc_ref, carry_ref = refs[3 * G:]
    s_idx = pl.program_id(1)
    n = k_refs[0].shape[0]
    dh = q_ref.shape[1]

    @pl.when(s_idx == 0)
    def _():
        m_ref[...] = jnp.full_like(m_ref, NEG)
        l_ref[...] = jnp.zeros_like(l_ref)
        acc_ref[...] = jnp.zeros_like(acc_ref)
        carry_ref[...] = jnp.zeros_like(carry_ref)

    q = q_ref[...]
    q8 = jnp.concatenate([q, jnp.zeros((8 - nh, dh), F32)], axis=0).astype(BF16)
    lfn = jnp.concatenate([lfn_ref[...], jnp.zeros((8 - nh, 1), F32)], axis=0)
    hrow = lax.broadcasted_iota(jnp.int32, (8, n), 0)
    hcol = lax.broadcasted_iota(jnp.int32, (8, n), 1) % nh
    match = hrow == hcol
    carry = carry_ref[...]
    s_parts = [None] * G
    for i in reversed(range(G)):
        wt = w_refs[i][...]
        bias = wt[:, :n] + carry
        carry = carry + wt[:, n:]
        s = _dot_nt(q8, k_refs[i][...].astype(BF16)) * scale + bias + lfn
        s_parts[i] = jnp.where(match, s, NEG)
    carry_ref[...] = carry
    m_prev = m_ref[...]
    m_new = jnp.maximum(m_prev, jnp.max(functools.reduce(jnp.maximum, s_parts), axis=-1, keepdims=True))
    alpha = jnp.exp(m_prev - m_new)
    p_parts = [jnp.exp(s - m_new) for s in s_parts]
    l_ref[...] = alpha * l_ref[...] + jnp.sum(functools.reduce(jnp.add, p_parts), axis=-1, keepdims=True)
    pv = _dot(p_parts[0].astype(BF16), v_refs[0][...].astype(BF16))
    for i in range(1, G):
        pv = pv + _dot(p_parts[i].astype(BF16), v_refs[i][...].astype(BF16))
    acc_ref[...] = alpha * acc_ref[...] + pv
    m_ref[...] = m_new

    @pl.when(s_idx == pl.num_programs(1) - 1)
    def _():
        qf = q.astype(BF16).astype(F32)
        kn = kn_ref[...].astype(BF16).astype(F32)
        s_new = jnp.sum(qf * kn, axis=-1, keepdims=True) * scale
        m_prev = m_ref[0:nh, :]
        m_new = jnp.maximum(m_prev, s_new)
        alpha = jnp.exp(m_prev - m_new)
        p_new = jnp.exp(s_new - m_new)
        l_fin = alpha * l_ref[0:nh, :] + p_new
        vn = vn_ref[...].astype(BF16).astype(F32)
        num = alpha * acc_ref[0:nh, :] + p_new.astype(BF16).astype(F32) * vn
        o_ref[...] = num / l_fin


def _fox_sample(q, k_new, v_new, lf_new, k_pages, v_pages, page_sums, page_table, nh):
    Bn, _, dh = q.shape
    n_pages = page_table.shape[1]
    n = k_pages.shape[1]
    G = 8 if n_pages % 8 == 0 else 1
    ns = n_pages // G

    def page_map(i):
        return lambda b, s, pt: (pt[b, (ns - 1 - s) * G + i], 0, 0)

    per_b = lambda b, s, pt: (b, 0, 0)
    grid_spec = pltpu.PrefetchScalarGridSpec(
        num_scalar_prefetch=1,
        grid=(Bn, ns),
        in_specs=[pl.BlockSpec((None, n, dh), page_map(i)) for i in range(G)]
        + [pl.BlockSpec((None, n, dh), page_map(i)) for i in range(G)]
        + [pl.BlockSpec((None, 1, 2 * n), page_map(i)) for i in range(G)]
        + [pl.BlockSpec((None, nh, dh), per_b), pl.BlockSpec((None, nh, dh), per_b),
           pl.BlockSpec((None, nh, dh), per_b), pl.BlockSpec((None, nh, 1), per_b)],
        out_specs=pl.BlockSpec((None, nh, dh), per_b),
        scratch_shapes=[pltpu.VMEM((8, 1), F32), pltpu.VMEM((8, 1), F32), pltpu.VMEM((8, dh), F32),
                        pltpu.VMEM((1, n), F32)],
    )
    return pl.pallas_call(
        functools.partial(_paged_body, G=G, nh=nh, scale=dh ** -0.5),
        out_shape=jax.ShapeDtypeStruct((Bn, nh, dh), F32),
        grid_spec=grid_spec,
        compiler_params=_cparams(2),
        name="fox_sample",
    )(page_table, *([k_pages] * G), *([v_pages] * G), *([page_sums] * G), q, k_new, v_new, lf_new)


def _pad8(row):
    return jnp.concatenate([row, jnp.zeros((8 - row.shape[0], row.shape[1]), row.dtype)], axis=0)


def _gdn_sample_body(cin_ref, prev_ref, z_ref, gt_ref, cw_ref, ng_ref, s0_ref, o_ref, cnew_ref, s_out_ref, *, nh, dk, dv):
    u = cin_ref[...]
    prev = prev_ref[...]
    w = cw_ref[...]
    kw = w.shape[0]
    conv = u * w[kw - 1:kw]
    for i in range(kw - 1):
        conv = conv + prev[i:i + 1] * w[i:i + 1]
    cnew_ref[0:kw - 2, :] = prev[1:kw - 1]
    cnew_ref[kw - 2:kw - 1, :] = u
    conv = jax.nn.silu(conv)
    gt = gt_ref[...]
    for h in range(nh):
        q = conv[:, h * dk:(h + 1) * dk]
        k = conv[:, nh * dk + h * dk:nh * dk + (h + 1) * dk]
        v = conv[:, 2 * nh * dk + h * dv:2 * nh * dk + (h + 1) * dv]
        q = q * lax.rsqrt(jnp.sum(q * q, axis=-1, keepdims=True) + EPS) * dk ** -0.5
        k = k * lax.rsqrt(jnp.sum(k * k, axis=-1, keepdims=True) + EPS)
        g = gt[:, nh + h:nh + h + 1]
        beta = gt[:, 2 * nh + h:2 * nh + h + 1]
        eg = jnp.exp(g)
        S = s0_ref[h]
        lhs = _pad8(jnp.concatenate([k * beta * eg, q * eg], axis=0)).astype(BF16)
        r = _dot(lhs, S.astype(BF16))
        v_new = v * beta - r[0:1]
        qk = jnp.sum(q.astype(BF16).astype(F32) * k.astype(BF16).astype(F32), axis=-1, keepdims=True)
        vn16 = v_new.astype(BF16)
        o = r[1:2] + qk.astype(BF16).astype(F32) * vn16.astype(F32)
        s_out_ref[h] = S * eg + _dot_tn(_pad8(k).astype(BF16), _pad8(vn16))
        o = _rms_rows(o, ng_ref[...])
        o_ref[:, h * dv:(h + 1) * dv] = o * jax.nn.silu(z_ref[:, h * dv:(h + 1) * dv])


def _gdn_sample(conv_in, conv_prev, z, gates_col, conv_w, norm_g, S0, nh, dk, dv):
    Bn, _, CD = conv_in.shape
    kw = conv_w.shape[0]
    per_b3 = lambda b: (b, 0, 0)
    kernel = functools.partial(_gdn_sample_body, nh=nh, dk=dk, dv=dv)
    return pl.pallas_call(
        kernel,
        out_shape=[jax.ShapeDtypeStruct((Bn, 1, nh * dv), F32), jax.ShapeDtypeStruct((Bn, kw - 1, CD), F32),
                   jax.ShapeDtypeStruct((Bn, nh, dk, dv), F32)],
        grid=(Bn,),
        in_specs=[
            pl.BlockSpec((None, 1, CD), per_b3),
            pl.BlockSpec((None, kw - 1, CD), per_b3),
            pl.BlockSpec((None, 1, nh * dv), per_b3),
            pl.BlockSpec((None, 1, LANES), per_b3),
            pl.BlockSpec(conv_w.shape, lambda b: (0, 0)),
            pl.BlockSpec((1, dv), lambda b: (0, 0)),
            pl.BlockSpec((None, nh, dk, dv), lambda b: (b, 0, 0, 0)),
        ],
        out_specs=[
            pl.BlockSpec((None, 1, nh * dv), per_b3),
            pl.BlockSpec((None, kw - 1, CD), per_b3),
            pl.BlockSpec((None, nh, dk, dv), lambda b: (b, 0, 0, 0)),
        ],
        compiler_params=_cparams(1),
        name="gdn_sample",
    )(conv_in, conv_prev, z, gates_col, conv_w, norm_g.reshape(1, dv), S0)


def _ret_sample_body(q_ref, k_ref, v_ref, gate_ref, fr_ref, gn_ref, s0_ref, o_ref, s_out_ref, *, nh, dk, dv, pos):
    cos2, sin2 = _rope_tables(jnp.full((1, 1), pos, F32), fr_ref[...])
    qs = _rope(q_ref[...], cos2, sin2, nh, dk)
    ks = _rope(k_ref[...], cos2, sin2, nh, dk)
    for h in range(nh):
        lg = _log_gamma(h)
        q = qs[h]
        k = ks[h] * dk ** -0.5
        v16 = v_ref[:, h * dv:(h + 1) * dv].astype(BF16)
        S = s0_ref[h]
        qk = jnp.sum(q.astype(BF16).astype(F32) * k.astype(BF16).astype(F32), axis=-1, keepdims=True)
        inner = qk.astype(BF16).astype(F32) * v16.astype(F32)
        cross = _dot(_pad8(q * math.exp(lg)).astype(BF16), S.astype(BF16))[0:1]
        s_out_ref[h] = S * math.exp(lg) + _dot_tn(_pad8(k).astype(BF16), _pad8(v16))
        o = _head_layernorm(inner + cross) * gn_ref[:, h * dv:(h + 1) * dv]
        o_ref[:, h * dv:(h + 1) * dv] = jax.nn.silu(gate_ref[:, h * dv:(h + 1) * dv]) * o


def _ret_sample(q, k, v, gate, freqs, gnorm, S0, pos, nh, dk, dv):
    Bn = q.shape[0]
    per_b3 = lambda b: (b, 0, 0)
    kernel = functools.partial(_ret_sample_body, nh=nh, dk=dk, dv=dv, pos=float(pos))
    return pl.pallas_call(
        kernel,
        out_shape=[jax.ShapeDtypeStruct((Bn, 1, nh * dv), F32), jax.ShapeDtypeStruct((Bn, nh, dk, dv), F32)],
        grid=(Bn,),
        in_specs=[
            pl.BlockSpec((None, 1, nh * dk), per_b3),
            pl.BlockSpec((None, 1, nh * dk), per_b3),
            pl.BlockSpec((None, 1, nh * dv), per_b3),
            pl.BlockSpec((None, 1, nh * dv), per_b3),
            pl.BlockSpec((1, dk // 2), lambda b: (0, 0)),
            pl.BlockSpec((1, nh * dv), lambda b: (0, 0)),
            pl.BlockSpec((None, nh, dk, dv), lambda b: (b, 0, 0, 0)),
        ],
        out_specs=[
            pl.BlockSpec((None, 1, nh * dv), per_b3),
            pl.BlockSpec((None, nh, dk, dv), lambda b: (b, 0, 0, 0)),
        ],
        compiler_params=_cparams(1),
        name="ret_sample",
    )(q, k, v, gate, freqs, gnorm.reshape(1, nh * dv), S0)


def kernel(x_prompt, x_sample, cache_fox_k, cache_fox_v, cache_fox_logf, page_table, state_gdn_conv, state_gdn_S, state_ret_S, norm_g, final_norm_g, ffn_w_gu, ffn_w_down, ab_w_in, ab_w_out, fox_b_f, gdn_conv_w, gdn_A_log, gdn_dt_bias, gdn_norm_g, c_w_in, c_w_out, ret_norm_g):
    B, L, D = x_prompt.shape
    Bn, Ls, _ = x_sample.shape
    assert Ls == 1, "the sample group decodes one token per sequence"
    depth = norm_g.shape[0]
    _, n_pool, page, fh, fd = cache_fox_k.shape
    fw = fh * fd
    _, _, gh, gdk, gdv = state_gdn_S.shape
    conv_dim = gdn_conv_w.shape[2]
    kw = gdn_conv_w.shape[1]
    _, _, rh, rdk, rdv = state_ret_S.shape
    n_pages = page_table.shape[1]
    assert fh == gh and 3 * fh <= 16

    xp = x_prompt.reshape(B * L, D)
    xs = x_sample.reshape(Bn, D)
    w_gu16 = ffn_w_gu.astype(BF16)
    w_down16 = ffn_w_down.astype(BF16)
    freqs = (ROPE_BASE ** (-jnp.arange(rdk // 2, dtype=F32) / (rdk // 2))).reshape(1, rdk // 2)

    fkp, fvp, flp, fks, fvs, fls = [], [], [], [], [], []
    gcp, gsp, gcs, gss = [], [], [], []
    rsp, rss = [], []
    for li in range(depth):
        xp = _ffn(xp, norm_g[li, 0], w_gu16[li, 0], w_down16[li, 0])
        xs = _ffn(xs, norm_g[li, 0], w_gu16[li, 0], w_down16[li, 0])
        j = li // 2
        if li % 2 == 0:
            w_in = ab_w_in[j]
            o0 = 3 * fw
            small_w = jnp.concatenate(
                [w_in[:, o0:o0 + fh], w_in[:, o0 + fh + conv_dim + gh * gdv:], jnp.zeros((D, LANES - 3 * fh), F32)], axis=1)
            ws = [w_in[:, :fw], w_in[:, fw:2 * fw], w_in[:, 2 * fw:3 * fw], w_in[:, o0 + fh:o0 + fh + conv_dim],
                  w_in[:, o0 + fh + conv_dim:o0 + fh + conv_dim + gh * gdv], small_w]
            ws = [w.astype(BF16) for w in ws]
            w_out16 = ab_w_out[j].astype(BF16)
            w_outs = [w_out16[:fw], w_out16[fw:]]

            q, k, v, cin, z, small = _rms_proj(xp, norm_g[li, 1], ws)
            gates = _gates(small, fox_b_f[j], gdn_dt_bias[j], gdn_A_log[j])
            gates_row = gates[:, :16].T
            c_rows = _cumsum_rows(gates_row, L)[:fh].reshape(fh, B, L).transpose(1, 0, 2).reshape(B * fh, 1, L)
            fo = _fox_prompt(q, k, v, c_rows, B, L, fh)
            go, S_p = _gdn_prompt(cin, z, gates, gates_row, gdn_conv_w[j], gdn_norm_g[j], B, L, gh, gdk, gdv)
            xp = _out_proj(xp, [fo, go], w_outs)
            fkp.append(k.reshape(B, L, fh, fd))
            fvp.append(v.reshape(B, L, fh, fd))
            flp.append(gates[:, :fh].reshape(B, L, fh))
            gcp.append(cin.reshape(B, L, conv_dim)[:, L - (kw - 1):])
            gsp.append(S_p)

            q, k, v, cin, z, small = _rms_proj(xs, norm_g[li, 1], ws)
            gates = _gates(small, fox_b_f[j], gdn_dt_bias[j], gdn_A_log[j])
            sums = _page_sums(cache_fox_logf[j].reshape(n_pool, page * fh), fh).reshape(n_pool, 1, 2 * page * fh)
            fo = _fox_sample(q.reshape(Bn, fh, fd), k.reshape(Bn, fh, fd), v.reshape(Bn, fh, fd),
                             gates[:, :fh].reshape(Bn, fh, 1),
                             cache_fox_k[j].reshape(n_pool, page * fh, fd), cache_fox_v[j].reshape(n_pool, page * fh, fd),
                             sums, page_table, fh)
            go, conv_s, S_s = _gdn_sample(cin.reshape(Bn, 1, conv_dim), state_gdn_conv[j], z.reshape(Bn, 1, gh * gdv),
                                          gates.reshape(Bn, 1, LANES), gdn_conv_w[j], gdn_norm_g[j], state_gdn_S[j],
                                          gh, gdk, gdv)
            xs = _out_proj(xs, [fo.reshape(Bn, fw), go.reshape(Bn, gh * gdv)], w_outs)
            fks.append(k.reshape(Bn, 1, fh, fd))
            fvs.append(v.reshape(Bn, 1, fh, fd))
            fls.append(gates[:, :fh].reshape(Bn, 1, fh))
            gcs.append(conv_s)
            gss.append(S_s)
        else:
            w_in = c_w_in[j].astype(BF16)
            qk_w, v_w = rh * rdk, rh * rdv
            ws = [w_in[:, :qk_w], w_in[:, qk_w:2 * qk_w], w_in[:, 2 * qk_w:2 * qk_w + v_w], w_in[:, 2 * qk_w + v_w:]]
            w_out16 = c_w_out[j].astype(BF16)

            q, k, v, gate = _rms_proj(xp, norm_g[li, 1], ws)
            y, R_p = _ret_prompt(q, k, v, gate, freqs, ret_norm_g[j], B, L, rh, rdk, rdv)
            xp = _out_proj(xp, [y], [w_out16])
            rsp.append(R_p)

            q, k, v, gate = _rms_proj(xs, norm_g[li, 1], ws)
            y, R_s = _ret_sample(q.reshape(Bn, 1, qk_w), k.reshape(Bn, 1, qk_w), v.reshape(Bn, 1, v_w),
                                 gate.reshape(Bn, 1, v_w), freqs, ret_norm_g[j], state_ret_S[j],
                                 n_pages * page, rh, rdk, rdv)
            xs = _out_proj(xs, [y.reshape(Bn, v_w)], [w_out16])
            rss.append(R_s)
        final_g = final_norm_g if li == depth - 1 else None
        xp = _ffn(xp, norm_g[li, 2], w_gu16[li, 1], w_down16[li, 1], final_g)
        xs = _ffn(xs, norm_g[li, 2], w_gu16[li, 1], w_down16[li, 1], final_g)
    y_prompt = xp.reshape(B, L, D)
    y_sample = xs.reshape(Bn, 1, D)
    return (y_prompt, y_sample,
            jnp.stack(fkp), jnp.stack(fvp), jnp.stack(flp),
            jnp.stack(fks), jnp.stack(fvs), jnp.stack(fls),
            jnp.stack(gcp), jnp.stack(gsp), jnp.stack(gcs), jnp.stack(gss),
            jnp.stack(rsp), jnp.stack(rss))
```

```python
import functools
import math

import jax
import jax.numpy as jnp
from jax import lax
from jax.experimental import pallas as pl
from jax.experimental.pallas import tpu as pltpu

F32 = jnp.float32
BF16 = jnp.bfloat16
EPS = 1e-6
ROPE_BASE = 10000.0
NEG = -1e30
LANES = 128
GDN_CHUNK = 64
VMEM_LIMIT = 56 * 1024 * 1024
HI = lax.Precision.HIGHEST


def _cparams(n_axes):
    return pltpu.CompilerParams(dimension_semantics=("arbitrary",) * n_axes,
                                vmem_limit_bytes=VMEM_LIMIT)


def _dot(a, b, precision=None):
    return jnp.dot(a, b, preferred_element_type=F32, precision=precision)


def _dot_nt(a, b, precision=None):
    return lax.dot_general(a, b, (((1,), (1,)), ((), ())), preferred_element_type=F32, precision=precision)


def _dot_tn(a, b, precision=None):
    return lax.dot_general(a, b, (((0,), (0,)), ((), ())), preferred_element_type=F32, precision=precision)


def _rms_rows(x, g):
    return x * lax.rsqrt(jnp.mean(x * x, axis=-1, keepdims=True) + EPS) * g


MXU_DEPTH = 256


def _ffn_body(*refs, F, chunk, final_norm):
    if final_norm:
        x_ref, g_ref, wgu_ref, wd_ref, gf_ref, o_ref = refs
    else:
        x_ref, g_ref, wgu_ref, wd_ref, o_ref = refs
    x = x_ref[...]
    xn = _rms_rows(x, g_ref[...]).astype(BF16)
    acc = None
    for c0 in range(0, F, chunk):
        c1 = min(F, c0 + chunk)
        a = _dot(xn, wgu_ref[:, c0:c1])
        b = _dot(xn, wgu_ref[:, F + c0:F + c1])
        h = (jax.nn.silu(a) * b).astype(BF16)
        d = _dot(h, wd_ref[c0:c1, :])
        acc = d if acc is None else acc + d
    y = x + 0.5 * acc
    o_ref[...] = _rms_rows(y, gf_ref[...]) if final_norm else y


def _ffn(x, g, w_gu, w_down, li, j, final_g=None):
    M, D = x.shape
    F = w_down.shape[2]
    tm = min(M, 512)
    final_norm = final_g is not None
    const = lambda m: (0, 0)
    in_specs = [pl.BlockSpec((tm, D), lambda m: (m, 0)), pl.BlockSpec((1, D), const),
                pl.BlockSpec((None, None, D, 2 * F), lambda m: (li, j, 0, 0)),
                pl.BlockSpec((None, None, F, D), lambda m: (li, j, 0, 0))]
    args = [x, g.reshape(1, D), w_gu, w_down]
    if final_norm:
        in_specs.append(pl.BlockSpec((1, D), const))
        args.append(final_g.reshape(1, D))
    return pl.pallas_call(
        functools.partial(_ffn_body, F=F, chunk=2 * MXU_DEPTH, final_norm=final_norm),
        out_shape=jax.ShapeDtypeStruct((M, D), F32),
        grid=(M // tm,),
        in_specs=in_specs,
        out_specs=pl.BlockSpec((tm, D), lambda m: (m, 0)),
        compiler_params=_cparams(1),
        name="ffn",
    )(*args)


def _proj_body(x_ref, g_ref, w_ref, *o_refs, widths, per_head, chunk):
    xn = _rms_rows(x_ref[...], g_ref[...]).astype(BF16)
    col = 0
    for idx, n in enumerate(widths):
        for s0 in range(0, n, chunk):
            s1 = min(n, s0 + chunk)
            val = _dot(xn, w_ref[:, col + s0:col + s1])
            o_refs[idx][:, s0:s1] = val
            if idx in per_head:
                o3 = o_refs[len(widths) + per_head.index(idx)]
                dh = o3.shape[2]
                for hh in range((s1 - s0) // dh):
                    o3[:, s0 // dh + hh, :] = val[:, hh * dh:(hh + 1) * dh]
        col += n


def _rms_proj(x, g, w, widths, per_head=(), dh=LANES):
    M, D = x.shape
    tm = min(M, 256)
    assert sum(widths) == w.shape[1] and all(n % LANES == 0 for n in widths)
    out_shape = [jax.ShapeDtypeStruct((M, n), F32) for n in widths]
    out_specs = [pl.BlockSpec((tm, n), lambda m: (m, 0)) for n in widths]
    for idx in per_head:
        out_shape.append(jax.ShapeDtypeStruct((M, widths[idx] // dh, dh), F32))
        out_specs.append(pl.BlockSpec((tm, widths[idx] // dh, dh), lambda m: (m, 0, 0)))
    return pl.pallas_call(
        functools.partial(_proj_body, widths=tuple(widths), per_head=tuple(per_head), chunk=512),
        out_shape=out_shape,
        grid=(M // tm,),
        in_specs=[pl.BlockSpec((tm, D), lambda m: (m, 0)), pl.BlockSpec((1, D), lambda m: (0, 0)),
                  pl.BlockSpec(w.shape, lambda m: (0, 0))],
        out_specs=out_specs,
        compiler_params=_cparams(1),
        name="rms_proj",
    )(x, g.reshape(1, D), w)


def _oproj_body(*refs, n_in):
    x_ref = refs[0]
    a_refs = refs[1:1 + n_in]
    w_ref, o_ref = refs[1 + n_in], refs[2 + n_in]
    acc = x_ref[...]
    row = 0
    for a_ref in a_refs:
        n = a_ref.shape[1]
        acc = acc + _dot(a_ref[...].astype(BF16), w_ref[row:row + n, :])
        row += n
    o_ref[...] = acc


def _out_proj(x, acts, w):
    M, D = x.shape
    tm = min(M, 512)
    n_in = len(acts)
    assert sum(a.shape[1] for a in acts) == w.shape[0]
    return pl.pallas_call(
        functools.partial(_oproj_body, n_in=n_in),
        out_shape=jax.ShapeDtypeStruct((M, D), F32),
        grid=(M // tm,),
        in_specs=[pl.BlockSpec((tm, D), lambda m: (m, 0))]
        + [pl.BlockSpec((tm, a.shape[1]), lambda m: (m, 0)) for a in acts]
        + [pl.BlockSpec(w.shape, lambda m: (0, 0))],
        out_specs=pl.BlockSpec((tm, D), lambda m: (m, 0)),
        compiler_params=_cparams(1),
        name="out_proj",
    )(x, *acts, w)


def _gates_body(s_ref, p_ref, o_ref, *, nh):
    s = s_ref[...] + p_ref[0:1, :]
    lane = lax.broadcasted_iota(jnp.int32, s.shape, 1)
    logf = jax.nn.log_sigmoid(s)
    g = -jnp.exp(p_ref[1:2, :]) * jax.nn.softplus(s)
    beta = jax.nn.sigmoid(s)
    o_ref[...] = jnp.where(lane < nh, logf, jnp.where(lane < 2 * nh, g, jnp.where(lane < 3 * nh, beta, 0.0)))


def _gates(small, b_f, dt_bias, a_log):
    M = small.shape[0]
    nh = b_f.shape[0]
    tm = min(M, 2048)
    pad = jnp.zeros((LANES - 3 * nh,), F32)
    p = jnp.zeros((8, LANES), F32)
    p = p.at[0].set(jnp.concatenate([b_f, dt_bias, jnp.zeros((nh,), F32), pad]))
    p = p.at[1].set(jnp.concatenate([jnp.zeros((nh,), F32), a_log, jnp.zeros((nh,), F32), pad]))
    return pl.pallas_call(
        functools.partial(_gates_body, nh=nh),
        out_shape=jax.ShapeDtypeStruct((M, LANES), F32),
        grid=(M // tm,),
        in_specs=[pl.BlockSpec((tm, LANES), lambda m: (m, 0)), pl.BlockSpec((8, LANES), lambda m: (0, 0))],
        out_specs=pl.BlockSpec((tm, LANES), lambda m: (m, 0)),
        compiler_params=_cparams(1),
        name="gates",
    )(small, p)


def _cumsum_body(x_ref, o_ref):
    L = x_ref.shape[1]
    r = lax.broadcasted_iota(jnp.int32, (LANES, LANES), 0)
    c = lax.broadcasted_iota(jnp.int32, (LANES, LANES), 1)
    upper = (r <= c).astype(F32)
    carry = jnp.zeros((x_ref.shape[0], 1), F32)
    for j in range(L // LANES):
        cs = _dot(x_ref[:, j * LANES:(j + 1) * LANES], upper, HI) + carry
        o_ref[:, j * LANES:(j + 1) * LANES] = cs
        carry = cs[:, LANES - 1:LANES]


def _cumsum_rows(x, seg):
    R, M = x.shape
    return pl.pallas_call(
        _cumsum_body,
        out_shape=jax.ShapeDtypeStruct((R, M), F32),
        grid=(M // seg,),
        in_specs=[pl.BlockSpec((R, seg), lambda b: (0, b))],
        out_specs=pl.BlockSpec((R, seg), lambda b: (0, b)),
        compiler_params=_cparams(1),
        name="cumsum_rows",
    )(x)


def _fox_body(q_ref, k_ref, v_ref, c_ref, o_ref, vt_ref, cb_ref, acc_ref, m_ref, *, t, scale):
    i = pl.program_id(2)
    L, dh = k_ref.shape
    log2e = math.log2(math.e)

    @pl.when(i == 0)
    def _():
        for jj in range(L // t):
            vt_ref[0:dh, jj * t:(jj + 1) * t] = v_ref[jj * t:(jj + 1) * t, :].T.astype(BF16)
        vt_ref[dh:, :] = jnp.ones((vt_ref.shape[0] - dh, L), BF16)
        for jj in range(L // LANES):
            row = c_ref[:, jj * LANES:(jj + 1) * LANES] * log2e
            cb_ref[jj * LANES:(jj + 1) * LANES, :] = jnp.broadcast_to(row, (LANES, LANES)).T

    q = q_ref[...].astype(BF16)
    q0 = pl.multiple_of(i * t, t)
    r2 = c_ref[:, pl.ds(q0, t)][:, t - 1:t] * log2e
    m_ref[...] = jnp.full_like(m_ref, NEG)
    acc_ref[...] = jnp.zeros_like(acc_ref)

    def step(j, masked):
        k0 = pl.multiple_of(j * t, t)
        k = k_ref[pl.ds(k0, t), :].astype(BF16)
        bias = r2 - cb_ref[pl.ds(k0, t), :]
        st = _dot_nt(k, q) * (scale * log2e) + jnp.concatenate([bias] * (t // LANES), axis=1)
        if masked:
            key = lax.broadcasted_iota(jnp.int32, (t, t), 0)
            qry = lax.broadcasted_iota(jnp.int32, (t, t), 1)
            st = jnp.where(key <= qry, st, NEG)
        m_prev = m_ref[...]
        m_new = jnp.maximum(m_prev, jnp.max(st, axis=0, keepdims=True))
        p = jnp.exp2(st - m_new).astype(BF16)
        alpha = jnp.exp2(m_prev - m_new)
        acc_ref[...] = acc_ref[...] * alpha + _dot(vt_ref[:, pl.ds(k0, t)], p)
        m_ref[...] = m_new

    def loop_body(j, carry):
        step(j, False)
        return carry

    lax.fori_loop(0, i, loop_body, 0)
    step(i, True)
    acc = acc_ref[...]
    o_ref[...] = (acc[0:dh] / acc[dh:dh + 1]).T


def _fox_prompt(q, k, v, c_rows, B, L, H):
    M, W = q.shape
    Dh = W // H
    t = min(L, 512)
    nq = L // t
    kernel = functools.partial(_fox_body, t=t, scale=Dh ** -0.5)
    return pl.pallas_call(
        kernel,
        out_shape=jax.ShapeDtypeStruct((M, W), F32),
        grid=(B, H, nq),
        in_specs=[
            pl.BlockSpec((t, Dh), lambda b, h, i: (b * nq + i, h)),
            pl.BlockSpec((L, Dh), lambda b, h, i: (b, h)),
            pl.BlockSpec((L, Dh), lambda b, h, i: (b, h)),
            pl.BlockSpec((None, 1, L), lambda b, h, i: (b * H + h, 0, 0)),
        ],
        out_specs=pl.BlockSpec((t, Dh), lambda b, h, i: (b * nq + i, h)),
        scratch_shapes=[pltpu.VMEM((Dh + 16, L), BF16), pltpu.VMEM((L, LANES), F32),
                        pltpu.VMEM((Dh + 16, t), F32), pltpu.VMEM((1, t), F32)],
        compiler_params=_cparams(3),
        name="fox_prompt",
    )(q, k, v, c_rows)


def _split2(x):
    hi = x.astype(BF16)
    return hi, (x - hi.astype(F32)).astype(BF16)


def _split3(x):
    h1 = x.astype(BF16)
    r1 = x - h1.astype(F32)
    h2 = r1.astype(BF16)
    return h1, h2, (r1 - h2.astype(F32)).astype(BF16)


def _dot3(a, b):
    (ah, al), (bh, bl) = a, b
    return _dot(ah, bh) + (_dot(ah, bl) + _dot(al, bh))


def _unit_lower_inverses(a_list, n):
    ri = lax.broadcasted_iota(jnp.int32, (n, n), 0)
    ci = lax.broadcasted_iota(jnp.int32, (n, n), 1)
    eye = (ri == ci).astype(F32)
    nm = len(a_list)
    diag16 = (ri >> 4) == (ci >> 4)
    pw = [jnp.where(diag16, a, 0.0) for a in a_list]
    t = [eye - p for p in pw]
    for _ in range(3):
        pw16 = [p.astype(BF16) for p in pw]
        pw = [_dot(p, p) for p in pw16]
        pn16 = [p.astype(BF16) for p in pw]
        t = [t[m] + _dot(t[m].astype(BF16), pn16[m]) for m in range(nm)]
    size = 16
    while size < GDN_CHUNK:
        sh = size.bit_length() - 1
        off = ((ri >> (sh + 1)) == (ci >> (sh + 1))) & ((ri >> sh) == (ci >> sh) + 1)
        t16 = [x.astype(BF16) for x in t]
        mid = [_dot(jnp.where(off, a_list[m], 0.0).astype(BF16), t16[m]) for m in range(nm)]
        t = [t[m] - _dot(t16[m], mid[m].astype(BF16)) for m in range(nm)]
        size *= 2
    ts = [_split2(x) for x in t]
    res = [(eye - t[m]) - _dot3(_split2(a_list[m]), ts[m]) for m in range(nm)]
    return [t[m] + _dot(ts[m][0], res[m].astype(BF16)) for m in range(nm)]


def _gdn_body(cin_ref, z_ref, gc_ref, *rest, nh, dk, dv, nseq):
    gr_refs = rest[:nseq]
    cw_ref, ng_ref, o_ref, s_out_ref, s_ref, win_ref, m16_ref = rest[nseq:]
    j = pl.program_id(1)
    T = cin_ref.shape[1]
    C = GDN_CHUNK
    nc = T // C

    ri = lax.broadcasted_iota(jnp.int32, (T, T), 0)
    ci = lax.broadcasted_iota(jnp.int32, (T, T), 1)
    csh = C.bit_length() - 1
    same = (ri >> csh) == (ci >> csh)
    tril = same & (ri >= ci)
    strict = same & (ri > ci)

    @pl.when((pl.program_id(0) == 0) & (j == 0))
    def _():
        m16_ref[0] = tril.astype(BF16)
        m16_ref[1] = (same & (ri <= ci)).astype(BF16)

    @pl.when(j == 0)
    def _():
        s_ref[...] = jnp.zeros_like(s_ref)
        win_ref[:, 0:8, :] = jnp.zeros((nseq, 8, win_ref.shape[2]), F32)

    w = cw_ref[...]
    kw = w.shape[0]
    rowi = lax.broadcasted_iota(jnp.int32, (T, 1), 0)
    units = [(s, h) for s in range(nseq) for h in range(nh)]
    nu = len(units)
    convs, gcols, grows = [], [], []
    for s in range(nseq):
        win_ref[s, 8:8 + T, :] = cin_ref[s]
        conv = win_ref[s, 8 - kw + 1:8 - kw + 1 + T, :] * w[0:1]
        for i in range(1, kw):
            conv = conv + win_ref[s, 8 - kw + 1 + i:8 - kw + 1 + i + T, :] * w[i:i + 1]
        win_ref[s, 0:8, :] = win_ref[s, T:T + 8, :]
        convs.append(jax.nn.silu(conv))
        gcols.append(sum(_dot(m16_ref[0], part) for part in _split3(gc_ref[s])))
        grows.append(sum(_dot(part, m16_ref[1]) for part in _split3(gr_refs[s][...])))

    q, k, beta, gcc, gcr, gam, kb, k16 = [], [], [], [], [], [], [], []
    for u, (s, h) in enumerate(units):
        qh = convs[s][:, h * dk:(h + 1) * dk]
        kh = convs[s][:, nh * dk + h * dk:nh * dk + (h + 1) * dk]
        q.append(qh * lax.rsqrt(jnp.sum(qh * qh, axis=-1, keepdims=True) + EPS) * dk ** -0.5)
        k.append(kh * lax.rsqrt(jnp.sum(kh * kh, axis=-1, keepdims=True) + EPS))
        beta.append(gc_ref[s, :, 2 * nh + h:2 * nh + h + 1])
        gcc.append(gcols[s][:, nh + h:nh + h + 1])
        gcr.append(grows[s][nh + h:nh + h + 1, :])
        gam.append(jnp.exp(jnp.where(tril, gcc[u] - gcr[u], NEG)))
        kb.append(k[u] * beta[u])
        k16.append(k[u].astype(BF16))
    a = [jnp.where(strict, _dot_nt(kb[u].astype(BF16), k16[u]) * gam[u], 0.0) for u in range(nu)]
    tinv = _unit_lower_inverses(a, T)
    x = []
    for u, (s, h) in enumerate(units):
        v = convs[s][:, 2 * nh * dk + h * dv:2 * nh * dk + (h + 1) * dv]
        rhs = jnp.concatenate([v * beta[u], kb[u] * jnp.exp(gcc[u])], axis=-1)
        x.append(_dot3(_split2(tinv[u]), _split2(rhs)))
    attn, qg, kd, egl = [], [], [], []
    for u in range(nu):
        attn.append((_dot_nt(q[u].astype(BF16), k16[u]) * gam[u]).astype(BF16))
        qg.append((q[u] * jnp.exp(gcc[u])).astype(BF16))
        gl = gcr[u][:, C - 1:C]
        for cidx in range(1, nc):
            gl = jnp.where(rowi >= cidx * C, gcr[u][:, (cidx + 1) * C - 1:(cidx + 1) * C], gl)
        kd.append((k[u] * jnp.exp(gl - gcc[u])).astype(BF16))
        egl.append(jnp.exp(gl))
    S = [s_ref[s, h] for s, h in units]
    outs = [[] for _ in units]
    for cidx in range(nc):
        sl = slice(cidx * C, (cidx + 1) * C)
        S16 = [st.astype(BF16) for st in S]
        vn16 = [(x[u][sl, :dv] - _dot(x[u][sl, dv:].astype(BF16), S16[u])).astype(BF16) for u in range(nu)]
        for u in range(nu):
            outs[u].append(_dot(qg[u][sl], S16[u]) + _dot(attn[u][sl, cidx * C:(cidx + 1) * C], vn16[u]))
        S = [S[u] * egl[u][cidx * C:cidx * C + 1] + _dot_tn(kd[u][sl], vn16[u]) for u in range(nu)]
    for u, (s, h) in enumerate(units):
        s_ref[s, h] = S[u]
        o = _rms_rows(jnp.concatenate(outs[u], axis=0), ng_ref[...])
        o_ref[s, :, h * dv:(h + 1) * dv] = o * jax.nn.silu(z_ref[s, :, h * dv:(h + 1) * dv])

    @pl.when(j == pl.num_programs(1) - 1)
    def _():
        s_out_ref[...] = s_ref[...]


def _gdn_prompt(conv_in, z, gates_col, gates_row, conv_w, norm_g, B, L, nh, dk, dv):
    M, CD = conv_in.shape
    T = 2 * GDN_CHUNK
    nb = L // T
    nseq = 2 if B % 2 == 0 else 1
    kernel = functools.partial(_gdn_body, nh=nh, dk=dk, dv=dv, nseq=nseq)

    def row_map(s):
        return lambda p, j: (0, (p * nseq + s) * nb + j)

    per_seq = lambda p, j: (p, j, 0)
    go, S = pl.pallas_call(
        kernel,
        out_shape=[jax.ShapeDtypeStruct((B, L, nh * dv), F32), jax.ShapeDtypeStruct((B, nh, dk, dv), F32)],
        grid=(B // nseq, nb),
        in_specs=[
            pl.BlockSpec((nseq, T, CD), per_seq),
            pl.BlockSpec((nseq, T, nh * dv), per_seq),
            pl.BlockSpec((nseq, T, LANES), per_seq),
        ] + [pl.BlockSpec((16, T), row_map(s)) for s in range(nseq)] + [
            pl.BlockSpec(conv_w.shape, lambda p, j: (0, 0)),
            pl.BlockSpec((1, dv), lambda p, j: (0, 0)),
        ],
        out_specs=[
            pl.BlockSpec((nseq, T, nh * dv), per_seq),
            pl.BlockSpec((nseq, nh, dk, dv), lambda p, j: (p, 0, 0, 0)),
        ],
        scratch_shapes=[pltpu.VMEM((nseq, nh, dk, dv), F32), pltpu.VMEM((nseq, 8 + T, CD), F32),
                        pltpu.VMEM((2, T, T), BF16)],
        compiler_params=_cparams(2),
        name="gdn_prompt",
    )(conv_in.reshape(B, L, CD), z.reshape(B, L, nh * dv), gates_col.reshape(B, L, LANES),
      *([gates_row] * nseq), conv_w, norm_g.reshape(1, dv))
    return go.reshape(M, nh * dv), S


def _rope(x, cos2, sin2, nh, dk):
    outs = []
    for h in range(nh):
        xh = x[:, h * dk:(h + 1) * dk]
        outs.append(xh * cos2 + pltpu.roll(xh, dk // 2, 1) * sin2)
    return outs


def _rope_tables(pos, freqs):
    ang = pos * freqs
    cos, sin = jnp.cos(ang), jnp.sin(ang)
    return jnp.concatenate([cos, cos], axis=-1), jnp.concatenate([-sin, sin], axis=-1)


def _head_layernorm(o):
    mu = jnp.mean(o, axis=-1, keepdims=True)
    oc = o - mu
    return oc * lax.rsqrt(jnp.mean(oc * oc, axis=-1, keepdims=True) + EPS)


def _log_gamma(h):
    return math.log(1.0 - 2.0 ** (-5.0 - h))


def _ret_body(q_ref, k_ref, v_ref, gate_ref, fr_ref, gn_ref, o_ref, s_out_ref, s_ref, cos_ref, sin_ref, dmat_ref,
              *, nh, dk, dv):
    b = pl.program_id(0)
    j = pl.program_id(1)
    C = q_ref.shape[0]
    rowi = lax.broadcasted_iota(jnp.int32, (C, 1), 0)
    rowf = rowi.astype(F32)

    @pl.when((b == 0) & (j == 0))
    def _():
        def fill(c, carry):
            c0 = pl.multiple_of(c * C, C)
            cos2, sin2 = _rope_tables((c0 + rowi).astype(F32), fr_ref[...])
            cos_ref[pl.ds(c0, C), :] = cos2
            sin_ref[pl.ds(c0, C), :] = sin2
            return carry

        lax.fori_loop(0, pl.num_programs(1), fill, 0)
        ri = lax.broadcasted_iota(jnp.int32, (C, C), 0)
        ci = lax.broadcasted_iota(jnp.int32, (C, C), 1)
        diff = (ri - ci).astype(F32)
        for h in range(nh):
            dmat_ref[h] = jnp.exp(jnp.where(ri >= ci, diff * _log_gamma(h), NEG))

    @pl.when(j == 0)
    def _():
        s_ref[...] = jnp.zeros_like(s_ref)

    j0 = pl.multiple_of(j * C, C)
    cos2 = cos_ref[pl.ds(j0, C), :]
    sin2 = sin_ref[pl.ds(j0, C), :]
    qs = _rope(q_ref[...], cos2, sin2, nh, dk)
    ks = _rope(k_ref[...], cos2, sin2, nh, dk)
    for h in range(nh):
        lg = _log_gamma(h)
        q = qs[h]
        k = ks[h] * dk ** -0.5
        v = v_ref[:, h * dv:(h + 1) * dv].astype(BF16)
        q16 = q.astype(BF16)
        inner = _dot((_dot_nt(q16, k.astype(BF16)) * dmat_ref[h]).astype(BF16), v)
        S = s_ref[h]
        cross = _dot((q * jnp.exp(lg * (rowf + 1.0))).astype(BF16), S.astype(BF16))
        s_ref[h] = S * math.exp(lg * C) + _dot_tn((k * jnp.exp(lg * (C - 1.0 - rowf))).astype(BF16), v)
        o = _head_layernorm(inner + cross) * gn_ref[:, h * dv:(h + 1) * dv]
        o_ref[:, h * dv:(h + 1) * dv] = jax.nn.silu(gate_ref[:, h * dv:(h + 1) * dv]) * o

    @pl.when(j == pl.num_programs(1) - 1)
    def _():
        s_out_ref[...] = s_ref[...]


def _ret_prompt(q, k, v, gate, freqs, gnorm, B, L, nh, dk, dv):
    M = q.shape[0]
    C = 128
    nb = L // C
    kernel = functools.partial(_ret_body, nh=nh, dk=dk, dv=dv)
    return pl.pallas_call(
        kernel,
        out_shape=[jax.ShapeDtypeStruct((M, nh * dv), F32), jax.ShapeDtypeStruct((B, nh, dk, dv), F32)],
        grid=(B, nb),
        in_specs=[
            pl.BlockSpec((C, nh * dk), lambda b, j: (b * nb + j, 0)),
            pl.BlockSpec((C, nh * dk), lambda b, j: (b * nb + j, 0)),
            pl.BlockSpec((C, nh * dv), lambda b, j: (b * nb + j, 0)),
            pl.BlockSpec((C, nh * dv), lambda b, j: (b * nb + j, 0)),
            pl.BlockSpec((1, dk // 2), lambda b, j: (0, 0)),
            pl.BlockSpec((1, nh * dv), lambda b, j: (0, 0)),
        ],
        out_specs=[
            pl.BlockSpec((C, nh * dv), lambda b, j: (b * nb + j, 0)),
            pl.BlockSpec((None, nh, dk, dv), lambda b, j: (b, 0, 0, 0)),
        ],
        scratch_shapes=[pltpu.VMEM((nh, dk, dv), F32), pltpu.VMEM((L, dk), F32), pltpu.VMEM((L, dk), F32),
                        pltpu.VMEM((nh, C, C), F32)],
        compiler_params=_cparams(2),
        name="ret_prompt",
    )(q, k, v, gate, freqs, gnorm.reshape(1, nh * dv))


def _page_sums_body(x_ref, o_ref, *, nh):
    n = x_ref.shape[1]
    fi = lax.broadcasted_iota(jnp.int32, (n, n), 0)
    fo = lax.broadcasted_iota(jnp.int32, (n, n), 1)
    same_head = (fi & (nh - 1)) == (fo & (nh - 1))
    parts = _split3(x_ref[...])
    o_ref[:, :n] = sum(_dot(part, (same_head & (fi > fo)).astype(BF16)) for part in parts)
    o_ref[:, n:] = sum(_dot(part, same_head.astype(BF16)) for part in parts)


def _page_sums(lf_pages, nh):
    P, n = lf_pages.shape
    tp = 512 if P % 512 == 0 else P
    return pl.pallas_call(
        functools.partial(_page_sums_body, nh=nh),
        out_shape=jax.ShapeDtypeStruct((P, 2 * n), F32),
        grid=(P // tp,),
        in_specs=[pl.BlockSpec((tp, n), lambda p: (p, 0))],
        out_specs=pl.BlockSpec((tp, 2 * n), lambda p: (p, 0)),
        compiler_params=_cparams(1),
        name="page_sums",
    )(lf_pages)


def _paged_body(pt_ref, *refs, G, nh, scale):
    k_refs = refs[:G]
    v_refs = refs[G:2 * G]
    w_refs = refs[2 * G:3 * G]
    q_ref, kn_ref, vn_ref, lfn_ref, o_ref, m_ref, l_ref, acc_ref, carry_ref = refs[3 * G:]
    b_idx = pl.program_id(0)
    s_idx = pl.program_id(1)
    ns = pl.num_programs(1)
    n = k_refs[0].shape[0]
    dh = q_ref.shape[1]
    sub = w_refs[0].shape[0]

    @pl.when(s_idx == 0)
    def _():
        m_ref[...] = jnp.full_like(m_ref, NEG)
        l_ref[...] = jnp.zeros_like(l_ref)
        acc_ref[...] = jnp.zeros_like(acc_ref)
        carry_ref[...] = jnp.zeros_like(carry_ref)

    q = q_ref[...]
    q8 = jnp.concatenate([q, jnp.zeros((8 - nh, dh), F32)], axis=0).astype(BF16)
    lfn = jnp.concatenate([lfn_ref[...], jnp.zeros((8 - nh, 1), F32)], axis=0)
    hrow = lax.broadcasted_iota(jnp.int32, (8, n), 0)
    hcol = lax.broadcasted_iota(jnp.int32, (8, n), 1) & (nh - 1)
    match = hrow == hcol
    carry = carry_ref[...]
    s_parts = [None] * G
    for i in reversed(range(G)):
        page_id = pt_ref[b_idx, (ns - 1 - s_idx) * G + i]
        wt = w_refs[i][pl.ds(lax.rem(page_id, sub), 1), :]
        bias = wt[:, :n] + carry
        carry = carry + wt[:, n:]
        s = _dot_nt(q8, k_refs[i][...].astype(BF16)) * scale + bias + lfn
        s_parts[i] = jnp.where(match, s, NEG)
    carry_ref[...] = carry
    m_run, l_run, acc_run = m_ref[...], l_ref[...], acc_ref[...]
    half = max(G // 2, 1)
    for grp in (range(half, G), range(0, half)) if G > 1 else (range(G),):
        sg = [s_parts[i] for i in grp]
        m_new = jnp.maximum(m_run, jnp.max(functools.reduce(jnp.maximum, sg), axis=-1, keepdims=True))
        alpha = jnp.exp(m_run - m_new)
        pg = [jnp.exp(s - m_new) for s in sg]
        l_run = alpha * l_run + jnp.sum(functools.reduce(jnp.add, pg), axis=-1, keepdims=True)
        pv = functools.reduce(jnp.add, [_dot(p.astype(BF16), v_refs[i][...].astype(BF16)) for p, i in zip(pg, grp)])
        acc_run = alpha * acc_run + pv
        m_run = m_new
    m_ref[...], l_ref[...], acc_ref[...] = m_run, l_run, acc_run

    @pl.when(s_idx == pl.num_programs(1) - 1)
    def _():
        qf = q.astype(BF16).astype(F32)
        kn = kn_ref[...].astype(BF16).astype(F32)
        s_new = jnp.sum(qf * kn, axis=-1, keepdims=True) * scale
        m_prev = m_ref[0:nh, :]
        m_new = jnp.maximum(m_prev, s_new)
        alpha = jnp.exp(m_prev - m_new)
        p_new = jnp.exp(s_new - m_new)
        l_fin = alpha * l_ref[0:nh, :] + p_new
        vn = vn_ref[...].astype(BF16).astype(F32)
        num = alpha * acc_ref[0:nh, :] + p_new.astype(BF16).astype(F32) * vn
        o_ref[...] = num / l_fin


def _fox_sample(q, k_new, v_new, lf_new, k_pages, v_pages, page_sums, page_table, nh):
    Bn, _, dh = q.shape
    n_pages = page_table.shape[1]
    n_pool, n, _ = k_pages.shape
    G = 16 if n_pages % 16 == 0 else (8 if n_pages % 8 == 0 else 1)
    ns = n_pages // G
    sub = 8
    assert n_pool % sub == 0
    page_sums = page_sums.reshape(n_pool // sub, sub, 2 * n)

    def page_map(i):
        return lambda b, s, pt: (pt[b, (ns - 1 - s) * G + i], 0, 0)

    def sums_map(i):
        return lambda b, s, pt: (lax.div(pt[b, (ns - 1 - s) * G + i], sub), 0, 0)

    per_b = lambda b, s, pt: (b, 0, 0)
    grid_spec = pltpu.PrefetchScalarGridSpec(
        num_scalar_prefetch=1,
        grid=(Bn, ns),
        in_specs=[pl.BlockSpec((None, n, dh), page_map(i)) for i in range(G)]
        + [pl.BlockSpec((None, n, dh), page_map(i)) for i in range(G)]
        + [pl.BlockSpec((None, sub, 2 * n), sums_map(i)) for i in range(G)]
        + [pl.BlockSpec((None, nh, dh), per_b), pl.BlockSpec((None, nh, dh), per_b),
           pl.BlockSpec((None, nh, dh), per_b), pl.BlockSpec((None, nh, 1), per_b)],
        out_specs=pl.BlockSpec((None, nh, dh), per_b),
        scratch_shapes=[pltpu.VMEM((8, 1), F32), pltpu.VMEM((8, 1), F32), pltpu.VMEM((8, dh), F32),
                        pltpu.VMEM((1, n), F32)],
    )
    return pl.pallas_call(
        functools.partial(_paged_body, G=G, nh=nh, scale=dh ** -0.5),
        out_shape=jax.ShapeDtypeStruct((Bn, nh, dh), F32),
        grid_spec=grid_spec,
        compiler_params=_cparams(2),
        name="fox_sample",
    )(page_table, *([k_pages] * G), *([v_pages] * G), *([page_sums] * G), q, k_new, v_new, lf_new)


def _pad8(row):
    return jnp.concatenate([row, jnp.zeros((8 - row.shape[0], row.shape[1]), row.dtype)], axis=0)


def _gdn_sample_body(cin_ref, prev_ref, z_ref, gt_ref, cw_ref, ng_ref, s0_ref, o_ref, cnew_ref, s_out_ref, *, nh, dk, dv):
    u = cin_ref[...]
    prev = prev_ref[...]
    w = cw_ref[...]
    kw = w.shape[0]
    conv = u * w[kw - 1:kw]
    for i in range(kw - 1):
        conv = conv + prev[i:i + 1] * w[i:i + 1]
    cnew_ref[0:kw - 2, :] = prev[1:kw - 1]
    cnew_ref[kw - 2:kw - 1, :] = u
    conv = jax.nn.silu(conv)
    gt = gt_ref[...]
    for h in range(nh):
        q = conv[:, h * dk:(h + 1) * dk]
        k = conv[:, nh * dk + h * dk:nh * dk + (h + 1) * dk]
        v = conv[:, 2 * nh * dk + h * dv:2 * nh * dk + (h + 1) * dv]
        q = q * lax.rsqrt(jnp.sum(q * q, axis=-1, keepdims=True) + EPS) * dk ** -0.5
        k = k * lax.rsqrt(jnp.sum(k * k, axis=-1, keepdims=True) + EPS)
        g = gt[:, nh + h:nh + h + 1]
        beta = gt[:, 2 * nh + h:2 * nh + h + 1]
        eg = jnp.exp(g)
        S = s0_ref[h]
        lhs = _pad8(jnp.concatenate([k * beta * eg, q * eg], axis=0)).astype(BF16)
        r = _dot(lhs, S.astype(BF16))
        v_new = v * beta - r[0:1]
        qk = jnp.sum(q.astype(BF16).astype(F32) * k.astype(BF16).astype(F32), axis=-1, keepdims=True)
        vn16 = v_new.astype(BF16)
        o = r[1:2] + qk.astype(BF16).astype(F32) * vn16.astype(F32)
        s_out_ref[h] = S * eg + _dot_tn(_pad8(k).astype(BF16), _pad8(vn16))
        o = _rms_rows(o, ng_ref[...])
        o_ref[:, h * dv:(h + 1) * dv] = o * jax.nn.silu(z_ref[:, h * dv:(h + 1) * dv])


def _gdn_sample(conv_in, conv_prev, z, gates_col, conv_w, norm_g, S0, nh, dk, dv):
    Bn, _, CD = conv_in.shape
    kw = conv_w.shape[0]
    per_b3 = lambda b: (b, 0, 0)
    kernel = functools.partial(_gdn_sample_body, nh=nh, dk=dk, dv=dv)
    return pl.pallas_call(
        kernel,
        out_shape=[jax.ShapeDtypeStruct((Bn, 1, nh * dv), F32), jax.ShapeDtypeStruct((Bn, kw - 1, CD), F32),
                   jax.ShapeDtypeStruct((Bn, nh, dk, dv), F32)],
        grid=(Bn,),
        in_specs=[
            pl.BlockSpec((None, 1, CD), per_b3),
            pl.BlockSpec((None, kw - 1, CD), per_b3),
            pl.BlockSpec((None, 1, nh * dv), per_b3),
            pl.BlockSpec((None, 1, LANES), per_b3),
            pl.BlockSpec(conv_w.shape, lambda b: (0, 0)),
            pl.BlockSpec((1, dv), lambda b: (0, 0)),
            pl.BlockSpec((None, nh, dk, dv), lambda b: (b, 0, 0, 0)),
        ],
        out_specs=[
            pl.BlockSpec((None, 1, nh * dv), per_b3),
            pl.BlockSpec((None, kw - 1, CD), per_b3),
            pl.BlockSpec((None, nh, dk, dv), lambda b: (b, 0, 0, 0)),
        ],
        compiler_params=_cparams(1),
        name="gdn_sample",
    )(conv_in, conv_prev, z, gates_col, conv_w, norm_g.reshape(1, dv), S0)


def _ret_sample_body(q_ref, k_ref, v_ref, gate_ref, fr_ref, gn_ref, s0_ref, o_ref, s_out_ref, *, nh, dk, dv, pos):
    cos2, sin2 = _rope_tables(jnp.full((1, 1), pos, F32), fr_ref[...])
    qs = _rope(q_ref[...], cos2, sin2, nh, dk)
    ks = _rope(k_ref[...], cos2, sin2, nh, dk)
    for h in range(nh):
        lg = _log_gamma(h)
        q = qs[h]
        k = ks[h] * dk ** -0.5
        v16 = v_ref[:, h * dv:(h + 1) * dv].astype(BF16)
        S = s0_ref[h]
        qk = jnp.sum(q.astype(BF16).astype(F32) * k.astype(BF16).astype(F32), axis=-1, keepdims=True)
        inner = qk.astype(BF16).astype(F32) * v16.astype(F32)
        cross = _dot(_pad8(q * math.exp(lg)).astype(BF16), S.astype(BF16))[0:1]
        s_out_ref[h] = S * math.exp(lg) + _dot_tn(_pad8(k).astype(BF16), _pad8(v16))
        o = _head_layernorm(inner + cross) * gn_ref[:, h * dv:(h + 1) * dv]
        o_ref[:, h * dv:(h + 1) * dv] = jax.nn.silu(gate_ref[:, h * dv:(h + 1) * dv]) * o


def _ret_sample(q, k, v, gate, freqs, gnorm, S0, pos, nh, dk, dv):
    Bn = q.shape[0]
    per_b3 = lambda b: (b, 0, 0)
    kernel = functools.partial(_ret_sample_body, nh=nh, dk=dk, dv=dv, pos=float(pos))
    return pl.pallas_call(
        kernel,
        out_shape=[jax.ShapeDtypeStruct((Bn, 1, nh * dv), F32), jax.ShapeDtypeStruct((Bn, nh, dk, dv), F32)],
        grid=(Bn,),
        in_specs=[
            pl.BlockSpec((None, 1, nh * dk), per_b3),
            pl.BlockSpec((None, 1, nh * dk), per_b3),
            pl.BlockSpec((None, 1, nh * dv), per_b3),
            pl.BlockSpec((None, 1, nh * dv), per_b3),
            pl.BlockSpec((1, dk // 2), lambda b: (0, 0)),
            pl.BlockSpec((1, nh * dv), lambda b: (0, 0)),
            pl.BlockSpec((None, nh, dk, dv), lambda b: (b, 0, 0, 0)),
        ],
        out_specs=[
            pl.BlockSpec((None, 1, nh * dv), per_b3),
            pl.BlockSpec((None, nh, dk, dv), lambda b: (b, 0, 0, 0)),
        ],
        compiler_params=_cparams(1),
        name="ret_sample",
    )(q, k, v, gate, freqs, gnorm.reshape(1, nh * dv), S0)


def kernel(x_prompt, x_sample, cache_fox_k, cache_fox_v, cache_fox_logf, page_table, state_gdn_conv, state_gdn_S, state_ret_S, norm_g, final_norm_g, ffn_w_gu, ffn_w_down, ab_w_in, ab_w_out, fox_b_f, gdn_conv_w, gdn_A_log, gdn_dt_bias, gdn_norm_g, c_w_in, c_w_out, ret_norm_g):
    B, L, D = x_prompt.shape
    Bn, Ls, _ = x_sample.shape
    assert Ls == 1, "the sample group decodes one token per sequence"
    depth = norm_g.shape[0]
    _, n_pool, page, fh, fd = cache_fox_k.shape
    fw = fh * fd
    _, _, gh, gdk, gdv = state_gdn_S.shape
    conv_dim = gdn_conv_w.shape[2]
    kw = gdn_conv_w.shape[1]
    _, _, rh, rdk, rdv = state_ret_S.shape
    n_pages = page_table.shape[1]
    assert fh == gh and 3 * fh <= 16 and fh & (fh - 1) == 0

    xp = x_prompt.reshape(B * L, D)
    xs = x_sample.reshape(Bn, D)
    w_gu16 = ffn_w_gu.astype(BF16)
    w_down16 = ffn_w_down.astype(BF16)
    freqs = (ROPE_BASE ** (-jnp.arange(rdk // 2, dtype=F32) / (rdk // 2))).reshape(1, rdk // 2)

    fkp, fvp, flp, fks, fvs, fls = [], [], [], [], [], []
    gcp, gsp, gcs, gss = [], [], [], []
    rsp, rss = [], []
    for li in range(depth):
        xp = _ffn(xp, norm_g[li, 0], w_gu16, w_down16, li, 0)
        xs = _ffn(xs, norm_g[li, 0], w_gu16, w_down16, li, 0)
        j = li // 2
        if li % 2 == 0:
            w_in = ab_w_in[j]
            o0 = 3 * fw
            zw = gh * gdv
            w_cat = jnp.concatenate(
                [w_in[:, :o0], w_in[:, o0 + fh:o0 + fh + conv_dim + zw], w_in[:, o0:o0 + fh],
                 w_in[:, o0 + fh + conv_dim + zw:], jnp.zeros((D, LANES - 3 * fh), F32)], axis=1).astype(BF16)
            widths = (fw, fw, fw, conv_dim, zw, LANES)
            w_out16 = ab_w_out[j].astype(BF16)

            q, k, v, cin, z, small, k3, v3 = _rms_proj(xp, norm_g[li, 1], w_cat, widths, per_head=(1, 2), dh=fd)
            gates = _gates(small, fox_b_f[j], gdn_dt_bias[j], gdn_A_log[j])
            gates_row = gates[:, :16].T
            c_rows = _cumsum_rows(gates_row, L)[:fh].reshape(fh, B, L).transpose(1, 0, 2).reshape(B * fh, 1, L)
            fo = _fox_prompt(q, k, v, c_rows, B, L, fh)
            go, S_p = _gdn_prompt(cin, z, gates, gates_row, gdn_conv_w[j], gdn_norm_g[j], B, L, gh, gdk, gdv)
            xp = _out_proj(xp, [fo, go], w_out16)
            fkp.append(k3.reshape(B, L, fh, fd))
            fvp.append(v3.reshape(B, L, fh, fd))
            flp.append(gates[:, :fh].reshape(B, L, fh))
            gcp.append(cin.reshape(B, L, conv_dim)[:, L - (kw - 1):])
            gsp.append(S_p)

            q, k, v, cin, z, small = _rms_proj(xs, norm_g[li, 1], w_cat, widths)
            gates = _gates(small, fox_b_f[j], gdn_dt_bias[j], gdn_A_log[j])
            sums = _page_sums(cache_fox_logf[j].reshape(n_pool, page * fh), fh)
            fo = _fox_sample(q.reshape(Bn, fh, fd), k.reshape(Bn, fh, fd), v.reshape(Bn, fh, fd),
                             gates[:, :fh].reshape(Bn, fh, 1),
                             cache_fox_k[j].reshape(n_pool, page * fh, fd), cache_fox_v[j].reshape(n_pool, page * fh, fd),
                             sums, page_table, fh)
            go, conv_s, S_s = _gdn_sample(cin.reshape(Bn, 1, conv_dim), state_gdn_conv[j], z.reshape(Bn, 1, gh * gdv),
                                          gates.reshape(Bn, 1, LANES), gdn_conv_w[j], gdn_norm_g[j], state_gdn_S[j],
                                          gh, gdk, gdv)
            xs = _out_proj(xs, [fo.reshape(Bn, fw), go.reshape(Bn, gh * gdv)], w_out16)
            fks.append(k.reshape(Bn, 1, fh, fd))
            fvs.append(v.reshape(Bn, 1, fh, fd))
            fls.append(gates[:, :fh].reshape(Bn, 1, fh))
            gcs.append(conv_s)
            gss.append(S_s)
        else:
            w_in16 = c_w_in[j].astype(BF16)
            qk_w, v_w = rh * rdk, rh * rdv
            widths = (qk_w, qk_w, v_w, v_w)
            w_out16 = c_w_out[j].astype(BF16)

            q, k, v, gate = _rms_proj(xp, norm_g[li, 1], w_in16, widths)
            y, R_p = _ret_prompt(q, k, v, gate, freqs, ret_norm_g[j], B, L, rh, rdk, rdv)
            xp = _out_proj(xp, [y], w_out16)
            rsp.append(R_p)

            q, k, v, gate = _rms_proj(xs, norm_g[li, 1], w_in16, widths)
            y, R_s = _ret_sample(q.reshape(Bn, 1, qk_w), k.reshape(Bn, 1, qk_w), v.reshape(Bn, 1, v_w),
                                 gate.reshape(Bn, 1, v_w), freqs, ret_norm_g[j], state_ret_S[j],
                                 n_pages * page, rh, rdk, rdv)
            xs = _out_proj(xs, [y.reshape(Bn, v_w)], w_out16)
            rss.append(R_s)
        final_g = final_norm_g if li == depth - 1 else None
        xp = _ffn(xp, norm_g[li, 2], w_gu16, w_down16, li, 1, final_g)
        xs = _ffn(xs, norm_g[li, 2], w_gu16, w_down16, li, 1, final_g)
    y_prompt = xp.reshape(B, L, D)
    y_sample = xs.reshape(Bn, 1, D)
    return (y_prompt, y_sample,
            jnp.stack(fkp), jnp.stack(fvp), jnp.stack(flp),
            jnp.stack(fks), jnp.stack(fvs), jnp.stack(fls),
            jnp.stack(gcp), jnp.stack(gsp), jnp.stack(gcs), jnp.stack(gss),
            jnp.stack(rsp), jnp.stack(rss))
```

```python
import functools
import math

import jax
import jax.numpy as jnp
from jax import lax
from jax.experimental import pallas as pl
from jax.experimental.pallas import tpu as pltpu

F32 = jnp.float32
BF16 = jnp.bfloat16
EPS = 1e-6
ROPE_BASE = 10000.0
NEG = -1e30
LANES = 128
GDN_CHUNK = 64
VMEM_LIMIT = 56 * 1024 * 1024
HI = lax.Precision.HIGHEST


def _cparams(n_axes):
    return pltpu.CompilerParams(dimension_semantics=("arbitrary",) * n_axes,
                                vmem_limit_bytes=VMEM_LIMIT)


def _dot(a, b, precision=None):
    return jnp.dot(a, b, preferred_element_type=F32, precision=precision)


def _dot_nt(a, b, precision=None):
    return lax.dot_general(a, b, (((1,), (1,)), ((), ())), preferred_element_type=F32, precision=precision)


def _dot_tn(a, b, precision=None):
    return lax.dot_general(a, b, (((0,), (0,)), ((), ())), preferred_element_type=F32, precision=precision)


def _rms_rows(x, g):
    return x * lax.rsqrt(jnp.mean(x * x, axis=-1, keepdims=True) + EPS) * g


MXU_DEPTH = 256


def _ffn_body(*refs, F, chunk, final_norm, n_pre):
    x_ref, g_ref, wgu_ref, wd_ref = refs[:4]
    a_refs = refs[4:4 + n_pre]
    rest = refs[4 + n_pre:]
    x = x_ref[...]
    if n_pre:
        wo_ref, rest = rest[0], rest[1:]
        row = 0
        for a_ref in a_refs:
            n = a_ref.shape[1]
            x = x + _dot(a_ref[...].astype(BF16), wo_ref[row:row + n, :])
            row += n
    if final_norm:
        gf_ref, rest = rest[0], rest[1:]
    o_ref = rest[0]
    xn = _rms_rows(x, g_ref[...]).astype(BF16)
    acc = None
    for c0 in range(0, F, chunk):
        c1 = min(F, c0 + chunk)
        a = _dot(xn, wgu_ref[:, c0:c1])
        b = _dot(xn, wgu_ref[:, F + c0:F + c1])
        h = (jax.nn.silu(a) * b).astype(BF16)
        d = _dot(h, wd_ref[c0:c1, :])
        acc = d if acc is None else acc + d
    y = x + 0.5 * acc
    o_ref[...] = _rms_rows(y, gf_ref[...]) if final_norm else y


def _ffn(x, g, w_gu, w_down, li, j, final_g=None, pre=None):
    M, D = x.shape
    F = w_down.shape[2]
    tm = min(M, 512)
    final_norm = final_g is not None
    const = lambda m: (0, 0)
    in_specs = [pl.BlockSpec((tm, D), lambda m: (m, 0)), pl.BlockSpec((1, D), const),
                pl.BlockSpec((None, None, D, 2 * F), lambda m: (li, j, 0, 0)),
                pl.BlockSpec((None, None, F, D), lambda m: (li, j, 0, 0))]
    args = [x, g.reshape(1, D), w_gu, w_down]
    acts, w_out = pre if pre is not None else ((), None)
    if acts:
        assert sum(a.shape[1] for a in acts) == w_out.shape[0]
        in_specs += [pl.BlockSpec((tm, a.shape[1]), lambda m: (m, 0)) for a in acts]
        in_specs.append(pl.BlockSpec(w_out.shape, const))
        args += [*acts, w_out]
    if final_norm:
        in_specs.append(pl.BlockSpec((1, D), const))
        args.append(final_g.reshape(1, D))
    return pl.pallas_call(
        functools.partial(_ffn_body, F=F, chunk=2 * MXU_DEPTH, final_norm=final_norm, n_pre=len(acts)),
        out_shape=jax.ShapeDtypeStruct((M, D), F32),
        grid=(M // tm,),
        in_specs=in_specs,
        out_specs=pl.BlockSpec((tm, D), lambda m: (m, 0)),
        compiler_params=_cparams(1),
        name="ffn",
    )(*args)


def _proj_body(x_ref, g_ref, w_ref, *o_refs, widths, per_head, chunk):
    xn = _rms_rows(x_ref[...], g_ref[...]).astype(BF16)
    col = 0
    for idx, n in enumerate(widths):
        for s0 in range(0, n, chunk):
            s1 = min(n, s0 + chunk)
            val = _dot(xn, w_ref[:, col + s0:col + s1])
            o_refs[idx][:, s0:s1] = val
            if idx in per_head:
                o3 = o_refs[len(widths) + per_head.index(idx)]
                dh = o3.shape[2]
                for hh in range((s1 - s0) // dh):
                    o3[:, s0 // dh + hh, :] = val[:, hh * dh:(hh + 1) * dh]
        col += n


def _rms_proj(x, g, w, widths, per_head=(), dh=LANES):
    M, D = x.shape
    tm = min(M, 256)
    assert sum(widths) == w.shape[1] and all(n % LANES == 0 for n in widths)
    out_shape = [jax.ShapeDtypeStruct((M, n), F32) for n in widths]
    out_specs = [pl.BlockSpec((tm, n), lambda m: (m, 0)) for n in widths]
    for idx in per_head:
        out_shape.append(jax.ShapeDtypeStruct((M, widths[idx] // dh, dh), F32))
        out_specs.append(pl.BlockSpec((tm, widths[idx] // dh, dh), lambda m: (m, 0, 0)))
    return pl.pallas_call(
        functools.partial(_proj_body, widths=tuple(widths), per_head=tuple(per_head), chunk=512),
        out_shape=out_shape,
        grid=(M // tm,),
        in_specs=[pl.BlockSpec((tm, D), lambda m: (m, 0)), pl.BlockSpec((1, D), lambda m: (0, 0)),
                  pl.BlockSpec(w.shape, lambda m: (0, 0))],
        out_specs=out_specs,
        compiler_params=_cparams(1),
        name="rms_proj",
    )(x, g.reshape(1, D), w)


def _cast_body(x_ref, o_ref):
    o_ref[...] = x_ref[...].astype(o_ref.dtype)


def _to_bf16(w):
    cols = w.shape[-1]
    rows = w.size // cols
    tr = 512 if rows % 512 == 0 else rows
    out = pl.pallas_call(
        _cast_body,
        out_shape=jax.ShapeDtypeStruct((rows, cols), BF16),
        grid=(rows // tr,),
        in_specs=[pl.BlockSpec((tr, cols), lambda r: (r, 0))],
        out_specs=pl.BlockSpec((tr, cols), lambda r: (r, 0)),
        compiler_params=_cparams(1),
        name="to_bf16",
    )(w.reshape(rows, cols))
    return out.reshape(w.shape)


def _gates_body(s_ref, p_ref, o_ref, *, nh):
    s = s_ref[...] + p_ref[0:1, :]
    lane = lax.broadcasted_iota(jnp.int32, s.shape, 1)
    logf = jax.nn.log_sigmoid(s)
    g = -jnp.exp(p_ref[1:2, :]) * jax.nn.softplus(s)
    beta = jax.nn.sigmoid(s)
    o_ref[...] = jnp.where(lane < nh, logf, jnp.where(lane < 2 * nh, g, jnp.where(lane < 3 * nh, beta, 0.0)))


def _gates(small, b_f, dt_bias, a_log):
    M = small.shape[0]
    nh = b_f.shape[0]
    tm = min(M, 2048)
    pad = jnp.zeros((LANES - 3 * nh,), F32)
    p = jnp.zeros((8, LANES), F32)
    p = p.at[0].set(jnp.concatenate([b_f, dt_bias, jnp.zeros((nh,), F32), pad]))
    p = p.at[1].set(jnp.concatenate([jnp.zeros((nh,), F32), a_log, jnp.zeros((nh,), F32), pad]))
    return pl.pallas_call(
        functools.partial(_gates_body, nh=nh),
        out_shape=jax.ShapeDtypeStruct((M, LANES), F32),
        grid=(M // tm,),
        in_specs=[pl.BlockSpec((tm, LANES), lambda m: (m, 0)), pl.BlockSpec((8, LANES), lambda m: (0, 0))],
        out_specs=pl.BlockSpec((tm, LANES), lambda m: (m, 0)),
        compiler_params=_cparams(1),
        name="gates",
    )(small, p)


def _cumsum_body(x_ref, o_ref):
    L = x_ref.shape[1]
    r = lax.broadcasted_iota(jnp.int32, (LANES, LANES), 0)
    c = lax.broadcasted_iota(jnp.int32, (LANES, LANES), 1)
    upper = (r <= c).astype(F32)
    carry = jnp.zeros((x_ref.shape[0], 1), F32)
    for j in range(L // LANES):
        cs = _dot(x_ref[:, j * LANES:(j + 1) * LANES], upper, HI) + carry
        o_ref[:, j * LANES:(j + 1) * LANES] = cs
        carry = cs[:, LANES - 1:LANES]


def _cumsum_rows(x, seg):
    R, M = x.shape
    return pl.pallas_call(
        _cumsum_body,
        out_shape=jax.ShapeDtypeStruct((R, M), F32),
        grid=(M // seg,),
        in_specs=[pl.BlockSpec((R, seg), lambda b: (0, b))],
        out_specs=pl.BlockSpec((R, seg), lambda b: (0, b)),
        compiler_params=_cparams(1),
        name="cumsum_rows",
    )(x)


def _fox_body(q_ref, k_ref, v_ref, c_ref, o_ref, vt_ref, cb_ref, acc_ref, m_ref, *, t, hg, dh, scale):
    i = pl.program_id(2)
    L = k_ref.shape[0]
    log2e = math.log2(math.e)
    heads = range(hg)

    @pl.when(i == 0)
    def _():
        for g in heads:
            for jj in range(L // t):
                vt_ref[g, 0:dh, jj * t:(jj + 1) * t] = v_ref[jj * t:(jj + 1) * t, g * dh:(g + 1) * dh].T.astype(BF16)
            vt_ref[g, dh:, :] = jnp.ones((vt_ref.shape[1] - dh, L), BF16)
            for jj in range(L // LANES):
                row = c_ref[g, :, jj * LANES:(jj + 1) * LANES] * log2e
                cb_ref[g, jj * LANES:(jj + 1) * LANES, :] = jnp.broadcast_to(row, (LANES, LANES)).T

    q0 = pl.multiple_of(i * t, t)
    qs = [q_ref[:, g * dh:(g + 1) * dh].astype(BF16) for g in heads]
    r2 = [c_ref[g, :, pl.ds(q0, t)][:, t - 1:t] * log2e for g in heads]
    m_ref[...] = jnp.full_like(m_ref, NEG)
    acc_ref[...] = jnp.zeros_like(acc_ref)

    def step(j, masked):
        k0 = pl.multiple_of(j * t, t)
        qk = [_dot_nt(k_ref[pl.ds(k0, t), g * dh:(g + 1) * dh].astype(BF16), qs[g]) for g in heads]
        ps, alphas = [], []
        for g in heads:
            bias = r2[g] - cb_ref[g, pl.ds(k0, t), :]
            st = qk[g] * (scale * log2e) + jnp.concatenate([bias] * (t // LANES), axis=1)
            if masked:
                key = lax.broadcasted_iota(jnp.int32, (t, t), 0)
                qry = lax.broadcasted_iota(jnp.int32, (t, t), 1)
                st = jnp.where(key <= qry, st, NEG)
            m_prev = m_ref[g]
            m_new = jnp.maximum(m_prev, jnp.max(st, axis=0, keepdims=True))
            ps.append(jnp.exp2(st - m_new).astype(BF16))
            alphas.append(jnp.exp2(m_prev - m_new))
            m_ref[g] = m_new
        for g in heads:
            acc_ref[g] = acc_ref[g] * alphas[g] + _dot(vt_ref[g, :, pl.ds(k0, t)], ps[g])

    def loop_body(j, carry):
        step(j, False)
        return carry

    lax.fori_loop(0, i, loop_body, 0)
    step(i, True)
    for g in heads:
        acc = acc_ref[g]
        o_ref[:, g * dh:(g + 1) * dh] = (acc[0:dh] / acc[dh:dh + 1]).T


def _fox_prompt(q, k, v, c_rows, B, L, H):
    M, W = q.shape
    Dh = W // H
    t = min(L, 512)
    nq = L // t
    hg = 2 if H % 2 == 0 else 1
    kernel = functools.partial(_fox_body, t=t, hg=hg, dh=Dh, scale=Dh ** -0.5)
    return pl.pallas_call(
        kernel,
        out_shape=jax.ShapeDtypeStruct((M, W), F32),
        grid=(B, H // hg, nq),
        in_specs=[
            pl.BlockSpec((t, hg * Dh), lambda b, h, i: (b * nq + i, h)),
            pl.BlockSpec((L, hg * Dh), lambda b, h, i: (b, h)),
            pl.BlockSpec((L, hg * Dh), lambda b, h, i: (b, h)),
            pl.BlockSpec((hg, 1, L), lambda b, h, i: (b * (H // hg) + h, 0, 0)),
        ],
        out_specs=pl.BlockSpec((t, hg * Dh), lambda b, h, i: (b * nq + i, h)),
        scratch_shapes=[pltpu.VMEM((hg, Dh + 16, L), BF16), pltpu.VMEM((hg, L, LANES), F32),
                        pltpu.VMEM((hg, Dh + 16, t), F32), pltpu.VMEM((hg, 1, t), F32)],
        compiler_params=_cparams(3),
        name="fox_prompt",
    )(q, k, v, c_rows)


def _split2(x):
    hi = x.astype(BF16)
    return hi, (x - hi.astype(F32)).astype(BF16)


def _split3(x):
    h1 = x.astype(BF16)
    r1 = x - h1.astype(F32)
    h2 = r1.astype(BF16)
    return h1, h2, (r1 - h2.astype(F32)).astype(BF16)


def _dot3(a, b):
    (ah, al), (bh, bl) = a, b
    return _dot(ah, bh) + (_dot(ah, bl) + _dot(al, bh))


def _unit_lower_inverses(a_list, n):
    ri = lax.broadcasted_iota(jnp.int32, (n, n), 0)
    ci = lax.broadcasted_iota(jnp.int32, (n, n), 1)
    eye = (ri == ci).astype(F32)
    nm = len(a_list)
    diag16 = (ri >> 4) == (ci >> 4)
    pw = [jnp.where(diag16, a, 0.0) for a in a_list]
    t = [eye - p for p in pw]
    for _ in range(3):
        pw16 = [p.astype(BF16) for p in pw]
        pw = [_dot(p, p) for p in pw16]
        pn16 = [p.astype(BF16) for p in pw]
        t = [t[m] + _dot(t[m].astype(BF16), pn16[m]) for m in range(nm)]
    size = 16
    while size < GDN_CHUNK:
        sh = size.bit_length() - 1
        off = ((ri >> (sh + 1)) == (ci >> (sh + 1))) & ((ri >> sh) == (ci >> sh) + 1)
        t16 = [x.astype(BF16) for x in t]
        mid = [_dot(jnp.where(off, a_list[m], 0.0).astype(BF16), t16[m]) for m in range(nm)]
        t = [t[m] - _dot(t16[m], mid[m].astype(BF16)) for m in range(nm)]
        size *= 2
    ts = [_split2(x) for x in t]
    res = [(eye - t[m]) - _dot3(_split2(a_list[m]), ts[m]) for m in range(nm)]
    return [t[m] + _dot(ts[m][0], res[m].astype(BF16)) for m in range(nm)]


def _gdn_body(cin_ref, z_ref, gc_ref, *rest, nh, dk, dv, nseq):
    gr_refs = rest[:nseq]
    cw_ref, ng_ref, o_ref, s_out_ref, s_ref, win_ref, m16_ref = rest[nseq:]
    j = pl.program_id(1)
    T = cin_ref.shape[1]
    C = GDN_CHUNK
    nc = T // C

    ri = lax.broadcasted_iota(jnp.int32, (T, T), 0)
    ci = lax.broadcasted_iota(jnp.int32, (T, T), 1)
    csh = C.bit_length() - 1
    same = (ri >> csh) == (ci >> csh)
    tril = same & (ri >= ci)
    strict = same & (ri > ci)

    @pl.when((pl.program_id(0) == 0) & (j == 0))
    def _():
        m16_ref[0] = tril.astype(BF16)
        m16_ref[1] = (same & (ri <= ci)).astype(BF16)

    @pl.when(j == 0)
    def _():
        s_ref[...] = jnp.zeros_like(s_ref)
        win_ref[:, 0:8, :] = jnp.zeros((nseq, 8, win_ref.shape[2]), F32)

    w = cw_ref[...]
    kw = w.shape[0]
    rowi = lax.broadcasted_iota(jnp.int32, (T, 1), 0)
    units = [(s, h) for s in range(nseq) for h in range(nh)]
    nu = len(units)
    convs, gcols, grows = [], [], []
    for s in range(nseq):
        win_ref[s, 8:8 + T, :] = cin_ref[s]
        conv = win_ref[s, 8 - kw + 1:8 - kw + 1 + T, :] * w[0:1]
        for i in range(1, kw):
            conv = conv + win_ref[s, 8 - kw + 1 + i:8 - kw + 1 + i + T, :] * w[i:i + 1]
        win_ref[s, 0:8, :] = win_ref[s, T:T + 8, :]
        convs.append(jax.nn.silu(conv))
        gcols.append(sum(_dot(m16_ref[0], part) for part in _split3(gc_ref[s])))
        grows.append(sum(_dot(part, m16_ref[1]) for part in _split3(gr_refs[s][...])))

    q, k, beta, gcc, gcr, gam, kb, k16 = [], [], [], [], [], [], [], []
    for u, (s, h) in enumerate(units):
        qh = convs[s][:, h * dk:(h + 1) * dk]
        kh = convs[s][:, nh * dk + h * dk:nh * dk + (h + 1) * dk]
        q.append(qh * lax.rsqrt(jnp.sum(qh * qh, axis=-1, keepdims=True) + EPS) * dk ** -0.5)
        k.append(kh * lax.rsqrt(jnp.sum(kh * kh, axis=-1, keepdims=True) + EPS))
        beta.append(gc_ref[s, :, 2 * nh + h:2 * nh + h + 1])
        gcc.append(gcols[s][:, nh + h:nh + h + 1])
        gcr.append(grows[s][nh + h:nh + h + 1, :])
        gam.append(jnp.exp(jnp.where(tril, gcc[u] - gcr[u], NEG)))
        kb.append(k[u] * beta[u])
        k16.append(k[u].astype(BF16))
    a = [jnp.where(strict, _dot_nt(kb[u].astype(BF16), k16[u]) * gam[u], 0.0) for u in range(nu)]
    tinv = _unit_lower_inverses(a, T)
    x = []
    for u, (s, h) in enumerate(units):
        v = convs[s][:, 2 * nh * dk + h * dv:2 * nh * dk + (h + 1) * dv]
        rhs = jnp.concatenate([v * beta[u], kb[u] * jnp.exp(gcc[u])], axis=-1)
        x.append(_dot3(_split2(tinv[u]), _split2(rhs)))
    attn, qg, kd, egl = [], [], [], []
    for u in range(nu):
        attn.append((_dot_nt(q[u].astype(BF16), k16[u]) * gam[u]).astype(BF16))
        qg.append((q[u] * jnp.exp(gcc[u])).astype(BF16))
        gl = gcr[u][:, C - 1:C]
        for cidx in range(1, nc):
            gl = jnp.where(rowi >= cidx * C, gcr[u][:, (cidx + 1) * C - 1:(cidx + 1) * C], gl)
        kd.append((k[u] * jnp.exp(gl - gcc[u])).astype(BF16))
        egl.append(jnp.exp(gl))
    S = [s_ref[s, h] for s, h in units]
    outs = [[] for _ in units]
    for cidx in range(nc):
        sl = slice(cidx * C, (cidx + 1) * C)
        S16 = [st.astype(BF16) for st in S]
        vn16 = [(x[u][sl, :dv] - _dot(x[u][sl, dv:].astype(BF16), S16[u])).astype(BF16) for u in range(nu)]
        for u in range(nu):
            outs[u].append(_dot(qg[u][sl], S16[u]) + _dot(attn[u][sl, cidx * C:(cidx + 1) * C], vn16[u]))
        S = [S[u] * egl[u][cidx * C:cidx * C + 1] + _dot_tn(kd[u][sl], vn16[u]) for u in range(nu)]
    for u, (s, h) in enumerate(units):
        s_ref[s, h] = S[u]
        o = _rms_rows(jnp.concatenate(outs[u], axis=0), ng_ref[...])
        o_ref[s, :, h * dv:(h + 1) * dv] = o * jax.nn.silu(z_ref[s, :, h * dv:(h + 1) * dv])

    @pl.when(j == pl.num_programs(1) - 1)
    def _():
        s_out_ref[...] = s_ref[...]


def _gdn_prompt(conv_in, z, gates_col, gates_row, conv_w, norm_g, B, L, nh, dk, dv):
    M, CD = conv_in.shape
    T = 2 * GDN_CHUNK
    nb = L // T
    nseq = 2 if B % 2 == 0 else 1
    kernel = functools.partial(_gdn_body, nh=nh, dk=dk, dv=dv, nseq=nseq)

    def row_map(s):
        return lambda p, j: (0, (p * nseq + s) * nb + j)

    per_seq = lambda p, j: (p, j, 0)
    go, S = pl.pallas_call(
        kernel,
        out_shape=[jax.ShapeDtypeStruct((B, L, nh * dv), F32), jax.ShapeDtypeStruct((B, nh, dk, dv), F32)],
        grid=(B // nseq, nb),
        in_specs=[
            pl.BlockSpec((nseq, T, CD), per_seq),
            pl.BlockSpec((nseq, T, nh * dv), per_seq),
            pl.BlockSpec((nseq, T, LANES), per_seq),
        ] + [pl.BlockSpec((16, T), row_map(s)) for s in range(nseq)] + [
            pl.BlockSpec(conv_w.shape, lambda p, j: (0, 0)),
            pl.BlockSpec((1, dv), lambda p, j: (0, 0)),
        ],
        out_specs=[
            pl.BlockSpec((nseq, T, nh * dv), per_seq),
            pl.BlockSpec((nseq, nh, dk, dv), lambda p, j: (p, 0, 0, 0)),
        ],
        scratch_shapes=[pltpu.VMEM((nseq, nh, dk, dv), F32), pltpu.VMEM((nseq, 8 + T, CD), F32),
                        pltpu.VMEM((2, T, T), BF16)],
        compiler_params=_cparams(2),
        name="gdn_prompt",
    )(conv_in.reshape(B, L, CD), z.reshape(B, L, nh * dv), gates_col.reshape(B, L, LANES),
      *([gates_row] * nseq), conv_w, norm_g.reshape(1, dv))
    return go.reshape(M, nh * dv), S


def _rope(x, cos2, sin2, nh, dk):
    outs = []
    for h in range(nh):
        xh = x[:, h * dk:(h + 1) * dk]
        outs.append(xh * cos2 + pltpu.roll(xh, dk // 2, 1) * sin2)
    return outs


def _rope_tables(pos, freqs):
    ang = pos * freqs
    cos, sin = jnp.cos(ang), jnp.sin(ang)
    return jnp.concatenate([cos, cos], axis=-1), jnp.concatenate([-sin, sin], axis=-1)


def _head_layernorm(o):
    mu = jnp.mean(o, axis=-1, keepdims=True)
    oc = o - mu
    return oc * lax.rsqrt(jnp.mean(oc * oc, axis=-1, keepdims=True) + EPS)


def _log_gamma(h):
    return math.log(1.0 - 2.0 ** (-5.0 - h))


def _ret_body(q_ref, k_ref, v_ref, gate_ref, fr_ref, gn_ref, o_ref, s_out_ref, s_ref, cos_ref, sin_ref, dmat_ref,
              *, nh, dk, dv):
    b = pl.program_id(0)
    j = pl.program_id(1)
    C = q_ref.shape[0]
    rowi = lax.broadcasted_iota(jnp.int32, (C, 1), 0)
    rowf = rowi.astype(F32)

    @pl.when((b == 0) & (j == 0))
    def _():
        def fill(c, carry):
            c0 = pl.multiple_of(c * C, C)
            cos2, sin2 = _rope_tables((c0 + rowi).astype(F32), fr_ref[...])
            cos_ref[pl.ds(c0, C), :] = cos2
            sin_ref[pl.ds(c0, C), :] = sin2
            return carry

        lax.fori_loop(0, pl.num_programs(1), fill, 0)
        ri = lax.broadcasted_iota(jnp.int32, (C, C), 0)
        ci = lax.broadcasted_iota(jnp.int32, (C, C), 1)
        diff = (ri - ci).astype(F32)
        for h in range(nh):
            dmat_ref[h] = jnp.exp(jnp.where(ri >= ci, diff * _log_gamma(h), NEG))

    @pl.when(j == 0)
    def _():
        s_ref[...] = jnp.zeros_like(s_ref)

    j0 = pl.multiple_of(j * C, C)
    cos2 = cos_ref[pl.ds(j0, C), :]
    sin2 = sin_ref[pl.ds(j0, C), :]
    qs = _rope(q_ref[...], cos2, sin2, nh, dk)
    ks = _rope(k_ref[...], cos2, sin2, nh, dk)
    for h in range(nh):
        lg = _log_gamma(h)
        q = qs[h]
        k = ks[h] * dk ** -0.5
        v = v_ref[:, h * dv:(h + 1) * dv].astype(BF16)
        q16 = q.astype(BF16)
        inner = _dot((_dot_nt(q16, k.astype(BF16)) * dmat_ref[h]).astype(BF16), v)
        S = s_ref[h]
        cross = _dot((q * jnp.exp(lg * (rowf + 1.0))).astype(BF16), S.astype(BF16))
        s_ref[h] = S * math.exp(lg * C) + _dot_tn((k * jnp.exp(lg * (C - 1.0 - rowf))).astype(BF16), v)
        o = _head_layernorm(inner + cross) * gn_ref[:, h * dv:(h + 1) * dv]
        o_ref[:, h * dv:(h + 1) * dv] = jax.nn.silu(gate_ref[:, h * dv:(h + 1) * dv]) * o

    @pl.when(j == pl.num_programs(1) - 1)
    def _():
        s_out_ref[...] = s_ref[...]


def _ret_prompt(q, k, v, gate, freqs, gnorm, B, L, nh, dk, dv):
    M = q.shape[0]
    C = 128
    nb = L // C
    kernel = functools.partial(_ret_body, nh=nh, dk=dk, dv=dv)
    return pl.pallas_call(
        kernel,
        out_shape=[jax.ShapeDtypeStruct((M, nh * dv), F32), jax.ShapeDtypeStruct((B, nh, dk, dv), F32)],
        grid=(B, nb),
        in_specs=[
            pl.BlockSpec((C, nh * dk), lambda b, j: (b * nb + j, 0)),
            pl.BlockSpec((C, nh * dk), lambda b, j: (b * nb + j, 0)),
            pl.BlockSpec((C, nh * dv), lambda b, j: (b * nb + j, 0)),
            pl.BlockSpec((C, nh * dv), lambda b, j: (b * nb + j, 0)),
            pl.BlockSpec((1, dk // 2), lambda b, j: (0, 0)),
            pl.BlockSpec((1, nh * dv), lambda b, j: (0, 0)),
        ],
        out_specs=[
            pl.BlockSpec((C, nh * dv), lambda b, j: (b * nb + j, 0)),
            pl.BlockSpec((None, nh, dk, dv), lambda b, j: (b, 0, 0, 0)),
        ],
        scratch_shapes=[pltpu.VMEM((nh, dk, dv), F32), pltpu.VMEM((L, dk), F32), pltpu.VMEM((L, dk), F32),
                        pltpu.VMEM((nh, C, C), F32)],
        compiler_params=_cparams(2),
        name="ret_prompt",
    )(q, k, v, gate, freqs, gnorm.reshape(1, nh * dv))


def _page_sums_body(x_ref, o_ref, *, nh):
    n = x_ref.shape[1]
    fi = lax.broadcasted_iota(jnp.int32, (n, n), 0)
    fo = lax.broadcasted_iota(jnp.int32, (n, n), 1)
    same_head = (fi & (nh - 1)) == (fo & (nh - 1))
    parts = _split3(x_ref[...])
    o_ref[:, :n] = sum(_dot(part, (same_head & (fi > fo)).astype(BF16)) for part in parts)
    o_ref[:, n:] = sum(_dot(part, same_head.astype(BF16)) for part in parts)


def _page_sums(lf_pages, nh):
    P, n = lf_pages.shape
    tp = 512 if P % 512 == 0 else P
    return pl.pallas_call(
        functools.partial(_page_sums_body, nh=nh),
        out_shape=jax.ShapeDtypeStruct((P, 2 * n), F32),
        grid=(P // tp,),
        in_specs=[pl.BlockSpec((tp, n), lambda p: (p, 0))],
        out_specs=pl.BlockSpec((tp, 2 * n), lambda p: (p, 0)),
        compiler_params=_cparams(1),
        name="page_sums",
    )(lf_pages)


def _paged_body(pt_ref, *refs, G, nh, scale):
    k_refs = refs[:G]
    v_refs = refs[G:2 * G]
    w_refs = refs[2 * G:3 * G]
    q_ref, kn_ref, vn_ref, lfn_ref, o_ref, m_ref, l_ref, acc_ref, carry_ref = refs[3 * G:]
    b_idx = pl.program_id(0)
    s_idx = pl.program_id(1)
    ns = pl.num_programs(1)
    n = k_refs[0].shape[0]
    dh = q_ref.shape[1]
    sub = w_refs[0].shape[0]

    @pl.when(s_idx == 0)
    def _():
        m_ref[...] = jnp.full_like(m_ref, NEG)
        l_ref[...] = jnp.zeros_like(l_ref)
        acc_ref[...] = jnp.zeros_like(acc_ref)
        carry_ref[...] = jnp.zeros_like(carry_ref)

    q = q_ref[...]
    q8 = jnp.concatenate([q, jnp.zeros((8 - nh, dh), F32)], axis=0).astype(BF16)
    lfn = jnp.concatenate([lfn_ref[...], jnp.zeros((8 - nh, 1), F32)], axis=0)
    hrow = lax.broadcasted_iota(jnp.int32, (8, n), 0)
    hcol = lax.broadcasted_iota(jnp.int32, (8, n), 1) & (nh - 1)
    match = hrow == hcol
    carry = carry_ref[...]
    s_parts = [None] * G
    for i in reversed(range(G)):
        page_id = pt_ref[b_idx, (ns - 1 - s_idx) * G + i]
        wt = w_refs[i][pl.ds(lax.rem(page_id, sub), 1), :]
        bias = wt[:, :n] + carry
        carry = carry + wt[:, n:]
        s = _dot_nt(q8, k_refs[i][...].astype(BF16)) * scale + bias + lfn
        s_parts[i] = jnp.where(match, s, NEG)
    carry_ref[...] = carry
    m_run, l_run, acc_run = m_ref[...], l_ref[...], acc_ref[...]
    half = max(G // 2, 1)
    for grp in (range(half, G), range(0, half)) if G > 1 else (range(G),):
        sg = [s_parts[i] for i in grp]
        m_new = jnp.maximum(m_run, jnp.max(functools.reduce(jnp.maximum, sg), axis=-1, keepdims=True))
        alpha = jnp.exp(m_run - m_new)
        pg = [jnp.exp(s - m_new) for s in sg]
        l_run = alpha * l_run + jnp.sum(functools.reduce(jnp.add, pg), axis=-1, keepdims=True)
        pv = functools.reduce(jnp.add, [_dot(p.astype(BF16), v_refs[i][...].astype(BF16)) for p, i in zip(pg, grp)])
        acc_run = alpha * acc_run + pv
        m_run = m_new
    m_ref[...], l_ref[...], acc_ref[...] = m_run, l_run, acc_run

    @pl.when(s_idx == pl.num_programs(1) - 1)
    def _():
        qf = q.astype(BF16).astype(F32)
        kn = kn_ref[...].astype(BF16).astype(F32)
        s_new = jnp.sum(qf * kn, axis=-1, keepdims=True) * scale
        m_prev = m_ref[0:nh, :]
        m_new = jnp.maximum(m_prev, s_new)
        alpha = jnp.exp(m_prev - m_new)
        p_new = jnp.exp(s_new - m_new)
        l_fin = alpha * l_ref[0:nh, :] + p_new
        vn = vn_ref[...].astype(BF16).astype(F32)
        num = alpha * acc_ref[0:nh, :] + p_new.astype(BF16).astype(F32) * vn
        o_ref[...] = num / l_fin


def _fox_sample(q, k_new, v_new, lf_new, k_pages, v_pages, page_sums, page_table, nh):
    Bn, _, dh = q.shape
    n_pages = page_table.shape[1]
    n_pool, n, _ = k_pages.shape
    G = 16 if n_pages % 16 == 0 else (8 if n_pages % 8 == 0 else 1)
    ns = n_pages // G
    sub = 8
    assert n_pool % sub == 0
    page_sums = page_sums.reshape(n_pool // sub, sub, 2 * n)

    def page_map(i):
        return lambda b, s, pt: (pt[b, (ns - 1 - s) * G + i], 0, 0)

    def sums_map(i):
        return lambda b, s, pt: (lax.div(pt[b, (ns - 1 - s) * G + i], sub), 0, 0)

    per_b = lambda b, s, pt: (b, 0, 0)
    grid_spec = pltpu.PrefetchScalarGridSpec(
        num_scalar_prefetch=1,
        grid=(Bn, ns),
        in_specs=[pl.BlockSpec((None, n, dh), page_map(i)) for i in range(G)]
        + [pl.BlockSpec((None, n, dh), page_map(i)) for i in range(G)]
        + [pl.BlockSpec((None, sub, 2 * n), sums_map(i)) for i in range(G)]
        + [pl.BlockSpec((None, nh, dh), per_b), pl.BlockSpec((None, nh, dh), per_b),
           pl.BlockSpec((None, nh, dh), per_b), pl.BlockSpec((None, nh, 1), per_b)],
        out_specs=pl.BlockSpec((None, nh, dh), per_b),
        scratch_shapes=[pltpu.VMEM((8, 1), F32), pltpu.VMEM((8, 1), F32), pltpu.VMEM((8, dh), F32),
                        pltpu.VMEM((1, n), F32)],
    )
    return pl.pallas_call(
        functools.partial(_paged_body, G=G, nh=nh, scale=dh ** -0.5),
        out_shape=jax.ShapeDtypeStruct((Bn, nh, dh), F32),
        grid_spec=grid_spec,
        compiler_params=_cparams(2),
        name="fox_sample",
    )(page_table, *([k_pages] * G), *([v_pages] * G), *([page_sums] * G), q, k_new, v_new, lf_new)


def _pad8(row):
    return jnp.concatenate([row, jnp.zeros((8 - row.shape[0], row.shape[1]), row.dtype)], axis=0)


def _gdn_sample_body(cin_ref, prev_ref, z_ref, gt_ref, cw_ref, ng_ref, s0_ref, o_ref, cnew_ref, s_out_ref, *, nh, dk, dv):
    u = cin_ref[...]
    prev = prev_ref[...]
    w = cw_ref[...]
    kw = w.shape[0]
    conv = u * w[kw - 1:kw]
    for i in range(kw - 1):
        conv = conv + prev[i:i + 1] * w[i:i + 1]
    cnew_ref[0:kw - 2, :] = prev[1:kw - 1]
    cnew_ref[kw - 2:kw - 1, :] = u
    conv = jax.nn.silu(conv)
    gt = gt_ref[...]
    for h in range(nh):
        q = conv[:, h * dk:(h + 1) * dk]
        k = conv[:, nh * dk + h * dk:nh * dk + (h + 1) * dk]
        v = conv[:, 2 * nh * dk + h * dv:2 * nh * dk + (h + 1) * dv]
        q = q * lax.rsqrt(jnp.sum(q * q, axis=-1, keepdims=True) + EPS) * dk ** -0.5
        k = k * lax.rsqrt(jnp.sum(k * k, axis=-1, keepdims=True) + EPS)
        g = gt[:, nh + h:nh + h + 1]
        beta = gt[:, 2 * nh + h:2 * nh + h + 1]
        eg = jnp.exp(g)
        S = s0_ref[h]
        lhs = _pad8(jnp.concatenate([k * beta * eg, q * eg], axis=0)).astype(BF16)
        r = _dot(lhs, S.astype(BF16))
        v_new = v * beta - r[0:1]
        qk = jnp.sum(q.astype(BF16).astype(F32) * k.astype(BF16).astype(F32), axis=-1, keepdims=True)
        vn16 = v_new.astype(BF16)
        o = r[1:2] + qk.astype(BF16).astype(F32) * vn16.astype(F32)
        s_out_ref[h] = S * eg + _dot_tn(_pad8(k).astype(BF16), _pad8(vn16))
        o = _rms_rows(o, ng_ref[...])
        o_ref[:, h * dv:(h + 1) * dv] = o * jax.nn.silu(z_ref[:, h * dv:(h + 1) * dv])


def _gdn_sample(conv_in, conv_prev, z, gates_col, conv_w, norm_g, S0, nh, dk, dv):
    Bn, _, CD = conv_in.shape
    kw = conv_w.shape[0]
    per_b3 = lambda b: (b, 0, 0)
    kernel = functools.partial(_gdn_sample_body, nh=nh, dk=dk, dv=dv)
    return pl.pallas_call(
        kernel,
        out_shape=[jax.ShapeDtypeStruct((Bn, 1, nh * dv), F32), jax.ShapeDtypeStruct((Bn, kw - 1, CD), F32),
                   jax.ShapeDtypeStruct((Bn, nh, dk, dv), F32)],
        grid=(Bn,),
        in_specs=[
            pl.BlockSpec((None, 1, CD), per_b3),
            pl.BlockSpec((None, kw - 1, CD), per_b3),
            pl.BlockSpec((None, 1, nh * dv), per_b3),
            pl.BlockSpec((None, 1, LANES), per_b3),
            pl.BlockSpec(conv_w.shape, lambda b: (0, 0)),
            pl.BlockSpec((1, dv), lambda b: (0, 0)),
            pl.BlockSpec((None, nh, dk, dv), lambda b: (b, 0, 0, 0)),
        ],
        out_specs=[
            pl.BlockSpec((None, 1, nh * dv), per_b3),
            pl.BlockSpec((None, kw - 1, CD), per_b3),
            pl.BlockSpec((None, nh, dk, dv), lambda b: (b, 0, 0, 0)),
        ],
        compiler_params=_cparams(1),
        name="gdn_sample",
    )(conv_in, conv_prev, z, gates_col, conv_w, norm_g.reshape(1, dv), S0)


def _ret_sample_body(q_ref, k_ref, v_ref, gate_ref, fr_ref, gn_ref, s0_ref, o_ref, s_out_ref, *, nh, dk, dv, pos):
    cos2, sin2 = _rope_tables(jnp.full((1, 1), pos, F32), fr_ref[...])
    qs = _rope(q_ref[...], cos2, sin2, nh, dk)
    ks = _rope(k_ref[...], cos2, sin2, nh, dk)
    for h in range(nh):
        lg = _log_gamma(h)
        q = qs[h]
        k = ks[h] * dk ** -0.5
        v16 = v_ref[:, h * dv:(h + 1) * dv].astype(BF16)
        S = s0_ref[h]
        qk = jnp.sum(q.astype(BF16).astype(F32) * k.astype(BF16).astype(F32), axis=-1, keepdims=True)
        inner = qk.astype(BF16).astype(F32) * v16.astype(F32)
        cross = _dot(_pad8(q * math.exp(lg)).astype(BF16), S.astype(BF16))[0:1]
        s_out_ref[h] = S * math.exp(lg) + _dot_tn(_pad8(k).astype(BF16), _pad8(v16))
        o = _head_layernorm(inner + cross) * gn_ref[:, h * dv:(h + 1) * dv]
        o_ref[:, h * dv:(h + 1) * dv] = jax.nn.silu(gate_ref[:, h * dv:(h + 1) * dv]) * o


def _ret_sample(q, k, v, gate, freqs, gnorm, S0, pos, nh, dk, dv):
    Bn = q.shape[0]
    per_b3 = lambda b: (b, 0, 0)
    kernel = functools.partial(_ret_sample_body, nh=nh, dk=dk, dv=dv, pos=float(pos))
    return pl.pallas_call(
        kernel,
        out_shape=[jax.ShapeDtypeStruct((Bn, 1, nh * dv), F32), jax.ShapeDtypeStruct((Bn, nh, dk, dv), F32)],
        grid=(Bn,),
        in_specs=[
            pl.BlockSpec((None, 1, nh * dk), per_b3),
            pl.BlockSpec((None, 1, nh * dk), per_b3),
            pl.BlockSpec((None, 1, nh * dv), per_b3),
            pl.BlockSpec((None, 1, nh * dv), per_b3),
            pl.BlockSpec((1, dk // 2), lambda b: (0, 0)),
            pl.BlockSpec((1, nh * dv), lambda b: (0, 0)),
            pl.BlockSpec((None, nh, dk, dv), lambda b: (b, 0, 0, 0)),
        ],
        out_specs=[
            pl.BlockSpec((None, 1, nh * dv), per_b3),
            pl.BlockSpec((None, nh, dk, dv), lambda b: (b, 0, 0, 0)),
        ],
        compiler_params=_cparams(1),
        name="ret_sample",
    )(q, k, v, gate, freqs, gnorm.reshape(1, nh * dv), S0)


def kernel(x_prompt, x_sample, cache_fox_k, cache_fox_v, cache_fox_logf, page_table, state_gdn_conv, state_gdn_S, state_ret_S, norm_g, final_norm_g, ffn_w_gu, ffn_w_down, ab_w_in, ab_w_out, fox_b_f, gdn_conv_w, gdn_A_log, gdn_dt_bias, gdn_norm_g, c_w_in, c_w_out, ret_norm_g):
    B, L, D = x_prompt.shape
    Bn, Ls, _ = x_sample.shape
    assert Ls == 1, "the sample group decodes one token per sequence"
    depth = norm_g.shape[0]
    _, n_pool, page, fh, fd = cache_fox_k.shape
    fw = fh * fd
    _, _, gh, gdk, gdv = state_gdn_S.shape
    conv_dim = gdn_conv_w.shape[2]
    kw = gdn_conv_w.shape[1]
    _, _, rh, rdk, rdv = state_ret_S.shape
    n_pages = page_table.shape[1]
    assert fh == gh and 3 * fh <= 16 and fh & (fh - 1) == 0

    xp = x_prompt.reshape(B * L, D)
    xs = x_sample.reshape(Bn, D)
    w_gu16 = _to_bf16(ffn_w_gu)
    w_down16 = _to_bf16(ffn_w_down)
    freqs = (ROPE_BASE ** (-jnp.arange(rdk // 2, dtype=F32) / (rdk // 2))).reshape(1, rdk // 2)

    fkp, fvp, flp, fks, fvs, fls = [], [], [], [], [], []
    gcp, gsp, gcs, gss = [], [], [], []
    rsp, rss = [], []
    for li in range(depth):
        xp = _ffn(xp, norm_g[li, 0], w_gu16, w_down16, li, 0)
        xs = _ffn(xs, norm_g[li, 0], w_gu16, w_down16, li, 0)
        j = li // 2
        if li % 2 == 0:
            w_in = ab_w_in[j]
            o0 = 3 * fw
            zw = gh * gdv
            w_cat = _to_bf16(jnp.concatenate(
                [w_in[:, :o0], w_in[:, o0 + fh:o0 + fh + conv_dim + zw], w_in[:, o0:o0 + fh],
                 w_in[:, o0 + fh + conv_dim + zw:], jnp.zeros((D, LANES - 3 * fh), F32)], axis=1))
            widths = (fw, fw, fw, conv_dim, zw, LANES)
            w_out16 = ab_w_out[j].astype(BF16)

            q, k, v, cin, z, small, k3, v3 = _rms_proj(xp, norm_g[li, 1], w_cat, widths, per_head=(1, 2), dh=fd)
            gates = _gates(small, fox_b_f[j], gdn_dt_bias[j], gdn_A_log[j])
            gates_row = gates[:, :16].T
            c_rows = _cumsum_rows(gates_row, L)[:fh].reshape(fh, B, L).transpose(1, 0, 2).reshape(B * fh, 1, L)
            fo = _fox_prompt(q, k, v, c_rows, B, L, fh)
            go, S_p = _gdn_prompt(cin, z, gates, gates_row, gdn_conv_w[j], gdn_norm_g[j], B, L, gh, gdk, gdv)
            pre_p = ([fo, go], w_out16)
            fkp.append(k3.reshape(B, L, fh, fd))
            fvp.append(v3.reshape(B, L, fh, fd))
            flp.append(gates[:, :fh].reshape(B, L, fh))
            gcp.append(cin.reshape(B, L, conv_dim)[:, L - (kw - 1):])
            gsp.append(S_p)

            q, k, v, cin, z, small = _rms_proj(xs, norm_g[li, 1], w_cat, widths)
            gates = _gates(small, fox_b_f[j], gdn_dt_bias[j], gdn_A_log[j])
            sums = _page_sums(cache_fox_logf[j].reshape(n_pool, page * fh), fh)
            fo = _fox_sample(q.reshape(Bn, fh, fd), k.reshape(Bn, fh, fd), v.reshape(Bn, fh, fd),
                             gates[:, :fh].reshape(Bn, fh, 1),
                             cache_fox_k[j].reshape(n_pool, page * fh, fd), cache_fox_v[j].reshape(n_pool, page * fh, fd),
                             sums, page_table, fh)
            go, conv_s, S_s = _gdn_sample(cin.reshape(Bn, 1, conv_dim), state_gdn_conv[j], z.reshape(Bn, 1, gh * gdv),
                                          gates.reshape(Bn, 1, LANES), gdn_conv_w[j], gdn_norm_g[j], state_gdn_S[j],
                                          gh, gdk, gdv)
            pre_s = ([fo.reshape(Bn, fw), go.reshape(Bn, gh * gdv)], w_out16)
            fks.append(k.reshape(Bn, 1, fh, fd))
            fvs.append(v.reshape(Bn, 1, fh, fd))
            fls.append(gates[:, :fh].reshape(Bn, 1, fh))
            gcs.append(conv_s)
            gss.append(S_s)
        else:
            w_in16 = _to_bf16(c_w_in[j])
            qk_w, v_w = rh * rdk, rh * rdv
            widths = (qk_w, qk_w, v_w, v_w)
            w_out16 = c_w_out[j].astype(BF16)

            q, k, v, gate = _rms_proj(xp, norm_g[li, 1], w_in16, widths)
            y, R_p = _ret_prompt(q, k, v, gate, freqs, ret_norm_g[j], B, L, rh, rdk, rdv)
            pre_p = ([y], w_out16)
            rsp.append(R_p)

            q, k, v, gate = _rms_proj(xs, norm_g[li, 1], w_in16, widths)
            y, R_s = _ret_sample(q.reshape(Bn, 1, qk_w), k.reshape(Bn, 1, qk_w), v.reshape(Bn, 1, v_w),
                                 gate.reshape(Bn, 1, v_w), freqs, ret_norm_g[j], state_ret_S[j],
                                 n_pages * page, rh, rdk, rdv)
            pre_s = ([y.reshape(Bn, v_w)], w_out16)
            rss.append(R_s)
        final_g = final_norm_g if li == depth - 1 else None
        xp = _ffn(xp, norm_g[li, 2], w_gu16, w_down16, li, 1, final_g, pre_p)
        xs = _ffn(xs, norm_g[li, 2], w_gu16, w_down16, li, 1, final_g, pre_s)
    y_prompt = xp.reshape(B, L, D)
    y_sample = xs.reshape(Bn, 1, D)
    return (y_prompt, y_sample,
            jnp.stack(fkp), jnp.stack(fvp), jnp.stack(flp),
            jnp.stack(fks), jnp.stack(fvs), jnp.stack(fls),
            jnp.stack(gcp), jnp.stack(gsp), jnp.stack(gcs), jnp.stack(gss),
            jnp.stack(rsp), jnp.stack(rss))
```

```python
import functools
import math

import jax
import jax.numpy as jnp
from jax import lax
from jax.experimental import pallas as pl
from jax.experimental.pallas import tpu as pltpu

F32 = jnp.float32
BF16 = jnp.bfloat16
EPS = 1e-6
ROPE_BASE = 10000.0
NEG = -1e30
LANES = 128
GDN_CHUNK = 64
VMEM_LIMIT = 56 * 1024 * 1024
HI = lax.Precision.HIGHEST


def _cparams(n_axes):
    return pltpu.CompilerParams(dimension_semantics=("arbitrary",) * n_axes,
                                vmem_limit_bytes=VMEM_LIMIT)


def _dot(a, b, precision=None):
    return jnp.dot(a, b, preferred_element_type=F32, precision=precision)


def _dot_nt(a, b, precision=None):
    return lax.dot_general(a, b, (((1,), (1,)), ((), ())), preferred_element_type=F32, precision=precision)


def _dot_tn(a, b, precision=None):
    return lax.dot_general(a, b, (((0,), (0,)), ((), ())), preferred_element_type=F32, precision=precision)


def _rms_rows(x, g):
    return x * lax.rsqrt(jnp.mean(x * x, axis=-1, keepdims=True) + EPS) * g


MXU_DEPTH = 256


def _ffn_body(*refs, F, chunk, final_norm, n_pre):
    x_ref, g_ref, wgu_ref, wd_ref = refs[:4]
    a_refs = refs[4:4 + n_pre]
    rest = refs[4 + n_pre:]
    x = x_ref[...]
    if n_pre:
        wo_ref, rest = rest[0], rest[1:]
        row = 0
        for a_ref in a_refs:
            n = a_ref.shape[1]
            x = x + _dot(a_ref[...].astype(BF16), wo_ref[row:row + n, :])
            row += n
    if final_norm:
        gf_ref, rest = rest[0], rest[1:]
    o_ref = rest[0]
    xn = _rms_rows(x, g_ref[...]).astype(BF16)
    acc = None
    for c0 in range(0, F, chunk):
        c1 = min(F, c0 + chunk)
        a = _dot(xn, wgu_ref[:, c0:c1])
        b = _dot(xn, wgu_ref[:, F + c0:F + c1])
        h = (jax.nn.silu(a) * b).astype(BF16)
        d = _dot(h, wd_ref[c0:c1, :])
        acc = d if acc is None else acc + d
    y = x + 0.5 * acc
    o_ref[...] = _rms_rows(y, gf_ref[...]) if final_norm else y


def _ffn(x, g, w_gu, w_down, li, j, final_g=None, pre=None):
    M, D = x.shape
    F = w_down.shape[2]
    tm = min(M, 512)
    final_norm = final_g is not None
    const = lambda m: (0, 0)
    in_specs = [pl.BlockSpec((tm, D), lambda m: (m, 0)), pl.BlockSpec((1, D), const),
                pl.BlockSpec((None, None, D, 2 * F), lambda m: (li, j, 0, 0)),
                pl.BlockSpec((None, None, F, D), lambda m: (li, j, 0, 0))]
    args = [x, g.reshape(1, D), w_gu, w_down]
    acts, w_out = pre if pre is not None else ((), None)
    if acts:
        assert sum(a.shape[1] for a in acts) == w_out.shape[0]
        in_specs += [pl.BlockSpec((tm, a.shape[1]), lambda m: (m, 0)) for a in acts]
        in_specs.append(pl.BlockSpec(w_out.shape, const))
        args += [*acts, w_out]
    if final_norm:
        in_specs.append(pl.BlockSpec((1, D), const))
        args.append(final_g.reshape(1, D))
    return pl.pallas_call(
        functools.partial(_ffn_body, F=F, chunk=2 * MXU_DEPTH, final_norm=final_norm, n_pre=len(acts)),
        out_shape=jax.ShapeDtypeStruct((M, D), F32),
        grid=(M // tm,),
        in_specs=in_specs,
        out_specs=pl.BlockSpec((tm, D), lambda m: (m, 0)),
        compiler_params=_cparams(1),
        name="ffn",
    )(*args)


PROJ_CHUNK = 512


def _project(xn, w_ref, col, o_ref, o3_ref=None):
    n = o_ref.shape[1]
    for s0 in range(0, n, PROJ_CHUNK):
        s1 = min(n, s0 + PROJ_CHUNK)
        val = _dot(xn, w_ref[:, col + s0:col + s1])
        o_ref[:, s0:s1] = val
        if o3_ref is not None:
            dh = o3_ref.shape[2]
            for hh in range((s1 - s0) // dh):
                o3_ref[:, s0 // dh + hh, :] = val[:, hh * dh:(hh + 1) * dh]
    return col + n


def _proj_body(x_ref, g_ref, w_ref, *o_refs):
    xn = _rms_rows(x_ref[...], g_ref[...]).astype(BF16)
    col = 0
    for o_ref in o_refs:
        col = _project(xn, w_ref, col, o_ref)


def _rms_proj(x, g, w, widths):
    M, D = x.shape
    tm = min(M, 256)
    assert sum(widths) == w.shape[1] and all(n % LANES == 0 for n in widths)
    return pl.pallas_call(
        _proj_body,
        out_shape=[jax.ShapeDtypeStruct((M, n), F32) for n in widths],
        grid=(M // tm,),
        in_specs=[pl.BlockSpec((tm, D), lambda m: (m, 0)), pl.BlockSpec((1, D), lambda m: (0, 0)),
                  pl.BlockSpec(w.shape, lambda m: (0, 0))],
        out_specs=[pl.BlockSpec((tm, n), lambda m: (m, 0)) for n in widths],
        compiler_params=_cparams(1),
        name="rms_proj",
    )(x, g.reshape(1, D), w)


def _l2_rows(x):
    return x * lax.rsqrt(jnp.sum(x * x, axis=-1, keepdims=True) + EPS)


def _ab_proj_body(x_ref, g_ref, w_ref, cw_ref, qkv_o, q_o, k_o, v_o, z_o, small_o, k3_o, v3_o, tail_o, win_ref,
                  *, tiles_per_seq, nh, dk):
    m = pl.program_id(0)
    tm = x_ref.shape[0]
    cd = qkv_o.shape[1]
    xn = _rms_rows(x_ref[...], g_ref[...]).astype(BF16)

    @pl.when(lax.rem(m, tiles_per_seq) == 0)
    def _():
        win_ref[0:8, :] = jnp.zeros((8, cd), F32)

    kw = cw_ref.shape[0]
    step = nh * dk

    def conv_product(s0):
        win_ref[8:8 + tm, s0:s0 + step] = _dot(xn, w_ref[:, s0:s0 + step])
        tail_o[:, s0:s0 + step] = win_ref[tm:tm + 8, s0:s0 + step]

    def conv_finish(s0):
        s1 = s0 + step
        conv = win_ref[8 - kw + 1:8 - kw + 1 + tm, s0:s1] * cw_ref[0:1, s0:s1]
        for i in range(1, kw):
            conv = conv + win_ref[8 - kw + 1 + i:8 - kw + 1 + i + tm, s0:s1] * cw_ref[i:i + 1, s0:s1]
        win_ref[0:8, s0:s1] = win_ref[tm:tm + 8, s0:s1]
        conv = jax.nn.silu(conv)
        if s0 >= 2 * step:
            qkv_o[:, s0:s1] = conv
        else:
            for h in range(nh):
                xh = _l2_rows(conv[:, h * dk:(h + 1) * dk])
                qkv_o[:, s0 + h * dk:s0 + (h + 1) * dk] = xh * dk ** -0.5 if s0 == 0 else xh

    assert cd == 3 * step
    conv_product(0)
    conv_product(step)
    col = _project(xn, w_ref, cd, q_o)
    conv_finish(0)
    conv_product(2 * step)
    col = _project(xn, w_ref, col, k_o, k3_o)
    conv_finish(step)
    col = _project(xn, w_ref, col, v_o, v3_o)
    col = _project(xn, w_ref, col, z_o)
    conv_finish(2 * step)
    _project(xn, w_ref, col, small_o)


def _ab_proj_prompt(x, g, w, conv_w, B, L, widths, fh, fd, gh, gdk):
    M, D = x.shape
    tm = min(L, 512)
    cd = widths[0]
    assert sum(widths) == w.shape[1] and L % tm == 0
    row = lambda m: (m, 0)
    const = lambda m: (0, 0)
    out_shape = [jax.ShapeDtypeStruct((M, n), F32) for n in widths]
    out_specs = [pl.BlockSpec((tm, n), row) for n in widths]
    for _ in range(2):
        out_shape.append(jax.ShapeDtypeStruct((M, fh, fd), F32))
        out_specs.append(pl.BlockSpec((tm, fh, fd), lambda m: (m, 0, 0)))
    out_shape.append(jax.ShapeDtypeStruct((B, 8, cd), F32))
    out_specs.append(pl.BlockSpec((None, 8, cd), lambda m: (m // (L // tm), 0, 0)))
    return pl.pallas_call(
        functools.partial(_ab_proj_body, tiles_per_seq=L // tm, nh=gh, dk=gdk),
        out_shape=out_shape,
        grid=(M // tm,),
        in_specs=[pl.BlockSpec((tm, D), row), pl.BlockSpec((1, D), const), pl.BlockSpec(w.shape, const),
                  pl.BlockSpec(conv_w.shape, const)],
        out_specs=out_specs,
        scratch_shapes=[pltpu.VMEM((8 + tm, cd), F32)],
        compiler_params=_cparams(1),
        name="ab_proj_prompt",
    )(x, g.reshape(1, D), w, conv_w)


def _c_proj_body(x_ref, g_ref, w_ref, cos_ref, sin_ref, q_o, k_o, v_o, sg_o, *, nh, dk):
    xn = _rms_rows(x_ref[...], g_ref[...]).astype(BF16)
    cos2, sin2 = cos_ref[...], sin_ref[...]
    hpc = PROJ_CHUNK // dk
    for idx, o_ref in enumerate((q_o, k_o)):
        for h0 in range(0, nh, hpc):
            c0 = idx * nh * dk + h0 * dk
            val = _dot(xn, w_ref[:, c0:c0 + hpc * dk])
            for hh in range(hpc):
                xh = val[:, hh * dk:(hh + 1) * dk]
                xh = xh * cos2 + pltpu.roll(xh, dk // 2, 1) * sin2
                if idx == 1:
                    xh = xh * dk ** -0.5
                o_ref[:, (h0 + hh) * dk:(h0 + hh + 1) * dk] = xh.astype(BF16)
    col = _project(xn, w_ref, 2 * nh * dk, v_o)
    n = sg_o.shape[1]
    for s0 in range(0, n, PROJ_CHUNK):
        s1 = min(n, s0 + PROJ_CHUNK)
        sg_o[:, s0:s1] = jax.nn.silu(_dot(xn, w_ref[:, col + s0:col + s1]))


def _c_proj_prompt(x, g, w, cos2, sin2, L, nh, dk, v_w):
    M, D = x.shape
    tm = min(L, 512)
    qk_w = nh * dk
    row = lambda m: (m, 0)
    const = lambda m: (0, 0)
    pos = lambda m: (lax.rem(m, L // tm), 0)
    return pl.pallas_call(
        functools.partial(_c_proj_body, nh=nh, dk=dk),
        out_shape=[jax.ShapeDtypeStruct((M, qk_w), BF16), jax.ShapeDtypeStruct((M, qk_w), BF16),
                   jax.ShapeDtypeStruct((M, v_w), F32), jax.ShapeDtypeStruct((M, v_w), F32)],
        grid=(M // tm,),
        in_specs=[pl.BlockSpec((tm, D), row), pl.BlockSpec((1, D), const), pl.BlockSpec(w.shape, const),
                  pl.BlockSpec((tm, dk), pos), pl.BlockSpec((tm, dk), pos)],
        out_specs=[pl.BlockSpec((tm, qk_w), row), pl.BlockSpec((tm, qk_w), row),
                   pl.BlockSpec((tm, v_w), row), pl.BlockSpec((tm, v_w), row)],
        compiler_params=_cparams(1),
        name="c_proj_prompt",
    )(x, g.reshape(1, D), w, cos2, sin2)


def _cast_body(x_ref, o_ref):
    o_ref[...] = x_ref[...].astype(o_ref.dtype)


def _to_bf16(w):
    cols = w.shape[-1]
    rows = w.size // cols
    tr = 512 if rows % 512 == 0 else rows
    out = pl.pallas_call(
        _cast_body,
        out_shape=jax.ShapeDtypeStruct((rows, cols), BF16),
        grid=(rows // tr,),
        in_specs=[pl.BlockSpec((tr, cols), lambda r: (r, 0))],
        out_specs=pl.BlockSpec((tr, cols), lambda r: (r, 0)),
        compiler_params=_cparams(1),
        name="to_bf16",
    )(w.reshape(rows, cols))
    return out.reshape(w.shape)


def _gates_body(s_ref, p_ref, o_ref, *, nh):
    s = s_ref[...] + p_ref[0:1, :]
    lane = lax.broadcasted_iota(jnp.int32, s.shape, 1)
    logf = jax.nn.log_sigmoid(s)
    g = -jnp.exp(p_ref[1:2, :]) * jax.nn.softplus(s)
    beta = jax.nn.sigmoid(s)
    o_ref[...] = jnp.where(lane < nh, logf, jnp.where(lane < 2 * nh, g, jnp.where(lane < 3 * nh, beta, 0.0)))


def _gates(small, b_f, dt_bias, a_log):
    M = small.shape[0]
    nh = b_f.shape[0]
    tm = min(M, 2048)
    pad = jnp.zeros((LANES - 3 * nh,), F32)
    p = jnp.zeros((8, LANES), F32)
    p = p.at[0].set(jnp.concatenate([b_f, dt_bias, jnp.zeros((nh,), F32), pad]))
    p = p.at[1].set(jnp.concatenate([jnp.zeros((nh,), F32), a_log, jnp.zeros((nh,), F32), pad]))
    return pl.pallas_call(
        functools.partial(_gates_body, nh=nh),
        out_shape=jax.ShapeDtypeStruct((M, LANES), F32),
        grid=(M // tm,),
        in_specs=[pl.BlockSpec((tm, LANES), lambda m: (m, 0)), pl.BlockSpec((8, LANES), lambda m: (0, 0))],
        out_specs=pl.BlockSpec((tm, LANES), lambda m: (m, 0)),
        compiler_params=_cparams(1),
        name="gates",
    )(small, p)


def _cumsum_body(x_ref, o_ref):
    L = x_ref.shape[1]
    r = lax.broadcasted_iota(jnp.int32, (LANES, LANES), 0)
    c = lax.broadcasted_iota(jnp.int32, (LANES, LANES), 1)
    upper = (r <= c).astype(F32)
    carry = jnp.zeros((x_ref.shape[0], 1), F32)
    for j in range(L // LANES):
        cs = _dot(x_ref[:, j * LANES:(j + 1) * LANES], upper, HI) + carry
        o_ref[:, j * LANES:(j + 1) * LANES] = cs
        carry = cs[:, LANES - 1:LANES]


def _cumsum_rows(x, seg):
    R, M = x.shape
    return pl.pallas_call(
        _cumsum_body,
        out_shape=jax.ShapeDtypeStruct((R, M), F32),
        grid=(M // seg,),
        in_specs=[pl.BlockSpec((R, seg), lambda b: (0, b))],
        out_specs=pl.BlockSpec((R, seg), lambda b: (0, b)),
        compiler_params=_cparams(1),
        name="cumsum_rows",
    )(x)


def _fox_body(q_ref, k_ref, v_ref, c_ref, o_ref, vt_ref, cb_ref, acc_ref, m_ref, *, t, hg, dh, scale):
    i = pl.program_id(2)
    L = k_ref.shape[0]
    log2e = math.log2(math.e)
    heads = range(hg)

    @pl.when(i == 0)
    def _():
        for g in heads:
            for jj in range(L // t):
                vt_ref[g, 0:dh, jj * t:(jj + 1) * t] = v_ref[jj * t:(jj + 1) * t, g * dh:(g + 1) * dh].T.astype(BF16)
            vt_ref[g, dh:, :] = jnp.ones((vt_ref.shape[1] - dh, L), BF16)
            for jj in range(L // LANES):
                row = c_ref[g, :, jj * LANES:(jj + 1) * LANES] * log2e
                cb_ref[g, jj * LANES:(jj + 1) * LANES, :] = jnp.broadcast_to(row, (LANES, LANES)).T

    q0 = pl.multiple_of(i * t, t)
    qs = [q_ref[:, g * dh:(g + 1) * dh].astype(BF16) for g in heads]
    r2 = [c_ref[g, :, pl.ds(q0, t)][:, t - 1:t] * log2e for g in heads]
    m_ref[...] = jnp.full_like(m_ref, NEG)
    acc_ref[...] = jnp.zeros_like(acc_ref)

    def step(j, masked):
        k0 = pl.multiple_of(j * t, t)
        qk = [_dot_nt(k_ref[pl.ds(k0, t), g * dh:(g + 1) * dh].astype(BF16), qs[g]) for g in heads]
        ps, alphas = [], []
        for g in heads:
            bias = r2[g] - cb_ref[g, pl.ds(k0, t), :]
            st = qk[g] * (scale * log2e) + jnp.concatenate([bias] * (t // LANES), axis=1)
            if masked:
                key = lax.broadcasted_iota(jnp.int32, (t, t), 0)
                qry = lax.broadcasted_iota(jnp.int32, (t, t), 1)
                st = jnp.where(key <= qry, st, NEG)
            m_prev = m_ref[g]
            m_new = jnp.maximum(m_prev, jnp.max(st, axis=0, keepdims=True))
            ps.append(jnp.exp2(st - m_new).astype(BF16))
            alphas.append(jnp.exp2(m_prev - m_new))
            m_ref[g] = m_new
        for g in heads:
            acc_ref[g] = acc_ref[g] * alphas[g] + _dot(vt_ref[g, :, pl.ds(k0, t)], ps[g])

    def loop_body(j, carry):
        step(j, False)
        return carry

    lax.fori_loop(0, i, loop_body, 0)
    step(i, True)
    for g in heads:
        acc = acc_ref[g]
        o_ref[:, g * dh:(g + 1) * dh] = (acc[0:dh] / acc[dh:dh + 1]).T


def _fox_prompt(q, k, v, c_rows, B, L, H):
    M, W = q.shape
    Dh = W // H
    t = min(L, 512)
    nq = L // t
    hg = 2 if H % 2 == 0 else 1
    kernel = functools.partial(_fox_body, t=t, hg=hg, dh=Dh, scale=Dh ** -0.5)
    return pl.pallas_call(
        kernel,
        out_shape=jax.ShapeDtypeStruct((M, W), F32),
        grid=(B, H // hg, nq),
        in_specs=[
            pl.BlockSpec((t, hg * Dh), lambda b, h, i: (b * nq + i, h)),
            pl.BlockSpec((L, hg * Dh), lambda b, h, i: (b, h)),
            pl.BlockSpec((L, hg * Dh), lambda b, h, i: (b, h)),
            pl.BlockSpec((hg, 1, L), lambda b, h, i: (b * (H // hg) + h, 0, 0)),
        ],
        out_specs=pl.BlockSpec((t, hg * Dh), lambda b, h, i: (b * nq + i, h)),
        scratch_shapes=[pltpu.VMEM((hg, Dh + 16, L), BF16), pltpu.VMEM((hg, L, LANES), F32),
                        pltpu.VMEM((hg, Dh + 16, t), F32), pltpu.VMEM((hg, 1, t), F32)],
        compiler_params=_cparams(3),
        name="fox_prompt",
    )(q, k, v, c_rows)


def _split2(x):
    hi = x.astype(BF16)
    return hi, (x - hi.astype(F32)).astype(BF16)


def _split3(x):
    h1 = x.astype(BF16)
    r1 = x - h1.astype(F32)
    h2 = r1.astype(BF16)
    return h1, h2, (r1 - h2.astype(F32)).astype(BF16)


def _dot3(a, b):
    (ah, al), (bh, bl) = a, b
    return _dot(ah, bh) + (_dot(ah, bl) + _dot(al, bh))


def _unit_lower_inverses(a_list, n):
    ri = lax.broadcasted_iota(jnp.int32, (n, n), 0)
    ci = lax.broadcasted_iota(jnp.int32, (n, n), 1)
    eye = (ri == ci).astype(F32)
    nm = len(a_list)
    diag16 = (ri >> 4) == (ci >> 4)
    pw = [jnp.where(diag16, a, 0.0) for a in a_list]
    t = [eye - p for p in pw]
    for _ in range(3):
        pw16 = [p.astype(BF16) for p in pw]
        pw = [_dot(p, p) for p in pw16]
        pn16 = [p.astype(BF16) for p in pw]
        t = [t[m] + _dot(t[m].astype(BF16), pn16[m]) for m in range(nm)]
    size = 16
    while size < GDN_CHUNK:
        sh = size.bit_length() - 1
        off = ((ri >> (sh + 1)) == (ci >> (sh + 1))) & ((ri >> sh) == (ci >> sh) + 1)
        t16 = [x.astype(BF16) for x in t]
        mid = [_dot(jnp.where(off, a_list[m], 0.0).astype(BF16), t16[m]) for m in range(nm)]
        t = [t[m] - _dot(t16[m], mid[m].astype(BF16)) for m in range(nm)]
        size *= 2
    ts = [_split2(x) for x in t]
    res = [(eye - t[m]) - _dot3(_split2(a_list[m]), ts[m]) for m in range(nm)]
    return [t[m] + _dot(ts[m][0], res[m].astype(BF16)) for m in range(nm)]


def _gdn_body(qkv_ref, z_ref, gc_ref, *rest, nh, dk, dv, nseq):
    gr_refs = rest[:nseq]
    ng_ref, o_ref, s_out_ref, s_ref, m16_ref = rest[nseq:]
    j = pl.program_id(1)
    T = qkv_ref.shape[1]
    C = GDN_CHUNK
    nc = T // C

    ri = lax.broadcasted_iota(jnp.int32, (T, T), 0)
    ci = lax.broadcasted_iota(jnp.int32, (T, T), 1)
    csh = C.bit_length() - 1
    same = (ri >> csh) == (ci >> csh)
    tril = same & (ri >= ci)
    strict = same & (ri > ci)

    @pl.when((pl.program_id(0) == 0) & (j == 0))
    def _():
        m16_ref[0] = tril.astype(BF16)
        m16_ref[1] = (same & (ri <= ci)).astype(BF16)

    @pl.when(j == 0)
    def _():
        s_ref[...] = jnp.zeros_like(s_ref)

    rowi = lax.broadcasted_iota(jnp.int32, (T, 1), 0)
    units = [(s, h) for s in range(nseq) for h in range(nh)]
    nu = len(units)
    gcols, grows = [], []
    for s in range(nseq):
        gcols.append(sum(_dot(m16_ref[0], part) for part in _split3(gc_ref[s])))
        grows.append(sum(_dot(part, m16_ref[1]) for part in _split3(gr_refs[s][...])))

    q, k, beta, gcc, gcr, gam, kb, k16 = [], [], [], [], [], [], [], []
    for u, (s, h) in enumerate(units):
        q.append(qkv_ref[s, :, h * dk:(h + 1) * dk])
        k.append(qkv_ref[s, :, nh * dk + h * dk:nh * dk + (h + 1) * dk])
        beta.append(gc_ref[s, :, 2 * nh + h:2 * nh + h + 1])
        gcc.append(gcols[s][:, nh + h:nh + h + 1])
        gcr.append(grows[s][nh + h:nh + h + 1, :])
        gam.append(jnp.exp(jnp.where(tril, gcc[u] - gcr[u], NEG)))
        kb.append(k[u] * beta[u])
        k16.append(k[u].astype(BF16))
    a = [jnp.where(strict, _dot_nt(kb[u].astype(BF16), k16[u]) * gam[u], 0.0) for u in range(nu)]
    tinv = _unit_lower_inverses(a, T)
    x = []
    for u, (s, h) in enumerate(units):
        v = qkv_ref[s, :, 2 * nh * dk + h * dv:2 * nh * dk + (h + 1) * dv]
        rhs = jnp.concatenate([v * beta[u], kb[u] * jnp.exp(gcc[u])], axis=-1)
        x.append(_dot3(_split2(tinv[u]), _split2(rhs)))
    attn, qg, kd, egl = [], [], [], []
    for u in range(nu):
        attn.append((_dot_nt(q[u].astype(BF16), k16[u]) * gam[u]).astype(BF16))
        qg.append((q[u] * jnp.exp(gcc[u])).astype(BF16))
        gl = gcr[u][:, C - 1:C]
        for cidx in range(1, nc):
            gl = jnp.where(rowi >= cidx * C, gcr[u][:, (cidx + 1) * C - 1:(cidx + 1) * C], gl)
        kd.append((k[u] * jnp.exp(gl - gcc[u])).astype(BF16))
        egl.append(jnp.exp(gl))
    S = [s_ref[s, h] for s, h in units]
    outs = [[] for _ in units]
    for cidx in range(nc):
        sl = slice(cidx * C, (cidx + 1) * C)
        S16 = [st.astype(BF16) for st in S]
        vn16 = [(x[u][sl, :dv] - _dot(x[u][sl, dv:].astype(BF16), S16[u])).astype(BF16) for u in range(nu)]
        for u in range(nu):
            outs[u].append(_dot(qg[u][sl], S16[u]) + _dot(attn[u][sl, cidx * C:(cidx + 1) * C], vn16[u]))
        S = [S[u] * egl[u][cidx * C:cidx * C + 1] + _dot_tn(kd[u][sl], vn16[u]) for u in range(nu)]
    for u, (s, h) in enumerate(units):
        s_ref[s, h] = S[u]
        o = _rms_rows(jnp.concatenate(outs[u], axis=0), ng_ref[...])
        o_ref[s, :, h * dv:(h + 1) * dv] = o * jax.nn.silu(z_ref[s, :, h * dv:(h + 1) * dv])

    @pl.when(j == pl.num_programs(1) - 1)
    def _():
        s_out_ref[...] = s_ref[...]


def _gdn_prompt(qkv, z, gates_col, gates_row, norm_g, B, L, nh, dk, dv):
    M, CD = qkv.shape
    T = 2 * GDN_CHUNK
    nb = L // T
    nseq = 2 if B % 2 == 0 else 1
    kernel = functools.partial(_gdn_body, nh=nh, dk=dk, dv=dv, nseq=nseq)

    def row_map(s):
        return lambda p, j: (0, (p * nseq + s) * nb + j)

    per_seq = lambda p, j: (p, j, 0)
    go, S = pl.pallas_call(
        kernel,
        out_shape=[jax.ShapeDtypeStruct((B, L, nh * dv), F32), jax.ShapeDtypeStruct((B, nh, dk, dv), F32)],
        grid=(B // nseq, nb),
        in_specs=[
            pl.BlockSpec((nseq, T, CD), per_seq),
            pl.BlockSpec((nseq, T, nh * dv), per_seq),
            pl.BlockSpec((nseq, T, LANES), per_seq),
        ] + [pl.BlockSpec((16, T), row_map(s)) for s in range(nseq)] + [
            pl.BlockSpec((1, dv), lambda p, j: (0, 0)),
        ],
        out_specs=[
            pl.BlockSpec((nseq, T, nh * dv), per_seq),
            pl.BlockSpec((nseq, nh, dk, dv), lambda p, j: (p, 0, 0, 0)),
        ],
        scratch_shapes=[pltpu.VMEM((nseq, nh, dk, dv), F32), pltpu.VMEM((2, T, T), BF16)],
        compiler_params=_cparams(2),
        name="gdn_prompt",
    )(qkv.reshape(B, L, CD), z.reshape(B, L, nh * dv), gates_col.reshape(B, L, LANES),
      *([gates_row] * nseq), norm_g.reshape(1, dv))
    return go.reshape(M, nh * dv), S


def _rope(x, cos2, sin2, nh, dk):
    outs = []
    for h in range(nh):
        xh = x[:, h * dk:(h + 1) * dk]
        outs.append(xh * cos2 + pltpu.roll(xh, dk // 2, 1) * sin2)
    return outs


def _rope_tables(pos, freqs):
    ang = pos * freqs
    cos, sin = jnp.cos(ang), jnp.sin(ang)
    return jnp.concatenate([cos, cos], axis=-1), jnp.concatenate([-sin, sin], axis=-1)


def _head_layernorm(o):
    mu = jnp.mean(o, axis=-1, keepdims=True)
    oc = o - mu
    return oc * lax.rsqrt(jnp.mean(oc * oc, axis=-1, keepdims=True) + EPS)


def _log_gamma(h):
    return math.log(1.0 - 2.0 ** (-5.0 - h))


def _rope_table_body(fr_ref, cos_o, sin_o):
    C = cos_o.shape[0]
    rowi = lax.broadcasted_iota(jnp.int32, (C, 1), 0)
    pos = (pl.program_id(0) * C + rowi).astype(F32)
    cos_o[...], sin_o[...] = _rope_tables(pos, fr_ref[...])


def _rope_table(L, freqs):
    C = min(L, 128)
    dk = 2 * freqs.shape[1]
    return pl.pallas_call(
        _rope_table_body,
        out_shape=[jax.ShapeDtypeStruct((L, dk), F32)] * 2,
        grid=(L // C,),
        in_specs=[pl.BlockSpec(freqs.shape, lambda c: (0, 0))],
        out_specs=[pl.BlockSpec((C, dk), lambda c: (c, 0))] * 2,
        compiler_params=_cparams(1),
        name="rope_table",
    )(freqs)


def _ret_body(q_ref, k_ref, v_ref, sg_ref, gn_ref, o_ref, s_out_ref, s_ref, dmat_ref, *, nh, dk, dv):
    b = pl.program_id(0)
    j = pl.program_id(1)
    C = q_ref.shape[0]
    rowf = lax.broadcasted_iota(jnp.int32, (C, 1), 0).astype(F32)

    @pl.when((b == 0) & (j == 0))
    def _():
        ri = lax.broadcasted_iota(jnp.int32, (C, C), 0)
        ci = lax.broadcasted_iota(jnp.int32, (C, C), 1)
        diff = (ri - ci).astype(F32)
        for h in range(nh):
            dmat_ref[h] = jnp.exp(jnp.where(ri >= ci, diff * _log_gamma(h), NEG))

    @pl.when(j == 0)
    def _():
        s_ref[...] = jnp.zeros_like(s_ref)

    heads = range(nh)
    q16 = [q_ref[:, h * dk:(h + 1) * dk] for h in heads]
    k16 = [k_ref[:, h * dk:(h + 1) * dk] for h in heads]
    v = [v_ref[:, h * dv:(h + 1) * dv] for h in heads]
    S = [s_ref[h] for h in heads]
    qk = [_dot_nt(q16[h], k16[h]) for h in heads]
    cross = [_dot(q16[h], S[h].astype(BF16)) for h in heads]
    for h in heads:
        vd = (v[h] * jnp.exp(_log_gamma(h) * (C - 1.0 - rowf))).astype(BF16)
        s_ref[h] = S[h] * math.exp(_log_gamma(h) * C) + _dot_tn(k16[h], vd)
    inner = [_dot((qk[h] * dmat_ref[h]).astype(BF16), v[h].astype(BF16)) for h in heads]
    for h in heads:
        o = inner[h] + cross[h] * jnp.exp(_log_gamma(h) * (rowf + 1.0))
        o = _head_layernorm(o) * gn_ref[:, h * dv:(h + 1) * dv]
        o_ref[:, h * dv:(h + 1) * dv] = sg_ref[:, h * dv:(h + 1) * dv] * o

    @pl.when(j == pl.num_programs(1) - 1)
    def _():
        s_out_ref[...] = s_ref[...]


def _ret_prompt(q16, k16, v, sgate, gnorm, B, L, nh, dk, dv):
    M = q16.shape[0]
    C = 128
    nb = L // C
    kernel = functools.partial(_ret_body, nh=nh, dk=dk, dv=dv)
    tok = lambda b, j: (b * nb + j, 0)
    return pl.pallas_call(
        kernel,
        out_shape=[jax.ShapeDtypeStruct((M, nh * dv), F32), jax.ShapeDtypeStruct((B, nh, dk, dv), F32)],
        grid=(B, nb),
        in_specs=[
            pl.BlockSpec((C, nh * dk), tok),
            pl.BlockSpec((C, nh * dk), tok),
            pl.BlockSpec((C, nh * dv), tok),
            pl.BlockSpec((C, nh * dv), tok),
            pl.BlockSpec((1, nh * dv), lambda b, j: (0, 0)),
        ],
        out_specs=[
            pl.BlockSpec((C, nh * dv), tok),
            pl.BlockSpec((None, nh, dk, dv), lambda b, j: (b, 0, 0, 0)),
        ],
        scratch_shapes=[pltpu.VMEM((nh, dk, dv), F32), pltpu.VMEM((nh, C, C), F32)],
        compiler_params=_cparams(2),
        name="ret_prompt",
    )(q16, k16, v, sgate, gnorm.reshape(1, nh * dv))


def _page_sums_body(x_ref, o_ref, *, nh):
    n = x_ref.shape[1]
    fi = lax.broadcasted_iota(jnp.int32, (n, n), 0)
    fo = lax.broadcasted_iota(jnp.int32, (n, n), 1)
    same_head = (fi & (nh - 1)) == (fo & (nh - 1))
    parts = _split3(x_ref[...])
    o_ref[:, :n] = sum(_dot(part, (same_head & (fi > fo)).astype(BF16)) for part in parts)
    o_ref[:, n:] = sum(_dot(part, same_head.astype(BF16)) for part in parts)


def _page_sums(lf_pages, nh):
    P, n = lf_pages.shape
    tp = 512 if P % 512 == 0 else P
    return pl.pallas_call(
        functools.partial(_page_sums_body, nh=nh),
        out_shape=jax.ShapeDtypeStruct((P, 2 * n), F32),
        grid=(P // tp,),
        in_specs=[pl.BlockSpec((tp, n), lambda p: (p, 0))],
        out_specs=pl.BlockSpec((tp, 2 * n), lambda p: (p, 0)),
        compiler_params=_cparams(1),
        name="page_sums",
    )(lf_pages)


def _paged_body(pt_ref, *refs, G, nh, scale):
    k_refs = refs[:G]
    v_refs = refs[G:2 * G]
    w_refs = refs[2 * G:3 * G]
    q_ref, kn_ref, vn_ref, lfn_ref, o_ref, m_ref, l_ref, acc_ref, carry_ref = refs[3 * G:]
    b_idx = pl.program_id(0)
    s_idx = pl.program_id(1)
    ns = pl.num_programs(1)
    n = k_refs[0].shape[0]
    dh = q_ref.shape[1]
    sub = w_refs[0].shape[0]

    @pl.when(s_idx == 0)
    def _():
        m_ref[...] = jnp.full_like(m_ref, NEG)
        l_ref[...] = jnp.zeros_like(l_ref)
        acc_ref[...] = jnp.zeros_like(acc_ref)
        carry_ref[...] = jnp.zeros_like(carry_ref)

    q = q_ref[...]
    q8 = jnp.concatenate([q, jnp.zeros((8 - nh, dh), F32)], axis=0).astype(BF16)
    lfn = jnp.concatenate([lfn_ref[...], jnp.zeros((8 - nh, 1), F32)], axis=0)
    hrow = lax.broadcasted_iota(jnp.int32, (8, n), 0)
    hcol = lax.broadcasted_iota(jnp.int32, (8, n), 1) & (nh - 1)
    match = hrow == hcol
    carry = carry_ref[...]
    s_parts = [None] * G
    for i in reversed(range(G)):
        page_id = pt_ref[b_idx, (ns - 1 - s_idx) * G + i]
        wt = w_refs[i][pl.ds(lax.rem(page_id, sub), 1), :]
        bias = wt[:, :n] + carry
        carry = carry + wt[:, n:]
        s = _dot_nt(q8, k_refs[i][...].astype(BF16)) * scale + bias + lfn
        s_parts[i] = jnp.where(match, s, NEG)
    carry_ref[...] = carry
    m_run, l_run, acc_run = m_ref[...], l_ref[...], acc_ref[...]
    half = max(G // 2, 1)
    for grp in (range(half, G), range(0, half)) if G > 1 else (range(G),):
        sg = [s_parts[i] for i in grp]
        m_new = jnp.maximum(m_run, jnp.max(functools.reduce(jnp.maximum, sg), axis=-1, keepdims=True))
        alpha = jnp.exp(m_run - m_new)
        pg = [jnp.exp(s - m_new) for s in sg]
        l_run = alpha * l_run + jnp.sum(functools.reduce(jnp.add, pg), axis=-1, keepdims=True)
        pv = functools.reduce(jnp.add, [_dot(p.astype(BF16), v_refs[i][...].astype(BF16)) for p, i in zip(pg, grp)])
        acc_run = alpha * acc_run + pv
        m_run = m_new
    m_ref[...], l_ref[...], acc_ref[...] = m_run, l_run, acc_run

    @pl.when(s_idx == pl.num_programs(1) - 1)
    def _():
        qf = q.astype(BF16).astype(F32)
        kn = kn_ref[...].astype(BF16).astype(F32)
        s_new = jnp.sum(qf * kn, axis=-1, keepdims=True) * scale
        m_prev = m_ref[0:nh, :]
        m_new = jnp.maximum(m_prev, s_new)
        alpha = jnp.exp(m_prev - m_new)
        p_new = jnp.exp(s_new - m_new)
        l_fin = alpha * l_ref[0:nh, :] + p_new
        vn = vn_ref[...].astype(BF16).astype(F32)
        num = alpha * acc_ref[0:nh, :] + p_new.astype(BF16).astype(F32) * vn
        o_ref[...] = num / l_fin


def _fox_sample(q, k_new, v_new, lf_new, k_pages, v_pages, page_sums, page_table, nh):
    Bn, _, dh = q.shape
    n_pages = page_table.shape[1]
    n_pool, n, _ = k_pages.shape
    G = 16 if n_pages % 16 == 0 else (8 if n_pages % 8 == 0 else 1)
    ns = n_pages // G
    sub = 8
    assert n_pool % sub == 0
    page_sums = page_sums.reshape(n_pool // sub, sub, 2 * n)

    def page_map(i):
        return lambda b, s, pt: (pt[b, (ns - 1 - s) * G + i], 0, 0)

    def sums_map(i):
        return lambda b, s, pt: (lax.div(pt[b, (ns - 1 - s) * G + i], sub), 0, 0)

    per_b = lambda b, s, pt: (b, 0, 0)
    grid_spec = pltpu.PrefetchScalarGridSpec(
        num_scalar_prefetch=1,
        grid=(Bn, ns),
        in_specs=[pl.BlockSpec((None, n, dh), page_map(i)) for i in range(G)]
        + [pl.BlockSpec((None, n, dh), page_map(i)) for i in range(G)]
        + [pl.BlockSpec((None, sub, 2 * n), sums_map(i)) for i in range(G)]
        + [pl.BlockSpec((None, nh, dh), per_b), pl.BlockSpec((None, nh, dh), per_b),
           pl.BlockSpec((None, nh, dh), per_b), pl.BlockSpec((None, nh, 1), per_b)],
        out_specs=pl.BlockSpec((None, nh, dh), per_b),
        scratch_shapes=[pltpu.VMEM((8, 1), F32), pltpu.VMEM((8, 1), F32), pltpu.VMEM((8, dh), F32),
                        pltpu.VMEM((1, n), F32)],
    )
    return pl.pallas_call(
        functools.partial(_paged_body, G=G, nh=nh, scale=dh ** -0.5),
        out_shape=jax.ShapeDtypeStruct((Bn, nh, dh), F32),
        grid_spec=grid_spec,
        compiler_params=_cparams(2),
        name="fox_sample",
    )(page_table, *([k_pages] * G), *([v_pages] * G), *([page_sums] * G), q, k_new, v_new, lf_new)


def _pad8(row):
    return jnp.concatenate([row, jnp.zeros((8 - row.shape[0], row.shape[1]), row.dtype)], axis=0)


def _gdn_sample_body(cin_ref, prev_ref, z_ref, gt_ref, cw_ref, ng_ref, s0_ref, o_ref, cnew_ref, s_out_ref, *, nh, dk, dv):
    u = cin_ref[...]
    prev = prev_ref[...]
    w = cw_ref[...]
    kw = w.shape[0]
    conv = u * w[kw - 1:kw]
    for i in range(kw - 1):
        conv = conv + prev[i:i + 1] * w[i:i + 1]
    cnew_ref[0:kw - 2, :] = prev[1:kw - 1]
    cnew_ref[kw - 2:kw - 1, :] = u
    conv = jax.nn.silu(conv)
    gt = gt_ref[...]
    for h in range(nh):
        q = conv[:, h * dk:(h + 1) * dk]
        k = conv[:, nh * dk + h * dk:nh * dk + (h + 1) * dk]
        v = conv[:, 2 * nh * dk + h * dv:2 * nh * dk + (h + 1) * dv]
        q = q * lax.rsqrt(jnp.sum(q * q, axis=-1, keepdims=True) + EPS) * dk ** -0.5
        k = k * lax.rsqrt(jnp.sum(k * k, axis=-1, keepdims=True) + EPS)
        g = gt[:, nh + h:nh + h + 1]
        beta = gt[:, 2 * nh + h:2 * nh + h + 1]
        eg = jnp.exp(g)
        S = s0_ref[h]
        lhs = _pad8(jnp.concatenate([k * beta * eg, q * eg], axis=0)).astype(BF16)
        r = _dot(lhs, S.astype(BF16))
        v_new = v * beta - r[0:1]
        qk = jnp.sum(q.astype(BF16).astype(F32) * k.astype(BF16).astype(F32), axis=-1, keepdims=True)
        vn16 = v_new.astype(BF16)
        o = r[1:2] + qk.astype(BF16).astype(F32) * vn16.astype(F32)
        s_out_ref[h] = S * eg + _dot_tn(_pad8(k).astype(BF16), _pad8(vn16))
        o = _rms_rows(o, ng_ref[...])
        o_ref[:, h * dv:(h + 1) * dv] = o * jax.nn.silu(z_ref[:, h * dv:(h + 1) * dv])


def _gdn_sample(conv_in, conv_prev, z, gates_col, conv_w, norm_g, S0, nh, dk, dv):
    Bn, _, CD = conv_in.shape
    kw = conv_w.shape[0]
    per_b3 = lambda b: (b, 0, 0)
    kernel = functools.partial(_gdn_sample_body, nh=nh, dk=dk, dv=dv)
    return pl.pallas_call(
        kernel,
        out_shape=[jax.ShapeDtypeStruct((Bn, 1, nh * dv), F32), jax.ShapeDtypeStruct((Bn, kw - 1, CD), F32),
                   jax.ShapeDtypeStruct((Bn, nh, dk, dv), F32)],
        grid=(Bn,),
        in_specs=[
            pl.BlockSpec((None, 1, CD), per_b3),
            pl.BlockSpec((None, kw - 1, CD), per_b3),
            pl.BlockSpec((None, 1, nh * dv), per_b3),
            pl.BlockSpec((None, 1, LANES), per_b3),
            pl.BlockSpec(conv_w.shape, lambda b: (0, 0)),
            pl.BlockSpec((1, dv), lambda b: (0, 0)),
            pl.BlockSpec((None, nh, dk, dv), lambda b: (b, 0, 0, 0)),
        ],
        out_specs=[
            pl.BlockSpec((None, 1, nh * dv), per_b3),
            pl.BlockSpec((None, kw - 1, CD), per_b3),
            pl.BlockSpec((None, nh, dk, dv), lambda b: (b, 0, 0, 0)),
        ],
        compiler_params=_cparams(1),
        name="gdn_sample",
    )(conv_in, conv_prev, z, gates_col, conv_w, norm_g.reshape(1, dv), S0)


def _ret_sample_body(q_ref, k_ref, v_ref, gate_ref, fr_ref, gn_ref, s0_ref, o_ref, s_out_ref, *, nh, dk, dv, pos):
    cos2, sin2 = _rope_tables(jnp.full((1, 1), pos, F32), fr_ref[...])
    qs = _rope(q_ref[...], cos2, sin2, nh, dk)
    ks = _rope(k_ref[...], cos2, sin2, nh, dk)
    for h in range(nh):
        lg = _log_gamma(h)
        q = qs[h]
        k = ks[h] * dk ** -0.5
        v16 = v_ref[:, h * dv:(h + 1) * dv].astype(BF16)
        S = s0_ref[h]
        qk = jnp.sum(q.astype(BF16).astype(F32) * k.astype(BF16).astype(F32), axis=-1, keepdims=True)
        inner = qk.astype(BF16).astype(F32) * v16.astype(F32)
        cross = _dot(_pad8(q * math.exp(lg)).astype(BF16), S.astype(BF16))[0:1]
        s_out_ref[h] = S * math.exp(lg) + _dot_tn(_pad8(k).astype(BF16), _pad8(v16))
        o = _head_layernorm(inner + cross) * gn_ref[:, h * dv:(h + 1) * dv]
        o_ref[:, h * dv:(h + 1) * dv] = jax.nn.silu(gate_ref[:, h * dv:(h + 1) * dv]) * o


def _ret_sample(q, k, v, gate, freqs, gnorm, S0, pos, nh, dk, dv):
    Bn = q.shape[0]
    per_b3 = lambda b: (b, 0, 0)
    kernel = functools.partial(_ret_sample_body, nh=nh, dk=dk, dv=dv, pos=float(pos))
    return pl.pallas_call(
        kernel,
        out_shape=[jax.ShapeDtypeStruct((Bn, 1, nh * dv), F32), jax.ShapeDtypeStruct((Bn, nh, dk, dv), F32)],
        grid=(Bn,),
        in_specs=[
            pl.BlockSpec((None, 1, nh * dk), per_b3),
            pl.BlockSpec((None, 1, nh * dk), per_b3),
            pl.BlockSpec((None, 1, nh * dv), per_b3),
            pl.BlockSpec((None, 1, nh * dv), per_b3),
            pl.BlockSpec((1, dk // 2), lambda b: (0, 0)),
            pl.BlockSpec((1, nh * dv), lambda b: (0, 0)),
            pl.BlockSpec((None, nh, dk, dv), lambda b: (b, 0, 0, 0)),
        ],
        out_specs=[
            pl.BlockSpec((None, 1, nh * dv), per_b3),
            pl.BlockSpec((None, nh, dk, dv), lambda b: (b, 0, 0, 0)),
        ],
        compiler_params=_cparams(1),
        name="ret_sample",
    )(q, k, v, gate, freqs, gnorm.reshape(1, nh * dv), S0)


def kernel(x_prompt, x_sample, cache_fox_k, cache_fox_v, cache_fox_logf, page_table, state_gdn_conv, state_gdn_S, state_ret_S, norm_g, final_norm_g, ffn_w_gu, ffn_w_down, ab_w_in, ab_w_out, fox_b_f, gdn_conv_w, gdn_A_log, gdn_dt_bias, gdn_norm_g, c_w_in, c_w_out, ret_norm_g):
    B, L, D = x_prompt.shape
    Bn, Ls, _ = x_sample.shape
    assert Ls == 1, "the sample group decodes one token per sequence"
    depth = norm_g.shape[0]
    _, n_pool, page, fh, fd = cache_fox_k.shape
    fw = fh * fd
    _, _, gh, gdk, gdv = state_gdn_S.shape
    conv_dim = gdn_conv_w.shape[2]
    kw = gdn_conv_w.shape[1]
    _, _, rh, rdk, rdv = state_ret_S.shape
    n_pages = page_table.shape[1]
    assert fh == gh and 3 * fh <= 16 and fh & (fh - 1) == 0

    xp = x_prompt.reshape(B * L, D)
    xs = x_sample.reshape(Bn, D)
    w_gu16 = _to_bf16(ffn_w_gu)
    w_down16 = _to_bf16(ffn_w_down)
    freqs = (ROPE_BASE ** (-jnp.arange(rdk // 2, dtype=F32) / (rdk // 2))).reshape(1, rdk // 2)
    cos2, sin2 = _rope_table(L, freqs)

    fkp, fvp, flp, fks, fvs, fls = [], [], [], [], [], []
    gcp, gsp, gcs, gss = [], [], [], []
    rsp, rss = [], []
    for li in range(depth):
        xp = _ffn(xp, norm_g[li, 0], w_gu16, w_down16, li, 0)
        xs = _ffn(xs, norm_g[li, 0], w_gu16, w_down16, li, 0)
        j = li // 2
        if li % 2 == 0:
            w_in = ab_w_in[j]
            o0 = 3 * fw
            zw = gh * gdv
            c0 = o0 + fh
            w_cat = _to_bf16(jnp.concatenate(
                [w_in[:, c0:c0 + conv_dim], w_in[:, :o0], w_in[:, c0 + conv_dim:c0 + conv_dim + zw], w_in[:, o0:c0],
                 w_in[:, c0 + conv_dim + zw:], jnp.zeros((D, LANES - 3 * fh), F32)], axis=1))
            widths = (conv_dim, fw, fw, fw, zw, LANES)
            w_out16 = ab_w_out[j].astype(BF16)

            gqkv, q, k, v, z, small, k3, v3, tails = _ab_proj_prompt(
                xp, norm_g[li, 1], w_cat, gdn_conv_w[j], B, L, widths, fh, fd, gh, gdk)
            gates = _gates(small, fox_b_f[j], gdn_dt_bias[j], gdn_A_log[j])
            gates_row = gates[:, :16].T
            c_rows = _cumsum_rows(gates_row, L)[:fh].reshape(fh, B, L).transpose(1, 0, 2).reshape(B * fh, 1, L)
            fo = _fox_prompt(q, k, v, c_rows, B, L, fh)
            go, S_p = _gdn_prompt(gqkv, z, gates, gates_row, gdn_norm_g[j], B, L, gh, gdk, gdv)
            pre_p = ([fo, go], w_out16)
            fkp.append(k3.reshape(B, L, fh, fd))
            fvp.append(v3.reshape(B, L, fh, fd))
            flp.append(gates[:, :fh].reshape(B, L, fh))
            gcp.append(tails[:, 8 - (kw - 1):])
            gsp.append(S_p)

            cin, q, k, v, z, small = _rms_proj(xs, norm_g[li, 1], w_cat, widths)
            gates = _gates(small, fox_b_f[j], gdn_dt_bias[j], gdn_A_log[j])
            sums = _page_sums(cache_fox_logf[j].reshape(n_pool, page * fh), fh)
            fo = _fox_sample(q.reshape(Bn, fh, fd), k.reshape(Bn, fh, fd), v.reshape(Bn, fh, fd),
                             gates[:, :fh].reshape(Bn, fh, 1),
                             cache_fox_k[j].reshape(n_pool, page * fh, fd), cache_fox_v[j].reshape(n_pool, page * fh, fd),
                             sums, page_table, fh)
            go, conv_s, S_s = _gdn_sample(cin.reshape(Bn, 1, conv_dim), state_gdn_conv[j], z.reshape(Bn, 1, gh * gdv),
                                          gates.reshape(Bn, 1, LANES), gdn_conv_w[j], gdn_norm_g[j], state_gdn_S[j],
                                          gh, gdk, gdv)
            pre_s = ([fo.reshape(Bn, fw), go.reshape(Bn, gh * gdv)], w_out16)
            fks.append(k.reshape(Bn, 1, fh, fd))
            fvs.append(v.reshape(Bn, 1, fh, fd))
            fls.append(gates[:, :fh].reshape(Bn, 1, fh))
            gcs.append(conv_s)
            gss.append(S_s)
        else:
            w_in16 = _to_bf16(c_w_in[j])
            qk_w, v_w = rh * rdk, rh * rdv
            widths = (qk_w, qk_w, v_w, v_w)
            w_out16 = c_w_out[j].astype(BF16)

            q16, k16, v, sgate = _c_proj_prompt(xp, norm_g[li, 1], w_in16, cos2, sin2, L, rh, rdk, v_w)
            y, R_p = _ret_prompt(q16, k16, v, sgate, ret_norm_g[j], B, L, rh, rdk, rdv)
            pre_p = ([y], w_out16)
            rsp.append(R_p)

            q, k, v, gate = _rms_proj(xs, norm_g[li, 1], w_in16, widths)
            y, R_s = _ret_sample(q.reshape(Bn, 1, qk_w), k.reshape(Bn, 1, qk_w), v.reshape(Bn, 1, v_w),
                                 gate.reshape(Bn, 1, v_w), freqs, ret_norm_g[j], state_ret_S[j],
                                 n_pages * page, rh, rdk, rdv)
            pre_s = ([y.reshape(Bn, v_w)], w_out16)
            rss.append(R_s)
        final_g = final_norm_g if li == depth - 1 else None
        xp = _ffn(xp, norm_g[li, 2], w_gu16, w_down16, li, 1, final_g, pre_p)
        xs = _ffn(xs, norm_g[li, 2], w_gu16, w_down16, li, 1, final_g, pre_s)
    y_prompt = xp.reshape(B, L, D)
    y_sample = xs.reshape(Bn, 1, D)
    return (y_prompt, y_sample,
            jnp.stack(fkp), jnp.stack(fvp), jnp.stack(flp),
            jnp.stack(fks), jnp.stack(fvs), jnp.stack(fls),
            jnp.stack(gcp), jnp.stack(gsp), jnp.stack(gcs), jnp.stack(gss),
            jnp.stack(rsp), jnp.stack(rss))
```

```python
import functools
import math

import jax
import jax.numpy as jnp
from jax import lax
from jax.experimental import pallas as pl
from jax.experimental.pallas import tpu as pltpu

F32 = jnp.float32
BF16 = jnp.bfloat16
EPS = 1e-6
ROPE_BASE = 10000.0
NEG = -1e30
LANES = 128
GDN_CHUNK = 64
VMEM_LIMIT = 56 * 1024 * 1024
HI = lax.Precision.HIGHEST


def _cparams(n_axes):
    return pltpu.CompilerParams(dimension_semantics=("arbitrary",) * n_axes,
                                vmem_limit_bytes=VMEM_LIMIT)


def _dot(a, b, precision=None):
    return jnp.dot(a, b, preferred_element_type=F32, precision=precision)


def _dot_nt(a, b, precision=None):
    return lax.dot_general(a, b, (((1,), (1,)), ((), ())), preferred_element_type=F32, precision=precision)


def _dot_tn(a, b, precision=None):
    return lax.dot_general(a, b, (((0,), (0,)), ((), ())), preferred_element_type=F32, precision=precision)


def _rms_rows(x, g):
    return x * lax.rsqrt(jnp.mean(x * x, axis=-1, keepdims=True) + EPS) * g


MXU_DEPTH = 256


def _ffn_body(*refs, F, chunk, final_norm, n_pre):
    x_ref, g_ref, wgu_ref, wd_ref = refs[:4]
    a_refs = refs[4:4 + n_pre]
    rest = refs[4 + n_pre:]
    x = x_ref[...]
    if n_pre:
        wo_ref, rest = rest[0], rest[1:]
        row = 0
        for a_ref in a_refs:
            n = a_ref.shape[1]
            x = x + _dot(a_ref[...].astype(BF16), wo_ref[row:row + n, :])
            row += n
    if final_norm:
        gf_ref, rest = rest[0], rest[1:]
    o_ref = rest[0]
    xn = _rms_rows(x, g_ref[...]).astype(BF16)
    acc = None
    for c0 in range(0, F, chunk):
        c1 = min(F, c0 + chunk)
        a = _dot(xn, wgu_ref[:, c0:c1])
        b = _dot(xn, wgu_ref[:, F + c0:F + c1])
        h = (jax.nn.silu(a) * b).astype(BF16)
        d = _dot(h, wd_ref[c0:c1, :])
        acc = d if acc is None else acc + d
    y = x + 0.5 * acc
    o_ref[...] = _rms_rows(y, gf_ref[...]) if final_norm else y


def _ffn(x, g, w_gu, w_down, li, j, final_g=None, pre=None):
    M, D = x.shape
    F = w_down.shape[2]
    tm = min(M, 512)
    final_norm = final_g is not None
    const = lambda m: (0, 0)
    in_specs = [pl.BlockSpec((tm, D), lambda m: (m, 0)), pl.BlockSpec((1, D), const),
                pl.BlockSpec((None, None, D, 2 * F), lambda m: (li, j, 0, 0)),
                pl.BlockSpec((None, None, F, D), lambda m: (li, j, 0, 0))]
    args = [x, g.reshape(1, D), w_gu, w_down]
    acts, w_out = pre if pre is not None else ((), None)
    if acts:
        assert sum(a.shape[1] for a in acts) == w_out.shape[0]
        in_specs += [pl.BlockSpec((tm, a.shape[1]), lambda m: (m, 0)) for a in acts]
        in_specs.append(pl.BlockSpec(w_out.shape, const))
        args += [*acts, w_out]
    if final_norm:
        in_specs.append(pl.BlockSpec((1, D), const))
        args.append(final_g.reshape(1, D))
    return pl.pallas_call(
        functools.partial(_ffn_body, F=F, chunk=2 * MXU_DEPTH, final_norm=final_norm, n_pre=len(acts)),
        out_shape=jax.ShapeDtypeStruct((M, D), F32),
        grid=(M // tm,),
        in_specs=in_specs,
        out_specs=pl.BlockSpec((tm, D), lambda m: (m, 0)),
        compiler_params=_cparams(1),
        name="ffn",
    )(*args)


PROJ_CHUNK = 512


def _project(xn, w_ref, col, o_ref, o3_ref=None):
    n = o_ref.shape[1]
    for s0 in range(0, n, PROJ_CHUNK):
        s1 = min(n, s0 + PROJ_CHUNK)
        val = _dot(xn, w_ref[:, col + s0:col + s1])
        o_ref[:, s0:s1] = val
        if o3_ref is not None:
            dh = o3_ref.shape[2]
            for hh in range((s1 - s0) // dh):
                o3_ref[:, s0 // dh + hh, :] = val[:, hh * dh:(hh + 1) * dh]
    return col + n


def _proj_body(x_ref, g_ref, w_ref, *o_refs):
    xn = _rms_rows(x_ref[...], g_ref[...]).astype(BF16)
    col = 0
    for o_ref in o_refs:
        col = _project(xn, w_ref, col, o_ref)


def _rms_proj(x, g, w, widths):
    M, D = x.shape
    tm = min(M, 256)
    assert sum(widths) == w.shape[1] and all(n % LANES == 0 for n in widths)
    return pl.pallas_call(
        _proj_body,
        out_shape=[jax.ShapeDtypeStruct((M, n), F32) for n in widths],
        grid=(M // tm,),
        in_specs=[pl.BlockSpec((tm, D), lambda m: (m, 0)), pl.BlockSpec((1, D), lambda m: (0, 0)),
                  pl.BlockSpec(w.shape, lambda m: (0, 0))],
        out_specs=[pl.BlockSpec((tm, n), lambda m: (m, 0)) for n in widths],
        compiler_params=_cparams(1),
        name="rms_proj",
    )(x, g.reshape(1, D), w)


def _l2_rows(x):
    return x * lax.rsqrt(jnp.sum(x * x, axis=-1, keepdims=True) + EPS)


def _ab_proj_body(x_ref, g_ref, w_ref, cw_ref, qkv_o, q_o, k_o, v_o, z_o, small_o, k3_o, v3_o, tail_o, win_ref,
                  *, tiles_per_seq, nh, dk):
    m = pl.program_id(0)
    tm = x_ref.shape[0]
    cd = qkv_o.shape[1]
    xn = _rms_rows(x_ref[...], g_ref[...]).astype(BF16)

    @pl.when(lax.rem(m, tiles_per_seq) == 0)
    def _():
        win_ref[0:8, :] = jnp.zeros((8, cd), F32)

    kw = cw_ref.shape[0]
    step = nh * dk

    def conv_product(s0):
        win_ref[8:8 + tm, s0:s0 + step] = _dot(xn, w_ref[:, s0:s0 + step])
        tail_o[:, s0:s0 + step] = win_ref[tm:tm + 8, s0:s0 + step]

    def conv_finish(s0):
        s1 = s0 + step
        conv = win_ref[8 - kw + 1:8 - kw + 1 + tm, s0:s1] * cw_ref[0:1, s0:s1]
        for i in range(1, kw):
            conv = conv + win_ref[8 - kw + 1 + i:8 - kw + 1 + i + tm, s0:s1] * cw_ref[i:i + 1, s0:s1]
        win_ref[0:8, s0:s1] = win_ref[tm:tm + 8, s0:s1]
        conv = jax.nn.silu(conv)
        if s0 >= 2 * step:
            qkv_o[:, s0:s1] = conv
        else:
            for h in range(nh):
                xh = _l2_rows(conv[:, h * dk:(h + 1) * dk])
                qkv_o[:, s0 + h * dk:s0 + (h + 1) * dk] = xh * dk ** -0.5 if s0 == 0 else xh

    assert cd == 3 * step
    conv_product(0)
    conv_product(step)
    col = _project(xn, w_ref, cd, q_o)
    conv_finish(0)
    conv_product(2 * step)
    col = _project(xn, w_ref, col, k_o, k3_o)
    conv_finish(step)
    col = _project(xn, w_ref, col, v_o, v3_o)
    col = _project(xn, w_ref, col, z_o)
    conv_finish(2 * step)
    _project(xn, w_ref, col, small_o)


def _ab_proj_prompt(x, g, w, conv_w, B, L, widths, fh, fd, gh, gdk):
    M, D = x.shape
    tm = min(L, 512)
    cd = widths[0]
    assert sum(widths) == w.shape[1] and L % tm == 0
    row = lambda m: (m, 0)
    const = lambda m: (0, 0)
    out_shape = [jax.ShapeDtypeStruct((M, n), F32) for n in widths]
    out_specs = [pl.BlockSpec((tm, n), row) for n in widths]
    for _ in range(2):
        out_shape.append(jax.ShapeDtypeStruct((M, fh, fd), F32))
        out_specs.append(pl.BlockSpec((tm, fh, fd), lambda m: (m, 0, 0)))
    out_shape.append(jax.ShapeDtypeStruct((B, 8, cd), F32))
    out_specs.append(pl.BlockSpec((None, 8, cd), lambda m: (m // (L // tm), 0, 0)))
    return pl.pallas_call(
        functools.partial(_ab_proj_body, tiles_per_seq=L // tm, nh=gh, dk=gdk),
        out_shape=out_shape,
        grid=(M // tm,),
        in_specs=[pl.BlockSpec((tm, D), row), pl.BlockSpec((1, D), const), pl.BlockSpec(w.shape, const),
                  pl.BlockSpec(conv_w.shape, const)],
        out_specs=out_specs,
        scratch_shapes=[pltpu.VMEM((8 + tm, cd), F32)],
        compiler_params=_cparams(1),
        name="ab_proj_prompt",
    )(x, g.reshape(1, D), w, conv_w)


def _c_proj_body(x_ref, g_ref, w_ref, cos_ref, sin_ref, q_o, k_o, v_o, sg_o, *, nh, dk):
    xn = _rms_rows(x_ref[...], g_ref[...]).astype(BF16)
    cos2, sin2 = cos_ref[...], sin_ref[...]
    hpc = PROJ_CHUNK // dk
    for idx, o_ref in enumerate((q_o, k_o)):
        for h0 in range(0, nh, hpc):
            c0 = idx * nh * dk + h0 * dk
            val = _dot(xn, w_ref[:, c0:c0 + hpc * dk])
            for hh in range(hpc):
                xh = val[:, hh * dk:(hh + 1) * dk]
                xh = xh * cos2 + pltpu.roll(xh, dk // 2, 1) * sin2
                if idx == 1:
                    xh = xh * dk ** -0.5
                o_ref[:, (h0 + hh) * dk:(h0 + hh + 1) * dk] = xh.astype(BF16)
    col = _project(xn, w_ref, 2 * nh * dk, v_o)
    n = sg_o.shape[1]
    for s0 in range(0, n, PROJ_CHUNK):
        s1 = min(n, s0 + PROJ_CHUNK)
        sg_o[:, s0:s1] = jax.nn.silu(_dot(xn, w_ref[:, col + s0:col + s1]))


def _c_proj_prompt(x, g, w, cos2, sin2, L, nh, dk, v_w):
    M, D = x.shape
    tm = min(L, 512)
    qk_w = nh * dk
    row = lambda m: (m, 0)
    const = lambda m: (0, 0)
    pos = lambda m: (lax.rem(m, L // tm), 0)
    return pl.pallas_call(
        functools.partial(_c_proj_body, nh=nh, dk=dk),
        out_shape=[jax.ShapeDtypeStruct((M, qk_w), BF16), jax.ShapeDtypeStruct((M, qk_w), BF16),
                   jax.ShapeDtypeStruct((M, v_w), F32), jax.ShapeDtypeStruct((M, v_w), F32)],
        grid=(M // tm,),
        in_specs=[pl.BlockSpec((tm, D), row), pl.BlockSpec((1, D), const), pl.BlockSpec(w.shape, const),
                  pl.BlockSpec((tm, dk), pos), pl.BlockSpec((tm, dk), pos)],
        out_specs=[pl.BlockSpec((tm, qk_w), row), pl.BlockSpec((tm, qk_w), row),
                   pl.BlockSpec((tm, v_w), row), pl.BlockSpec((tm, v_w), row)],
        compiler_params=_cparams(1),
        name="c_proj_prompt",
    )(x, g.reshape(1, D), w, cos2, sin2)


def _cast_body(x_ref, o_ref):
    o_ref[...] = x_ref[...].astype(o_ref.dtype)


def _to_bf16(w):
    cols = w.shape[-1]
    rows = w.size // cols
    tr = 512 if rows % 512 == 0 else rows
    out = pl.pallas_call(
        _cast_body,
        out_shape=jax.ShapeDtypeStruct((rows, cols), BF16),
        grid=(rows // tr,),
        in_specs=[pl.BlockSpec((tr, cols), lambda r: (r, 0))],
        out_specs=pl.BlockSpec((tr, cols), lambda r: (r, 0)),
        compiler_params=_cparams(1),
        name="to_bf16",
    )(w.reshape(rows, cols))
    return out.reshape(w.shape)


def _regroup_body(w_ref, o_ref, *, pieces):
    col = 0
    for start, width in pieces:
        o_ref[:, col:col + width] = w_ref[:, start:start + width].astype(o_ref.dtype)
        col += width
    if col < o_ref.shape[1]:
        o_ref[:, col:] = jnp.zeros((o_ref.shape[0], o_ref.shape[1] - col), o_ref.dtype)


def _regroup_bf16(w, li, pieces, n_out):
    _, D, N = w.shape
    tr = 256 if D % 256 == 0 else D
    return pl.pallas_call(
        functools.partial(_regroup_body, pieces=tuple(pieces)),
        out_shape=jax.ShapeDtypeStruct((D, n_out), BF16),
        grid=(D // tr,),
        in_specs=[pl.BlockSpec((None, tr, N), lambda r: (li, r, 0))],
        out_specs=pl.BlockSpec((tr, n_out), lambda r: (r, 0)),
        compiler_params=_cparams(1),
        name="regroup_bf16",
    )(w)


def _gates_body(s_ref, p_ref, o_ref, *, nh):
    s = s_ref[...] + p_ref[0:1, :]
    lane = lax.broadcasted_iota(jnp.int32, s.shape, 1)
    logf = jax.nn.log_sigmoid(s)
    g = -jnp.exp(p_ref[1:2, :]) * jax.nn.softplus(s)
    beta = jax.nn.sigmoid(s)
    o_ref[...] = jnp.where(lane < nh, logf, jnp.where(lane < 2 * nh, g, jnp.where(lane < 3 * nh, beta, 0.0)))


def _gates(small, b_f, dt_bias, a_log):
    M = small.shape[0]
    nh = b_f.shape[0]
    tm = min(M, 2048)
    pad = jnp.zeros((LANES - 3 * nh,), F32)
    p = jnp.zeros((8, LANES), F32)
    p = p.at[0].set(jnp.concatenate([b_f, dt_bias, jnp.zeros((nh,), F32), pad]))
    p = p.at[1].set(jnp.concatenate([jnp.zeros((nh,), F32), a_log, jnp.zeros((nh,), F32), pad]))
    return pl.pallas_call(
        functools.partial(_gates_body, nh=nh),
        out_shape=jax.ShapeDtypeStruct((M, LANES), F32),
        grid=(M // tm,),
        in_specs=[pl.BlockSpec((tm, LANES), lambda m: (m, 0)), pl.BlockSpec((8, LANES), lambda m: (0, 0))],
        out_specs=pl.BlockSpec((tm, LANES), lambda m: (m, 0)),
        compiler_params=_cparams(1),
        name="gates",
    )(small, p)


def _cumsum_body(x_ref, o_ref):
    L = x_ref.shape[1]
    r = lax.broadcasted_iota(jnp.int32, (LANES, LANES), 0)
    c = lax.broadcasted_iota(jnp.int32, (LANES, LANES), 1)
    upper = (r <= c).astype(F32)
    carry = jnp.zeros((x_ref.shape[0], 1), F32)
    for j in range(L // LANES):
        cs = _dot(x_ref[:, j * LANES:(j + 1) * LANES], upper, HI) + carry
        o_ref[:, j * LANES:(j + 1) * LANES] = cs
        carry = cs[:, LANES - 1:LANES]


def _cumsum_rows(x, seg):
    R, M = x.shape
    return pl.pallas_call(
        _cumsum_body,
        out_shape=jax.ShapeDtypeStruct((R, M), F32),
        grid=(M // seg,),
        in_specs=[pl.BlockSpec((R, seg), lambda b: (0, b))],
        out_specs=pl.BlockSpec((R, seg), lambda b: (0, b)),
        compiler_params=_cparams(1),
        name="cumsum_rows",
    )(x)


def _fox_body(q_ref, k_ref, v_ref, c_ref, o_ref, vt_ref, cb_ref, acc_ref, m_ref, *, t, hg, dh, scale):
    i = pl.program_id(2)
    L = k_ref.shape[0]
    log2e = math.log2(math.e)
    heads = range(hg)

    @pl.when(i == 0)
    def _():
        for g in heads:
            for jj in range(L // t):
                vt_ref[g, 0:dh, jj * t:(jj + 1) * t] = v_ref[jj * t:(jj + 1) * t, g * dh:(g + 1) * dh].T.astype(BF16)
            vt_ref[g, dh:, :] = jnp.ones((vt_ref.shape[1] - dh, L), BF16)
            for jj in range(L // LANES):
                row = c_ref[g, :, jj * LANES:(jj + 1) * LANES] * log2e
                cb_ref[g, jj * LANES:(jj + 1) * LANES, :] = jnp.broadcast_to(row, (LANES, LANES)).T

    q0 = pl.multiple_of(i * t, t)
    qs = [q_ref[:, g * dh:(g + 1) * dh].astype(BF16) for g in heads]
    r2 = [c_ref[g, :, pl.ds(q0, t)][:, t - 1:t] * log2e for g in heads]
    m_ref[...] = jnp.full_like(m_ref, NEG)
    acc_ref[...] = jnp.zeros_like(acc_ref)

    def step(j, masked):
        k0 = pl.multiple_of(j * t, t)
        qk = [_dot_nt(k_ref[pl.ds(k0, t), g * dh:(g + 1) * dh].astype(BF16), qs[g]) for g in heads]
        ps, alphas = [], []
        for g in heads:
            bias = r2[g] - cb_ref[g, pl.ds(k0, t), :]
            st = qk[g] * (scale * log2e) + jnp.concatenate([bias] * (t // LANES), axis=1)
            if masked:
                key = lax.broadcasted_iota(jnp.int32, (t, t), 0)
                qry = lax.broadcasted_iota(jnp.int32, (t, t), 1)
                st = jnp.where(key <= qry, st, NEG)
            m_prev = m_ref[g]
            m_new = jnp.maximum(m_prev, jnp.max(st, axis=0, keepdims=True))
            ps.append(jnp.exp2(st - m_new).astype(BF16))
            alphas.append(jnp.exp2(m_prev - m_new))
            m_ref[g] = m_new
        for g in heads:
            acc_ref[g] = acc_ref[g] * alphas[g] + _dot(vt_ref[g, :, pl.ds(k0, t)], ps[g])

    def loop_body(j, carry):
        step(j, False)
        return carry

    lax.fori_loop(0, i, loop_body, 0)
    step(i, True)
    for g in heads:
        acc = acc_ref[g]
        o_ref[:, g * dh:(g + 1) * dh] = (acc[0:dh] / acc[dh:dh + 1]).T


def _fox_prompt(q, k, v, c_rows, B, L, H):
    M, W = q.shape
    Dh = W // H
    t = min(L, 512)
    nq = L // t
    hg = 2 if H % 2 == 0 else 1
    kernel = functools.partial(_fox_body, t=t, hg=hg, dh=Dh, scale=Dh ** -0.5)
    return pl.pallas_call(
        kernel,
        out_shape=jax.ShapeDtypeStruct((M, W), F32),
        grid=(B, H // hg, nq),
        in_specs=[
            pl.BlockSpec((t, hg * Dh), lambda b, h, i: (b * nq + i, h)),
            pl.BlockSpec((L, hg * Dh), lambda b, h, i: (b, h)),
            pl.BlockSpec((L, hg * Dh), lambda b, h, i: (b, h)),
            pl.BlockSpec((hg, 1, L), lambda b, h, i: (b * (H // hg) + h, 0, 0)),
        ],
        out_specs=pl.BlockSpec((t, hg * Dh), lambda b, h, i: (b * nq + i, h)),
        scratch_shapes=[pltpu.VMEM((hg, Dh + 16, L), BF16), pltpu.VMEM((hg, L, LANES), F32),
                        pltpu.VMEM((hg, Dh + 16, t), F32), pltpu.VMEM((hg, 1, t), F32)],
        compiler_params=_cparams(3),
        name="fox_prompt",
    )(q, k, v, c_rows)


def _split2(x):
    hi = x.astype(BF16)
    return hi, (x - hi.astype(F32)).astype(BF16)


def _split3(x):
    h1 = x.astype(BF16)
    r1 = x - h1.astype(F32)
    h2 = r1.astype(BF16)
    return h1, h2, (r1 - h2.astype(F32)).astype(BF16)


def _dot3(a, b):
    (ah, al), (bh, bl) = a, b
    return _dot(ah, bh) + (_dot(ah, bl) + _dot(al, bh))


def _unit_lower_inverses(a_list, n):
    ri = lax.broadcasted_iota(jnp.int32, (n, n), 0)
    ci = lax.broadcasted_iota(jnp.int32, (n, n), 1)
    eye = (ri == ci).astype(F32)
    nm = len(a_list)
    diag16 = (ri >> 4) == (ci >> 4)
    pw = [jnp.where(diag16, a, 0.0) for a in a_list]
    t = [eye - p for p in pw]
    for _ in range(3):
        pw16 = [p.astype(BF16) for p in pw]
        pw = [_dot(p, p) for p in pw16]
        pn16 = [p.astype(BF16) for p in pw]
        t = [t[m] + _dot(t[m].astype(BF16), pn16[m]) for m in range(nm)]
    size = 16
    while size < GDN_CHUNK:
        sh = size.bit_length() - 1
        off = ((ri >> (sh + 1)) == (ci >> (sh + 1))) & ((ri >> sh) == (ci >> sh) + 1)
        t16 = [x.astype(BF16) for x in t]
        mid = [_dot(jnp.where(off, a_list[m], 0.0).astype(BF16), t16[m]) for m in range(nm)]
        t = [t[m] - _dot(t16[m], mid[m].astype(BF16)) for m in range(nm)]
        size *= 2
    ts = [_split2(x) for x in t]
    res = [(eye - t[m]) - _dot3(_split2(a_list[m]), ts[m]) for m in range(nm)]
    return [t[m] + _dot(ts[m][0], res[m].astype(BF16)) for m in range(nm)]


def _gdn_body(qkv_ref, z_ref, gc_ref, *rest, nh, dk, dv, nseq):
    gr_refs = rest[:nseq]
    ng_ref, o_ref, s_out_ref, s_ref, m16_ref = rest[nseq:]
    j = pl.program_id(1)
    T = qkv_ref.shape[1]
    C = GDN_CHUNK
    nc = T // C

    ri = lax.broadcasted_iota(jnp.int32, (T, T), 0)
    ci = lax.broadcasted_iota(jnp.int32, (T, T), 1)
    csh = C.bit_length() - 1
    same = (ri >> csh) == (ci >> csh)
    tril = same & (ri >= ci)
    strict = same & (ri > ci)

    @pl.when((pl.program_id(0) == 0) & (j == 0))
    def _():
        m16_ref[0] = tril.astype(BF16)
        m16_ref[1] = (same & (ri <= ci)).astype(BF16)

    @pl.when(j == 0)
    def _():
        s_ref[...] = jnp.zeros_like(s_ref)

    rowi = lax.broadcasted_iota(jnp.int32, (T, 1), 0)
    units = [(s, h) for s in range(nseq) for h in range(nh)]
    nu = len(units)
    gcols, grows = [], []
    for s in range(nseq):
        gcols.append(sum(_dot(m16_ref[0], part) for part in _split3(gc_ref[s])))
        grows.append(sum(_dot(part, m16_ref[1]) for part in _split3(gr_refs[s][...])))

    q, k, beta, gcc, gcr, gam, kb, k16 = [], [], [], [], [], [], [], []
    for u, (s, h) in enumerate(units):
        q.append(qkv_ref[s, :, h * dk:(h + 1) * dk])
        k.append(qkv_ref[s, :, nh * dk + h * dk:nh * dk + (h + 1) * dk])
        beta.append(gc_ref[s, :, 2 * nh + h:2 * nh + h + 1])
        gcc.append(gcols[s][:, nh + h:nh + h + 1])
        gcr.append(grows[s][nh + h:nh + h + 1, :])
        gam.append(jnp.exp(jnp.where(tril, gcc[u] - gcr[u], NEG)))
        kb.append(k[u] * beta[u])
        k16.append(k[u].astype(BF16))
    a = [jnp.where(strict, _dot_nt(kb[u].astype(BF16), k16[u]) * gam[u], 0.0) for u in range(nu)]
    tinv = _unit_lower_inverses(a, T)
    x = []
    for u, (s, h) in enumerate(units):
        v = qkv_ref[s, :, 2 * nh * dk + h * dv:2 * nh * dk + (h + 1) * dv]
        rhs = jnp.concatenate([v * beta[u], kb[u] * jnp.exp(gcc[u])], axis=-1)
        x.append(_dot3(_split2(tinv[u]), _split2(rhs)))
    attn, qg, kd, egl = [], [], [], []
    for u in range(nu):
        attn.append((_dot_nt(q[u].astype(BF16), k16[u]) * gam[u]).astype(BF16))
        qg.append((q[u] * jnp.exp(gcc[u])).astype(BF16))
        gl = gcr[u][:, C - 1:C]
        for cidx in range(1, nc):
            gl = jnp.where(rowi >= cidx * C, gcr[u][:, (cidx + 1) * C - 1:(cidx + 1) * C], gl)
        kd.append((k[u] * jnp.exp(gl - gcc[u])).astype(BF16))
        egl.append(jnp.exp(gl))
    S = [s_ref[s, h] for s, h in units]
    outs = [[] for _ in units]
    for cidx in range(nc):
        sl = slice(cidx * C, (cidx + 1) * C)
        S16 = [st.astype(BF16) for st in S]
        vn16 = [(x[u][sl, :dv] - _dot(x[u][sl, dv:].astype(BF16), S16[u])).astype(BF16) for u in range(nu)]
        for u in range(nu):
            outs[u].append(_dot(qg[u][sl], S16[u]) + _dot(attn[u][sl, cidx * C:(cidx + 1) * C], vn16[u]))
        S = [S[u] * egl[u][cidx * C:cidx * C + 1] + _dot_tn(kd[u][sl], vn16[u]) for u in range(nu)]
    for u, (s, h) in enumerate(units):
        s_ref[s, h] = S[u]
        o = _rms_rows(jnp.concatenate(outs[u], axis=0), ng_ref[...])
        o_ref[s, :, h * dv:(h + 1) * dv] = o * jax.nn.silu(z_ref[s, :, h * dv:(h + 1) * dv])

    @pl.when(j == pl.num_programs(1) - 1)
    def _():
        s_out_ref[...] = s_ref[...]


def _gdn_prompt(qkv, z, gates_col, gates_row, norm_g, B, L, nh, dk, dv):
    M, CD = qkv.shape
    T = 2 * GDN_CHUNK
    nb = L // T
    nseq = 2 if B % 2 == 0 else 1
    kernel = functools.partial(_gdn_body, nh=nh, dk=dk, dv=dv, nseq=nseq)

    def row_map(s):
        return lambda p, j: (0, (p * nseq + s) * nb + j)

    per_seq = lambda p, j: (p, j, 0)
    go, S = pl.pallas_call(
        kernel,
        out_shape=[jax.ShapeDtypeStruct((B, L, nh * dv), F32), jax.ShapeDtypeStruct((B, nh, dk, dv), F32)],
        grid=(B // nseq, nb),
        in_specs=[
            pl.BlockSpec((nseq, T, CD), per_seq),
            pl.BlockSpec((nseq, T, nh * dv), per_seq),
            pl.BlockSpec((nseq, T, LANES), per_seq),
        ] + [pl.BlockSpec((16, T), row_map(s)) for s in range(nseq)] + [
            pl.BlockSpec((1, dv), lambda p, j: (0, 0)),
        ],
        out_specs=[
            pl.BlockSpec((nseq, T, nh * dv), per_seq),
            pl.BlockSpec((nseq, nh, dk, dv), lambda p, j: (p, 0, 0, 0)),
        ],
        scratch_shapes=[pltpu.VMEM((nseq, nh, dk, dv), F32), pltpu.VMEM((2, T, T), BF16)],
        compiler_params=_cparams(2),
        name="gdn_prompt",
    )(qkv.reshape(B, L, CD), z.reshape(B, L, nh * dv), gates_col.reshape(B, L, LANES),
      *([gates_row] * nseq), norm_g.reshape(1, dv))
    return go.reshape(M, nh * dv), S


def _rope(x, cos2, sin2, nh, dk):
    outs = []
    for h in range(nh):
        xh = x[:, h * dk:(h + 1) * dk]
        outs.append(xh * cos2 + pltpu.roll(xh, dk // 2, 1) * sin2)
    return outs


def _rope_tables(pos, freqs):
    ang = pos * freqs
    cos, sin = jnp.cos(ang), jnp.sin(ang)
    return jnp.concatenate([cos, cos], axis=-1), jnp.concatenate([-sin, sin], axis=-1)


def _head_layernorm(o):
    mu = jnp.mean(o, axis=-1, keepdims=True)
    oc = o - mu
    return oc * lax.rsqrt(jnp.mean(oc * oc, axis=-1, keepdims=True) + EPS)


def _log_gamma(h):
    return math.log(1.0 - 2.0 ** (-5.0 - h))


def _rope_table_body(fr_ref, cos_o, sin_o):
    C = cos_o.shape[0]
    rowi = lax.broadcasted_iota(jnp.int32, (C, 1), 0)
    pos = (pl.program_id(0) * C + rowi).astype(F32)
    cos_o[...], sin_o[...] = _rope_tables(pos, fr_ref[...])


def _rope_table(L, freqs):
    C = min(L, 128)
    dk = 2 * freqs.shape[1]
    return pl.pallas_call(
        _rope_table_body,
        out_shape=[jax.ShapeDtypeStruct((L, dk), F32)] * 2,
        grid=(L // C,),
        in_specs=[pl.BlockSpec(freqs.shape, lambda c: (0, 0))],
        out_specs=[pl.BlockSpec((C, dk), lambda c: (c, 0))] * 2,
        compiler_params=_cparams(1),
        name="rope_table",
    )(freqs)


def _ret_body(q_ref, k_ref, v_ref, sg_ref, gn_ref, o_ref, s_out_ref, s_ref, dmat_ref, *, nh, dk, dv):
    b = pl.program_id(0)
    j = pl.program_id(1)
    C = q_ref.shape[0]
    rowf = lax.broadcasted_iota(jnp.int32, (C, 1), 0).astype(F32)

    @pl.when((b == 0) & (j == 0))
    def _():
        ri = lax.broadcasted_iota(jnp.int32, (C, C), 0)
        ci = lax.broadcasted_iota(jnp.int32, (C, C), 1)
        diff = (ri - ci).astype(F32)
        for h in range(nh):
            dmat_ref[h] = jnp.exp(jnp.where(ri >= ci, diff * _log_gamma(h), NEG))

    @pl.when(j == 0)
    def _():
        s_ref[...] = jnp.zeros_like(s_ref)

    heads = range(nh)
    q16 = [q_ref[:, h * dk:(h + 1) * dk] for h in heads]
    k16 = [k_ref[:, h * dk:(h + 1) * dk] for h in heads]
    v = [v_ref[:, h * dv:(h + 1) * dv] for h in heads]
    S = [s_ref[h] for h in heads]
    qk = [_dot_nt(q16[h], k16[h]) for h in heads]
    cross = [_dot(q16[h], S[h].astype(BF16)) for h in heads]
    for h in heads:
        vd = (v[h] * jnp.exp(_log_gamma(h) * (C - 1.0 - rowf))).astype(BF16)
        s_ref[h] = S[h] * math.exp(_log_gamma(h) * C) + _dot_tn(k16[h], vd)
    inner = [_dot((qk[h] * dmat_ref[h]).astype(BF16), v[h].astype(BF16)) for h in heads]
    for h in heads:
        o = inner[h] + cross[h] * jnp.exp(_log_gamma(h) * (rowf + 1.0))
        o = _head_layernorm(o) * gn_ref[:, h * dv:(h + 1) * dv]
        o_ref[:, h * dv:(h + 1) * dv] = sg_ref[:, h * dv:(h + 1) * dv] * o

    @pl.when(j == pl.num_programs(1) - 1)
    def _():
        s_out_ref[...] = s_ref[...]


def _ret_prompt(q16, k16, v, sgate, gnorm, B, L, nh, dk, dv):
    M = q16.shape[0]
    C = 128
    nb = L // C
    kernel = functools.partial(_ret_body, nh=nh, dk=dk, dv=dv)
    tok = lambda b, j: (b * nb + j, 0)
    return pl.pallas_call(
        kernel,
        out_shape=[jax.ShapeDtypeStruct((M, nh * dv), F32), jax.ShapeDtypeStruct((B, nh, dk, dv), F32)],
        grid=(B, nb),
        in_specs=[
            pl.BlockSpec((C, nh * dk), tok),
            pl.BlockSpec((C, nh * dk), tok),
            pl.BlockSpec((C, nh * dv), tok),
            pl.BlockSpec((C, nh * dv), tok),
            pl.BlockSpec((1, nh * dv), lambda b, j: (0, 0)),
        ],
        out_specs=[
            pl.BlockSpec((C, nh * dv), tok),
            pl.BlockSpec((None, nh, dk, dv), lambda b, j: (b, 0, 0, 0)),
        ],
        scratch_shapes=[pltpu.VMEM((nh, dk, dv), F32), pltpu.VMEM((nh, C, C), F32)],
        compiler_params=_cparams(2),
        name="ret_prompt",
    )(q16, k16, v, sgate, gnorm.reshape(1, nh * dv))


def _page_sums_body(x_ref, o_ref, op_ref):
    _, nh, plen = x_ref.shape
    n = nh * plen
    wide = o_ref.shape[1]

    @pl.when(pl.program_id(0) == 0)
    def _():
        kk = lax.broadcasted_iota(jnp.int32, (plen, wide), 0)
        cc = lax.broadcasted_iota(jnp.int32, (plen, wide), 1)
        key_c = cc >> (nh.bit_length() - 1)
        later_or_total = (cc >= n) | (kk > key_c)
        for h in range(nh):
            op_ref[h] = (((cc & (nh - 1)) == h) & later_or_total).astype(BF16)

    acc = None
    for h in range(nh):
        for part in _split2(x_ref[:, h, :]):
            d = _dot(part, op_ref[h])
            acc = d if acc is None else acc + d
    o_ref[...] = acc


def _page_sums(lf_pages):
    P, nh, plen = lf_pages.shape
    wide = nh * plen + LANES
    tp = 512 if P % 512 == 0 else P
    return pl.pallas_call(
        _page_sums_body,
        out_shape=jax.ShapeDtypeStruct((P, wide), F32),
        grid=(P // tp,),
        in_specs=[pl.BlockSpec((tp, nh, plen), lambda p: (p, 0, 0))],
        out_specs=pl.BlockSpec((tp, wide), lambda p: (p, 0)),
        scratch_shapes=[pltpu.VMEM((nh, plen, wide), BF16)],
        compiler_params=_cparams(1),
        name="page_sums",
    )(lf_pages)


def _paged_body(pt_ref, *refs, G, nh, scale):
    k_refs = refs[:G]
    v_refs = refs[G:2 * G]
    sums_ref, q_ref, kn_ref, vn_ref, lfn_ref, o_ref, m_ref, l_ref, acc_ref, carry_ref = refs[2 * G:]
    b_idx = pl.program_id(0)
    s_idx = pl.program_id(1)
    ns = pl.num_programs(1)
    n = k_refs[0].shape[0]
    dh = q_ref.shape[1]

    @pl.when(s_idx == 0)
    def _():
        m_ref[...] = jnp.full_like(m_ref, NEG)
        l_ref[...] = jnp.zeros_like(l_ref)
        acc_ref[...] = jnp.zeros_like(acc_ref)
        carry_ref[...] = jnp.zeros_like(carry_ref)

    q = q_ref[...]
    q8 = jnp.concatenate([q, jnp.zeros((8 - nh, dh), F32)], axis=0).astype(BF16)
    lfn = jnp.concatenate([lfn_ref[...], jnp.zeros((8 - nh, 1), F32)], axis=0)
    hrow = lax.broadcasted_iota(jnp.int32, (8, n), 0)
    hcol = lax.broadcasted_iota(jnp.int32, (8, n), 1) & (nh - 1)
    match = hrow == hcol
    carry = carry_ref[...]
    s_parts = [None] * G
    for i in reversed(range(G)):
        page_id = pt_ref[b_idx, (ns - 1 - s_idx) * G + i]
        wt = sums_ref[pl.ds(page_id, 1), :]
        bias = wt[:, :n] + carry
        carry = carry + jnp.concatenate([wt[:, n:]] * (n // LANES), axis=1)
        s = _dot_nt(q8, k_refs[i][...].astype(BF16)) * scale + bias + lfn
        s_parts[i] = jnp.where(match, s, NEG)
    carry_ref[...] = carry
    m_run, l_run, acc_run = m_ref[...], l_ref[...], acc_ref[...]
    half = max(G // 2, 1)
    for grp in (range(half, G), range(0, half)) if G > 1 else (range(G),):
        sg = [s_parts[i] for i in grp]
        m_new = jnp.maximum(m_run, jnp.max(functools.reduce(jnp.maximum, sg), axis=-1, keepdims=True))
        alpha = jnp.exp(m_run - m_new)
        pg = [jnp.exp(s - m_new) for s in sg]
        l_run = alpha * l_run + jnp.sum(functools.reduce(jnp.add, pg), axis=-1, keepdims=True)
        pv = functools.reduce(jnp.add, [_dot(p.astype(BF16), v_refs[i][...].astype(BF16)) for p, i in zip(pg, grp)])
        acc_run = alpha * acc_run + pv
        m_run = m_new
    m_ref[...], l_ref[...], acc_ref[...] = m_run, l_run, acc_run

    @pl.when(s_idx == pl.num_programs(1) - 1)
    def _():
        qf = q.astype(BF16).astype(F32)
        kn = kn_ref[...].astype(BF16).astype(F32)
        s_new = jnp.sum(qf * kn, axis=-1, keepdims=True) * scale
        m_prev = m_ref[0:nh, :]
        m_new = jnp.maximum(m_prev, s_new)
        alpha = jnp.exp(m_prev - m_new)
        p_new = jnp.exp(s_new - m_new)
        l_fin = alpha * l_ref[0:nh, :] + p_new
        vn = vn_ref[...].astype(BF16).astype(F32)
        num = alpha * acc_ref[0:nh, :] + p_new.astype(BF16).astype(F32) * vn
        o_ref[...] = num / l_fin


def _fox_sample(q, k_new, v_new, lf_new, k_pages, v_pages, page_sums, page_table, nh):
    Bn, _, dh = q.shape
    n_pages = page_table.shape[1]
    n_pool, n, _ = k_pages.shape
    G = 16 if n_pages % 16 == 0 else (8 if n_pages % 8 == 0 else 1)
    ns = n_pages // G

    def page_map(i):
        return lambda b, s, pt: (pt[b, (ns - 1 - s) * G + i], 0, 0)

    per_b = lambda b, s, pt: (b, 0, 0)
    grid_spec = pltpu.PrefetchScalarGridSpec(
        num_scalar_prefetch=1,
        grid=(Bn, ns),
        in_specs=[pl.BlockSpec((None, n, dh), page_map(i)) for i in range(G)]
        + [pl.BlockSpec((None, n, dh), page_map(i)) for i in range(G)]
        + [pl.BlockSpec(page_sums.shape, lambda b, s, pt: (0, 0)),
           pl.BlockSpec((None, nh, dh), per_b), pl.BlockSpec((None, nh, dh), per_b),
           pl.BlockSpec((None, nh, dh), per_b), pl.BlockSpec((None, nh, 1), per_b)],
        out_specs=pl.BlockSpec((None, nh, dh), per_b),
        scratch_shapes=[pltpu.VMEM((8, 1), F32), pltpu.VMEM((8, 1), F32), pltpu.VMEM((8, dh), F32),
                        pltpu.VMEM((1, n), F32)],
    )
    return pl.pallas_call(
        functools.partial(_paged_body, G=G, nh=nh, scale=dh ** -0.5),
        out_shape=jax.ShapeDtypeStruct((Bn, nh, dh), F32),
        grid_spec=grid_spec,
        compiler_params=_cparams(2),
        name="fox_sample",
    )(page_table, *([k_pages] * G), *([v_pages] * G), page_sums, q, k_new, v_new, lf_new)


def _pad8(row):
    return jnp.concatenate([row, jnp.zeros((8 - row.shape[0], row.shape[1]), row.dtype)], axis=0)


def _gdn_sample_body(cin_ref, prev_ref, z_ref, gt_ref, cw_ref, ng_ref, s0_ref, o_ref, cnew_ref, s_out_ref, *, nh, dk, dv):
    u = cin_ref[...]
    prev = prev_ref[...]
    w = cw_ref[...]
    kw = w.shape[0]
    conv = u * w[kw - 1:kw]
    for i in range(kw - 1):
        conv = conv + prev[i:i + 1] * w[i:i + 1]
    cnew_ref[0:kw - 2, :] = prev[1:kw - 1]
    cnew_ref[kw - 2:kw - 1, :] = u
    conv = jax.nn.silu(conv)
    gt = gt_ref[...]
    heads = range(nh)
    q = [_l2_rows(conv[:, h * dk:(h + 1) * dk]) * dk ** -0.5 for h in heads]
    k = [_l2_rows(conv[:, nh * dk + h * dk:nh * dk + (h + 1) * dk]) for h in heads]
    v = [conv[:, 2 * nh * dk + h * dv:2 * nh * dk + (h + 1) * dv] for h in heads]
    beta = [gt[:, 2 * nh + h:2 * nh + h + 1] for h in heads]
    eg = [jnp.exp(gt[:, nh + h:nh + h + 1]) for h in heads]
    S = [s0_ref[h] for h in heads]
    r = [_dot(_pad8(jnp.concatenate([k[h] * beta[h] * eg[h], q[h] * eg[h]], axis=0)).astype(BF16), S[h].astype(BF16))
         for h in heads]
    vn16 = [(v[h] * beta[h] - r[h][0:1]).astype(BF16) for h in heads]
    for h in heads:
        s_out_ref[h] = S[h] * eg[h] + _dot_tn(_pad8(k[h]).astype(BF16), _pad8(vn16[h]))
    for h in heads:
        qk = jnp.sum(q[h].astype(BF16).astype(F32) * k[h].astype(BF16).astype(F32), axis=-1, keepdims=True)
        o = r[h][1:2] + qk.astype(BF16).astype(F32) * vn16[h].astype(F32)
        o = _rms_rows(o, ng_ref[...])
        o_ref[:, h * dv:(h + 1) * dv] = o * jax.nn.silu(z_ref[:, h * dv:(h + 1) * dv])


def _gdn_sample(conv_in, conv_prev, z, gates_col, conv_w, norm_g, S0, nh, dk, dv):
    Bn, _, CD = conv_in.shape
    kw = conv_w.shape[0]
    per_b3 = lambda b: (b, 0, 0)
    kernel = functools.partial(_gdn_sample_body, nh=nh, dk=dk, dv=dv)
    return pl.pallas_call(
        kernel,
        out_shape=[jax.ShapeDtypeStruct((Bn, 1, nh * dv), F32), jax.ShapeDtypeStruct((Bn, kw - 1, CD), F32),
                   jax.ShapeDtypeStruct((Bn, nh, dk, dv), F32)],
        grid=(Bn,),
        in_specs=[
            pl.BlockSpec((None, 1, CD), per_b3),
            pl.BlockSpec((None, kw - 1, CD), per_b3),
            pl.BlockSpec((None, 1, nh * dv), per_b3),
            pl.BlockSpec((None, 1, LANES), per_b3),
            pl.BlockSpec(conv_w.shape, lambda b: (0, 0)),
            pl.BlockSpec((1, dv), lambda b: (0, 0)),
            pl.BlockSpec((None, nh, dk, dv), lambda b: (b, 0, 0, 0)),
        ],
        out_specs=[
            pl.BlockSpec((None, 1, nh * dv), per_b3),
            pl.BlockSpec((None, kw - 1, CD), per_b3),
            pl.BlockSpec((None, nh, dk, dv), lambda b: (b, 0, 0, 0)),
        ],
        compiler_params=_cparams(1),
        name="gdn_sample",
    )(conv_in, conv_prev, z, gates_col, conv_w, norm_g.reshape(1, dv), S0)


def _ret_sample_body(q_ref, k_ref, v_ref, gate_ref, fr_ref, gn_ref, s0_ref, o_ref, s_out_ref, *, nh, dk, dv, pos):
    cos2, sin2 = _rope_tables(jnp.full((1, 1), pos, F32), fr_ref[...])
    qs = _rope(q_ref[...], cos2, sin2, nh, dk)
    ks = _rope(k_ref[...], cos2, sin2, nh, dk)
    heads = range(nh)
    ks = [kh * dk ** -0.5 for kh in ks]
    v16 = [v_ref[:, h * dv:(h + 1) * dv].astype(BF16) for h in heads]
    S = [s0_ref[h] for h in heads]
    cross = [_dot(_pad8(qs[h] * math.exp(_log_gamma(h))).astype(BF16), S[h].astype(BF16))[0:1] for h in heads]
    for h in heads:
        s_out_ref[h] = S[h] * math.exp(_log_gamma(h)) + _dot_tn(_pad8(ks[h]).astype(BF16), _pad8(v16[h]))
    for h in heads:
        qk = jnp.sum(qs[h].astype(BF16).astype(F32) * ks[h].astype(BF16).astype(F32), axis=-1, keepdims=True)
        inner = qk.astype(BF16).astype(F32) * v16[h].astype(F32)
        o = _head_layernorm(inner + cross[h]) * gn_ref[:, h * dv:(h + 1) * dv]
        o_ref[:, h * dv:(h + 1) * dv] = jax.nn.silu(gate_ref[:, h * dv:(h + 1) * dv]) * o


def _ret_sample(q, k, v, gate, freqs, gnorm, S0, pos, nh, dk, dv):
    Bn = q.shape[0]
    per_b3 = lambda b: (b, 0, 0)
    kernel = functools.partial(_ret_sample_body, nh=nh, dk=dk, dv=dv, pos=float(pos))
    return pl.pallas_call(
        kernel,
        out_shape=[jax.ShapeDtypeStruct((Bn, 1, nh * dv), F32), jax.ShapeDtypeStruct((Bn, nh, dk, dv), F32)],
        grid=(Bn,),
        in_specs=[
            pl.BlockSpec((None, 1, nh * dk), per_b3),
            pl.BlockSpec((None, 1, nh * dk), per_b3),
            pl.BlockSpec((None, 1, nh * dv), per_b3),
            pl.BlockSpec((None, 1, nh * dv), per_b3),
            pl.BlockSpec((1, dk // 2), lambda b: (0, 0)),
            pl.BlockSpec((1, nh * dv), lambda b: (0, 0)),
            pl.BlockSpec((None, nh, dk, dv), lambda b: (b, 0, 0, 0)),
        ],
        out_specs=[
            pl.BlockSpec((None, 1, nh * dv), per_b3),
            pl.BlockSpec((None, nh, dk, dv), lambda b: (b, 0, 0, 0)),
        ],
        compiler_params=_cparams(1),
        name="ret_sample",
    )(q, k, v, gate, freqs, gnorm.reshape(1, nh * dv), S0)


def kernel(x_prompt, x_sample, cache_fox_k, cache_fox_v, cache_fox_logf, page_table, state_gdn_conv, state_gdn_S, state_ret_S, norm_g, final_norm_g, ffn_w_gu, ffn_w_down, ab_w_in, ab_w_out, fox_b_f, gdn_conv_w, gdn_A_log, gdn_dt_bias, gdn_norm_g, c_w_in, c_w_out, ret_norm_g):
    B, L, D = x_prompt.shape
    Bn, Ls, _ = x_sample.shape
    assert Ls == 1, "the sample group decodes one token per sequence"
    depth = norm_g.shape[0]
    _, n_pool, page, fh, fd = cache_fox_k.shape
    fw = fh * fd
    _, _, gh, gdk, gdv = state_gdn_S.shape
    conv_dim = gdn_conv_w.shape[2]
    kw = gdn_conv_w.shape[1]
    _, _, rh, rdk, rdv = state_ret_S.shape
    n_pages = page_table.shape[1]
    assert fh == gh and 3 * fh <= 16 and fh & (fh - 1) == 0

    xp = x_prompt.reshape(B * L, D)
    xs = x_sample.reshape(Bn, D)
    w_gu16 = _to_bf16(ffn_w_gu)
    w_down16 = _to_bf16(ffn_w_down)
    freqs = (ROPE_BASE ** (-jnp.arange(rdk // 2, dtype=F32) / (rdk // 2))).reshape(1, rdk // 2)
    cos2, sin2 = _rope_table(L, freqs)

    fkp, fvp, flp, fks, fvs, fls = [], [], [], [], [], []
    gcp, gsp, gcs, gss = [], [], [], []
    rsp, rss = [], []
    for li in range(depth):
        xp = _ffn(xp, norm_g[li, 0], w_gu16, w_down16, li, 0)
        xs = _ffn(xs, norm_g[li, 0], w_gu16, w_down16, li, 0)
        j = li // 2
        if li % 2 == 0:
            o0 = 3 * fw
            zw = gh * gdv
            c0 = o0 + fh
            widths = (conv_dim, fw, fw, fw, zw, LANES)
            w_cat = _regroup_bf16(ab_w_in, j, [(c0, conv_dim), (0, o0), (c0 + conv_dim, zw), (o0, fh),
                                               (c0 + conv_dim + zw, 2 * gh)], sum(widths))
            w_out16 = ab_w_out[j].astype(BF16)

            gqkv, q, k, v, z, small, k3, v3, tails = _ab_proj_prompt(
                xp, norm_g[li, 1], w_cat, gdn_conv_w[j], B, L, widths, fh, fd, gh, gdk)
            gates = _gates(small, fox_b_f[j], gdn_dt_bias[j], gdn_A_log[j])
            gates_row = gates[:, :16].T
            c_rows = _cumsum_rows(gates_row, L)[:fh].reshape(fh, B, L).transpose(1, 0, 2).reshape(B * fh, 1, L)
            fo = _fox_prompt(q, k, v, c_rows, B, L, fh)
            go, S_p = _gdn_prompt(gqkv, z, gates, gates_row, gdn_norm_g[j], B, L, gh, gdk, gdv)
            pre_p = ([fo, go], w_out16)
            fkp.append(k3.reshape(B, L, fh, fd))
            fvp.append(v3.reshape(B, L, fh, fd))
            flp.append(gates[:, :fh].reshape(B, L, fh))
            gcp.append(tails[:, 8 - (kw - 1):])
            gsp.append(S_p)

            cin, q, k, v, z, small = _rms_proj(xs, norm_g[li, 1], w_cat, widths)
            gates = _gates(small, fox_b_f[j], gdn_dt_bias[j], gdn_A_log[j])
            sums = _page_sums(jnp.transpose(cache_fox_logf[j], (0, 2, 1)))
            fo = _fox_sample(q.reshape(Bn, fh, fd), k.reshape(Bn, fh, fd), v.reshape(Bn, fh, fd),
                             gates[:, :fh].reshape(Bn, fh, 1),
                             cache_fox_k[j].reshape(n_pool, page * fh, fd), cache_fox_v[j].reshape(n_pool, page * fh, fd),
                             sums, page_table, fh)
            go, conv_s, S_s = _gdn_sample(cin.reshape(Bn, 1, conv_dim), state_gdn_conv[j], z.reshape(Bn, 1, gh * gdv),
                                          gates.reshape(Bn, 1, LANES), gdn_conv_w[j], gdn_norm_g[j], state_gdn_S[j],
                                          gh, gdk, gdv)
            pre_s = ([fo.reshape(Bn, fw), go.reshape(Bn, gh * gdv)], w_out16)
            fks.append(k.reshape(Bn, 1, fh, fd))
            fvs.append(v.reshape(Bn, 1, fh, fd))
            fls.append(gates[:, :fh].reshape(Bn, 1, fh))
            gcs.append(conv_s)
            gss.append(S_s)
        else:
            w_in16 = _to_bf16(c_w_in[j])
            qk_w, v_w = rh * rdk, rh * rdv
            widths = (qk_w, qk_w, v_w, v_w)
            w_out16 = c_w_out[j].astype(BF16)

            q16, k16, v, sgate = _c_proj_prompt(xp, norm_g[li, 1], w_in16, cos2, sin2, L, rh, rdk, v_w)
            y, R_p = _ret_prompt(q16, k16, v, sgate, ret_norm_g[j], B, L, rh, rdk, rdv)
            pre_p = ([y], w_out16)
            rsp.append(R_p)

            q, k, v, gate = _rms_proj(xs, norm_g[li, 1], w_in16, widths)
            y, R_s = _ret_sample(q.reshape(Bn, 1, qk_w), k.reshape(Bn, 1, qk_w), v.reshape(Bn, 1, v_w),
                                 gate.reshape(Bn, 1, v_w), freqs, ret_norm_g[j], state_ret_S[j],
                                 n_pages * page, rh, rdk, rdv)
            pre_s = ([y.reshape(Bn, v_w)], w_out16)
            rss.append(R_s)
        final_g = final_norm_g if li == depth - 1 else None
        xp = _ffn(xp, norm_g[li, 2], w_gu16, w_down16, li, 1, final_g, pre_p)
        xs = _ffn(xs, norm_g[li, 2], w_gu16, w_down16, li, 1, final_g, pre_s)
    y_prompt = xp.reshape(B, L, D)
    y_sample = xs.reshape(Bn, 1, D)
    return (y_prompt, y_sample,
            jnp.stack(fkp), jnp.stack(fvp), jnp.stack(flp),
            jnp.stack(fks), jnp.stack(fvs), jnp.stack(fls),
            jnp.stack(gcp), jnp.stack(gsp), jnp.stack(gcs), jnp.stack(gss),
            jnp.stack(rsp), jnp.stack(rss))
```

```python
import functools
import math

import jax
import jax.numpy as jnp
from jax import lax
from jax.experimental import pallas as pl
from jax.experimental.pallas import tpu as pltpu

F32 = jnp.float32
BF16 = jnp.bfloat16
EPS = 1e-6
ROPE_BASE = 10000.0
NEG = -1e30
LANES = 128
GDN_CHUNK = 64
VMEM_LIMIT = 56 * 1024 * 1024
HI = lax.Precision.HIGHEST


def _cparams(n_axes):
    return pltpu.CompilerParams(dimension_semantics=("arbitrary",) * n_axes,
                                vmem_limit_bytes=VMEM_LIMIT)


def _dot(a, b, precision=None):
    return jnp.dot(a, b, preferred_element_type=F32, precision=precision)


def _dot_nt(a, b, precision=None):
    return lax.dot_general(a, b, (((1,), (1,)), ((), ())), preferred_element_type=F32, precision=precision)


def _dot_tn(a, b, precision=None):
    return lax.dot_general(a, b, (((0,), (0,)), ((), ())), preferred_element_type=F32, precision=precision)


def _rms_rows(x, g):
    return x * lax.rsqrt(jnp.mean(x * x, axis=-1, keepdims=True) + EPS) * g


MXU_DEPTH = 256


def _ffn_body(*refs, F, chunk, final_norm, n_pre):
    x_ref, g_ref, wgu_ref, wd_ref = refs[:4]
    a_refs = refs[4:4 + n_pre]
    rest = refs[4 + n_pre:]
    x = x_ref[...]
    if n_pre:
        wo_ref, rest = rest[0], rest[1:]
        row = 0
        for a_ref in a_refs:
            n = a_ref.shape[1]
            x = x + _dot(a_ref[...].astype(BF16), wo_ref[row:row + n, :])
            row += n
    if final_norm:
        gf_ref, rest = rest[0], rest[1:]
    o_ref = rest[0]
    xn = _rms_rows(x, g_ref[...]).astype(BF16)
    acc = None
    for c0 in range(0, F, chunk):
        c1 = min(F, c0 + chunk)
        a = _dot(xn, wgu_ref[:, c0:c1])
        b = _dot(xn, wgu_ref[:, F + c0:F + c1])
        h = (jax.nn.silu(a) * b).astype(BF16)
        d = _dot(h, wd_ref[c0:c1, :])
        acc = d if acc is None else acc + d
    y = x + 0.5 * acc
    o_ref[...] = _rms_rows(y, gf_ref[...]) if final_norm else y


def _ffn(x, g, w_gu, w_down, li, j, final_g=None, pre=None):
    M, D = x.shape
    F = w_down.shape[2]
    tm = min(M, 512)
    final_norm = final_g is not None
    const = lambda m: (0, 0)
    in_specs = [pl.BlockSpec((tm, D), lambda m: (m, 0)), pl.BlockSpec((1, D), const),
                pl.BlockSpec((None, None, D, 2 * F), lambda m: (li, j, 0, 0)),
                pl.BlockSpec((None, None, F, D), lambda m: (li, j, 0, 0))]
    args = [x, g.reshape(1, D), w_gu, w_down]
    acts, w_out = pre if pre is not None else ((), None)
    if acts:
        assert sum(a.shape[1] for a in acts) == w_out.shape[0]
        in_specs += [pl.BlockSpec((tm, a.shape[1]), lambda m: (m, 0)) for a in acts]
        in_specs.append(pl.BlockSpec(w_out.shape, const))
        args += [*acts, w_out]
    if final_norm:
        in_specs.append(pl.BlockSpec((1, D), const))
        args.append(final_g.reshape(1, D))
    return pl.pallas_call(
        functools.partial(_ffn_body, F=F, chunk=2 * MXU_DEPTH, final_norm=final_norm, n_pre=len(acts)),
        out_shape=jax.ShapeDtypeStruct((M, D), F32),
        grid=(M // tm,),
        in_specs=in_specs,
        out_specs=pl.BlockSpec((tm, D), lambda m: (m, 0)),
        compiler_params=_cparams(1),
        name="ffn",
    )(*args)


PROJ_CHUNK = 512


def _project(xn, w_ref, col, o_ref, o3_ref=None):
    n = o_ref.shape[1]
    for s0 in range(0, n, PROJ_CHUNK):
        s1 = min(n, s0 + PROJ_CHUNK)
        val = _dot(xn, w_ref[:, col + s0:col + s1])
        o_ref[:, s0:s1] = val
        if o3_ref is not None:
            dh = o3_ref.shape[2]
            for hh in range((s1 - s0) // dh):
                o3_ref[:, s0 // dh + hh, :] = val[:, hh * dh:(hh + 1) * dh]
    return col + n


def _proj_body(x_ref, g_ref, w_ref, *o_refs):
    xn = _rms_rows(x_ref[...], g_ref[...]).astype(BF16)
    col = 0
    for o_ref in o_refs:
        col = _project(xn, w_ref, col, o_ref)


def _rms_proj(x, g, w, widths):
    M, D = x.shape
    tm = min(M, 256)
    assert sum(widths) == w.shape[1] and all(n % LANES == 0 for n in widths)
    return pl.pallas_call(
        _proj_body,
        out_shape=[jax.ShapeDtypeStruct((M, n), F32) for n in widths],
        grid=(M // tm,),
        in_specs=[pl.BlockSpec((tm, D), lambda m: (m, 0)), pl.BlockSpec((1, D), lambda m: (0, 0)),
                  pl.BlockSpec(w.shape, lambda m: (0, 0))],
        out_specs=[pl.BlockSpec((tm, n), lambda m: (m, 0)) for n in widths],
        compiler_params=_cparams(1),
        name="rms_proj",
    )(x, g.reshape(1, D), w)


def _l2_rows(x):
    return x * lax.rsqrt(jnp.sum(x * x, axis=-1, keepdims=True) + EPS)


def _ab_proj_body(x_ref, g_ref, w_ref, cw_ref, gp_ref, qkv_o, q_o, k_o, v_o, z_o, gates_o, k3_o, v3_o, tail_o,
                  win_ref, *, tiles_per_seq, nh, dk):
    m = pl.program_id(0)
    tm = x_ref.shape[0]
    cd = qkv_o.shape[1]
    xn = _rms_rows(x_ref[...], g_ref[...]).astype(BF16)

    @pl.when(lax.rem(m, tiles_per_seq) == 0)
    def _():
        win_ref[0:8, :] = jnp.zeros((8, cd), F32)

    kw = cw_ref.shape[0]
    step = nh * dk

    def conv_product(s0):
        win_ref[8:8 + tm, s0:s0 + step] = _dot(xn, w_ref[:, s0:s0 + step])
        tail_o[:, s0:s0 + step] = win_ref[tm:tm + 8, s0:s0 + step]

    def conv_finish(s0):
        s1 = s0 + step
        conv = win_ref[8 - kw + 1:8 - kw + 1 + tm, s0:s1] * cw_ref[0:1, s0:s1]
        for i in range(1, kw):
            conv = conv + win_ref[8 - kw + 1 + i:8 - kw + 1 + i + tm, s0:s1] * cw_ref[i:i + 1, s0:s1]
        win_ref[0:8, s0:s1] = win_ref[tm:tm + 8, s0:s1]
        conv = jax.nn.silu(conv)
        if s0 >= 2 * step:
            qkv_o[:, s0:s1] = conv
        else:
            for h in range(nh):
                xh = _l2_rows(conv[:, h * dk:(h + 1) * dk])
                qkv_o[:, s0 + h * dk:s0 + (h + 1) * dk] = xh * dk ** -0.5 if s0 == 0 else xh

    assert cd == 3 * step
    conv_product(0)
    conv_product(step)
    col = _project(xn, w_ref, cd, q_o)
    conv_finish(0)
    conv_product(2 * step)
    col = _project(xn, w_ref, col, k_o, k3_o)
    conv_finish(step)
    col = _project(xn, w_ref, col, v_o, v3_o)
    col = _project(xn, w_ref, col, z_o)
    conv_finish(2 * step)
    gates_o[...] = _gate_values(_dot(xn, w_ref[:, col:col + gates_o.shape[1]]), gp_ref, nh)


def _ab_proj_prompt(x, g, w, conv_w, gate_p, B, L, widths, fh, fd, gh, gdk):
    M, D = x.shape
    tm = min(L, 512)
    cd = widths[0]
    assert sum(widths) == w.shape[1] and L % tm == 0
    row = lambda m: (m, 0)
    const = lambda m: (0, 0)
    out_shape = [jax.ShapeDtypeStruct((M, n), F32) for n in widths]
    out_specs = [pl.BlockSpec((tm, n), row) for n in widths]
    for _ in range(2):
        out_shape.append(jax.ShapeDtypeStruct((M, fh, fd), F32))
        out_specs.append(pl.BlockSpec((tm, fh, fd), lambda m: (m, 0, 0)))
    out_shape.append(jax.ShapeDtypeStruct((B, 8, cd), F32))
    out_specs.append(pl.BlockSpec((None, 8, cd), lambda m: (m // (L // tm), 0, 0)))
    return pl.pallas_call(
        functools.partial(_ab_proj_body, tiles_per_seq=L // tm, nh=gh, dk=gdk),
        out_shape=out_shape,
        grid=(M // tm,),
        in_specs=[pl.BlockSpec((tm, D), row), pl.BlockSpec((1, D), const), pl.BlockSpec(w.shape, const),
                  pl.BlockSpec(conv_w.shape, const), pl.BlockSpec(gate_p.shape, const)],
        out_specs=out_specs,
        scratch_shapes=[pltpu.VMEM((8 + tm, cd), F32)],
        compiler_params=_cparams(1),
        name="ab_proj_prompt",
    )(x, g.reshape(1, D), w, conv_w, gate_p)


def _c_proj_body(x_ref, g_ref, w_ref, cos_ref, sin_ref, q_o, k_o, v_o, sg_o, *, nh, dk):
    xn = _rms_rows(x_ref[...], g_ref[...]).astype(BF16)
    cos2, sin2 = cos_ref[...], sin_ref[...]
    hpc = PROJ_CHUNK // dk
    for idx, o_ref in enumerate((q_o, k_o)):
        for h0 in range(0, nh, hpc):
            c0 = idx * nh * dk + h0 * dk
            val = _dot(xn, w_ref[:, c0:c0 + hpc * dk])
            for hh in range(hpc):
                xh = val[:, hh * dk:(hh + 1) * dk]
                xh = xh * cos2 + pltpu.roll(xh, dk // 2, 1) * sin2
                if idx == 1:
                    xh = xh * dk ** -0.5
                o_ref[:, (h0 + hh) * dk:(h0 + hh + 1) * dk] = xh.astype(BF16)
    col = _project(xn, w_ref, 2 * nh * dk, v_o)
    n = sg_o.shape[1]
    for s0 in range(0, n, PROJ_CHUNK):
        s1 = min(n, s0 + PROJ_CHUNK)
        sg_o[:, s0:s1] = jax.nn.silu(_dot(xn, w_ref[:, col + s0:col + s1]))


def _c_proj_prompt(x, g, w, cos2, sin2, L, nh, dk, v_w):
    M, D = x.shape
    tm = min(L, 512)
    qk_w = nh * dk
    row = lambda m: (m, 0)
    const = lambda m: (0, 0)
    pos = lambda m: (lax.rem(m, L // tm), 0)
    return pl.pallas_call(
        functools.partial(_c_proj_body, nh=nh, dk=dk),
        out_shape=[jax.ShapeDtypeStruct((M, qk_w), BF16), jax.ShapeDtypeStruct((M, qk_w), BF16),
                   jax.ShapeDtypeStruct((M, v_w), F32), jax.ShapeDtypeStruct((M, v_w), F32)],
        grid=(M // tm,),
        in_specs=[pl.BlockSpec((tm, D), row), pl.BlockSpec((1, D), const), pl.BlockSpec(w.shape, const),
                  pl.BlockSpec((tm, dk), pos), pl.BlockSpec((tm, dk), pos)],
        out_specs=[pl.BlockSpec((tm, qk_w), row), pl.BlockSpec((tm, qk_w), row),
                   pl.BlockSpec((tm, v_w), row), pl.BlockSpec((tm, v_w), row)],
        compiler_params=_cparams(1),
        name="c_proj_prompt",
    )(x, g.reshape(1, D), w, cos2, sin2)


def _cast_body(x_ref, o_ref):
    o_ref[...] = x_ref[...].astype(o_ref.dtype)


def _to_bf16(w):
    cols = w.shape[-1]
    rows = w.size // cols
    tr = 512 if rows % 512 == 0 else rows
    out = pl.pallas_call(
        _cast_body,
        out_shape=jax.ShapeDtypeStruct((rows, cols), BF16),
        grid=(rows // tr,),
        in_specs=[pl.BlockSpec((tr, cols), lambda r: (r, 0))],
        out_specs=pl.BlockSpec((tr, cols), lambda r: (r, 0)),
        compiler_params=_cparams(1),
        name="to_bf16",
    )(w.reshape(rows, cols))
    return out.reshape(w.shape)


def _regroup_body(wt_ref, o_ref, *, pieces):
    N, D = wt_ref.shape
    col = 0
    narrow = []
    for start, width in pieces:
        if width % LANES == 0:
            assert not narrow and col % LANES == 0
            for r0 in range(0, width, PROJ_CHUNK):
                r1 = min(width, r0 + PROJ_CHUNK)
                o_ref[:, col + r0:col + r1] = wt_ref[start + r0:start + r1, :].T.astype(o_ref.dtype)
        else:
            w0 = min(start, N - LANES)
            narrow.append(wt_ref[w0:w0 + LANES, :].T[:, start - w0:start - w0 + width])
        col += width
    if narrow:
        first = col - sum(p.shape[1] for p in narrow)
        assert first % LANES == 0 and o_ref.shape[1] - first == LANES
        pad = jnp.zeros((D, o_ref.shape[1] - col), F32)
        o_ref[:, first:] = jnp.concatenate(narrow + [pad], axis=1).astype(o_ref.dtype)


def _regroup_bf16(wt, pieces, n_out):
    N, D = wt.shape
    return pl.pallas_call(
        functools.partial(_regroup_body, pieces=tuple(pieces)),
        out_shape=jax.ShapeDtypeStruct((D, n_out), BF16),
        grid=(1,),
        in_specs=[pl.BlockSpec((N, D), lambda r: (0, 0))],
        out_specs=pl.BlockSpec((D, n_out), lambda r: (0, 0)),
        compiler_params=_cparams(1),
        name="regroup_bf16",
    )(wt)


def _gate_values(s, p_ref, nh):
    s = s + p_ref[0:1, :]
    lane = lax.broadcasted_iota(jnp.int32, s.shape, 1)
    logf = jax.nn.log_sigmoid(s)
    g = -jnp.exp(p_ref[1:2, :]) * jax.nn.softplus(s)
    beta = jax.nn.sigmoid(s)
    return jnp.where(lane < nh, logf, jnp.where(lane < 2 * nh, g, jnp.where(lane < 3 * nh, beta, 0.0)))


def _gate_params(b_f, dt_bias, a_log):
    nh = b_f.shape[0]
    pad = jnp.zeros((LANES - 3 * nh,), F32)
    p = jnp.zeros((8, LANES), F32)
    p = p.at[0].set(jnp.concatenate([b_f, dt_bias, jnp.zeros((nh,), F32), pad]))
    return p.at[1].set(jnp.concatenate([jnp.zeros((nh,), F32), a_log, jnp.zeros((nh,), F32), pad]))


def _gates_body(s_ref, p_ref, o_ref, *, nh):
    o_ref[...] = _gate_values(s_ref[...], p_ref, nh)


def _gates(small, gate_p, nh):
    M = small.shape[0]
    tm = min(M, 2048)
    return pl.pallas_call(
        functools.partial(_gates_body, nh=nh),
        out_shape=jax.ShapeDtypeStruct((M, LANES), F32),
        grid=(M // tm,),
        in_specs=[pl.BlockSpec((tm, LANES), lambda m: (m, 0)), pl.BlockSpec((8, LANES), lambda m: (0, 0))],
        out_specs=pl.BlockSpec((tm, LANES), lambda m: (m, 0)),
        compiler_params=_cparams(1),
        name="gates",
    )(small, gate_p)


def _cumsum_body(x_ref, o_ref):
    L = x_ref.shape[1]
    r = lax.broadcasted_iota(jnp.int32, (LANES, LANES), 0)
    c = lax.broadcasted_iota(jnp.int32, (LANES, LANES), 1)
    upper = (r <= c).astype(F32)
    carry = jnp.zeros((x_ref.shape[0], 1), F32)
    for j in range(L // LANES):
        cs = _dot(x_ref[:, j * LANES:(j + 1) * LANES], upper, HI) + carry
        o_ref[:, j * LANES:(j + 1) * LANES] = cs
        carry = cs[:, LANES - 1:LANES]


def _cumsum_rows(x, seg):
    R, M = x.shape
    return pl.pallas_call(
        _cumsum_body,
        out_shape=jax.ShapeDtypeStruct((R, M), F32),
        grid=(M // seg,),
        in_specs=[pl.BlockSpec((R, seg), lambda b: (0, b))],
        out_specs=pl.BlockSpec((R, seg), lambda b: (0, b)),
        compiler_params=_cparams(1),
        name="cumsum_rows",
    )(x)


def _fox_body(q_ref, k_ref, v_ref, c_ref, o_ref, vt_ref, cb_ref, acc_ref, m_ref, *, t, hg, dh, scale):
    i = pl.program_id(2)
    L = k_ref.shape[0]
    log2e = math.log2(math.e)
    heads = range(hg)

    @pl.when(i == 0)
    def _():
        for g in heads:
            for jj in range(L // t):
                vt_ref[g, 0:dh, jj * t:(jj + 1) * t] = v_ref[jj * t:(jj + 1) * t, g * dh:(g + 1) * dh].T.astype(BF16)
            vt_ref[g, dh:, :] = jnp.ones((vt_ref.shape[1] - dh, L), BF16)
            for jj in range(L // LANES):
                row = c_ref[g, :, jj * LANES:(jj + 1) * LANES] * log2e
                cb_ref[g, jj * LANES:(jj + 1) * LANES, :] = jnp.broadcast_to(row, (LANES, LANES)).T

    q0 = pl.multiple_of(i * t, t)
    qs = [q_ref[:, g * dh:(g + 1) * dh].astype(BF16) for g in heads]
    r2 = [c_ref[g, :, pl.ds(q0, t)][:, t - 1:t] * log2e for g in heads]
    m_ref[...] = jnp.full_like(m_ref, NEG)
    acc_ref[...] = jnp.zeros_like(acc_ref)

    def step(j, masked):
        k0 = pl.multiple_of(j * t, t)
        qk = [_dot_nt(k_ref[pl.ds(k0, t), g * dh:(g + 1) * dh].astype(BF16), qs[g]) for g in heads]
        ps, alphas = [], []
        for g in heads:
            bias = r2[g] - cb_ref[g, pl.ds(k0, t), :]
            st = qk[g] * (scale * log2e) + jnp.concatenate([bias] * (t // LANES), axis=1)
            if masked:
                key = lax.broadcasted_iota(jnp.int32, (t, t), 0)
                qry = lax.broadcasted_iota(jnp.int32, (t, t), 1)
                st = jnp.where(key <= qry, st, NEG)
            m_prev = m_ref[g]
            m_new = jnp.maximum(m_prev, jnp.max(st, axis=0, keepdims=True))
            ps.append(jnp.exp2(st - m_new).astype(BF16))
            alphas.append(jnp.exp2(m_prev - m_new))
            m_ref[g] = m_new
        for g in heads:
            acc_ref[g] = acc_ref[g] * alphas[g] + _dot(vt_ref[g, :, pl.ds(k0, t)], ps[g])

    def loop_body(j, carry):
        step(j, False)
        return carry

    lax.fori_loop(0, i, loop_body, 0)
    step(i, True)
    for g in heads:
        acc = acc_ref[g]
        o_ref[:, g * dh:(g + 1) * dh] = (acc[0:dh] / acc[dh:dh + 1]).T


def _fox_prompt(q, k, v, c_rows, B, L, H):
    M, W = q.shape
    Dh = W // H
    t = min(L, 512)
    nq = L // t
    hg = 2 if H % 2 == 0 else 1
    kernel = functools.partial(_fox_body, t=t, hg=hg, dh=Dh, scale=Dh ** -0.5)
    return pl.pallas_call(
        kernel,
        out_shape=jax.ShapeDtypeStruct((M, W), F32),
        grid=(B, H // hg, nq),
        in_specs=[
            pl.BlockSpec((t, hg * Dh), lambda b, h, i: (b * nq + i, h)),
            pl.BlockSpec((L, hg * Dh), lambda b, h, i: (b, h)),
            pl.BlockSpec((L, hg * Dh), lambda b, h, i: (b, h)),
            pl.BlockSpec((hg, 1, L), lambda b, h, i: (b * (H // hg) + h, 0, 0)),
        ],
        out_specs=pl.BlockSpec((t, hg * Dh), lambda b, h, i: (b * nq + i, h)),
        scratch_shapes=[pltpu.VMEM((hg, Dh + 16, L), BF16), pltpu.VMEM((hg, L, LANES), F32),
                        pltpu.VMEM((hg, Dh + 16, t), F32), pltpu.VMEM((hg, 1, t), F32)],
        compiler_params=_cparams(3),
        name="fox_prompt",
    )(q, k, v, c_rows)


def _split2(x):
    hi = x.astype(BF16)
    return hi, (x - hi.astype(F32)).astype(BF16)


def _split3(x):
    h1 = x.astype(BF16)
    r1 = x - h1.astype(F32)
    h2 = r1.astype(BF16)
    return h1, h2, (r1 - h2.astype(F32)).astype(BF16)


def _dot3(a, b):
    (ah, al), (bh, bl) = a, b
    return _dot(ah, bh) + (_dot(ah, bl) + _dot(al, bh))


def _unit_lower_inverses(a_list, n):
    ri = lax.broadcasted_iota(jnp.int32, (n, n), 0)
    ci = lax.broadcasted_iota(jnp.int32, (n, n), 1)
    eye = (ri == ci).astype(F32)
    nm = len(a_list)
    diag16 = (ri >> 4) == (ci >> 4)
    pw = [jnp.where(diag16, a, 0.0) for a in a_list]
    t = [eye - p for p in pw]
    for _ in range(3):
        pw16 = [p.astype(BF16) for p in pw]
        pw = [_dot(p, p) for p in pw16]
        pn16 = [p.astype(BF16) for p in pw]
        t = [t[m] + _dot(t[m].astype(BF16), pn16[m]) for m in range(nm)]
    size = 16
    while size < GDN_CHUNK:
        sh = size.bit_length() - 1
        off = ((ri >> (sh + 1)) == (ci >> (sh + 1))) & ((ri >> sh) == (ci >> sh) + 1)
        t16 = [x.astype(BF16) for x in t]
        mid = [_dot(jnp.where(off, a_list[m], 0.0).astype(BF16), t16[m]) for m in range(nm)]
        t = [t[m] - _dot(t16[m], mid[m].astype(BF16)) for m in range(nm)]
        size *= 2
    ts = [_split2(x) for x in t]
    res = [(eye - t[m]) - _dot3(_split2(a_list[m]), ts[m]) for m in range(nm)]
    return [t[m] + _dot(ts[m][0], res[m].astype(BF16)) for m in range(nm)]


def _gdn_body(qkv_ref, z_ref, gc_ref, *rest, nh, dk, dv, nseq):
    gr_refs = rest[:nseq]
    ng_ref, o_ref, s_out_ref, s_ref, m16_ref = rest[nseq:]
    j = pl.program_id(1)
    T = qkv_ref.shape[1]
    C = GDN_CHUNK
    nc = T // C

    ri = lax.broadcasted_iota(jnp.int32, (T, T), 0)
    ci = lax.broadcasted_iota(jnp.int32, (T, T), 1)
    csh = C.bit_length() - 1
    same = (ri >> csh) == (ci >> csh)
    tril = same & (ri >= ci)
    strict = same & (ri > ci)

    @pl.when((pl.program_id(0) == 0) & (j == 0))
    def _():
        m16_ref[0] = tril.astype(BF16)
        m16_ref[1] = (same & (ri <= ci)).astype(BF16)

    @pl.when(j == 0)
    def _():
        s_ref[...] = jnp.zeros_like(s_ref)

    rowi = lax.broadcasted_iota(jnp.int32, (T, 1), 0)
    units = [(s, h) for s in range(nseq) for h in range(nh)]
    nu = len(units)
    gcols, grows = [], []
    for s in range(nseq):
        gcols.append(sum(_dot(m16_ref[0], part) for part in _split3(gc_ref[s])))
        grows.append(sum(_dot(part, m16_ref[1]) for part in _split3(gr_refs[s][...])))

    q, k, beta, gcc, gcr, gam, kb, k16 = [], [], [], [], [], [], [], []
    for u, (s, h) in enumerate(units):
        q.append(qkv_ref[s, :, h * dk:(h + 1) * dk])
        k.append(qkv_ref[s, :, nh * dk + h * dk:nh * dk + (h + 1) * dk])
        beta.append(gc_ref[s, :, 2 * nh + h:2 * nh + h + 1])
        gcc.append(gcols[s][:, nh + h:nh + h + 1])
        gcr.append(grows[s][nh + h:nh + h + 1, :])
        gam.append(jnp.exp(jnp.where(tril, gcc[u] - gcr[u], NEG)))
        kb.append(k[u] * beta[u])
        k16.append(k[u].astype(BF16))
    a = [jnp.where(strict, _dot_nt(kb[u].astype(BF16), k16[u]) * gam[u], 0.0) for u in range(nu)]
    tinv = _unit_lower_inverses(a, T)
    x = []
    for u, (s, h) in enumerate(units):
        v = qkv_ref[s, :, 2 * nh * dk + h * dv:2 * nh * dk + (h + 1) * dv]
        rhs = jnp.concatenate([v * beta[u], kb[u] * jnp.exp(gcc[u])], axis=-1)
        x.append(_dot3(_split2(tinv[u]), _split2(rhs)))
    attn, qg, kd, egl = [], [], [], []
    for u in range(nu):
        attn.append((_dot_nt(q[u].astype(BF16), k16[u]) * gam[u]).astype(BF16))
        qg.append((q[u] * jnp.exp(gcc[u])).astype(BF16))
        gl = gcr[u][:, C - 1:C]
        for cidx in range(1, nc):
            gl = jnp.where(rowi >= cidx * C, gcr[u][:, (cidx + 1) * C - 1:(cidx + 1) * C], gl)
        kd.append((k[u] * jnp.exp(gl - gcc[u])).astype(BF16))
        egl.append(jnp.exp(gl))
    S = [s_ref[s, h] for s, h in units]
    outs = [[] for _ in units]
    for cidx in range(nc):
        sl = slice(cidx * C, (cidx + 1) * C)
        S16 = [st.astype(BF16) for st in S]
        vn16 = [(x[u][sl, :dv] - _dot(x[u][sl, dv:].astype(BF16), S16[u])).astype(BF16) for u in range(nu)]
        for u in range(nu):
            outs[u].append(_dot(qg[u][sl], S16[u]) + _dot(attn[u][sl, cidx * C:(cidx + 1) * C], vn16[u]))
        S = [S[u] * egl[u][cidx * C:cidx * C + 1] + _dot_tn(kd[u][sl], vn16[u]) for u in range(nu)]
    for u, (s, h) in enumerate(units):
        s_ref[s, h] = S[u]
        o = _rms_rows(jnp.concatenate(outs[u], axis=0), ng_ref[...])
        o_ref[s, :, h * dv:(h + 1) * dv] = o * jax.nn.silu(z_ref[s, :, h * dv:(h + 1) * dv])

    @pl.when(j == pl.num_programs(1) - 1)
    def _():
        s_out_ref[...] = s_ref[...]


def _gdn_prompt(qkv, z, gates_col, gates_row, norm_g, B, L, nh, dk, dv):
    M, CD = qkv.shape
    T = 2 * GDN_CHUNK
    nb = L // T
    nseq = 2 if B % 2 == 0 else 1
    kernel = functools.partial(_gdn_body, nh=nh, dk=dk, dv=dv, nseq=nseq)

    def row_map(s):
        return lambda p, j: (0, (p * nseq + s) * nb + j)

    per_seq = lambda p, j: (p, j, 0)
    go, S = pl.pallas_call(
        kernel,
        out_shape=[jax.ShapeDtypeStruct((B, L, nh * dv), F32), jax.ShapeDtypeStruct((B, nh, dk, dv), F32)],
        grid=(B // nseq, nb),
        in_specs=[
            pl.BlockSpec((nseq, T, CD), per_seq),
            pl.BlockSpec((nseq, T, nh * dv), per_seq),
            pl.BlockSpec((nseq, T, LANES), per_seq),
        ] + [pl.BlockSpec((16, T), row_map(s)) for s in range(nseq)] + [
            pl.BlockSpec((1, dv), lambda p, j: (0, 0)),
        ],
        out_specs=[
            pl.BlockSpec((nseq, T, nh * dv), per_seq),
            pl.BlockSpec((nseq, nh, dk, dv), lambda p, j: (p, 0, 0, 0)),
        ],
        scratch_shapes=[pltpu.VMEM((nseq, nh, dk, dv), F32), pltpu.VMEM((2, T, T), BF16)],
        compiler_params=_cparams(2),
        name="gdn_prompt",
    )(qkv.reshape(B, L, CD), z.reshape(B, L, nh * dv), gates_col.reshape(B, L, LANES),
      *([gates_row] * nseq), norm_g.reshape(1, dv))
    return go.reshape(M, nh * dv), S


def _rope(x, cos2, sin2, nh, dk):
    outs = []
    for h in range(nh):
        xh = x[:, h * dk:(h + 1) * dk]
        outs.append(xh * cos2 + pltpu.roll(xh, dk // 2, 1) * sin2)
    return outs


def _rope_tables(pos, freqs):
    ang = pos * freqs
    cos, sin = jnp.cos(ang), jnp.sin(ang)
    return jnp.concatenate([cos, cos], axis=-1), jnp.concatenate([-sin, sin], axis=-1)


def _head_layernorm(o):
    mu = jnp.mean(o, axis=-1, keepdims=True)
    oc = o - mu
    return oc * lax.rsqrt(jnp.mean(oc * oc, axis=-1, keepdims=True) + EPS)


def _log_gamma(h):
    return math.log(1.0 - 2.0 ** (-5.0 - h))


def _rope_table_body(fr_ref, cos_o, sin_o):
    C = cos_o.shape[0]
    rowi = lax.broadcasted_iota(jnp.int32, (C, 1), 0)
    pos = (pl.program_id(0) * C + rowi).astype(F32)
    cos_o[...], sin_o[...] = _rope_tables(pos, fr_ref[...])


def _rope_table(L, freqs):
    C = min(L, 512)
    dk = 2 * freqs.shape[1]
    return pl.pallas_call(
        _rope_table_body,
        out_shape=[jax.ShapeDtypeStruct((L, dk), F32)] * 2,
        grid=(L // C,),
        in_specs=[pl.BlockSpec(freqs.shape, lambda c: (0, 0))],
        out_specs=[pl.BlockSpec((C, dk), lambda c: (c, 0))] * 2,
        compiler_params=_cparams(1),
        name="rope_table",
    )(freqs)


def _ret_body(q_ref, k_ref, v_ref, sg_ref, gn_ref, o_ref, s_out_ref, s_ref, dmat_ref, *, nh, dk, dv):
    b = pl.program_id(0)
    j = pl.program_id(1)
    C = q_ref.shape[0]
    rowf = lax.broadcasted_iota(jnp.int32, (C, 1), 0).astype(F32)

    @pl.when((b == 0) & (j == 0))
    def _():
        ri = lax.broadcasted_iota(jnp.int32, (C, C), 0)
        ci = lax.broadcasted_iota(jnp.int32, (C, C), 1)
        diff = (ri - ci).astype(F32)
        for h in range(nh):
            dmat_ref[h] = jnp.exp(jnp.where(ri >= ci, diff * _log_gamma(h), NEG))

    @pl.when(j == 0)
    def _():
        s_ref[...] = jnp.zeros_like(s_ref)

    heads = range(nh)
    q16 = [q_ref[:, h * dk:(h + 1) * dk] for h in heads]
    k16 = [k_ref[:, h * dk:(h + 1) * dk] for h in heads]
    v = [v_ref[:, h * dv:(h + 1) * dv] for h in heads]
    S = [s_ref[h] for h in heads]
    qk = [_dot_nt(q16[h], k16[h]) for h in heads]
    cross = [_dot(q16[h], S[h].astype(BF16)) for h in heads]
    for h in heads:
        vd = (v[h] * jnp.exp(_log_gamma(h) * (C - 1.0 - rowf))).astype(BF16)
        s_ref[h] = S[h] * math.exp(_log_gamma(h) * C) + _dot_tn(k16[h], vd)
    inner = [_dot((qk[h] * dmat_ref[h]).astype(BF16), v[h].astype(BF16)) for h in heads]
    for h in heads:
        o = inner[h] + cross[h] * jnp.exp(_log_gamma(h) * (rowf + 1.0))
        o = _head_layernorm(o) * gn_ref[:, h * dv:(h + 1) * dv]
        o_ref[:, h * dv:(h + 1) * dv] = sg_ref[:, h * dv:(h + 1) * dv] * o

    @pl.when(j == pl.num_programs(1) - 1)
    def _():
        s_out_ref[...] = s_ref[...]


def _ret_prompt(q16, k16, v, sgate, gnorm, B, L, nh, dk, dv):
    M = q16.shape[0]
    C = 128
    nb = L // C
    kernel = functools.partial(_ret_body, nh=nh, dk=dk, dv=dv)
    tok = lambda b, j: (b * nb + j, 0)
    return pl.pallas_call(
        kernel,
        out_shape=[jax.ShapeDtypeStruct((M, nh * dv), F32), jax.ShapeDtypeStruct((B, nh, dk, dv), F32)],
        grid=(B, nb),
        in_specs=[
            pl.BlockSpec((C, nh * dk), tok),
            pl.BlockSpec((C, nh * dk), tok),
            pl.BlockSpec((C, nh * dv), tok),
            pl.BlockSpec((C, nh * dv), tok),
            pl.BlockSpec((1, nh * dv), lambda b, j: (0, 0)),
        ],
        out_specs=[
            pl.BlockSpec((C, nh * dv), tok),
            pl.BlockSpec((None, nh, dk, dv), lambda b, j: (b, 0, 0, 0)),
        ],
        scratch_shapes=[pltpu.VMEM((nh, dk, dv), F32), pltpu.VMEM((nh, C, C), F32)],
        compiler_params=_cparams(2),
        name="ret_prompt",
    )(q16, k16, v, sgate, gnorm.reshape(1, nh * dv))


def _page_sums_body(x_ref, o_ref, op_ref):
    _, nh, plen = x_ref.shape
    n = nh * plen
    wide = o_ref.shape[1]

    @pl.when(pl.program_id(0) == 0)
    def _():
        kk = lax.broadcasted_iota(jnp.int32, (plen, wide), 0)
        cc = lax.broadcasted_iota(jnp.int32, (plen, wide), 1)
        key_c = cc >> (nh.bit_length() - 1)
        later_or_total = (cc >= n) | (kk > key_c)
        for h in range(nh):
            op_ref[h] = (((cc & (nh - 1)) == h) & later_or_total).astype(BF16)

    acc = None
    for h in range(nh):
        for part in _split2(x_ref[:, h, :]):
            d = _dot(part, op_ref[h])
            acc = d if acc is None else acc + d
    o_ref[...] = acc


def _page_sums(lf_pages):
    P, nh, plen = lf_pages.shape
    wide = nh * plen + LANES
    tp = 512 if P % 512 == 0 else P
    return pl.pallas_call(
        _page_sums_body,
        out_shape=jax.ShapeDtypeStruct((P, wide), F32),
        grid=(P // tp,),
        in_specs=[pl.BlockSpec((tp, nh, plen), lambda p: (p, 0, 0))],
        out_specs=pl.BlockSpec((tp, wide), lambda p: (p, 0)),
        scratch_shapes=[pltpu.VMEM((nh, plen, wide), BF16)],
        compiler_params=_cparams(1),
        name="page_sums",
    )(lf_pages)


def _paged_body(pt_ref, *refs, G, nh, scale):
    k_refs = refs[:G]
    v_refs = refs[G:2 * G]
    sums_ref, q_ref, kn_ref, vn_ref, lfn_ref, o_ref, m_ref, l_ref, acc_ref, carry_ref = refs[2 * G:]
    b_idx = pl.program_id(0)
    s_idx = pl.program_id(1)
    ns = pl.num_programs(1)
    n = k_refs[0].shape[0]
    dh = q_ref.shape[1]

    @pl.when(s_idx == 0)
    def _():
        m_ref[...] = jnp.full_like(m_ref, NEG)
        l_ref[...] = jnp.zeros_like(l_ref)
        acc_ref[...] = jnp.zeros_like(acc_ref)
        carry_ref[...] = jnp.zeros_like(carry_ref)

    q = q_ref[...]
    q8 = jnp.concatenate([q, jnp.zeros((8 - nh, dh), F32)], axis=0).astype(BF16)
    lfn = jnp.concatenate([lfn_ref[...], jnp.zeros((8 - nh, 1), F32)], axis=0)
    hrow = lax.broadcasted_iota(jnp.int32, (8, n), 0)
    hcol = lax.broadcasted_iota(jnp.int32, (8, n), 1) & (nh - 1)
    match = hrow == hcol
    carry = carry_ref[...]
    s_parts = [None] * G
    for i in reversed(range(G)):
        page_id = pt_ref[b_idx, (ns - 1 - s_idx) * G + i]
        wt = sums_ref[pl.ds(page_id, 1), :]
        bias = wt[:, :n] + carry
        carry = carry + jnp.concatenate([wt[:, n:]] * (n // LANES), axis=1)
        s = _dot_nt(q8, k_refs[i][...].astype(BF16)) * scale + bias + lfn
        s_parts[i] = jnp.where(match, s, NEG)
    carry_ref[...] = carry
    m_run, l_run, acc_run = m_ref[...], l_ref[...], acc_ref[...]
    half = max(G // 2, 1)
    for grp in (range(half, G), range(0, half)) if G > 1 else (range(G),):
        sg = [s_parts[i] for i in grp]
        m_new = jnp.maximum(m_run, jnp.max(functools.reduce(jnp.maximum, sg), axis=-1, keepdims=True))
        alpha = jnp.exp(m_run - m_new)
        pg = [jnp.exp(s - m_new) for s in sg]
        l_run = alpha * l_run + jnp.sum(functools.reduce(jnp.add, pg), axis=-1, keepdims=True)
        pv = functools.reduce(jnp.add, [_dot(p.astype(BF16), v_refs[i][...].astype(BF16)) for p, i in zip(pg, grp)])
        acc_run = alpha * acc_run + pv
        m_run = m_new
    m_ref[...], l_ref[...], acc_ref[...] = m_run, l_run, acc_run

    @pl.when(s_idx == pl.num_programs(1) - 1)
    def _():
        qf = q.astype(BF16).astype(F32)
        kn = kn_ref[...].astype(BF16).astype(F32)
        s_new = jnp.sum(qf * kn, axis=-1, keepdims=True) * scale
        m_prev = m_ref[0:nh, :]
        m_new = jnp.maximum(m_prev, s_new)
        alpha = jnp.exp(m_prev - m_new)
        p_new = jnp.exp(s_new - m_new)
        l_fin = alpha * l_ref[0:nh, :] + p_new
        vn = vn_ref[...].astype(BF16).astype(F32)
        num = alpha * acc_ref[0:nh, :] + p_new.astype(BF16).astype(F32) * vn
        o_ref[...] = num / l_fin


def _fox_sample(q, k_new, v_new, lf_new, k_pages, v_pages, page_sums, page_table, nh):
    Bn, _, dh = q.shape
    n_pages = page_table.shape[1]
    n_pool, n, _ = k_pages.shape
    G = 16 if n_pages % 16 == 0 else (8 if n_pages % 8 == 0 else 1)
    ns = n_pages // G

    def page_map(i):
        return lambda b, s, pt: (pt[b, (ns - 1 - s) * G + i], 0, 0)

    per_b = lambda b, s, pt: (b, 0, 0)
    grid_spec = pltpu.PrefetchScalarGridSpec(
        num_scalar_prefetch=1,
        grid=(Bn, ns),
        in_specs=[pl.BlockSpec((None, n, dh), page_map(i)) for i in range(G)]
        + [pl.BlockSpec((None, n, dh), page_map(i)) for i in range(G)]
        + [pl.BlockSpec(page_sums.shape, lambda b, s, pt: (0, 0)),
           pl.BlockSpec((None, nh, dh), per_b), pl.BlockSpec((None, nh, dh), per_b),
           pl.BlockSpec((None, nh, dh), per_b), pl.BlockSpec((None, nh, 1), per_b)],
        out_specs=pl.BlockSpec((None, nh, dh), per_b),
        scratch_shapes=[pltpu.VMEM((8, 1), F32), pltpu.VMEM((8, 1), F32), pltpu.VMEM((8, dh), F32),
                        pltpu.VMEM((1, n), F32)],
    )
    return pl.pallas_call(
        functools.partial(_paged_body, G=G, nh=nh, scale=dh ** -0.5),
        out_shape=jax.ShapeDtypeStruct((Bn, nh, dh), F32),
        grid_spec=grid_spec,
        compiler_params=_cparams(2),
        name="fox_sample",
    )(page_table, *([k_pages] * G), *([v_pages] * G), page_sums, q, k_new, v_new, lf_new)


def _pad8(row):
    return jnp.concatenate([row, jnp.zeros((8 - row.shape[0], row.shape[1]), row.dtype)], axis=0)


def _gdn_sample_body(cin_ref, prev_ref, z_ref, gt_ref, cw_ref, ng_ref, s0_ref, o_ref, cnew_ref, s_out_ref, *, nh, dk, dv):
    u = cin_ref[...]
    prev = prev_ref[...]
    w = cw_ref[...]
    kw = w.shape[0]
    conv = u * w[kw - 1:kw]
    for i in range(kw - 1):
        conv = conv + prev[i:i + 1] * w[i:i + 1]
    cnew_ref[0:kw - 2, :] = prev[1:kw - 1]
    cnew_ref[kw - 2:kw - 1, :] = u
    conv = jax.nn.silu(conv)
    gt = gt_ref[...]
    heads = range(nh)
    q = [_l2_rows(conv[:, h * dk:(h + 1) * dk]) * dk ** -0.5 for h in heads]
    k = [_l2_rows(conv[:, nh * dk + h * dk:nh * dk + (h + 1) * dk]) for h in heads]
    v = [conv[:, 2 * nh * dk + h * dv:2 * nh * dk + (h + 1) * dv] for h in heads]
    beta = [gt[:, 2 * nh + h:2 * nh + h + 1] for h in heads]
    eg = [jnp.exp(gt[:, nh + h:nh + h + 1]) for h in heads]
    S = [s0_ref[h] for h in heads]
    r = [_dot(_pad8(jnp.concatenate([k[h] * beta[h] * eg[h], q[h] * eg[h]], axis=0)).astype(BF16), S[h].astype(BF16))
         for h in heads]
    vn16 = [(v[h] * beta[h] - r[h][0:1]).astype(BF16) for h in heads]
    for h in heads:
        s_out_ref[h] = S[h] * eg[h] + _dot_tn(_pad8(k[h]).astype(BF16), _pad8(vn16[h]))
    for h in heads:
        qk = jnp.sum(q[h].astype(BF16).astype(F32) * k[h].astype(BF16).astype(F32), axis=-1, keepdims=True)
        o = r[h][1:2] + qk.astype(BF16).astype(F32) * vn16[h].astype(F32)
        o = _rms_rows(o, ng_ref[...])
        o_ref[:, h * dv:(h + 1) * dv] = o * jax.nn.silu(z_ref[:, h * dv:(h + 1) * dv])


def _gdn_sample(conv_in, conv_prev, z, gates_col, conv_w, norm_g, S0, nh, dk, dv):
    Bn, _, CD = conv_in.shape
    kw = conv_w.shape[0]
    per_b3 = lambda b: (b, 0, 0)
    kernel = functools.partial(_gdn_sample_body, nh=nh, dk=dk, dv=dv)
    return pl.pallas_call(
        kernel,
        out_shape=[jax.ShapeDtypeStruct((Bn, 1, nh * dv), F32), jax.ShapeDtypeStruct((Bn, kw - 1, CD), F32),
                   jax.ShapeDtypeStruct((Bn, nh, dk, dv), F32)],
        grid=(Bn,),
        in_specs=[
            pl.BlockSpec((None, 1, CD), per_b3),
            pl.BlockSpec((None, kw - 1, CD), per_b3),
            pl.BlockSpec((None, 1, nh * dv), per_b3),
            pl.BlockSpec((None, 1, LANES), per_b3),
            pl.BlockSpec(conv_w.shape, lambda b: (0, 0)),
            pl.BlockSpec((1, dv), lambda b: (0, 0)),
            pl.BlockSpec((None, nh, dk, dv), lambda b: (b, 0, 0, 0)),
        ],
        out_specs=[
            pl.BlockSpec((None, 1, nh * dv), per_b3),
            pl.BlockSpec((None, kw - 1, CD), per_b3),
            pl.BlockSpec((None, nh, dk, dv), lambda b: (b, 0, 0, 0)),
        ],
        compiler_params=_cparams(1),
        name="gdn_sample",
    )(conv_in, conv_prev, z, gates_col, conv_w, norm_g.reshape(1, dv), S0)


def _ret_sample_body(q_ref, k_ref, v_ref, gate_ref, fr_ref, gn_ref, s0_ref, o_ref, s_out_ref, *, nh, dk, dv, pos):
    cos2, sin2 = _rope_tables(jnp.full((1, 1), pos, F32), fr_ref[...])
    qs = _rope(q_ref[...], cos2, sin2, nh, dk)
    ks = _rope(k_ref[...], cos2, sin2, nh, dk)
    heads = range(nh)
    ks = [kh * dk ** -0.5 for kh in ks]
    v16 = [v_ref[:, h * dv:(h + 1) * dv].astype(BF16) for h in heads]
    S = [s0_ref[h] for h in heads]
    cross = [_dot(_pad8(qs[h] * math.exp(_log_gamma(h))).astype(BF16), S[h].astype(BF16))[0:1] for h in heads]
    for h in heads:
        s_out_ref[h] = S[h] * math.exp(_log_gamma(h)) + _dot_tn(_pad8(ks[h]).astype(BF16), _pad8(v16[h]))
    for h in heads:
        qk = jnp.sum(qs[h].astype(BF16).astype(F32) * ks[h].astype(BF16).astype(F32), axis=-1, keepdims=True)
        inner = qk.astype(BF16).astype(F32) * v16[h].astype(F32)
        o = _head_layernorm(inner + cross[h]) * gn_ref[:, h * dv:(h + 1) * dv]
        o_ref[:, h * dv:(h + 1) * dv] = jax.nn.silu(gate_ref[:, h * dv:(h + 1) * dv]) * o


def _ret_sample(q, k, v, gate, freqs, gnorm, S0, pos, nh, dk, dv):
    Bn = q.shape[0]
    per_b3 = lambda b: (b, 0, 0)
    kernel = functools.partial(_ret_sample_body, nh=nh, dk=dk, dv=dv, pos=float(pos))
    return pl.pallas_call(
        kernel,
        out_shape=[jax.ShapeDtypeStruct((Bn, 1, nh * dv), F32), jax.ShapeDtypeStruct((Bn, nh, dk, dv), F32)],
        grid=(Bn,),
        in_specs=[
            pl.BlockSpec((None, 1, nh * dk), per_b3),
            pl.BlockSpec((None, 1, nh * dk), per_b3),
            pl.BlockSpec((None, 1, nh * dv), per_b3),
            pl.BlockSpec((None, 1, nh * dv), per_b3),
            pl.BlockSpec((1, dk // 2), lambda b: (0, 0)),
            pl.BlockSpec((1, nh * dv), lambda b: (0, 0)),
            pl.BlockSpec((None, nh, dk, dv), lambda b: (b, 0, 0, 0)),
        ],
        out_specs=[
            pl.BlockSpec((None, 1, nh * dv), per_b3),
            pl.BlockSpec((None, nh, dk, dv), lambda b: (b, 0, 0, 0)),
        ],
        compiler_params=_cparams(1),
        name="ret_sample",
    )(q, k, v, gate, freqs, gnorm.reshape(1, nh * dv), S0)


def kernel(x_prompt, x_sample, cache_fox_k, cache_fox_v, cache_fox_logf, page_table, state_gdn_conv, state_gdn_S, state_ret_S, norm_g, final_norm_g, ffn_w_gu, ffn_w_down, ab_w_in, ab_w_out, fox_b_f, gdn_conv_w, gdn_A_log, gdn_dt_bias, gdn_norm_g, c_w_in, c_w_out, ret_norm_g):
    B, L, D = x_prompt.shape
    Bn, Ls, _ = x_sample.shape
    assert Ls == 1, "the sample group decodes one token per sequence"
    depth = norm_g.shape[0]
    _, n_pool, page, fh, fd = cache_fox_k.shape
    fw = fh * fd
    _, _, gh, gdk, gdv = state_gdn_S.shape
    conv_dim = gdn_conv_w.shape[2]
    kw = gdn_conv_w.shape[1]
    _, _, rh, rdk, rdv = state_ret_S.shape
    n_pages = page_table.shape[1]
    assert fh == gh and 3 * fh <= 16 and fh & (fh - 1) == 0

    xp = x_prompt.reshape(B * L, D)
    xs = x_sample.reshape(Bn, D)
    w_gu16 = _to_bf16(ffn_w_gu)
    w_down16 = _to_bf16(ffn_w_down)
    freqs = (ROPE_BASE ** (-jnp.arange(rdk // 2, dtype=F32) / (rdk // 2))).reshape(1, rdk // 2)
    cos2, sin2 = _rope_table(L, freqs)

    fkp, fvp, flp, fks, fvs, fls = [], [], [], [], [], []
    gcp, gsp, gcs, gss = [], [], [], []
    rsp, rss = [], []
    for li in range(depth):
        xp = _ffn(xp, norm_g[li, 0], w_gu16, w_down16, li, 0)
        xs = _ffn(xs, norm_g[li, 0], w_gu16, w_down16, li, 0)
        j = li // 2
        if li % 2 == 0:
            o0 = 3 * fw
            zw = gh * gdv
            c0 = o0 + fh
            widths = (conv_dim, fw, fw, fw, zw, LANES)
            w_cat = _regroup_bf16(ab_w_in[j].T, [(c0, conv_dim), (0, o0), (c0 + conv_dim, zw), (o0, fh),
                                                 (c0 + conv_dim + zw, 2 * gh)], sum(widths))
            w_out16 = ab_w_out[j].astype(BF16)

            gate_p = _gate_params(fox_b_f[j], gdn_dt_bias[j], gdn_A_log[j])
            gqkv, q, k, v, z, gates, k3, v3, tails = _ab_proj_prompt(
                xp, norm_g[li, 1], w_cat, gdn_conv_w[j], gate_p, B, L, widths, fh, fd, gh, gdk)
            gates_row = gates[:, :16].T
            c_rows = _cumsum_rows(gates_row, L)[:fh].reshape(fh, B, L).transpose(1, 0, 2).reshape(B * fh, 1, L)
            fo = _fox_prompt(q, k, v, c_rows, B, L, fh)
            go, S_p = _gdn_prompt(gqkv, z, gates, gates_row, gdn_norm_g[j], B, L, gh, gdk, gdv)
            pre_p = ([fo, go], w_out16)
            fkp.append(k3.reshape(B, L, fh, fd))
            fvp.append(v3.reshape(B, L, fh, fd))
            flp.append(gates[:, :fh].reshape(B, L, fh))
            gcp.append(tails[:, 8 - (kw - 1):])
            gsp.append(S_p)

            cin, q, k, v, z, small = _rms_proj(xs, norm_g[li, 1], w_cat, widths)
            gates = _gates(small, gate_p, fh)
            sums = _page_sums(jnp.transpose(cache_fox_logf[j], (0, 2, 1)))
            fo = _fox_sample(q.reshape(Bn, fh, fd), k.reshape(Bn, fh, fd), v.reshape(Bn, fh, fd),
                             gates[:, :fh].reshape(Bn, fh, 1),
                             cache_fox_k[j].reshape(n_pool, page * fh, fd), cache_fox_v[j].reshape(n_pool, page * fh, fd),
                             sums, page_table, fh)
            go, conv_s, S_s = _gdn_sample(cin.reshape(Bn, 1, conv_dim), state_gdn_conv[j], z.reshape(Bn, 1, gh * gdv),
                                          gates.reshape(Bn, 1, LANES), gdn_conv_w[j], gdn_norm_g[j], state_gdn_S[j],
                                          gh, gdk, gdv)
            pre_s = ([fo.reshape(Bn, fw), go.reshape(Bn, gh * gdv)], w_out16)
            fks.append(k.reshape(Bn, 1, fh, fd))
            fvs.append(v.reshape(Bn, 1, fh, fd))
            fls.append(gates[:, :fh].reshape(Bn, 1, fh))
            gcs.append(conv_s)
            gss.append(S_s)
        else:
            w_in16 = _to_bf16(c_w_in[j])
            qk_w, v_w = rh * rdk, rh * rdv
            widths = (qk_w, qk_w, v_w, v_w)
            w_out16 = c_w_out[j].astype(BF16)

            q16, k16, v, sgate = _c_proj_prompt(xp, norm_g[li, 1], w_in16, cos2, sin2, L, rh, rdk, v_w)
            y, R_p = _ret_prompt(q16, k16, v, sgate, ret_norm_g[j], B, L, rh, rdk, rdv)
            pre_p = ([y], w_out16)
            rsp.append(R_p)

            q, k, v, gate = _rms_proj(xs, norm_g[li, 1], w_in16, widths)
            y, R_s = _ret_sample(q.reshape(Bn, 1, qk_w), k.reshape(Bn, 1, qk_w), v.reshape(Bn, 1, v_w),
                                 gate.reshape(Bn, 1, v_w), freqs, ret_norm_g[j], state_ret_S[j],
                                 n_pages * page, rh, rdk, rdv)
            pre_s = ([y.reshape(Bn, v_w)], w_out16)
            rss.append(R_s)
        final_g = final_norm_g if li == depth - 1 else None
        xp = _ffn(xp, norm_g[li, 2], w_gu16, w_down16, li, 1, final_g, pre_p)
        xs = _ffn(xs, norm_g[li, 2], w_gu16, w_down16, li, 1, final_g, pre_s)
    y_prompt = xp.reshape(B, L, D)
    y_sample = xs.reshape(Bn, 1, D)
    return (y_prompt, y_sample,
            jnp.stack(fkp), jnp.stack(fvp), jnp.stack(flp),
            jnp.stack(fks), jnp.stack(fvs), jnp.stack(fls),
            jnp.stack(gcp), jnp.stack(gsp), jnp.stack(gcs), jnp.stack(gss),
            jnp.stack(rsp), jnp.stack(rss))
```

```python
import functools
import math

import jax
import jax.numpy as jnp
from jax import lax
from jax.experimental import pallas as pl
from jax.experimental.pallas import tpu as pltpu

F32 = jnp.float32
BF16 = jnp.bfloat16
EPS = 1e-6
ROPE_BASE = 10000.0
NEG = -1e30
LANES = 128
GDN_CHUNK = 64
VMEM_LIMIT = 56 * 1024 * 1024
HI = lax.Precision.HIGHEST


def _cparams(n_axes):
    return pltpu.CompilerParams(dimension_semantics=("arbitrary",) * n_axes,
                                vmem_limit_bytes=VMEM_LIMIT)


def _dot(a, b, precision=None):
    return jnp.dot(a, b, preferred_element_type=F32, precision=precision)


def _dot_nt(a, b, precision=None):
    return lax.dot_general(a, b, (((1,), (1,)), ((), ())), preferred_element_type=F32, precision=precision)


def _dot_tn(a, b, precision=None):
    return lax.dot_general(a, b, (((0,), (0,)), ((), ())), preferred_element_type=F32, precision=precision)


def _rms_rows(x, g):
    return x * lax.rsqrt(jnp.mean(x * x, axis=-1, keepdims=True) + EPS) * g


MXU_DEPTH = 256


def _ffn_body(*refs, F, chunk, final_norm, n_pre):
    x_ref, g_ref, wgu_ref, wd_ref = refs[:4]
    a_refs = refs[4:4 + n_pre]
    rest = refs[4 + n_pre:]
    x = x_ref[...]
    if n_pre:
        wo_ref, rest = rest[0], rest[1:]
        row = 0
        for a_ref in a_refs:
            n = a_ref.shape[1]
            x = x + _dot(a_ref[...].astype(BF16), wo_ref[row:row + n, :])
            row += n
    if final_norm:
        gf_ref, rest = rest[0], rest[1:]
    o_ref = rest[0]
    xn = _rms_rows(x, g_ref[...]).astype(BF16)
    acc = None
    for c0 in range(0, F, chunk):
        c1 = min(F, c0 + chunk)
        a = _dot(xn, wgu_ref[:, c0:c1])
        b = _dot(xn, wgu_ref[:, F + c0:F + c1])
        h = (jax.nn.silu(a) * b).astype(BF16)
        d = _dot(h, wd_ref[c0:c1, :])
        acc = d if acc is None else acc + d
    y = x + 0.5 * acc
    o_ref[...] = _rms_rows(y, gf_ref[...]) if final_norm else y


def _ffn(x, g, w_gu, w_down, li, j, final_g=None, pre=None):
    M, D = x.shape
    F = w_down.shape[2]
    tm = min(M, 512)
    final_norm = final_g is not None
    const = lambda m: (0, 0)
    in_specs = [pl.BlockSpec((tm, D), lambda m: (m, 0)), pl.BlockSpec((1, D), const),
                pl.BlockSpec((None, None, D, 2 * F), lambda m: (li, j, 0, 0)),
                pl.BlockSpec((None, None, F, D), lambda m: (li, j, 0, 0))]
    args = [x, g.reshape(1, D), w_gu, w_down]
    acts, w_out = pre if pre is not None else ((), None)
    if acts:
        assert sum(a.shape[1] for a in acts) == w_out.shape[0]
        in_specs += [pl.BlockSpec((tm, a.shape[1]), lambda m: (m, 0)) for a in acts]
        in_specs.append(pl.BlockSpec(w_out.shape, const))
        args += [*acts, w_out]
    if final_norm:
        in_specs.append(pl.BlockSpec((1, D), const))
        args.append(final_g.reshape(1, D))
    return pl.pallas_call(
        functools.partial(_ffn_body, F=F, chunk=2 * MXU_DEPTH, final_norm=final_norm, n_pre=len(acts)),
        out_shape=jax.ShapeDtypeStruct((M, D), F32),
        grid=(M // tm,),
        in_specs=in_specs,
        out_specs=pl.BlockSpec((tm, D), lambda m: (m, 0)),
        compiler_params=_cparams(1),
        name="ffn",
    )(*args)


PROJ_CHUNK = 512


def _project(xn, w_ref, col, o_ref, o3_ref=None):
    n = o_ref.shape[1]
    for s0 in range(0, n, PROJ_CHUNK):
        s1 = min(n, s0 + PROJ_CHUNK)
        val = _dot(xn, w_ref[:, col + s0:col + s1])
        o_ref[:, s0:s1] = val
        if o3_ref is not None:
            dh = o3_ref.shape[2]
            for hh in range((s1 - s0) // dh):
                o3_ref[:, s0 // dh + hh, :] = val[:, hh * dh:(hh + 1) * dh]
    return col + n


def _proj_body(x_ref, g_ref, w_ref, *o_refs):
    xn = _rms_rows(x_ref[...], g_ref[...]).astype(BF16)
    col = 0
    for o_ref in o_refs:
        col = _project(xn, w_ref, col, o_ref)


def _rms_proj(x, g, w, widths):
    M, D = x.shape
    tm = min(M, 256)
    assert sum(widths) == w.shape[1] and all(n % LANES == 0 for n in widths)
    return pl.pallas_call(
        _proj_body,
        out_shape=[jax.ShapeDtypeStruct((M, n), F32) for n in widths],
        grid=(M // tm,),
        in_specs=[pl.BlockSpec((tm, D), lambda m: (m, 0)), pl.BlockSpec((1, D), lambda m: (0, 0)),
                  pl.BlockSpec(w.shape, lambda m: (0, 0))],
        out_specs=[pl.BlockSpec((tm, n), lambda m: (m, 0)) for n in widths],
        compiler_params=_cparams(1),
        name="rms_proj",
    )(x, g.reshape(1, D), w)


def _l2_rows(x):
    return x * lax.rsqrt(jnp.sum(x * x, axis=-1, keepdims=True) + EPS)


def _ab_proj_body(x_ref, g_ref, w_ref, cw_ref, gp_ref, qkv_o, q_o, k_o, v_o, z_o, gates_o, k3_o, v3_o, tail_o,
                  win_ref, *, tiles_per_seq, nh, dk):
    m = pl.program_id(0)
    tm = x_ref.shape[0]
    cd = qkv_o.shape[1]
    xn = _rms_rows(x_ref[...], g_ref[...]).astype(BF16)

    @pl.when(lax.rem(m, tiles_per_seq) == 0)
    def _():
        win_ref[0:8, :] = jnp.zeros((8, cd), F32)

    kw = cw_ref.shape[0]
    step = nh * dk

    def conv_product(s0):
        win_ref[8:8 + tm, s0:s0 + step] = _dot(xn, w_ref[:, s0:s0 + step])
        tail_o[:, s0:s0 + step] = win_ref[tm:tm + 8, s0:s0 + step]

    def conv_finish(s0):
        s1 = s0 + step
        conv = win_ref[8 - kw + 1:8 - kw + 1 + tm, s0:s1] * cw_ref[0:1, s0:s1]
        for i in range(1, kw):
            conv = conv + win_ref[8 - kw + 1 + i:8 - kw + 1 + i + tm, s0:s1] * cw_ref[i:i + 1, s0:s1]
        win_ref[0:8, s0:s1] = win_ref[tm:tm + 8, s0:s1]
        conv = jax.nn.silu(conv)
        if s0 >= 2 * step:
            qkv_o[:, s0:s1] = conv
        else:
            for h in range(nh):
                xh = _l2_rows(conv[:, h * dk:(h + 1) * dk])
                qkv_o[:, s0 + h * dk:s0 + (h + 1) * dk] = xh * dk ** -0.5 if s0 == 0 else xh

    assert cd == 3 * step
    conv_product(0)
    conv_product(step)
    col = _project(xn, w_ref, cd, q_o)
    conv_finish(0)
    conv_product(2 * step)
    col = _project(xn, w_ref, col, k_o, k3_o)
    conv_finish(step)
    col = _project(xn, w_ref, col, v_o, v3_o)
    col = _project(xn, w_ref, col, z_o)
    conv_finish(2 * step)
    gates_o[...] = _gate_values(_dot(xn, w_ref[:, col:col + gates_o.shape[1]]), gp_ref, nh)


def _ab_proj_prompt(x, g, w, conv_w, gate_p, B, L, widths, fh, fd, gh, gdk):
    M, D = x.shape
    tm = min(L, 512)
    cd = widths[0]
    assert sum(widths) == w.shape[1] and L % tm == 0
    row = lambda m: (m, 0)
    const = lambda m: (0, 0)
    out_shape = [jax.ShapeDtypeStruct((M, n), F32) for n in widths]
    out_specs = [pl.BlockSpec((tm, n), row) for n in widths]
    for _ in range(2):
        out_shape.append(jax.ShapeDtypeStruct((M, fh, fd), F32))
        out_specs.append(pl.BlockSpec((tm, fh, fd), lambda m: (m, 0, 0)))
    out_shape.append(jax.ShapeDtypeStruct((B, 8, cd), F32))
    out_specs.append(pl.BlockSpec((None, 8, cd), lambda m: (m // (L // tm), 0, 0)))
    return pl.pallas_call(
        functools.partial(_ab_proj_body, tiles_per_seq=L // tm, nh=gh, dk=gdk),
        out_shape=out_shape,
        grid=(M // tm,),
        in_specs=[pl.BlockSpec((tm, D), row), pl.BlockSpec((1, D), const), pl.BlockSpec(w.shape, const),
                  pl.BlockSpec(conv_w.shape, const), pl.BlockSpec(gate_p.shape, const)],
        out_specs=out_specs,
        scratch_shapes=[pltpu.VMEM((8 + tm, cd), F32)],
        compiler_params=_cparams(1),
        name="ab_proj_prompt",
    )(x, g.reshape(1, D), w, conv_w, gate_p)


def _c_proj_body(x_ref, g_ref, w_ref, cos_ref, sin_ref, q_o, k_o, v_o, sg_o, *, nh, dk):
    xn = _rms_rows(x_ref[...], g_ref[...]).astype(BF16)
    cos2, sin2 = cos_ref[...], sin_ref[...]
    hpc = PROJ_CHUNK // dk
    for idx, o_ref in enumerate((q_o, k_o)):
        for h0 in range(0, nh, hpc):
            c0 = idx * nh * dk + h0 * dk
            val = _dot(xn, w_ref[:, c0:c0 + hpc * dk])
            for hh in range(hpc):
                xh = val[:, hh * dk:(hh + 1) * dk]
                xh = xh * cos2 + pltpu.roll(xh, dk // 2, 1) * sin2
                if idx == 1:
                    xh = xh * dk ** -0.5
                o_ref[:, (h0 + hh) * dk:(h0 + hh + 1) * dk] = xh.astype(BF16)
    col = _project(xn, w_ref, 2 * nh * dk, v_o)
    n = sg_o.shape[1]
    for s0 in range(0, n, PROJ_CHUNK):
        s1 = min(n, s0 + PROJ_CHUNK)
        sg_o[:, s0:s1] = jax.nn.silu(_dot(xn, w_ref[:, col + s0:col + s1]))


def _c_proj_prompt(x, g, w, cos2, sin2, L, nh, dk, v_w):
    M, D = x.shape
    tm = min(L, 512)
    qk_w = nh * dk
    row = lambda m: (m, 0)
    const = lambda m: (0, 0)
    pos = lambda m: (lax.rem(m, L // tm), 0)
    return pl.pallas_call(
        functools.partial(_c_proj_body, nh=nh, dk=dk),
        out_shape=[jax.ShapeDtypeStruct((M, qk_w), BF16), jax.ShapeDtypeStruct((M, qk_w), BF16),
                   jax.ShapeDtypeStruct((M, v_w), F32), jax.ShapeDtypeStruct((M, v_w), F32)],
        grid=(M // tm,),
        in_specs=[pl.BlockSpec((tm, D), row), pl.BlockSpec((1, D), const), pl.BlockSpec(w.shape, const),
                  pl.BlockSpec((tm, dk), pos), pl.BlockSpec((tm, dk), pos)],
        out_specs=[pl.BlockSpec((tm, qk_w), row), pl.BlockSpec((tm, qk_w), row),
                   pl.BlockSpec((tm, v_w), row), pl.BlockSpec((tm, v_w), row)],
        compiler_params=_cparams(1),
        name="c_proj_prompt",
    )(x, g.reshape(1, D), w, cos2, sin2)


def _cast_body(x_ref, o_ref):
    o_ref[...] = x_ref[...].astype(o_ref.dtype)


def _to_bf16(w):
    cols = w.shape[-1]
    rows = w.size // cols
    tr = 512 if rows % 512 == 0 else rows
    out = pl.pallas_call(
        _cast_body,
        out_shape=jax.ShapeDtypeStruct((rows, cols), BF16),
        grid=(rows // tr,),
        in_specs=[pl.BlockSpec((tr, cols), lambda r: (r, 0))],
        out_specs=pl.BlockSpec((tr, cols), lambda r: (r, 0)),
        compiler_params=_cparams(1),
        name="to_bf16",
    )(w.reshape(rows, cols))
    return out.reshape(w.shape)


def _regroup_body(wt_ref, o_ref, *, pieces):
    N, D = wt_ref.shape
    col = 0
    narrow = []
    for start, width in pieces:
        if width % LANES == 0:
            assert not narrow and col % LANES == 0
            for r0 in range(0, width, PROJ_CHUNK):
                r1 = min(width, r0 + PROJ_CHUNK)
                o_ref[:, col + r0:col + r1] = wt_ref[start + r0:start + r1, :].T.astype(o_ref.dtype)
        else:
            w0 = min(start, N - LANES)
            narrow.append(wt_ref[w0:w0 + LANES, :].T[:, start - w0:start - w0 + width])
        col += width
    if narrow:
        first = col - sum(p.shape[1] for p in narrow)
        assert first % LANES == 0 and o_ref.shape[1] - first == LANES
        pad = jnp.zeros((D, o_ref.shape[1] - col), F32)
        o_ref[:, first:] = jnp.concatenate(narrow + [pad], axis=1).astype(o_ref.dtype)


def _regroup_bf16(wt, pieces, n_out):
    N, D = wt.shape
    return pl.pallas_call(
        functools.partial(_regroup_body, pieces=tuple(pieces)),
        out_shape=jax.ShapeDtypeStruct((D, n_out), BF16),
        grid=(1,),
        in_specs=[pl.BlockSpec((N, D), lambda r: (0, 0))],
        out_specs=pl.BlockSpec((D, n_out), lambda r: (0, 0)),
        compiler_params=_cparams(1),
        name="regroup_bf16",
    )(wt)


def _gate_values(s, p_ref, nh):
    s = s + p_ref[0:1, :]
    lane = lax.broadcasted_iota(jnp.int32, s.shape, 1)
    logf = jax.nn.log_sigmoid(s)
    g = -jnp.exp(p_ref[1:2, :]) * jax.nn.softplus(s)
    beta = jax.nn.sigmoid(s)
    return jnp.where(lane < nh, logf, jnp.where(lane < 2 * nh, g, jnp.where(lane < 3 * nh, beta, 0.0)))


def _gate_params(b_f, dt_bias, a_log):
    nh = b_f.shape[0]
    pad = jnp.zeros((LANES - 3 * nh,), F32)
    p = jnp.zeros((8, LANES), F32)
    p = p.at[0].set(jnp.concatenate([b_f, dt_bias, jnp.zeros((nh,), F32), pad]))
    return p.at[1].set(jnp.concatenate([jnp.zeros((nh,), F32), a_log, jnp.zeros((nh,), F32), pad]))


def _gates_body(s_ref, p_ref, o_ref, *, nh):
    o_ref[...] = _gate_values(s_ref[...], p_ref, nh)


def _gates(small, gate_p, nh):
    M = small.shape[0]
    tm = min(M, 2048)
    return pl.pallas_call(
        functools.partial(_gates_body, nh=nh),
        out_shape=jax.ShapeDtypeStruct((M, LANES), F32),
        grid=(M // tm,),
        in_specs=[pl.BlockSpec((tm, LANES), lambda m: (m, 0)), pl.BlockSpec((8, LANES), lambda m: (0, 0))],
        out_specs=pl.BlockSpec((tm, LANES), lambda m: (m, 0)),
        compiler_params=_cparams(1),
        name="gates",
    )(small, gate_p)


def _cumsum_body(x_ref, o_ref):
    L = x_ref.shape[1]
    r = lax.broadcasted_iota(jnp.int32, (LANES, LANES), 0)
    c = lax.broadcasted_iota(jnp.int32, (LANES, LANES), 1)
    upper = (r <= c).astype(F32)
    carry = jnp.zeros((x_ref.shape[0], 1), F32)
    for j in range(L // LANES):
        cs = _dot(x_ref[:, j * LANES:(j + 1) * LANES], upper, HI) + carry
        o_ref[:, j * LANES:(j + 1) * LANES] = cs
        carry = cs[:, LANES - 1:LANES]


def _cumsum_rows(x, seg):
    R, M = x.shape
    return pl.pallas_call(
        _cumsum_body,
        out_shape=jax.ShapeDtypeStruct((R, M), F32),
        grid=(M // seg,),
        in_specs=[pl.BlockSpec((R, seg), lambda b: (0, b))],
        out_specs=pl.BlockSpec((R, seg), lambda b: (0, b)),
        compiler_params=_cparams(1),
        name="cumsum_rows",
    )(x)


def _fox_body(q_ref, k_ref, v_ref, c_ref, o_ref, vt_ref, cb_ref, acc_ref, m_ref, *, t, hg, dh, scale):
    i = pl.program_id(2)
    L = k_ref.shape[0]
    log2e = math.log2(math.e)
    heads = range(hg)

    @pl.when(i == 0)
    def _():
        for g in heads:
            for jj in range(L // t):
                vt_ref[g, 0:dh, jj * t:(jj + 1) * t] = v_ref[jj * t:(jj + 1) * t, g * dh:(g + 1) * dh].T.astype(BF16)
            vt_ref[g, dh:, :] = jnp.ones((vt_ref.shape[1] - dh, L), BF16)
            for jj in range(L // LANES):
                row = c_ref[g, :, jj * LANES:(jj + 1) * LANES] * log2e
                cb_ref[g, jj * LANES:(jj + 1) * LANES, :] = jnp.broadcast_to(row, (LANES, LANES)).T

    q0 = pl.multiple_of(i * t, t)
    qs = [q_ref[:, g * dh:(g + 1) * dh].astype(BF16) for g in heads]
    r2 = [c_ref[g, :, pl.ds(q0, t)][:, t - 1:t] * log2e for g in heads]
    m_ref[...] = jnp.full_like(m_ref, NEG)
    acc_ref[...] = jnp.zeros_like(acc_ref)

    def step(j, masked):
        k0 = pl.multiple_of(j * t, t)
        qk = [_dot_nt(k_ref[pl.ds(k0, t), g * dh:(g + 1) * dh].astype(BF16), qs[g]) for g in heads]
        ps, alphas = [], []
        for g in heads:
            bias = r2[g] - cb_ref[g, pl.ds(k0, t), :]
            st = qk[g] * (scale * log2e) + jnp.concatenate([bias] * (t // LANES), axis=1)
            if masked:
                key = lax.broadcasted_iota(jnp.int32, (t, t), 0)
                qry = lax.broadcasted_iota(jnp.int32, (t, t), 1)
                st = jnp.where(key <= qry, st, NEG)
            m_prev = m_ref[g]
            m_new = jnp.maximum(m_prev, jnp.max(st, axis=0, keepdims=True))
            ps.append(jnp.exp2(st - m_new).astype(BF16))
            alphas.append(jnp.exp2(m_prev - m_new))
            m_ref[g] = m_new
        for g in heads:
            acc_ref[g] = acc_ref[g] * alphas[g] + _dot(vt_ref[g, :, pl.ds(k0, t)], ps[g])

    def loop_body(j, carry):
        step(j, False)
        return carry

    lax.fori_loop(0, i, loop_body, 0)
    step(i, True)
    for g in heads:
        acc = acc_ref[g]
        o_ref[:, g * dh:(g + 1) * dh] = (acc[0:dh] / acc[dh:dh + 1]).T


def _fox_prompt(q, k, v, c_rows, B, L, H):
    M, W = q.shape
    Dh = W // H
    t = min(L, 512)
    nq = L // t
    hg = 2 if H % 2 == 0 else 1
    kernel = functools.partial(_fox_body, t=t, hg=hg, dh=Dh, scale=Dh ** -0.5)
    return pl.pallas_call(
        kernel,
        out_shape=jax.ShapeDtypeStruct((M, W), F32),
        grid=(B, H // hg, nq),
        in_specs=[
            pl.BlockSpec((t, hg * Dh), lambda b, h, i: (b * nq + i, h)),
            pl.BlockSpec((L, hg * Dh), lambda b, h, i: (b, h)),
            pl.BlockSpec((L, hg * Dh), lambda b, h, i: (b, h)),
            pl.BlockSpec((hg, 1, L), lambda b, h, i: (b * (H // hg) + h, 0, 0)),
        ],
        out_specs=pl.BlockSpec((t, hg * Dh), lambda b, h, i: (b * nq + i, h)),
        scratch_shapes=[pltpu.VMEM((hg, Dh + 16, L), BF16), pltpu.VMEM((hg, L, LANES), F32),
                        pltpu.VMEM((hg, Dh + 16, t), F32), pltpu.VMEM((hg, 1, t), F32)],
        compiler_params=_cparams(3),
        name="fox_prompt",
    )(q, k, v, c_rows)


def _split2(x):
    hi = x.astype(BF16)
    return hi, (x - hi.astype(F32)).astype(BF16)


def _split3(x):
    h1 = x.astype(BF16)
    r1 = x - h1.astype(F32)
    h2 = r1.astype(BF16)
    return h1, h2, (r1 - h2.astype(F32)).astype(BF16)


def _dot3(a, b):
    (ah, al), (bh, bl) = a, b
    return _dot(ah, bh) + (_dot(ah, bl) + _dot(al, bh))


def _unit_lower_inverses(a_list, n):
    ri = lax.broadcasted_iota(jnp.int32, (n, n), 0)
    ci = lax.broadcasted_iota(jnp.int32, (n, n), 1)
    eye = (ri == ci).astype(F32)
    nm = len(a_list)
    diag16 = (ri >> 4) == (ci >> 4)
    pw = [jnp.where(diag16, a, 0.0) for a in a_list]
    t = [eye - p for p in pw]
    for _ in range(3):
        pw16 = [p.astype(BF16) for p in pw]
        pw = [_dot(p, p) for p in pw16]
        pn16 = [p.astype(BF16) for p in pw]
        t = [t[m] + _dot(t[m].astype(BF16), pn16[m]) for m in range(nm)]
    size = 16
    while size < GDN_CHUNK:
        sh = size.bit_length() - 1
        off = ((ri >> (sh + 1)) == (ci >> (sh + 1))) & ((ri >> sh) == (ci >> sh) + 1)
        t16 = [x.astype(BF16) for x in t]
        mid = [_dot(jnp.where(off, a_list[m], 0.0).astype(BF16), t16[m]) for m in range(nm)]
        t = [t[m] - _dot(t16[m], mid[m].astype(BF16)) for m in range(nm)]
        size *= 2
    ts = [_split2(x) for x in t]
    res = [(eye - t[m]) - _dot3(_split2(a_list[m]), ts[m]) for m in range(nm)]
    return [t[m] + _dot(ts[m][0], res[m].astype(BF16)) for m in range(nm)]


def _gdn_body(qkv_ref, z_ref, gc_ref, *rest, nh, dk, dv, nseq):
    gr_refs = rest[:nseq]
    ng_ref, o_ref, s_out_ref, s_ref, m16_ref = rest[nseq:]
    j = pl.program_id(1)
    T = qkv_ref.shape[1]
    C = GDN_CHUNK
    nc = T // C

    ri = lax.broadcasted_iota(jnp.int32, (T, T), 0)
    ci = lax.broadcasted_iota(jnp.int32, (T, T), 1)
    csh = C.bit_length() - 1
    same = (ri >> csh) == (ci >> csh)
    tril = same & (ri >= ci)
    strict = same & (ri > ci)

    @pl.when((pl.program_id(0) == 0) & (j == 0))
    def _():
        m16_ref[0] = tril.astype(BF16)
        m16_ref[1] = (same & (ri <= ci)).astype(BF16)

    @pl.when(j == 0)
    def _():
        s_ref[...] = jnp.zeros_like(s_ref)

    rowi = lax.broadcasted_iota(jnp.int32, (T, 1), 0)
    units = [(s, h) for s in range(nseq) for h in range(nh)]
    nu = len(units)
    gcols, grows = [], []
    for s in range(nseq):
        gcols.append(sum(_dot(m16_ref[0], part) for part in _split3(gc_ref[s])))
        grows.append(sum(_dot(part, m16_ref[1]) for part in _split3(gr_refs[s][...])))

    q, k, beta, gcc, gcr, gam, kb, k16 = [], [], [], [], [], [], [], []
    for u, (s, h) in enumerate(units):
        q.append(qkv_ref[s, :, h * dk:(h + 1) * dk])
        k.append(qkv_ref[s, :, nh * dk + h * dk:nh * dk + (h + 1) * dk])
        beta.append(gc_ref[s, :, 2 * nh + h:2 * nh + h + 1])
        gcc.append(gcols[s][:, nh + h:nh + h + 1])
        gcr.append(grows[s][nh + h:nh + h + 1, :])
        gam.append(jnp.exp(jnp.where(tril, gcc[u] - gcr[u], NEG)))
        kb.append(k[u] * beta[u])
        k16.append(k[u].astype(BF16))
    a = [jnp.where(strict, _dot_nt(kb[u].astype(BF16), k16[u]) * gam[u], 0.0) for u in range(nu)]
    tinv = _unit_lower_inverses(a, T)
    x = []
    for u, (s, h) in enumerate(units):
        v = qkv_ref[s, :, 2 * nh * dk + h * dv:2 * nh * dk + (h + 1) * dv]
        rhs = jnp.concatenate([v * beta[u], kb[u] * jnp.exp(gcc[u])], axis=-1)
        x.append(_dot3(_split2(tinv[u]), _split2(rhs)))
    attn, qg, kd, egl = [], [], [], []
    for u in range(nu):
        attn.append((_dot_nt(q[u].astype(BF16), k16[u]) * gam[u]).astype(BF16))
        qg.append((q[u] * jnp.exp(gcc[u])).astype(BF16))
        gl = gcr[u][:, C - 1:C]
        for cidx in range(1, nc):
            gl = jnp.where(rowi >= cidx * C, gcr[u][:, (cidx + 1) * C - 1:(cidx + 1) * C], gl)
        kd.append((k[u] * jnp.exp(gl - gcc[u])).astype(BF16))
        egl.append(jnp.exp(gl))
    S = [s_ref[s, h] for s, h in units]
    outs = [[] for _ in units]
    for cidx in range(nc):
        sl = slice(cidx * C, (cidx + 1) * C)
        S16 = [st.astype(BF16) for st in S]
        vn16 = [(x[u][sl, :dv] - _dot(x[u][sl, dv:].astype(BF16), S16[u])).astype(BF16) for u in range(nu)]
        for u in range(nu):
            outs[u].append(_dot(qg[u][sl], S16[u]) + _dot(attn[u][sl, cidx * C:(cidx + 1) * C], vn16[u]))
        S = [S[u] * egl[u][cidx * C:cidx * C + 1] + _dot_tn(kd[u][sl], vn16[u]) for u in range(nu)]
    for u, (s, h) in enumerate(units):
        s_ref[s, h] = S[u]
        o = _rms_rows(jnp.concatenate(outs[u], axis=0), ng_ref[...])
        o_ref[s, :, h * dv:(h + 1) * dv] = o * jax.nn.silu(z_ref[s, :, h * dv:(h + 1) * dv])

    @pl.when(j == pl.num_programs(1) - 1)
    def _():
        s_out_ref[...] = s_ref[...]


def _gdn_prompt(qkv, z, gates_col, gates_row, norm_g, B, L, nh, dk, dv):
    M, CD = qkv.shape
    T = 2 * GDN_CHUNK
    nb = L // T
    nseq = 2 if B % 2 == 0 else 1
    kernel = functools.partial(_gdn_body, nh=nh, dk=dk, dv=dv, nseq=nseq)

    def row_map(s):
        return lambda p, j: (0, (p * nseq + s) * nb + j)

    per_seq = lambda p, j: (p, j, 0)
    go, S = pl.pallas_call(
        kernel,
        out_shape=[jax.ShapeDtypeStruct((B, L, nh * dv), F32), jax.ShapeDtypeStruct((B, nh, dk, dv), F32)],
        grid=(B // nseq, nb),
        in_specs=[
            pl.BlockSpec((nseq, T, CD), per_seq),
            pl.BlockSpec((nseq, T, nh * dv), per_seq),
            pl.BlockSpec((nseq, T, LANES), per_seq),
        ] + [pl.BlockSpec((16, T), row_map(s)) for s in range(nseq)] + [
            pl.BlockSpec((1, dv), lambda p, j: (0, 0)),
        ],
        out_specs=[
            pl.BlockSpec((nseq, T, nh * dv), per_seq),
            pl.BlockSpec((nseq, nh, dk, dv), lambda p, j: (p, 0, 0, 0)),
        ],
        scratch_shapes=[pltpu.VMEM((nseq, nh, dk, dv), F32), pltpu.VMEM((2, T, T), BF16)],
        compiler_params=_cparams(2),
        name="gdn_prompt",
    )(qkv.reshape(B, L, CD), z.reshape(B, L, nh * dv), gates_col.reshape(B, L, LANES),
      *([gates_row] * nseq), norm_g.reshape(1, dv))
    return go.reshape(M, nh * dv), S


def _rope(x, cos2, sin2, nh, dk):
    outs = []
    for h in range(nh):
        xh = x[:, h * dk:(h + 1) * dk]
        outs.append(xh * cos2 + pltpu.roll(xh, dk // 2, 1) * sin2)
    return outs


def _rope_tables(pos, freqs):
    ang = pos * freqs
    cos, sin = jnp.cos(ang), jnp.sin(ang)
    return jnp.concatenate([cos, cos], axis=-1), jnp.concatenate([-sin, sin], axis=-1)


def _head_layernorm(o):
    mu = jnp.mean(o, axis=-1, keepdims=True)
    oc = o - mu
    return oc * lax.rsqrt(jnp.mean(oc * oc, axis=-1, keepdims=True) + EPS)


def _log_gamma(h):
    return math.log(1.0 - 2.0 ** (-5.0 - h))


def _rope_table_body(fr_ref, cos_o, sin_o):
    C = cos_o.shape[0]
    rowi = lax.broadcasted_iota(jnp.int32, (C, 1), 0)
    pos = (pl.program_id(0) * C + rowi).astype(F32)
    cos_o[...], sin_o[...] = _rope_tables(pos, fr_ref[...])


def _rope_table(L, freqs):
    C = min(L, 512)
    dk = 2 * freqs.shape[1]
    return pl.pallas_call(
        _rope_table_body,
        out_shape=[jax.ShapeDtypeStruct((L, dk), F32)] * 2,
        grid=(L // C,),
        in_specs=[pl.BlockSpec(freqs.shape, lambda c: (0, 0))],
        out_specs=[pl.BlockSpec((C, dk), lambda c: (c, 0))] * 2,
        compiler_params=_cparams(1),
        name="rope_table",
    )(freqs)


def _ret_body(q_ref, k_ref, v_ref, sg_ref, gn_ref, o_ref, s_out_ref, s_ref, dmat_ref, *, nh, dk, dv):
    b = pl.program_id(0)
    j = pl.program_id(1)
    nseq, C = q_ref.shape[0], q_ref.shape[1]
    rowf = lax.broadcasted_iota(jnp.int32, (C, 1), 0).astype(F32)

    @pl.when((b == 0) & (j == 0))
    def _():
        ri = lax.broadcasted_iota(jnp.int32, (C, C), 0)
        ci = lax.broadcasted_iota(jnp.int32, (C, C), 1)
        diff = (ri - ci).astype(F32)
        for h in range(nh):
            dmat_ref[h] = jnp.exp(jnp.where(ri >= ci, diff * _log_gamma(h), NEG))

    @pl.when(j == 0)
    def _():
        s_ref[...] = jnp.zeros_like(s_ref)

    units = [(s, h) for s in range(nseq) for h in range(nh)]
    nu = range(len(units))
    q16 = [q_ref[s, :, h * dk:(h + 1) * dk] for s, h in units]
    k16 = [k_ref[s, :, h * dk:(h + 1) * dk] for s, h in units]
    v = [v_ref[s, :, h * dv:(h + 1) * dv] for s, h in units]
    S = [s_ref[s, h] for s, h in units]
    qk = [_dot_nt(q16[u], k16[u]) for u in nu]
    cross = [_dot(q16[u], S[u].astype(BF16)) for u in nu]
    for u, (s, h) in enumerate(units):
        vd = (v[u] * jnp.exp(_log_gamma(h) * (C - 1.0 - rowf))).astype(BF16)
        s_ref[s, h] = S[u] * math.exp(_log_gamma(h) * C) + _dot_tn(k16[u], vd)
    inner = [_dot((qk[u] * dmat_ref[h]).astype(BF16), v[u].astype(BF16)) for u, (s, h) in enumerate(units)]
    for u, (s, h) in enumerate(units):
        o = inner[u] + cross[u] * jnp.exp(_log_gamma(h) * (rowf + 1.0))
        o = _head_layernorm(o) * gn_ref[:, h * dv:(h + 1) * dv]
        o_ref[s, :, h * dv:(h + 1) * dv] = sg_ref[s, :, h * dv:(h + 1) * dv] * o

    @pl.when(j == pl.num_programs(1) - 1)
    def _():
        s_out_ref[...] = s_ref[...]


def _ret_prompt(q16, k16, v, sgate, gnorm, B, L, nh, dk, dv):
    M = q16.shape[0]
    C = 128
    nb = L // C
    nseq = 2 if B % 2 == 0 else 1
    kernel = functools.partial(_ret_body, nh=nh, dk=dk, dv=dv)
    tok = lambda p, j: (p, j, 0)
    y, S = pl.pallas_call(
        kernel,
        out_shape=[jax.ShapeDtypeStruct((B, L, nh * dv), F32), jax.ShapeDtypeStruct((B, nh, dk, dv), F32)],
        grid=(B // nseq, nb),
        in_specs=[
            pl.BlockSpec((nseq, C, nh * dk), tok),
            pl.BlockSpec((nseq, C, nh * dk), tok),
            pl.BlockSpec((nseq, C, nh * dv), tok),
            pl.BlockSpec((nseq, C, nh * dv), tok),
            pl.BlockSpec((1, nh * dv), lambda p, j: (0, 0)),
        ],
        out_specs=[
            pl.BlockSpec((nseq, C, nh * dv), tok),
            pl.BlockSpec((nseq, nh, dk, dv), lambda p, j: (p, 0, 0, 0)),
        ],
        scratch_shapes=[pltpu.VMEM((nseq, nh, dk, dv), F32), pltpu.VMEM((nh, C, C), F32)],
        compiler_params=_cparams(2),
        name="ret_prompt",
    )(q16.reshape(B, L, nh * dk), k16.reshape(B, L, nh * dk), v.reshape(B, L, nh * dv),
      sgate.reshape(B, L, nh * dv), gnorm.reshape(1, nh * dv))
    return y.reshape(M, nh * dv), S


def _page_sums_body(x_ref, o_ref, op_ref):
    _, nh, plen = x_ref.shape
    n = nh * plen
    wide = o_ref.shape[1]

    @pl.when(pl.program_id(0) == 0)
    def _():
        kk = lax.broadcasted_iota(jnp.int32, (plen, wide), 0)
        cc = lax.broadcasted_iota(jnp.int32, (plen, wide), 1)
        key_c = cc >> (nh.bit_length() - 1)
        later_or_total = (cc >= n) | (kk > key_c)
        for h in range(nh):
            op_ref[h] = (((cc & (nh - 1)) == h) & later_or_total).astype(BF16)

    acc = None
    for h in range(nh):
        for part in _split2(x_ref[:, h, :]):
            d = _dot(part, op_ref[h])
            acc = d if acc is None else acc + d
    o_ref[...] = acc


def _page_sums(lf_pages):
    P, nh, plen = lf_pages.shape
    wide = nh * plen + LANES
    tp = 512 if P % 512 == 0 else P
    return pl.pallas_call(
        _page_sums_body,
        out_shape=jax.ShapeDtypeStruct((P, wide), F32),
        grid=(P // tp,),
        in_specs=[pl.BlockSpec((tp, nh, plen), lambda p: (p, 0, 0))],
        out_specs=pl.BlockSpec((tp, wide), lambda p: (p, 0)),
        scratch_shapes=[pltpu.VMEM((nh, plen, wide), BF16)],
        compiler_params=_cparams(1),
        name="page_sums",
    )(lf_pages)


def _paged_body(pt_ref, *refs, G, nh, scale):
    k_refs = refs[:G]
    v_refs = refs[G:2 * G]
    sums_ref, q_ref, kn_ref, vn_ref, lfn_ref, o_ref, m_ref, l_ref, acc_ref, carry_ref = refs[2 * G:]
    b_idx = pl.program_id(0)
    s_idx = pl.program_id(1)
    ns = pl.num_programs(1)
    n = k_refs[0].shape[0]
    dh = q_ref.shape[1]

    @pl.when(s_idx == 0)
    def _():
        m_ref[...] = jnp.full_like(m_ref, NEG)
        l_ref[...] = jnp.zeros_like(l_ref)
        acc_ref[...] = jnp.zeros_like(acc_ref)
        carry_ref[...] = jnp.zeros_like(carry_ref)

    q = q_ref[...]
    q8 = jnp.concatenate([q, jnp.zeros((8 - nh, dh), F32)], axis=0).astype(BF16)
    lfn = jnp.concatenate([lfn_ref[...], jnp.zeros((8 - nh, 1), F32)], axis=0)
    hrow = lax.broadcasted_iota(jnp.int32, (8, n), 0)
    hcol = lax.broadcasted_iota(jnp.int32, (8, n), 1) & (nh - 1)
    match = hrow == hcol
    carry = carry_ref[...]
    s_parts = [None] * G
    for i in reversed(range(G)):
        page_id = pt_ref[b_idx, (ns - 1 - s_idx) * G + i]
        wt = sums_ref[pl.ds(page_id, 1), :]
        bias = wt[:, :n] + carry
        carry = carry + jnp.concatenate([wt[:, n:]] * (n // LANES), axis=1)
        s = _dot_nt(q8, k_refs[i][...].astype(BF16)) * scale + bias + lfn
        s_parts[i] = jnp.where(match, s, NEG)
    carry_ref[...] = carry
    m_run, l_run, acc_run = m_ref[...], l_ref[...], acc_ref[...]
    half = max(G // 2, 1)
    for grp in (range(half, G), range(0, half)) if G > 1 else (range(G),):
        sg = [s_parts[i] for i in grp]
        m_new = jnp.maximum(m_run, jnp.max(functools.reduce(jnp.maximum, sg), axis=-1, keepdims=True))
        alpha = jnp.exp(m_run - m_new)
        pg = [jnp.exp(s - m_new) for s in sg]
        l_run = alpha * l_run + jnp.sum(functools.reduce(jnp.add, pg), axis=-1, keepdims=True)
        pv = functools.reduce(jnp.add, [_dot(p.astype(BF16), v_refs[i][...].astype(BF16)) for p, i in zip(pg, grp)])
        acc_run = alpha * acc_run + pv
        m_run = m_new
    m_ref[...], l_ref[...], acc_ref[...] = m_run, l_run, acc_run

    @pl.when(s_idx == pl.num_programs(1) - 1)
    def _():
        qf = q.astype(BF16).astype(F32)
        kn = kn_ref[...].astype(BF16).astype(F32)
        s_new = jnp.sum(qf * kn, axis=-1, keepdims=True) * scale
        m_prev = m_ref[0:nh, :]
        m_new = jnp.maximum(m_prev, s_new)
        alpha = jnp.exp(m_prev - m_new)
        p_new = jnp.exp(s_new - m_new)
        l_fin = alpha * l_ref[0:nh, :] + p_new
        vn = vn_ref[...].astype(BF16).astype(F32)
        num = alpha * acc_ref[0:nh, :] + p_new.astype(BF16).astype(F32) * vn
        o_ref[...] = num / l_fin


def _fox_sample(q, k_new, v_new, lf_new, k_pages, v_pages, page_sums, page_table, nh):
    Bn, _, dh = q.shape
    n_pages = page_table.shape[1]
    n_pool, n, _ = k_pages.shape
    G = 16 if n_pages % 16 == 0 else (8 if n_pages % 8 == 0 else 1)
    ns = n_pages // G

    def page_map(i):
        return lambda b, s, pt: (pt[b, (ns - 1 - s) * G + i], 0, 0)

    per_b = lambda b, s, pt: (b, 0, 0)
    grid_spec = pltpu.PrefetchScalarGridSpec(
        num_scalar_prefetch=1,
        grid=(Bn, ns),
        in_specs=[pl.BlockSpec((None, n, dh), page_map(i)) for i in range(G)]
        + [pl.BlockSpec((None, n, dh), page_map(i)) for i in range(G)]
        + [pl.BlockSpec(page_sums.shape, lambda b, s, pt: (0, 0)),
           pl.BlockSpec((None, nh, dh), per_b), pl.BlockSpec((None, nh, dh), per_b),
           pl.BlockSpec((None, nh, dh), per_b), pl.BlockSpec((None, nh, 1), per_b)],
        out_specs=pl.BlockSpec((None, nh, dh), per_b),
        scratch_shapes=[pltpu.VMEM((8, 1), F32), pltpu.VMEM((8, 1), F32), pltpu.VMEM((8, dh), F32),
                        pltpu.VMEM((1, n), F32)],
    )
    return pl.pallas_call(
        functools.partial(_paged_body, G=G, nh=nh, scale=dh ** -0.5),
        out_shape=jax.ShapeDtypeStruct((Bn, nh, dh), F32),
        grid_spec=grid_spec,
        compiler_params=_cparams(2),
        name="fox_sample",
    )(page_table, *([k_pages] * G), *([v_pages] * G), page_sums, q, k_new, v_new, lf_new)


def _pad8(row):
    return jnp.concatenate([row, jnp.zeros((8 - row.shape[0], row.shape[1]), row.dtype)], axis=0)


def _gdn_sample_body(cin_ref, prev_ref, z_ref, gt_ref, cw_ref, ng_ref, s0_ref, o_ref, cnew_ref, s_out_ref, *, nh, dk, dv):
    nb = cin_ref.shape[0]
    w = cw_ref[...]
    kw = w.shape[0]
    convs, gts = [], []
    for b in range(nb):
        u = cin_ref[b]
        prev = prev_ref[b]
        conv = u * w[kw - 1:kw]
        for i in range(kw - 1):
            conv = conv + prev[i:i + 1] * w[i:i + 1]
        cnew_ref[b, 0:kw - 2, :] = prev[1:kw - 1]
        cnew_ref[b, kw - 2:kw - 1, :] = u
        convs.append(jax.nn.silu(conv))
        gts.append(gt_ref[b])
    units = [(b, h) for b in range(nb) for h in range(nh)]
    nu = range(len(units))
    q = [_l2_rows(convs[b][:, h * dk:(h + 1) * dk]) * dk ** -0.5 for b, h in units]
    k = [_l2_rows(convs[b][:, nh * dk + h * dk:nh * dk + (h + 1) * dk]) for b, h in units]
    v = [convs[b][:, 2 * nh * dk + h * dv:2 * nh * dk + (h + 1) * dv] for b, h in units]
    beta = [gts[b][:, 2 * nh + h:2 * nh + h + 1] for b, h in units]
    eg = [jnp.exp(gts[b][:, nh + h:nh + h + 1]) for b, h in units]
    S = [s0_ref[b, h] for b, h in units]
    r = [_dot(_pad8(jnp.concatenate([k[u] * beta[u] * eg[u], q[u] * eg[u]], axis=0)).astype(BF16), S[u].astype(BF16))
         for u in nu]
    vn16 = [(v[u] * beta[u] - r[u][0:1]).astype(BF16) for u in nu]
    for u, (b, h) in enumerate(units):
        s_out_ref[b, h] = S[u] * eg[u] + _dot_tn(_pad8(k[u]).astype(BF16), _pad8(vn16[u]))
    for u, (b, h) in enumerate(units):
        qk = jnp.sum(q[u].astype(BF16).astype(F32) * k[u].astype(BF16).astype(F32), axis=-1, keepdims=True)
        o = r[u][1:2] + qk.astype(BF16).astype(F32) * vn16[u].astype(F32)
        o = _rms_rows(o, ng_ref[...])
        o_ref[b, :, h * dv:(h + 1) * dv] = o * jax.nn.silu(z_ref[b, :, h * dv:(h + 1) * dv])


def _gdn_sample(conv_in, conv_prev, z, gates_col, conv_w, norm_g, S0, nh, dk, dv):
    Bn, _, CD = conv_in.shape
    kw = conv_w.shape[0]
    nb = 4 if Bn % 4 == 0 else 1
    per_b3 = lambda b: (b, 0, 0)
    per_b4 = lambda b: (b, 0, 0, 0)
    kernel = functools.partial(_gdn_sample_body, nh=nh, dk=dk, dv=dv)
    return pl.pallas_call(
        kernel,
        out_shape=[jax.ShapeDtypeStruct((Bn, 1, nh * dv), F32), jax.ShapeDtypeStruct((Bn, kw - 1, CD), F32),
                   jax.ShapeDtypeStruct((Bn, nh, dk, dv), F32)],
        grid=(Bn // nb,),
        in_specs=[
            pl.BlockSpec((nb, 1, CD), per_b3),
            pl.BlockSpec((nb, kw - 1, CD), per_b3),
            pl.BlockSpec((nb, 1, nh * dv), per_b3),
            pl.BlockSpec((nb, 1, LANES), per_b3),
            pl.BlockSpec(conv_w.shape, lambda b: (0, 0)),
            pl.BlockSpec((1, dv), lambda b: (0, 0)),
            pl.BlockSpec((nb, nh, dk, dv), per_b4),
        ],
        out_specs=[
            pl.BlockSpec((nb, 1, nh * dv), per_b3),
            pl.BlockSpec((nb, kw - 1, CD), per_b3),
            pl.BlockSpec((nb, nh, dk, dv), per_b4),
        ],
        compiler_params=_cparams(1),
        name="gdn_sample",
    )(conv_in, conv_prev, z, gates_col, conv_w, norm_g.reshape(1, dv), S0)


def _ret_sample_body(q_ref, k_ref, v_ref, gate_ref, fr_ref, gn_ref, s0_ref, o_ref, s_out_ref, *, nh, dk, dv, pos):
    nb = q_ref.shape[0]
    cos2, sin2 = _rope_tables(jnp.full((1, 1), pos, F32), fr_ref[...])
    units = [(b, h) for b in range(nb) for h in range(nh)]
    qs, ks = [], []
    for b in range(nb):
        qs += _rope(q_ref[b], cos2, sin2, nh, dk)
        ks += [kh * dk ** -0.5 for kh in _rope(k_ref[b], cos2, sin2, nh, dk)]
    v16 = [v_ref[b, :, h * dv:(h + 1) * dv].astype(BF16) for b, h in units]
    S = [s0_ref[b, h] for b, h in units]
    cross = [_dot(_pad8(qs[u] * math.exp(_log_gamma(h))).astype(BF16), S[u].astype(BF16))[0:1]
             for u, (b, h) in enumerate(units)]
    for u, (b, h) in enumerate(units):
        s_out_ref[b, h] = S[u] * math.exp(_log_gamma(h)) + _dot_tn(_pad8(ks[u]).astype(BF16), _pad8(v16[u]))
    for u, (b, h) in enumerate(units):
        qk = jnp.sum(qs[u].astype(BF16).astype(F32) * ks[u].astype(BF16).astype(F32), axis=-1, keepdims=True)
        inner = qk.astype(BF16).astype(F32) * v16[u].astype(F32)
        o = _head_layernorm(inner + cross[u]) * gn_ref[:, h * dv:(h + 1) * dv]
        o_ref[b, :, h * dv:(h + 1) * dv] = jax.nn.silu(gate_ref[b, :, h * dv:(h + 1) * dv]) * o


def _ret_sample(q, k, v, gate, freqs, gnorm, S0, pos, nh, dk, dv):
    Bn = q.shape[0]
    nb = 2 if Bn % 2 == 0 else 1
    per_b3 = lambda b: (b, 0, 0)
    per_b4 = lambda b: (b, 0, 0, 0)
    kernel = functools.partial(_ret_sample_body, nh=nh, dk=dk, dv=dv, pos=float(pos))
    return pl.pallas_call(
        kernel,
        out_shape=[jax.ShapeDtypeStruct((Bn, 1, nh * dv), F32), jax.ShapeDtypeStruct((Bn, nh, dk, dv), F32)],
        grid=(Bn // nb,),
        in_specs=[
            pl.BlockSpec((nb, 1, nh * dk), per_b3),
            pl.BlockSpec((nb, 1, nh * dk), per_b3),
            pl.BlockSpec((nb, 1, nh * dv), per_b3),
            pl.BlockSpec((nb, 1, nh * dv), per_b3),
            pl.BlockSpec((1, dk // 2), lambda b: (0, 0)),
            pl.BlockSpec((1, nh * dv), lambda b: (0, 0)),
            pl.BlockSpec((nb, nh, dk, dv), per_b4),
        ],
        out_specs=[
            pl.BlockSpec((nb, 1, nh * dv), per_b3),
            pl.BlockSpec((nb, nh, dk, dv), per_b4),
        ],
        compiler_params=_cparams(1),
        name="ret_sample",
    )(q, k, v, gate, freqs, gnorm.reshape(1, nh * dv), S0)


def kernel(x_prompt, x_sample, cache_fox_k, cache_fox_v, cache_fox_logf, page_table, state_gdn_conv, state_gdn_S, state_ret_S, norm_g, final_norm_g, ffn_w_gu, ffn_w_down, ab_w_in, ab_w_out, fox_b_f, gdn_conv_w, gdn_A_log, gdn_dt_bias, gdn_norm_g, c_w_in, c_w_out, ret_norm_g):
    B, L, D = x_prompt.shape
    Bn, Ls, _ = x_sample.shape
    assert Ls == 1, "the sample group decodes one token per sequence"
    depth = norm_g.shape[0]
    _, n_pool, page, fh, fd = cache_fox_k.shape
    fw = fh * fd
    _, _, gh, gdk, gdv = state_gdn_S.shape
    conv_dim = gdn_conv_w.shape[2]
    kw = gdn_conv_w.shape[1]
    _, _, rh, rdk, rdv = state_ret_S.shape
    n_pages = page_table.shape[1]
    assert fh == gh and 3 * fh <= 16 and fh & (fh - 1) == 0

    xp = x_prompt.reshape(B * L, D)
    xs = x_sample.reshape(Bn, D)
    w_gu16 = _to_bf16(ffn_w_gu)
    w_down16 = _to_bf16(ffn_w_down)
    freqs = (ROPE_BASE ** (-jnp.arange(rdk // 2, dtype=F32) / (rdk // 2))).reshape(1, rdk // 2)
    cos2, sin2 = _rope_table(L, freqs)

    fkp, fvp, flp, fks, fvs, fls = [], [], [], [], [], []
    gcp, gsp, gcs, gss = [], [], [], []
    rsp, rss = [], []
    for li in range(depth):
        xp = _ffn(xp, norm_g[li, 0], w_gu16, w_down16, li, 0)
        xs = _ffn(xs, norm_g[li, 0], w_gu16, w_down16, li, 0)
        j = li // 2
        if li % 2 == 0:
            o0 = 3 * fw
            zw = gh * gdv
            c0 = o0 + fh
            widths = (conv_dim, fw, fw, fw, zw, LANES)
            w_cat = _regroup_bf16(ab_w_in[j].T, [(c0, conv_dim), (0, o0), (c0 + conv_dim, zw), (o0, fh),
                                                 (c0 + conv_dim + zw, 2 * gh)], sum(widths))
            w_out16 = ab_w_out[j].astype(BF16)

            gate_p = _gate_params(fox_b_f[j], gdn_dt_bias[j], gdn_A_log[j])
            gqkv, q, k, v, z, gates, k3, v3, tails = _ab_proj_prompt(
                xp, norm_g[li, 1], w_cat, gdn_conv_w[j], gate_p, B, L, widths, fh, fd, gh, gdk)
            gates_row = gates[:, :16].T
            c_rows = _cumsum_rows(gates_row, L)[:fh].reshape(fh, B, L).transpose(1, 0, 2).reshape(B * fh, 1, L)
            fo = _fox_prompt(q, k, v, c_rows, B, L, fh)
            go, S_p = _gdn_prompt(gqkv, z, gates, gates_row, gdn_norm_g[j], B, L, gh, gdk, gdv)
            pre_p = ([fo, go], w_out16)
            fkp.append(k3.reshape(B, L, fh, fd))
            fvp.append(v3.reshape(B, L, fh, fd))
            flp.append(gates[:, :fh].reshape(B, L, fh))
            gcp.append(tails[:, 8 - (kw - 1):])
            gsp.append(S_p)

            cin, q, k, v, z, small = _rms_proj(xs, norm_g[li, 1], w_cat, widths)
            gates = _gates(small, gate_p, fh)
            sums = _page_sums(jnp.transpose(cache_fox_logf[j], (0, 2, 1)))
            fo = _fox_sample(q.reshape(Bn, fh, fd), k.reshape(Bn, fh, fd), v.reshape(Bn, fh, fd),
                             gates[:, :fh].reshape(Bn, fh, 1),
                             cache_fox_k[j].reshape(n_pool, page * fh, fd), cache_fox_v[j].reshape(n_pool, page * fh, fd),
                             sums, page_table, fh)
            go, conv_s, S_s = _gdn_sample(cin.reshape(Bn, 1, conv_dim), state_gdn_conv[j], z.reshape(Bn, 1, gh * gdv),
                                          gates.reshape(Bn, 1, LANES), gdn_conv_w[j], gdn_norm_g[j], state_gdn_S[j],
                                          gh, gdk, gdv)
            pre_s = ([fo.reshape(Bn, fw), go.reshape(Bn, gh * gdv)], w_out16)
            fks.append(k.reshape(Bn, 1, fh, fd))
            fvs.append(v.reshape(Bn, 1, fh, fd))
            fls.append(gates[:, :fh].reshape(Bn, 1, fh))
            gcs.append(conv_s)
            gss.append(S_s)
        else:
            w_in16 = _to_bf16(c_w_in[j])
            qk_w, v_w = rh * rdk, rh * rdv
            widths = (qk_w, qk_w, v_w, v_w)
            w_out16 = c_w_out[j].astype(BF16)

            q16, k16, v, sgate = _c_proj_prompt(xp, norm_g[li, 1], w_in16, cos2, sin2, L, rh, rdk, v_w)
            y, R_p = _ret_prompt(q16, k16, v, sgate, ret_norm_g[j], B, L, rh, rdk, rdv)
            pre_p = ([y], w_out16)
            rsp.append(R_p)

            q, k, v, gate = _rms_proj(xs, norm_g[li, 1], w_in16, widths)
            y, R_s = _ret_sample(q.reshape(Bn, 1, qk_w), k.reshape(Bn, 1, qk_w), v.reshape(Bn, 1, v_w),
                                 gate.reshape(Bn, 1, v_w), freqs, ret_norm_g[j], state_ret_S[j],
                                 n_pages * page, rh, rdk, rdv)
            pre_s = ([y.reshape(Bn, v_w)], w_out16)
            rss.append(R_s)
        final_g = final_norm_g if li == depth - 1 else None
        xp = _ffn(xp, norm_g[li, 2], w_gu16, w_down16, li, 1, final_g, pre_p)
        xs = _ffn(xs, norm_g[li, 2], w_gu16, w_down16, li, 1, final_g, pre_s)
    y_prompt = xp.reshape(B, L, D)
    y_sample = xs.reshape(Bn, 1, D)
    return (y_prompt, y_sample,
            jnp.stack(fkp), jnp.stack(fvp), jnp.stack(flp),
            jnp.stack(fks), jnp.stack(fvs), jnp.stack(fls),
            jnp.stack(gcp), jnp.stack(gsp), jnp.stack(gcs), jnp.stack(gss),
            jnp.stack(rsp), jnp.stack(rss))
```

```python
import functools
import math

import jax
import jax.numpy as jnp
from jax import lax
from jax.experimental import pallas as pl
from jax.experimental.pallas import tpu as pltpu

F32 = jnp.float32
BF16 = jnp.bfloat16
EPS = 1e-6
ROPE_BASE = 10000.0
NEG = -1e30
LANES = 128
GDN_CHUNK = 64
VMEM_LIMIT = 56 * 1024 * 1024
HI = lax.Precision.HIGHEST


def _cparams(n_axes):
    return pltpu.CompilerParams(dimension_semantics=("arbitrary",) * n_axes,
                                vmem_limit_bytes=VMEM_LIMIT)


def _dot(a, b, precision=None):
    return jnp.dot(a, b, preferred_element_type=F32, precision=precision)


def _dot_nt(a, b, precision=None):
    return lax.dot_general(a, b, (((1,), (1,)), ((), ())), preferred_element_type=F32, precision=precision)


def _dot_tn(a, b, precision=None):
    return lax.dot_general(a, b, (((0,), (0,)), ((), ())), preferred_element_type=F32, precision=precision)


def _rms_rows(x, g):
    return x * lax.rsqrt(jnp.mean(x * x, axis=-1, keepdims=True) + EPS) * g


MXU_DEPTH = 256


def _ffn_body(*refs, F, chunk, final_norm, n_pre):
    x_ref, g_ref, wgu_ref, wd_ref = refs[:4]
    a_refs = refs[4:4 + n_pre]
    rest = refs[4 + n_pre:]
    x = x_ref[...]
    if n_pre:
        wo_ref, rest = rest[0], rest[1:]
        row = 0
        for a_ref in a_refs:
            n = a_ref.shape[1]
            x = x + _dot(a_ref[...].astype(BF16), wo_ref[row:row + n, :])
            row += n
    if final_norm:
        gf_ref, rest = rest[0], rest[1:]
    o_ref = rest[0]
    xn = _rms_rows(x, g_ref[...]).astype(BF16)
    acc = None
    for c0 in range(0, F, chunk):
        c1 = min(F, c0 + chunk)
        a = _dot(xn, wgu_ref[:, c0:c1])
        b = _dot(xn, wgu_ref[:, F + c0:F + c1])
        h = (jax.nn.silu(a) * b).astype(BF16)
        d = _dot(h, wd_ref[c0:c1, :])
        acc = d if acc is None else acc + d
    y = x + 0.5 * acc
    o_ref[...] = _rms_rows(y, gf_ref[...]) if final_norm else y


def _ffn(x, g, w_gu, w_down, li, j, final_g=None, pre=None):
    M, D = x.shape
    F = w_down.shape[2]
    tm = min(M, 512)
    final_norm = final_g is not None
    const = lambda m: (0, 0)
    in_specs = [pl.BlockSpec((tm, D), lambda m: (m, 0)), pl.BlockSpec((1, D), const),
                pl.BlockSpec((None, None, D, 2 * F), lambda m: (li, j, 0, 0)),
                pl.BlockSpec((None, None, F, D), lambda m: (li, j, 0, 0))]
    args = [x, g.reshape(1, D), w_gu, w_down]
    acts, w_out = pre if pre is not None else ((), None)
    if acts:
        assert sum(a.shape[1] for a in acts) == w_out.shape[0]
        in_specs += [pl.BlockSpec((tm, a.shape[1]), lambda m: (m, 0)) for a in acts]
        in_specs.append(pl.BlockSpec(w_out.shape, const))
        args += [*acts, w_out]
    if final_norm:
        in_specs.append(pl.BlockSpec((1, D), const))
        args.append(final_g.reshape(1, D))
    return pl.pallas_call(
        functools.partial(_ffn_body, F=F, chunk=2 * MXU_DEPTH, final_norm=final_norm, n_pre=len(acts)),
        out_shape=jax.ShapeDtypeStruct((M, D), F32),
        grid=(M // tm,),
        in_specs=in_specs,
        out_specs=pl.BlockSpec((tm, D), lambda m: (m, 0)),
        compiler_params=_cparams(1),
        name="ffn",
    )(*args)


PROJ_CHUNK = 512


def _project(xn, w_ref, col, o_ref, o3_ref=None):
    n = o_ref.shape[1]
    for s0 in range(0, n, PROJ_CHUNK):
        s1 = min(n, s0 + PROJ_CHUNK)
        val = _dot(xn, w_ref[:, col + s0:col + s1])
        o_ref[:, s0:s1] = val
        if o3_ref is not None:
            dh = o3_ref.shape[2]
            for hh in range((s1 - s0) // dh):
                o3_ref[:, s0 // dh + hh, :] = val[:, hh * dh:(hh + 1) * dh]
    return col + n


def _proj_body(x_ref, g_ref, w_ref, *o_refs):
    xn = _rms_rows(x_ref[...], g_ref[...]).astype(BF16)
    col = 0
    for o_ref in o_refs:
        col = _project(xn, w_ref, col, o_ref)


def _rms_proj(x, g, w, widths):
    M, D = x.shape
    tm = min(M, 256)
    assert sum(widths) == w.shape[1] and all(n % LANES == 0 for n in widths)
    return pl.pallas_call(
        _proj_body,
        out_shape=[jax.ShapeDtypeStruct((M, n), F32) for n in widths],
        grid=(M // tm,),
        in_specs=[pl.BlockSpec((tm, D), lambda m: (m, 0)), pl.BlockSpec((1, D), lambda m: (0, 0)),
                  pl.BlockSpec(w.shape, lambda m: (0, 0))],
        out_specs=[pl.BlockSpec((tm, n), lambda m: (m, 0)) for n in widths],
        compiler_params=_cparams(1),
        name="rms_proj",
    )(x, g.reshape(1, D), w)


def _l2_rows(x):
    return x * lax.rsqrt(jnp.sum(x * x, axis=-1, keepdims=True) + EPS)


def _ab_proj_body(x_ref, g_ref, w_ref, cw_ref, gp_ref, qkv_o, q_o, k_o, v_o, z_o, gates_o, k3_o, v3_o, tail_o,
                  win_ref, *, tiles_per_seq, nh, dk):
    m = pl.program_id(0)
    tm = x_ref.shape[0]
    cd = qkv_o.shape[1]
    xn = _rms_rows(x_ref[...], g_ref[...]).astype(BF16)

    @pl.when(lax.rem(m, tiles_per_seq) == 0)
    def _():
        win_ref[0:8, :] = jnp.zeros((8, cd), F32)

    kw = cw_ref.shape[0]
    step = nh * dk

    def conv_product(s0):
        win_ref[8:8 + tm, s0:s0 + step] = _dot(xn, w_ref[:, s0:s0 + step])
        tail_o[:, s0:s0 + step] = win_ref[tm:tm + 8, s0:s0 + step]

    def conv_finish(s0):
        s1 = s0 + step
        conv = win_ref[8 - kw + 1:8 - kw + 1 + tm, s0:s1] * cw_ref[0:1, s0:s1]
        for i in range(1, kw):
            conv = conv + win_ref[8 - kw + 1 + i:8 - kw + 1 + i + tm, s0:s1] * cw_ref[i:i + 1, s0:s1]
        win_ref[0:8, s0:s1] = win_ref[tm:tm + 8, s0:s1]
        conv = jax.nn.silu(conv)
        if s0 >= 2 * step:
            qkv_o[:, s0:s1] = conv
        else:
            for h in range(nh):
                xh = _l2_rows(conv[:, h * dk:(h + 1) * dk])
                qkv_o[:, s0 + h * dk:s0 + (h + 1) * dk] = xh * dk ** -0.5 if s0 == 0 else xh

    assert cd == 3 * step
    conv_product(0)
    conv_product(step)
    col = _project(xn, w_ref, cd, q_o)
    conv_finish(0)
    conv_product(2 * step)
    col = _project(xn, w_ref, col, k_o, k3_o)
    conv_finish(step)
    col = _project(xn, w_ref, col, v_o, v3_o)
    col = _project(xn, w_ref, col, z_o)
    conv_finish(2 * step)
    gates_o[...] = _gate_values(_dot(xn, w_ref[:, col:col + gates_o.shape[1]]), gp_ref, nh)


def _ab_proj_prompt(x, g, w, conv_w, gate_p, B, L, widths, fh, fd, gh, gdk):
    M, D = x.shape
    tm = min(L, 512)
    cd = widths[0]
    assert sum(widths) == w.shape[1] and L % tm == 0
    row = lambda m: (m, 0)
    const = lambda m: (0, 0)
    out_shape = [jax.ShapeDtypeStruct((M, n), F32) for n in widths]
    out_specs = [pl.BlockSpec((tm, n), row) for n in widths]
    for _ in range(2):
        out_shape.append(jax.ShapeDtypeStruct((M, fh, fd), F32))
        out_specs.append(pl.BlockSpec((tm, fh, fd), lambda m: (m, 0, 0)))
    out_shape.append(jax.ShapeDtypeStruct((B, 8, cd), F32))
    out_specs.append(pl.BlockSpec((None, 8, cd), lambda m: (m // (L // tm), 0, 0)))
    return pl.pallas_call(
        functools.partial(_ab_proj_body, tiles_per_seq=L // tm, nh=gh, dk=gdk),
        out_shape=out_shape,
        grid=(M // tm,),
        in_specs=[pl.BlockSpec((tm, D), row), pl.BlockSpec((1, D), const), pl.BlockSpec(w.shape, const),
                  pl.BlockSpec(conv_w.shape, const), pl.BlockSpec(gate_p.shape, const)],
        out_specs=out_specs,
        scratch_shapes=[pltpu.VMEM((8 + tm, cd), F32)],
        compiler_params=_cparams(1),
        name="ab_proj_prompt",
    )(x, g.reshape(1, D), w, conv_w, gate_p)


def _c_proj_body(x_ref, g_ref, w_ref, cos_ref, sin_ref, q_o, k_o, v_o, sg_o, *, nh, dk):
    xn = _rms_rows(x_ref[...], g_ref[...]).astype(BF16)
    cos2, sin2 = cos_ref[...], sin_ref[...]
    hpc = PROJ_CHUNK // dk
    for idx, o_ref in enumerate((q_o, k_o)):
        for h0 in range(0, nh, hpc):
            c0 = idx * nh * dk + h0 * dk
            val = _dot(xn, w_ref[:, c0:c0 + hpc * dk])
            for hh in range(hpc):
                xh = val[:, hh * dk:(hh + 1) * dk]
                xh = xh * cos2 + pltpu.roll(xh, dk // 2, 1) * sin2
                if idx == 1:
                    xh = xh * dk ** -0.5
                o_ref[:, (h0 + hh) * dk:(h0 + hh + 1) * dk] = xh.astype(BF16)
    col = _project(xn, w_ref, 2 * nh * dk, v_o)
    n = sg_o.shape[1]
    for s0 in range(0, n, PROJ_CHUNK):
        s1 = min(n, s0 + PROJ_CHUNK)
        sg_o[:, s0:s1] = jax.nn.silu(_dot(xn, w_ref[:, col + s0:col + s1]))


def _c_proj_prompt(x, g, w, cos2, sin2, L, nh, dk, v_w):
    M, D = x.shape
    tm = min(L, 512)
    qk_w = nh * dk
    row = lambda m: (m, 0)
    const = lambda m: (0, 0)
    pos = lambda m: (lax.rem(m, L // tm), 0)
    return pl.pallas_call(
        functools.partial(_c_proj_body, nh=nh, dk=dk),
        out_shape=[jax.ShapeDtypeStruct((M, qk_w), BF16), jax.ShapeDtypeStruct((M, qk_w), BF16),
                   jax.ShapeDtypeStruct((M, v_w), F32), jax.ShapeDtypeStruct((M, v_w), F32)],
        grid=(M // tm,),
        in_specs=[pl.BlockSpec((tm, D), row), pl.BlockSpec((1, D), const), pl.BlockSpec(w.shape, const),
                  pl.BlockSpec((tm, dk), pos), pl.BlockSpec((tm, dk), pos)],
        out_specs=[pl.BlockSpec((tm, qk_w), row), pl.BlockSpec((tm, qk_w), row),
                   pl.BlockSpec((tm, v_w), row), pl.BlockSpec((tm, v_w), row)],
        compiler_params=_cparams(1),
        name="c_proj_prompt",
    )(x, g.reshape(1, D), w, cos2, sin2)


def _cast_body(x_ref, o_ref):
    o_ref[...] = x_ref[...].astype(o_ref.dtype)


def _to_bf16(w):
    cols = w.shape[-1]
    rows = w.size // cols
    tr = 512 if rows % 512 == 0 else rows
    out = pl.pallas_call(
        _cast_body,
        out_shape=jax.ShapeDtypeStruct((rows, cols), BF16),
        grid=(rows // tr,),
        in_specs=[pl.BlockSpec((tr, cols), lambda r: (r, 0))],
        out_specs=pl.BlockSpec((tr, cols), lambda r: (r, 0)),
        compiler_params=_cparams(1),
        name="to_bf16",
    )(w.reshape(rows, cols))
    return out.reshape(w.shape)


def _regroup_body(wt_ref, o_ref, *, pieces):
    N, D = wt_ref.shape
    col = 0
    narrow = []
    for start, width in pieces:
        if width % LANES == 0:
            assert not narrow and col % LANES == 0
            for r0 in range(0, width, PROJ_CHUNK):
                r1 = min(width, r0 + PROJ_CHUNK)
                o_ref[:, col + r0:col + r1] = wt_ref[start + r0:start + r1, :].T.astype(o_ref.dtype)
        else:
            w0 = min(start, N - LANES)
            narrow.append(wt_ref[w0:w0 + LANES, :].T[:, start - w0:start - w0 + width])
        col += width
    if narrow:
        first = col - sum(p.shape[1] for p in narrow)
        assert first % LANES == 0 and o_ref.shape[1] - first == LANES
        pad = jnp.zeros((D, o_ref.shape[1] - col), F32)
        o_ref[:, first:] = jnp.concatenate(narrow + [pad], axis=1).astype(o_ref.dtype)


def _regroup_bf16(wt, pieces, n_out):
    N, D = wt.shape
    return pl.pallas_call(
        functools.partial(_regroup_body, pieces=tuple(pieces)),
        out_shape=jax.ShapeDtypeStruct((D, n_out), BF16),
        grid=(1,),
        in_specs=[pl.BlockSpec((N, D), lambda r: (0, 0))],
        out_specs=pl.BlockSpec((D, n_out), lambda r: (0, 0)),
        compiler_params=_cparams(1),
        name="regroup_bf16",
    )(wt)


def _gate_values(s, p_ref, nh):
    s = s + p_ref[0:1, :]
    lane = lax.broadcasted_iota(jnp.int32, s.shape, 1)
    logf = jax.nn.log_sigmoid(s)
    g = -jnp.exp(p_ref[1:2, :]) * jax.nn.softplus(s)
    beta = jax.nn.sigmoid(s)
    return jnp.where(lane < nh, logf, jnp.where(lane < 2 * nh, g, jnp.where(lane < 3 * nh, beta, 0.0)))


def _gate_params(b_f, dt_bias, a_log):
    nh = b_f.shape[0]
    pad = jnp.zeros((LANES - 3 * nh,), F32)
    p = jnp.zeros((8, LANES), F32)
    p = p.at[0].set(jnp.concatenate([b_f, dt_bias, jnp.zeros((nh,), F32), pad]))
    return p.at[1].set(jnp.concatenate([jnp.zeros((nh,), F32), a_log, jnp.zeros((nh,), F32), pad]))


def _gates_body(s_ref, p_ref, o_ref, *, nh):
    o_ref[...] = _gate_values(s_ref[...], p_ref, nh)


def _gates(small, gate_p, nh):
    M = small.shape[0]
    tm = min(M, 2048)
    return pl.pallas_call(
        functools.partial(_gates_body, nh=nh),
        out_shape=jax.ShapeDtypeStruct((M, LANES), F32),
        grid=(M // tm,),
        in_specs=[pl.BlockSpec((tm, LANES), lambda m: (m, 0)), pl.BlockSpec((8, LANES), lambda m: (0, 0))],
        out_specs=pl.BlockSpec((tm, LANES), lambda m: (m, 0)),
        compiler_params=_cparams(1),
        name="gates",
    )(small, gate_p)


def _cumsum_body(x_ref, o_ref):
    L = x_ref.shape[1]
    r = lax.broadcasted_iota(jnp.int32, (LANES, LANES), 0)
    c = lax.broadcasted_iota(jnp.int32, (LANES, LANES), 1)
    upper = (r <= c).astype(F32)
    carry = jnp.zeros((x_ref.shape[0], 1), F32)
    for j in range(L // LANES):
        cs = _dot(x_ref[:, j * LANES:(j + 1) * LANES], upper, HI) + carry
        o_ref[:, j * LANES:(j + 1) * LANES] = cs
        carry = cs[:, LANES - 1:LANES]


def _cumsum_rows(x, seg):
    R, M = x.shape
    return pl.pallas_call(
        _cumsum_body,
        out_shape=jax.ShapeDtypeStruct((R, M), F32),
        grid=(M // seg,),
        in_specs=[pl.BlockSpec((R, seg), lambda b: (0, b))],
        out_specs=pl.BlockSpec((R, seg), lambda b: (0, b)),
        compiler_params=_cparams(1),
        name="cumsum_rows",
    )(x)


def _fox_body(q_ref, k_ref, v_ref, *rest, t, hg, dh, scale):
    nbg = q_ref.shape[0]
    c_refs = rest[:nbg]
    o_ref, vt_ref, cb_ref, acc_ref, m_ref = rest[nbg:]
    i = pl.program_id(2)
    L = k_ref.shape[1]
    log2e = math.log2(math.e)
    units = [(s, g) for s in range(nbg) for g in range(hg)]

    @pl.when(i == 0)
    def _():
        for u, (s, g) in enumerate(units):
            for jj in range(L // t):
                vt_ref[u, 0:dh, jj * t:(jj + 1) * t] = v_ref[s, jj * t:(jj + 1) * t, g * dh:(g + 1) * dh].T.astype(BF16)
            vt_ref[u, dh:, :] = jnp.ones((vt_ref.shape[1] - dh, L), BF16)
            for jj in range(L // LANES):
                row = c_refs[s][g, :, jj * LANES:(jj + 1) * LANES] * log2e
                cb_ref[u, jj * LANES:(jj + 1) * LANES, :] = jnp.broadcast_to(row, (LANES, LANES)).T

    q0 = pl.multiple_of(i * t, t)
    qs = [q_ref[s, :, g * dh:(g + 1) * dh].astype(BF16) for s, g in units]
    r2 = [c_refs[s][g, :, pl.ds(q0, t)][:, t - 1:t] * log2e for s, g in units]
    m_ref[...] = jnp.full_like(m_ref, NEG)
    acc_ref[...] = jnp.zeros_like(acc_ref)

    def step(j, masked):
        k0 = pl.multiple_of(j * t, t)
        qk = [_dot_nt(k_ref[s, pl.ds(k0, t), g * dh:(g + 1) * dh].astype(BF16), qs[u])
              for u, (s, g) in enumerate(units)]
        ps, alphas = [], []
        for u in range(len(units)):
            bias = r2[u] - cb_ref[u, pl.ds(k0, t), :]
            st = qk[u] * (scale * log2e) + jnp.concatenate([bias] * (t // LANES), axis=1)
            if masked:
                key = lax.broadcasted_iota(jnp.int32, (t, t), 0)
                qry = lax.broadcasted_iota(jnp.int32, (t, t), 1)
                st = jnp.where(key <= qry, st, NEG)
            m_prev = m_ref[u]
            m_new = jnp.maximum(m_prev, jnp.max(st, axis=0, keepdims=True))
            ps.append(jnp.exp2(st - m_new).astype(BF16))
            alphas.append(jnp.exp2(m_prev - m_new))
            m_ref[u] = m_new
        for u in range(len(units)):
            acc_ref[u] = acc_ref[u] * alphas[u] + _dot(vt_ref[u, :, pl.ds(k0, t)], ps[u])

    def loop_body(j, carry):
        step(j, False)
        return carry

    lax.fori_loop(0, i, loop_body, 0)
    step(i, True)
    for u, (s, g) in enumerate(units):
        acc = acc_ref[u]
        o_ref[s, :, g * dh:(g + 1) * dh] = (acc[0:dh] / acc[dh:dh + 1]).T


def _fox_prompt(q, k, v, c_rows, B, L, H):
    M, W = q.shape
    Dh = W // H
    t = min(L, 512)
    nq = L // t
    hg = 2 if H % 2 == 0 else 1
    nbg = 2 if B % 2 == 0 else 1
    nu = nbg * hg
    kernel = functools.partial(_fox_body, t=t, hg=hg, dh=Dh, scale=Dh ** -0.5)

    def c_map(s):
        return lambda p, h, i: ((p * nbg + s) * (H // hg) + h, 0, 0)

    o = pl.pallas_call(
        kernel,
        out_shape=jax.ShapeDtypeStruct((B, L, W), F32),
        grid=(B // nbg, H // hg, nq),
        in_specs=[
            pl.BlockSpec((nbg, t, hg * Dh), lambda p, h, i: (p, i, h)),
            pl.BlockSpec((nbg, L, hg * Dh), lambda p, h, i: (p, 0, h)),
            pl.BlockSpec((nbg, L, hg * Dh), lambda p, h, i: (p, 0, h)),
        ] + [pl.BlockSpec((hg, 1, L), c_map(s)) for s in range(nbg)],
        out_specs=pl.BlockSpec((nbg, t, hg * Dh), lambda p, h, i: (p, i, h)),
        scratch_shapes=[pltpu.VMEM((nu, Dh + 16, L), BF16), pltpu.VMEM((nu, L, LANES), F32),
                        pltpu.VMEM((nu, Dh + 16, t), F32), pltpu.VMEM((nu, 1, t), F32)],
        compiler_params=_cparams(3),
        name="fox_prompt",
    )(q.reshape(B, L, W), k.reshape(B, L, W), v.reshape(B, L, W), *([c_rows] * nbg))
    return o.reshape(M, W)


def _split2(x):
    hi = x.astype(BF16)
    return hi, (x - hi.astype(F32)).astype(BF16)


def _split3(x):
    h1 = x.astype(BF16)
    r1 = x - h1.astype(F32)
    h2 = r1.astype(BF16)
    return h1, h2, (r1 - h2.astype(F32)).astype(BF16)


def _dot3(a, b):
    (ah, al), (bh, bl) = a, b
    return _dot(ah, bh) + (_dot(ah, bl) + _dot(al, bh))


def _unit_lower_inverses(a_list, n):
    ri = lax.broadcasted_iota(jnp.int32, (n, n), 0)
    ci = lax.broadcasted_iota(jnp.int32, (n, n), 1)
    eye = (ri == ci).astype(F32)
    nm = len(a_list)
    diag16 = (ri >> 4) == (ci >> 4)
    pw = [jnp.where(diag16, a, 0.0) for a in a_list]
    t = [eye - p for p in pw]
    for _ in range(3):
        pw16 = [p.astype(BF16) for p in pw]
        pw = [_dot(p, p) for p in pw16]
        pn16 = [p.astype(BF16) for p in pw]
        t = [t[m] + _dot(t[m].astype(BF16), pn16[m]) for m in range(nm)]
    size = 16
    while size < GDN_CHUNK:
        sh = size.bit_length() - 1
        off = ((ri >> (sh + 1)) == (ci >> (sh + 1))) & ((ri >> sh) == (ci >> sh) + 1)
        t16 = [x.astype(BF16) for x in t]
        mid = [_dot(jnp.where(off, a_list[m], 0.0).astype(BF16), t16[m]) for m in range(nm)]
        t = [t[m] - _dot(t16[m], mid[m].astype(BF16)) for m in range(nm)]
        size *= 2
    ts = [_split2(x) for x in t]
    res = [(eye - t[m]) - _dot3(_split2(a_list[m]), ts[m]) for m in range(nm)]
    return [t[m] + _dot(ts[m][0], res[m].astype(BF16)) for m in range(nm)]


def _gdn_body(qkv_ref, z_ref, gc_ref, *rest, nh, dk, dv, nseq):
    gr_refs = rest[:nseq]
    ng_ref, o_ref, s_out_ref, s_ref, m16_ref = rest[nseq:]
    j = pl.program_id(1)
    T = qkv_ref.shape[1]
    C = GDN_CHUNK
    nc = T // C

    ri = lax.broadcasted_iota(jnp.int32, (T, T), 0)
    ci = lax.broadcasted_iota(jnp.int32, (T, T), 1)
    csh = C.bit_length() - 1
    same = (ri >> csh) == (ci >> csh)
    tril = same & (ri >= ci)
    strict = same & (ri > ci)

    @pl.when((pl.program_id(0) == 0) & (j == 0))
    def _():
        m16_ref[0] = tril.astype(BF16)
        m16_ref[1] = (same & (ri <= ci)).astype(BF16)

    @pl.when(j == 0)
    def _():
        s_ref[...] = jnp.zeros_like(s_ref)

    rowi = lax.broadcasted_iota(jnp.int32, (T, 1), 0)
    units = [(s, h) for s in range(nseq) for h in range(nh)]
    nu = len(units)
    gcols, grows = [], []
    for s in range(nseq):
        gcols.append(sum(_dot(m16_ref[0], part) for part in _split3(gc_ref[s])))
        grows.append(sum(_dot(part, m16_ref[1]) for part in _split3(gr_refs[s][...])))

    q, k, beta, gcc, gcr, gam, kb, k16 = [], [], [], [], [], [], [], []
    for u, (s, h) in enumerate(units):
        q.append(qkv_ref[s, :, h * dk:(h + 1) * dk])
        k.append(qkv_ref[s, :, nh * dk + h * dk:nh * dk + (h + 1) * dk])
        beta.append(gc_ref[s, :, 2 * nh + h:2 * nh + h + 1])
        gcc.append(gcols[s][:, nh + h:nh + h + 1])
        gcr.append(grows[s][nh + h:nh + h + 1, :])
        gam.append(jnp.exp(jnp.where(tril, gcc[u] - gcr[u], NEG)))
        kb.append(k[u] * beta[u])
        k16.append(k[u].astype(BF16))
    a = [jnp.where(strict, _dot_nt(kb[u].astype(BF16), k16[u]) * gam[u], 0.0) for u in range(nu)]
    tinv = _unit_lower_inverses(a, T)
    x = []
    for u, (s, h) in enumerate(units):
        v = qkv_ref[s, :, 2 * nh * dk + h * dv:2 * nh * dk + (h + 1) * dv]
        rhs = jnp.concatenate([v * beta[u], kb[u] * jnp.exp(gcc[u])], axis=-1)
        x.append(_dot3(_split2(tinv[u]), _split2(rhs)))
    attn, qg, kd, egl = [], [], [], []
    for u in range(nu):
        attn.append((_dot_nt(q[u].astype(BF16), k16[u]) * gam[u]).astype(BF16))
        qg.append((q[u] * jnp.exp(gcc[u])).astype(BF16))
        gl = gcr[u][:, C - 1:C]
        for cidx in range(1, nc):
            gl = jnp.where(rowi >= cidx * C, gcr[u][:, (cidx + 1) * C - 1:(cidx + 1) * C], gl)
        kd.append((k[u] * jnp.exp(gl - gcc[u])).astype(BF16))
        egl.append(jnp.exp(gl))
    S = [s_ref[s, h] for s, h in units]
    outs = [[] for _ in units]
    for cidx in range(nc):
        sl = slice(cidx * C, (cidx + 1) * C)
        S16 = [st.astype(BF16) for st in S]
        vn16 = [(x[u][sl, :dv] - _dot(x[u][sl, dv:].astype(BF16), S16[u])).astype(BF16) for u in range(nu)]
        for u in range(nu):
            outs[u].append(_dot(qg[u][sl], S16[u]) + _dot(attn[u][sl, cidx * C:(cidx + 1) * C], vn16[u]))
        S = [S[u] * egl[u][cidx * C:cidx * C + 1] + _dot_tn(kd[u][sl], vn16[u]) for u in range(nu)]
    for u, (s, h) in enumerate(units):
        s_ref[s, h] = S[u]
        o = _rms_rows(jnp.concatenate(outs[u], axis=0), ng_ref[...])
        o_ref[s, :, h * dv:(h + 1) * dv] = o * jax.nn.silu(z_ref[s, :, h * dv:(h + 1) * dv])

    @pl.when(j == pl.num_programs(1) - 1)
    def _():
        s_out_ref[...] = s_ref[...]


def _gdn_prompt(qkv, z, gates_col, gates_row, norm_g, B, L, nh, dk, dv):
    M, CD = qkv.shape
    T = 2 * GDN_CHUNK
    nb = L // T
    nseq = 4 if B % 4 == 0 else (2 if B % 2 == 0 else 1)
    kernel = functools.partial(_gdn_body, nh=nh, dk=dk, dv=dv, nseq=nseq)

    def row_map(s):
        return lambda p, j: (0, (p * nseq + s) * nb + j)

    per_seq = lambda p, j: (p, j, 0)
    go, S = pl.pallas_call(
        kernel,
        out_shape=[jax.ShapeDtypeStruct((B, L, nh * dv), F32), jax.ShapeDtypeStruct((B, nh, dk, dv), F32)],
        grid=(B // nseq, nb),
        in_specs=[
            pl.BlockSpec((nseq, T, CD), per_seq),
            pl.BlockSpec((nseq, T, nh * dv), per_seq),
            pl.BlockSpec((nseq, T, LANES), per_seq),
        ] + [pl.BlockSpec((16, T), row_map(s)) for s in range(nseq)] + [
            pl.BlockSpec((1, dv), lambda p, j: (0, 0)),
        ],
        out_specs=[
            pl.BlockSpec((nseq, T, nh * dv), per_seq),
            pl.BlockSpec((nseq, nh, dk, dv), lambda p, j: (p, 0, 0, 0)),
        ],
        scratch_shapes=[pltpu.VMEM((nseq, nh, dk, dv), F32), pltpu.VMEM((2, T, T), BF16)],
        compiler_params=_cparams(2),
        name="gdn_prompt",
    )(qkv.reshape(B, L, CD), z.reshape(B, L, nh * dv), gates_col.reshape(B, L, LANES),
      *([gates_row] * nseq), norm_g.reshape(1, dv))
    return go.reshape(M, nh * dv), S


def _rope(x, cos2, sin2, nh, dk):
    outs = []
    for h in range(nh):
        xh = x[:, h * dk:(h + 1) * dk]
        outs.append(xh * cos2 + pltpu.roll(xh, dk // 2, 1) * sin2)
    return outs


def _rope_tables(pos, freqs):
    ang = pos * freqs
    cos, sin = jnp.cos(ang), jnp.sin(ang)
    return jnp.concatenate([cos, cos], axis=-1), jnp.concatenate([-sin, sin], axis=-1)


def _head_layernorm(o):
    mu = jnp.mean(o, axis=-1, keepdims=True)
    oc = o - mu
    return oc * lax.rsqrt(jnp.mean(oc * oc, axis=-1, keepdims=True) + EPS)


def _log_gamma(h):
    return math.log(1.0 - 2.0 ** (-5.0 - h))


def _rope_table_body(fr_ref, cos_o, sin_o):
    C = cos_o.shape[0]
    rowi = lax.broadcasted_iota(jnp.int32, (C, 1), 0)
    pos = (pl.program_id(0) * C + rowi).astype(F32)
    cos_o[...], sin_o[...] = _rope_tables(pos, fr_ref[...])


def _rope_table(L, freqs):
    C = min(L, 512)
    dk = 2 * freqs.shape[1]
    return pl.pallas_call(
        _rope_table_body,
        out_shape=[jax.ShapeDtypeStruct((L, dk), F32)] * 2,
        grid=(L // C,),
        in_specs=[pl.BlockSpec(freqs.shape, lambda c: (0, 0))],
        out_specs=[pl.BlockSpec((C, dk), lambda c: (c, 0))] * 2,
        compiler_params=_cparams(1),
        name="rope_table",
    )(freqs)


def _ret_body(q_ref, k_ref, v_ref, sg_ref, gn_ref, o_ref, s_out_ref, s_ref, dmat_ref, *, nh, dk, dv):
    b = pl.program_id(0)
    j = pl.program_id(1)
    nseq, C = q_ref.shape[0], q_ref.shape[1]
    rowf = lax.broadcasted_iota(jnp.int32, (C, 1), 0).astype(F32)

    @pl.when((b == 0) & (j == 0))
    def _():
        ri = lax.broadcasted_iota(jnp.int32, (C, C), 0)
        ci = lax.broadcasted_iota(jnp.int32, (C, C), 1)
        diff = (ri - ci).astype(F32)
        for h in range(nh):
            dmat_ref[h] = jnp.exp(jnp.where(ri >= ci, diff * _log_gamma(h), NEG))

    @pl.when(j == 0)
    def _():
        s_ref[...] = jnp.zeros_like(s_ref)

    units = [(s, h) for s in range(nseq) for h in range(nh)]
    nu = range(len(units))
    q16 = [q_ref[s, :, h * dk:(h + 1) * dk] for s, h in units]
    k16 = [k_ref[s, :, h * dk:(h + 1) * dk] for s, h in units]
    v = [v_ref[s, :, h * dv:(h + 1) * dv] for s, h in units]
    S = [s_ref[s, h] for s, h in units]
    qk = [_dot_nt(q16[u], k16[u]) for u in nu]
    cross = [_dot(q16[u], S[u].astype(BF16)) for u in nu]
    for u, (s, h) in enumerate(units):
        vd = (v[u] * jnp.exp(_log_gamma(h) * (C - 1.0 - rowf))).astype(BF16)
        s_ref[s, h] = S[u] * math.exp(_log_gamma(h) * C) + _dot_tn(k16[u], vd)
    inner = [_dot((qk[u] * dmat_ref[h]).astype(BF16), v[u].astype(BF16)) for u, (s, h) in enumerate(units)]
    for u, (s, h) in enumerate(units):
        o = inner[u] + cross[u] * jnp.exp(_log_gamma(h) * (rowf + 1.0))
        o = _head_layernorm(o) * gn_ref[:, h * dv:(h + 1) * dv]
        o_ref[s, :, h * dv:(h + 1) * dv] = sg_ref[s, :, h * dv:(h + 1) * dv] * o

    @pl.when(j == pl.num_programs(1) - 1)
    def _():
        s_out_ref[...] = s_ref[...]


def _ret_prompt(q16, k16, v, sgate, gnorm, B, L, nh, dk, dv):
    M = q16.shape[0]
    C = 128
    nb = L // C
    nseq = 4 if B % 4 == 0 else (2 if B % 2 == 0 else 1)
    kernel = functools.partial(_ret_body, nh=nh, dk=dk, dv=dv)
    tok = lambda p, j: (p, j, 0)
    y, S = pl.pallas_call(
        kernel,
        out_shape=[jax.ShapeDtypeStruct((B, L, nh * dv), F32), jax.ShapeDtypeStruct((B, nh, dk, dv), F32)],
        grid=(B // nseq, nb),
        in_specs=[
            pl.BlockSpec((nseq, C, nh * dk), tok),
            pl.BlockSpec((nseq, C, nh * dk), tok),
            pl.BlockSpec((nseq, C, nh * dv), tok),
            pl.BlockSpec((nseq, C, nh * dv), tok),
            pl.BlockSpec((1, nh * dv), lambda p, j: (0, 0)),
        ],
        out_specs=[
            pl.BlockSpec((nseq, C, nh * dv), tok),
            pl.BlockSpec((nseq, nh, dk, dv), lambda p, j: (p, 0, 0, 0)),
        ],
        scratch_shapes=[pltpu.VMEM((nseq, nh, dk, dv), F32), pltpu.VMEM((nh, C, C), F32)],
        compiler_params=_cparams(2),
        name="ret_prompt",
    )(q16.reshape(B, L, nh * dk), k16.reshape(B, L, nh * dk), v.reshape(B, L, nh * dv),
      sgate.reshape(B, L, nh * dv), gnorm.reshape(1, nh * dv))
    return y.reshape(M, nh * dv), S


def _page_sums_body(x_ref, o_ref, op_ref):
    _, nh, plen = x_ref.shape
    n = nh * plen
    wide = o_ref.shape[1]

    @pl.when(pl.program_id(0) == 0)
    def _():
        kk = lax.broadcasted_iota(jnp.int32, (plen, wide), 0)
        cc = lax.broadcasted_iota(jnp.int32, (plen, wide), 1)
        key_c = cc >> (nh.bit_length() - 1)
        later_or_total = (cc >= n) | (kk > key_c)
        for h in range(nh):
            op_ref[h] = (((cc & (nh - 1)) == h) & later_or_total).astype(BF16)

    acc = None
    for h in range(nh):
        for part in _split2(x_ref[:, h, :]):
            d = _dot(part, op_ref[h])
            acc = d if acc is None else acc + d
    o_ref[...] = acc


def _page_sums(lf_pages):
    P, nh, plen = lf_pages.shape
    wide = nh * plen + LANES
    tp = 512 if P % 512 == 0 else P
    return pl.pallas_call(
        _page_sums_body,
        out_shape=jax.ShapeDtypeStruct((P, wide), F32),
        grid=(P // tp,),
        in_specs=[pl.BlockSpec((tp, nh, plen), lambda p: (p, 0, 0))],
        out_specs=pl.BlockSpec((tp, wide), lambda p: (p, 0)),
        scratch_shapes=[pltpu.VMEM((nh, plen, wide), BF16)],
        compiler_params=_cparams(1),
        name="page_sums",
    )(lf_pages)


def _paged_body(pt_ref, *refs, G, nh, scale):
    k_refs = refs[:G]
    v_refs = refs[G:2 * G]
    sums_ref, q_ref, kn_ref, vn_ref, lfn_ref, o_ref, m_ref, l_ref, acc_ref, carry_ref = refs[2 * G:]
    b_idx = pl.program_id(0)
    s_idx = pl.program_id(1)
    ns = pl.num_programs(1)
    n = k_refs[0].shape[0]
    dh = q_ref.shape[1]

    @pl.when(s_idx == 0)
    def _():
        m_ref[...] = jnp.full_like(m_ref, NEG)
        l_ref[...] = jnp.zeros_like(l_ref)
        acc_ref[...] = jnp.zeros_like(acc_ref)
        carry_ref[...] = jnp.zeros_like(carry_ref)

    q = q_ref[...]
    q8 = jnp.concatenate([q, jnp.zeros((8 - nh, dh), F32)], axis=0).astype(BF16)
    lfn = jnp.concatenate([lfn_ref[...], jnp.zeros((8 - nh, 1), F32)], axis=0)
    hrow = lax.broadcasted_iota(jnp.int32, (8, n), 0)
    hcol = lax.broadcasted_iota(jnp.int32, (8, n), 1) & (nh - 1)
    match = hrow == hcol
    carry = carry_ref[...]
    s_parts = [None] * G
    for i in reversed(range(G)):
        page_id = pt_ref[b_idx, (ns - 1 - s_idx) * G + i]
        wt = sums_ref[pl.ds(page_id, 1), :]
        bias = wt[:, :n] + carry
        carry = carry + jnp.concatenate([wt[:, n:]] * (n // LANES), axis=1)
        s = _dot_nt(q8, k_refs[i][...].astype(BF16)) * scale + bias + lfn
        s_parts[i] = jnp.where(match, s, NEG)
    carry_ref[...] = carry
    m_run, l_run, acc_run = m_ref[...], l_ref[...], acc_ref[...]
    half = max(G // 2, 1)
    for grp in (range(half, G), range(0, half)) if G > 1 else (range(G),):
        sg = [s_parts[i] for i in grp]
        m_new = jnp.maximum(m_run, jnp.max(functools.reduce(jnp.maximum, sg), axis=-1, keepdims=True))
        alpha = jnp.exp(m_run - m_new)
        pg = [jnp.exp(s - m_new) for s in sg]
        l_run = alpha * l_run + jnp.sum(functools.reduce(jnp.add, pg), axis=-1, keepdims=True)
        pv = functools.reduce(jnp.add, [_dot(p.astype(BF16), v_refs[i][...].astype(BF16)) for p, i in zip(pg, grp)])
        acc_run = alpha * acc_run + pv
        m_run = m_new
    m_ref[...], l_ref[...], acc_ref[...] = m_run, l_run, acc_run

    @pl.when(s_idx == pl.num_programs(1) - 1)
    def _():
        qf = q.astype(BF16).astype(F32)
        kn = kn_ref[...].astype(BF16).astype(F32)
        s_new = jnp.sum(qf * kn, axis=-1, keepdims=True) * scale
        m_prev = m_ref[0:nh, :]
        m_new = jnp.maximum(m_prev, s_new)
        alpha = jnp.exp(m_prev - m_new)
        p_new = jnp.exp(s_new - m_new)
        l_fin = alpha * l_ref[0:nh, :] + p_new
        vn = vn_ref[...].astype(BF16).astype(F32)
        num = alpha * acc_ref[0:nh, :] + p_new.astype(BF16).astype(F32) * vn
        o_ref[...] = num / l_fin


def _fox_sample(q, k_new, v_new, lf_new, k_pages, v_pages, page_sums, page_table, nh):
    Bn, _, dh = q.shape
    n_pages = page_table.shape[1]
    n_pool, n, _ = k_pages.shape
    G = 16 if n_pages % 16 == 0 else (8 if n_pages % 8 == 0 else 1)
    ns = n_pages // G

    def page_map(i):
        return lambda b, s, pt: (pt[b, (ns - 1 - s) * G + i], 0, 0)

    per_b = lambda b, s, pt: (b, 0, 0)
    grid_spec = pltpu.PrefetchScalarGridSpec(
        num_scalar_prefetch=1,
        grid=(Bn, ns),
        in_specs=[pl.BlockSpec((None, n, dh), page_map(i)) for i in range(G)]
        + [pl.BlockSpec((None, n, dh), page_map(i)) for i in range(G)]
        + [pl.BlockSpec(page_sums.shape, lambda b, s, pt: (0, 0)),
           pl.BlockSpec((None, nh, dh), per_b), pl.BlockSpec((None, nh, dh), per_b),
           pl.BlockSpec((None, nh, dh), per_b), pl.BlockSpec((None, nh, 1), per_b)],
        out_specs=pl.BlockSpec((None, nh, dh), per_b),
        scratch_shapes=[pltpu.VMEM((8, 1), F32), pltpu.VMEM((8, 1), F32), pltpu.VMEM((8, dh), F32),
                        pltpu.VMEM((1, n), F32)],
    )
    return pl.pallas_call(
        functools.partial(_paged_body, G=G, nh=nh, scale=dh ** -0.5),
        out_shape=jax.ShapeDtypeStruct((Bn, nh, dh), F32),
        grid_spec=grid_spec,
        compiler_params=_cparams(2),
        name="fox_sample",
    )(page_table, *([k_pages] * G), *([v_pages] * G), page_sums, q, k_new, v_new, lf_new)


def _pad8(row):
    return jnp.concatenate([row, jnp.zeros((8 - row.shape[0], row.shape[1]), row.dtype)], axis=0)


def _gdn_sample_body(cin_ref, prev_ref, z_ref, gt_ref, cw_ref, ng_ref, s0_ref, o_ref, cnew_ref, s_out_ref, *, nh, dk, dv):
    nb = cin_ref.shape[0]
    w = cw_ref[...]
    kw = w.shape[0]
    convs, gts = [], []
    for b in range(nb):
        u = cin_ref[b]
        prev = prev_ref[b]
        conv = u * w[kw - 1:kw]
        for i in range(kw - 1):
            conv = conv + prev[i:i + 1] * w[i:i + 1]
        cnew_ref[b, 0:kw - 2, :] = prev[1:kw - 1]
        cnew_ref[b, kw - 2:kw - 1, :] = u
        convs.append(jax.nn.silu(conv))
        gts.append(gt_ref[b])
    units = [(b, h) for b in range(nb) for h in range(nh)]
    nu = range(len(units))
    q = [_l2_rows(convs[b][:, h * dk:(h + 1) * dk]) * dk ** -0.5 for b, h in units]
    k = [_l2_rows(convs[b][:, nh * dk + h * dk:nh * dk + (h + 1) * dk]) for b, h in units]
    v = [convs[b][:, 2 * nh * dk + h * dv:2 * nh * dk + (h + 1) * dv] for b, h in units]
    beta = [gts[b][:, 2 * nh + h:2 * nh + h + 1] for b, h in units]
    eg = [jnp.exp(gts[b][:, nh + h:nh + h + 1]) for b, h in units]
    S = [s0_ref[b, h] for b, h in units]
    r = [_dot(_pad8(jnp.concatenate([k[u] * beta[u] * eg[u], q[u] * eg[u]], axis=0)).astype(BF16), S[u].astype(BF16))
         for u in nu]
    vn16 = [(v[u] * beta[u] - r[u][0:1]).astype(BF16) for u in nu]
    for u, (b, h) in enumerate(units):
        s_out_ref[b, h] = S[u] * eg[u] + _dot_tn(_pad8(k[u]).astype(BF16), _pad8(vn16[u]))
    for u, (b, h) in enumerate(units):
        qk = jnp.sum(q[u].astype(BF16).astype(F32) * k[u].astype(BF16).astype(F32), axis=-1, keepdims=True)
        o = r[u][1:2] + qk.astype(BF16).astype(F32) * vn16[u].astype(F32)
        o = _rms_rows(o, ng_ref[...])
        o_ref[b, :, h * dv:(h + 1) * dv] = o * jax.nn.silu(z_ref[b, :, h * dv:(h + 1) * dv])


def _gdn_sample(conv_in, conv_prev, z, gates_col, conv_w, norm_g, S0, nh, dk, dv):
    Bn, _, CD = conv_in.shape
    kw = conv_w.shape[0]
    nb = 4 if Bn % 4 == 0 else 1
    per_b3 = lambda b: (b, 0, 0)
    per_b4 = lambda b: (b, 0, 0, 0)
    kernel = functools.partial(_gdn_sample_body, nh=nh, dk=dk, dv=dv)
    return pl.pallas_call(
        kernel,
        out_shape=[jax.ShapeDtypeStruct((Bn, 1, nh * dv), F32), jax.ShapeDtypeStruct((Bn, kw - 1, CD), F32),
                   jax.ShapeDtypeStruct((Bn, nh, dk, dv), F32)],
        grid=(Bn // nb,),
        in_specs=[
            pl.BlockSpec((nb, 1, CD), per_b3),
            pl.BlockSpec((nb, kw - 1, CD), per_b3),
            pl.BlockSpec((nb, 1, nh * dv), per_b3),
            pl.BlockSpec((nb, 1, LANES), per_b3),
            pl.BlockSpec(conv_w.shape, lambda b: (0, 0)),
            pl.BlockSpec((1, dv), lambda b: (0, 0)),
            pl.BlockSpec((nb, nh, dk, dv), per_b4),
        ],
        out_specs=[
            pl.BlockSpec((nb, 1, nh * dv), per_b3),
            pl.BlockSpec((nb, kw - 1, CD), per_b3),
            pl.BlockSpec((nb, nh, dk, dv), per_b4),
        ],
        compiler_params=_cparams(1),
        name="gdn_sample",
    )(conv_in, conv_prev, z, gates_col, conv_w, norm_g.reshape(1, dv), S0)


def _ret_sample_body(q_ref, k_ref, v_ref, gate_ref, fr_ref, gn_ref, s0_ref, o_ref, s_out_ref, *, nh, dk, dv, pos):
    nb = q_ref.shape[0]
    cos2, sin2 = _rope_tables(jnp.full((1, 1), pos, F32), fr_ref[...])
    units = [(b, h) for b in range(nb) for h in range(nh)]
    qs, ks = [], []
    for b in range(nb):
        qs += _rope(q_ref[b], cos2, sin2, nh, dk)
        ks += [kh * dk ** -0.5 for kh in _rope(k_ref[b], cos2, sin2, nh, dk)]
    v16 = [v_ref[b, :, h * dv:(h + 1) * dv].astype(BF16) for b, h in units]
    S = [s0_ref[b, h] for b, h in units]
    cross = [_dot(_pad8(qs[u] * math.exp(_log_gamma(h))).astype(BF16), S[u].astype(BF16))[0:1]
             for u, (b, h) in enumerate(units)]
    for u, (b, h) in enumerate(units):
        s_out_ref[b, h] = S[u] * math.exp(_log_gamma(h)) + _dot_tn(_pad8(ks[u]).astype(BF16), _pad8(v16[u]))
    for u, (b, h) in enumerate(units):
        qk = jnp.sum(qs[u].astype(BF16).astype(F32) * ks[u].astype(BF16).astype(F32), axis=-1, keepdims=True)
        inner = qk.astype(BF16).astype(F32) * v16[u].astype(F32)
        o = _head_layernorm(inner + cross[u]) * gn_ref[:, h * dv:(h + 1) * dv]
        o_ref[b, :, h * dv:(h + 1) * dv] = jax.nn.silu(gate_ref[b, :, h * dv:(h + 1) * dv]) * o


def _ret_sample(q, k, v, gate, freqs, gnorm, S0, pos, nh, dk, dv):
    Bn = q.shape[0]
    nb = 2 if Bn % 2 == 0 else 1
    per_b3 = lambda b: (b, 0, 0)
    per_b4 = lambda b: (b, 0, 0, 0)
    kernel = functools.partial(_ret_sample_body, nh=nh, dk=dk, dv=dv, pos=float(pos))
    return pl.pallas_call(
        kernel,
        out_shape=[jax.ShapeDtypeStruct((Bn, 1, nh * dv), F32), jax.ShapeDtypeStruct((Bn, nh, dk, dv), F32)],
        grid=(Bn // nb,),
        in_specs=[
            pl.BlockSpec((nb, 1, nh * dk), per_b3),
            pl.BlockSpec((nb, 1, nh * dk), per_b3),
            pl.BlockSpec((nb, 1, nh * dv), per_b3),
            pl.BlockSpec((nb, 1, nh * dv), per_b3),
            pl.BlockSpec((1, dk // 2), lambda b: (0, 0)),
            pl.BlockSpec((1, nh * dv), lambda b: (0, 0)),
            pl.BlockSpec((nb, nh, dk, dv), per_b4),
        ],
        out_specs=[
            pl.BlockSpec((nb, 1, nh * dv), per_b3),
            pl.BlockSpec((nb, nh, dk, dv), per_b4),
        ],
        compiler_params=_cparams(1),
        name="ret_sample",
    )(q, k, v, gate, freqs, gnorm.reshape(1, nh * dv), S0)


def kernel(x_prompt, x_sample, cache_fox_k, cache_fox_v, cache_fox_logf, page_table, state_gdn_conv, state_gdn_S, state_ret_S, norm_g, final_norm_g, ffn_w_gu, ffn_w_down, ab_w_in, ab_w_out, fox_b_f, gdn_conv_w, gdn_A_log, gdn_dt_bias, gdn_norm_g, c_w_in, c_w_out, ret_norm_g):
    B, L, D = x_prompt.shape
    Bn, Ls, _ = x_sample.shape
    assert Ls == 1, "the sample group decodes one token per sequence"
    depth = norm_g.shape[0]
    _, n_pool, page, fh, fd = cache_fox_k.shape
    fw = fh * fd
    _, _, gh, gdk, gdv = state_gdn_S.shape
    conv_dim = gdn_conv_w.shape[2]
    kw = gdn_conv_w.shape[1]
    _, _, rh, rdk, rdv = state_ret_S.shape
    n_pages = page_table.shape[1]
    assert fh == gh and 3 * fh <= 16 and fh & (fh - 1) == 0

    xp = x_prompt.reshape(B * L, D)
    xs = x_sample.reshape(Bn, D)
    w_gu16 = _to_bf16(ffn_w_gu)
    w_down16 = _to_bf16(ffn_w_down)
    freqs = (ROPE_BASE ** (-jnp.arange(rdk // 2, dtype=F32) / (rdk // 2))).reshape(1, rdk // 2)
    cos2, sin2 = _rope_table(L, freqs)

    fkp, fvp, flp, fks, fvs, fls = [], [], [], [], [], []
    gcp, gsp, gcs, gss = [], [], [], []
    rsp, rss = [], []
    for li in range(depth):
        xp = _ffn(xp, norm_g[li, 0], w_gu16, w_down16, li, 0)
        xs = _ffn(xs, norm_g[li, 0], w_gu16, w_down16, li, 0)
        j = li // 2
        if li % 2 == 0:
            o0 = 3 * fw
            zw = gh * gdv
            c0 = o0 + fh
            widths = (conv_dim, fw, fw, fw, zw, LANES)
            w_cat = _regroup_bf16(ab_w_in[j].T, [(c0, conv_dim), (0, o0), (c0 + conv_dim, zw), (o0, fh),
                                                 (c0 + conv_dim + zw, 2 * gh)], sum(widths))
            w_out16 = ab_w_out[j].astype(BF16)

            gate_p = _gate_params(fox_b_f[j], gdn_dt_bias[j], gdn_A_log[j])
            gqkv, q, k, v, z, gates, k3, v3, tails = _ab_proj_prompt(
                xp, norm_g[li, 1], w_cat, gdn_conv_w[j], gate_p, B, L, widths, fh, fd, gh, gdk)
            gates_row = gates[:, :16].T
            c_rows = _cumsum_rows(gates_row, L)[:fh].reshape(fh, B, L).transpose(1, 0, 2).reshape(B * fh, 1, L)
            fo = _fox_prompt(q, k, v, c_rows, B, L, fh)
            go, S_p = _gdn_prompt(gqkv, z, gates, gates_row, gdn_norm_g[j], B, L, gh, gdk, gdv)
            pre_p = ([fo, go], w_out16)
            fkp.append(k3.reshape(B, L, fh, fd))
            fvp.append(v3.reshape(B, L, fh, fd))
            flp.append(gates[:, :fh].reshape(B, L, fh))
            gcp.append(tails[:, 8 - (kw - 1):])
            gsp.append(S_p)

            cin, q, k, v, z, small = _rms_proj(xs, norm_g[li, 1], w_cat, widths)
            gates = _gates(small, gate_p, fh)
            sums = _page_sums(jnp.transpose(cache_fox_logf[j], (0, 2, 1)))
            fo = _fox_sample(q.reshape(Bn, fh, fd), k.reshape(Bn, fh, fd), v.reshape(Bn, fh, fd),
                             gates[:, :fh].reshape(Bn, fh, 1),
                             cache_fox_k[j].reshape(n_pool, page * fh, fd), cache_fox_v[j].reshape(n_pool, page * fh, fd),
                             sums, page_table, fh)
            go, conv_s, S_s = _gdn_sample(cin.reshape(Bn, 1, conv_dim), state_gdn_conv[j], z.reshape(Bn, 1, gh * gdv),
                                          gates.reshape(Bn, 1, LANES), gdn_conv_w[j], gdn_norm_g[j], state_gdn_S[j],
                                          gh, gdk, gdv)
            pre_s = ([fo.reshape(Bn, fw), go.reshape(Bn, gh * gdv)], w_out16)
            fks.append(k.reshape(Bn, 1, fh, fd))
            fvs.append(v.reshape(Bn, 1, fh, fd))
            fls.append(gates[:, :fh].reshape(Bn, 1, fh))
            gcs.append(conv_s)
            gss.append(S_s)
        else:
            w_in16 = _to_bf16(c_w_in[j])
            qk_w, v_w = rh * rdk, rh * rdv
            widths = (qk_w, qk_w, v_w, v_w)
            w_out16 = c_w_out[j].astype(BF16)

            q16, k16, v, sgate = _c_proj_prompt(xp, norm_g[li, 1], w_in16, cos2, sin2, L, rh, rdk, v_w)
            y, R_p = _ret_prompt(q16, k16, v, sgate, ret_norm_g[j], B, L, rh, rdk, rdv)
            pre_p = ([y], w_out16)
            rsp.append(R_p)

            q, k, v, gate = _rms_proj(xs, norm_g[li, 1], w_in16, widths)
            y, R_s = _ret_sample(q.reshape(Bn, 1, qk_w), k.reshape(Bn, 1, qk_w), v.reshape(Bn, 1, v_w),
                                 gate.reshape(Bn, 1, v_w), freqs, ret_norm_g[j], state_ret_S[j],
                                 n_pages * page, rh, rdk, rdv)
            pre_s = ([y.reshape(Bn, v_w)], w_out16)
            rss.append(R_s)
        final_g = final_norm_g if li == depth - 1 else None
        xp = _ffn(xp, norm_g[li, 2], w_gu16, w_down16, li, 1, final_g, pre_p)
        xs = _ffn(xs, norm_g[li, 2], w_gu16, w_down16, li, 1, final_g, pre_s)
    y_prompt = xp.reshape(B, L, D)
    y_sample = xs.reshape(Bn, 1, D)
    return (y_prompt, y_sample,
            jnp.stack(fkp), jnp.stack(fvp), jnp.stack(flp),
            jnp.stack(fks), jnp.stack(fvs), jnp.stack(fls),
            jnp.stack(gcp), jnp.stack(gsp), jnp.stack(gcs), jnp.stack(gss),
            jnp.stack(rsp), jnp.stack(rss))
```

```python
import functools
import math

import jax
import jax.numpy as jnp
from jax import lax
from jax.experimental import pallas as pl
from jax.experimental.pallas import tpu as pltpu

F32 = jnp.float32
BF16 = jnp.bfloat16
EPS = 1e-6
ROPE_BASE = 10000.0
NEG = -1e30
LANES = 128
GDN_CHUNK = 64
VMEM_LIMIT = 56 * 1024 * 1024
HI = lax.Precision.HIGHEST


def _cparams(n_axes):
    return pltpu.CompilerParams(dimension_semantics=("arbitrary",) * n_axes,
                                vmem_limit_bytes=VMEM_LIMIT)


def _dot(a, b, precision=None):
    return jnp.dot(a, b, preferred_element_type=F32, precision=precision)


def _dot_nt(a, b, precision=None):
    return lax.dot_general(a, b, (((1,), (1,)), ((), ())), preferred_element_type=F32, precision=precision)


def _dot_tn(a, b, precision=None):
    return lax.dot_general(a, b, (((0,), (0,)), ((), ())), preferred_element_type=F32, precision=precision)


def _rms_rows(x, g):
    return x * lax.rsqrt(jnp.mean(x * x, axis=-1, keepdims=True) + EPS) * g


MXU_DEPTH = 256


def _ffn_body(*refs, F, chunk, final_norm, n_pre):
    x_ref, g_ref, wgu_ref, wd_ref = refs[:4]
    a_refs = refs[4:4 + n_pre]
    rest = refs[4 + n_pre:]
    x = x_ref[...]
    if n_pre:
        wo_ref, rest = rest[0], rest[1:]
        row = 0
        for a_ref in a_refs:
            n = a_ref.shape[1]
            x = x + _dot(a_ref[...].astype(BF16), wo_ref[row:row + n, :])
            row += n
    if final_norm:
        gf_ref, rest = rest[0], rest[1:]
    o_ref = rest[0]
    xn = _rms_rows(x, g_ref[...]).astype(BF16)
    acc = None
    for c0 in range(0, F, chunk):
        c1 = min(F, c0 + chunk)
        a = _dot(xn, wgu_ref[:, c0:c1])
        b = _dot(xn, wgu_ref[:, F + c0:F + c1])
        h = (jax.nn.silu(a) * b).astype(BF16)
        d = _dot(h, wd_ref[c0:c1, :])
        acc = d if acc is None else acc + d
    y = x + 0.5 * acc
    o_ref[...] = _rms_rows(y, gf_ref[...]) if final_norm else y


def _ffn(x, g, w_gu, w_down, li, j, final_g=None, pre=None):
    M, D = x.shape
    F = w_down.shape[2]
    tm = min(M, 512)
    final_norm = final_g is not None
    const = lambda m: (0, 0)
    in_specs = [pl.BlockSpec((tm, D), lambda m: (m, 0)), pl.BlockSpec((1, D), const),
                pl.BlockSpec((None, None, D, 2 * F), lambda m: (li, j, 0, 0)),
                pl.BlockSpec((None, None, F, D), lambda m: (li, j, 0, 0))]
    args = [x, g.reshape(1, D), w_gu, w_down]
    acts, w_out = pre if pre is not None else ((), None)
    if acts:
        assert sum(a.shape[1] for a in acts) == w_out.shape[0]
        in_specs += [pl.BlockSpec((tm, a.shape[1]), lambda m: (m, 0)) for a in acts]
        in_specs.append(pl.BlockSpec(w_out.shape, const))
        args += [*acts, w_out]
    if final_norm:
        in_specs.append(pl.BlockSpec((1, D), const))
        args.append(final_g.reshape(1, D))
    return pl.pallas_call(
        functools.partial(_ffn_body, F=F, chunk=2 * MXU_DEPTH, final_norm=final_norm, n_pre=len(acts)),
        out_shape=jax.ShapeDtypeStruct((M, D), F32),
        grid=(M // tm,),
        in_specs=in_specs,
        out_specs=pl.BlockSpec((tm, D), lambda m: (m, 0)),
        compiler_params=_cparams(1),
        name="ffn",
    )(*args)


PROJ_CHUNK = 512


def _project(xn, w_ref, col, o_ref, o3_ref=None):
    n = o_ref.shape[1]
    for s0 in range(0, n, PROJ_CHUNK):
        s1 = min(n, s0 + PROJ_CHUNK)
        val = _dot(xn, w_ref[:, col + s0:col + s1])
        o_ref[:, s0:s1] = val
        if o3_ref is not None:
            dh = o3_ref.shape[2]
            nhh = (s1 - s0) // dh
            o3_ref[:, s0 // dh:s0 // dh + nhh, :] = val.reshape(val.shape[0], nhh, dh)
    return col + n


def _proj_body(x_ref, g_ref, w_ref, *o_refs):
    xn = _rms_rows(x_ref[...], g_ref[...]).astype(BF16)
    col = 0
    for o_ref in o_refs:
        col = _project(xn, w_ref, col, o_ref)


def _rms_proj(x, g, w, widths):
    M, D = x.shape
    tm = min(M, 256)
    assert sum(widths) == w.shape[1] and all(n % LANES == 0 for n in widths)
    return pl.pallas_call(
        _proj_body,
        out_shape=[jax.ShapeDtypeStruct((M, n), F32) for n in widths],
        grid=(M // tm,),
        in_specs=[pl.BlockSpec((tm, D), lambda m: (m, 0)), pl.BlockSpec((1, D), lambda m: (0, 0)),
                  pl.BlockSpec(w.shape, lambda m: (0, 0))],
        out_specs=[pl.BlockSpec((tm, n), lambda m: (m, 0)) for n in widths],
        compiler_params=_cparams(1),
        name="rms_proj",
    )(x, g.reshape(1, D), w)


def _l2_rows(x):
    return x * lax.rsqrt(jnp.sum(x * x, axis=-1, keepdims=True) + EPS)


def _ab_proj_body(x_ref, g_ref, w_ref, cw_ref, gp_ref, qkv_o, q_o, k_o, v_o, z_o, gates_o, k3_o, v3_o, tail_o,
                  win_ref, *, tiles_per_seq, nh, dk):
    m = pl.program_id(0)
    tm = x_ref.shape[0]
    cd = qkv_o.shape[1]
    xn = _rms_rows(x_ref[...], g_ref[...]).astype(BF16)

    @pl.when(lax.rem(m, tiles_per_seq) == 0)
    def _():
        win_ref[0:8, :] = jnp.zeros((8, cd), F32)

    kw = cw_ref.shape[0]
    step = nh * dk

    def conv_product(s0):
        win_ref[8:8 + tm, s0:s0 + step] = _dot(xn, w_ref[:, s0:s0 + step])
        tail_o[:, s0:s0 + step] = win_ref[tm:tm + 8, s0:s0 + step]

    def conv_finish(s0):
        s1 = s0 + step
        conv = win_ref[8 - kw + 1:8 - kw + 1 + tm, s0:s1] * cw_ref[0:1, s0:s1]
        for i in range(1, kw):
            conv = conv + win_ref[8 - kw + 1 + i:8 - kw + 1 + i + tm, s0:s1] * cw_ref[i:i + 1, s0:s1]
        win_ref[0:8, s0:s1] = win_ref[tm:tm + 8, s0:s1]
        conv = jax.nn.silu(conv)
        if s0 >= 2 * step:
            qkv_o[:, s0:s1] = conv
        else:
            for h in range(nh):
                xh = _l2_rows(conv[:, h * dk:(h + 1) * dk])
                qkv_o[:, s0 + h * dk:s0 + (h + 1) * dk] = xh * dk ** -0.5 if s0 == 0 else xh

    assert cd == 3 * step
    conv_product(0)
    conv_product(step)
    col = _project(xn, w_ref, cd, q_o)
    conv_finish(0)
    conv_product(2 * step)
    col = _project(xn, w_ref, col, k_o, k3_o)
    conv_finish(step)
    col = _project(xn, w_ref, col, v_o, v3_o)
    col = _project(xn, w_ref, col, z_o)
    conv_finish(2 * step)
    gates_o[...] = _gate_values(_dot(xn, w_ref[:, col:col + gates_o.shape[1]]), gp_ref, nh)


def _ab_proj_prompt(x, g, w, conv_w, gate_p, B, L, widths, fh, fd, gh, gdk):
    M, D = x.shape
    tm = min(L, 512)
    cd = widths[0]
    assert sum(widths) == w.shape[1] and L % tm == 0
    row = lambda m: (m, 0)
    const = lambda m: (0, 0)
    out_shape = [jax.ShapeDtypeStruct((M, n), F32) for n in widths]
    out_specs = [pl.BlockSpec((tm, n), row) for n in widths]
    for _ in range(2):
        out_shape.append(jax.ShapeDtypeStruct((M, fh, fd), F32))
        out_specs.append(pl.BlockSpec((tm, fh, fd), lambda m: (m, 0, 0)))
    out_shape.append(jax.ShapeDtypeStruct((B, 8, cd), F32))
    out_specs.append(pl.BlockSpec((None, 8, cd), lambda m: (m // (L // tm), 0, 0)))
    return pl.pallas_call(
        functools.partial(_ab_proj_body, tiles_per_seq=L // tm, nh=gh, dk=gdk),
        out_shape=out_shape,
        grid=(M // tm,),
        in_specs=[pl.BlockSpec((tm, D), row), pl.BlockSpec((1, D), const), pl.BlockSpec(w.shape, const),
                  pl.BlockSpec(conv_w.shape, const), pl.BlockSpec(gate_p.shape, const)],
        out_specs=out_specs,
        scratch_shapes=[pltpu.VMEM((8 + tm, cd), F32)],
        compiler_params=_cparams(1),
        name="ab_proj_prompt",
    )(x, g.reshape(1, D), w, conv_w, gate_p)


def _c_proj_body(x_ref, g_ref, w_ref, cos_ref, sin_ref, q_o, k_o, v_o, sg_o, *, nh, dk):
    xn = _rms_rows(x_ref[...], g_ref[...]).astype(BF16)
    cos2, sin2 = cos_ref[...], sin_ref[...]
    hpc = PROJ_CHUNK // dk
    for idx, o_ref in enumerate((q_o, k_o)):
        for h0 in range(0, nh, hpc):
            c0 = idx * nh * dk + h0 * dk
            val = _dot(xn, w_ref[:, c0:c0 + hpc * dk])
            for hh in range(hpc):
                xh = val[:, hh * dk:(hh + 1) * dk]
                xh = xh * cos2 + pltpu.roll(xh, dk // 2, 1) * sin2
                if idx == 1:
                    xh = xh * dk ** -0.5
                o_ref[:, (h0 + hh) * dk:(h0 + hh + 1) * dk] = xh.astype(BF16)
    col = _project(xn, w_ref, 2 * nh * dk, v_o)
    n = sg_o.shape[1]
    for s0 in range(0, n, PROJ_CHUNK):
        s1 = min(n, s0 + PROJ_CHUNK)
        sg_o[:, s0:s1] = jax.nn.silu(_dot(xn, w_ref[:, col + s0:col + s1]))


def _c_proj_prompt(x, g, w, cos2, sin2, L, nh, dk, v_w):
    M, D = x.shape
    tm = min(L, 512)
    qk_w = nh * dk
    row = lambda m: (m, 0)
    const = lambda m: (0, 0)
    pos = lambda m: (lax.rem(m, L // tm), 0)
    return pl.pallas_call(
        functools.partial(_c_proj_body, nh=nh, dk=dk),
        out_shape=[jax.ShapeDtypeStruct((M, qk_w), BF16), jax.ShapeDtypeStruct((M, qk_w), BF16),
                   jax.ShapeDtypeStruct((M, v_w), F32), jax.ShapeDtypeStruct((M, v_w), F32)],
        grid=(M // tm,),
        in_specs=[pl.BlockSpec((tm, D), row), pl.BlockSpec((1, D), const), pl.BlockSpec(w.shape, const),
                  pl.BlockSpec((tm, dk), pos), pl.BlockSpec((tm, dk), pos)],
        out_specs=[pl.BlockSpec((tm, qk_w), row), pl.BlockSpec((tm, qk_w), row),
                   pl.BlockSpec((tm, v_w), row), pl.BlockSpec((tm, v_w), row)],
        compiler_params=_cparams(1),
        name="c_proj_prompt",
    )(x, g.reshape(1, D), w, cos2, sin2)


def _cast_body(x_ref, o_ref):
    o_ref[...] = x_ref[...].astype(o_ref.dtype)


def _to_bf16(w):
    cols = w.shape[-1]
    rows = w.size // cols
    tr = 512 if rows % 512 == 0 else rows
    out = pl.pallas_call(
        _cast_body,
        out_shape=jax.ShapeDtypeStruct((rows, cols), BF16),
        grid=(rows // tr,),
        in_specs=[pl.BlockSpec((tr, cols), lambda r: (r, 0))],
        out_specs=pl.BlockSpec((tr, cols), lambda r: (r, 0)),
        compiler_params=_cparams(1),
        name="to_bf16",
    )(w.reshape(rows, cols))
    return out.reshape(w.shape)


def _regroup_body(wt_ref, o_ref, *, pieces):
    N, D = wt_ref.shape
    col = 0
    narrow = []
    for start, width in pieces:
        if width % LANES == 0:
            assert not narrow and col % LANES == 0
            for r0 in range(0, width, PROJ_CHUNK):
                r1 = min(width, r0 + PROJ_CHUNK)
                o_ref[:, col + r0:col + r1] = wt_ref[start + r0:start + r1, :].T.astype(o_ref.dtype)
        else:
            w0 = min(start, N - LANES)
            narrow.append(wt_ref[w0:w0 + LANES, :].T[:, start - w0:start - w0 + width])
        col += width
    if narrow:
        first = col - sum(p.shape[1] for p in narrow)
        assert first % LANES == 0 and o_ref.shape[1] - first == LANES
        pad = jnp.zeros((D, o_ref.shape[1] - col), F32)
        o_ref[:, first:] = jnp.concatenate(narrow + [pad], axis=1).astype(o_ref.dtype)


def _regroup_bf16(wt, pieces, n_out):
    N, D = wt.shape
    return pl.pallas_call(
        functools.partial(_regroup_body, pieces=tuple(pieces)),
        out_shape=jax.ShapeDtypeStruct((D, n_out), BF16),
        grid=(1,),
        in_specs=[pl.BlockSpec((N, D), lambda r: (0, 0))],
        out_specs=pl.BlockSpec((D, n_out), lambda r: (0, 0)),
        compiler_params=_cparams(1),
        name="regroup_bf16",
    )(wt)


def _gate_values(s, p_ref, nh):
    s = s + p_ref[0:1, :]
    lane = lax.broadcasted_iota(jnp.int32, s.shape, 1)
    logf = jax.nn.log_sigmoid(s)
    g = -jnp.exp(p_ref[1:2, :]) * jax.nn.softplus(s)
    beta = jax.nn.sigmoid(s)
    return jnp.where(lane < nh, logf, jnp.where(lane < 2 * nh, g, jnp.where(lane < 3 * nh, beta, 0.0)))


def _gate_params(b_f, dt_bias, a_log):
    nh = b_f.shape[0]
    pad = jnp.zeros((LANES - 3 * nh,), F32)
    p = jnp.zeros((8, LANES), F32)
    p = p.at[0].set(jnp.concatenate([b_f, dt_bias, jnp.zeros((nh,), F32), pad]))
    return p.at[1].set(jnp.concatenate([jnp.zeros((nh,), F32), a_log, jnp.zeros((nh,), F32), pad]))


def _gates_body(s_ref, p_ref, o_ref, *, nh):
    o_ref[...] = _gate_values(s_ref[...], p_ref, nh)


def _gates(small, gate_p, nh):
    M = small.shape[0]
    tm = min(M, 2048)
    return pl.pallas_call(
        functools.partial(_gates_body, nh=nh),
        out_shape=jax.ShapeDtypeStruct((M, LANES), F32),
        grid=(M // tm,),
        in_specs=[pl.BlockSpec((tm, LANES), lambda m: (m, 0)), pl.BlockSpec((8, LANES), lambda m: (0, 0))],
        out_specs=pl.BlockSpec((tm, LANES), lambda m: (m, 0)),
        compiler_params=_cparams(1),
        name="gates",
    )(small, gate_p)


def _cumsum_body(x_ref, o_ref):
    L = x_ref.shape[1]
    r = lax.broadcasted_iota(jnp.int32, (LANES, LANES), 0)
    c = lax.broadcasted_iota(jnp.int32, (LANES, LANES), 1)
    upper = (r <= c).astype(F32)
    carry = jnp.zeros((x_ref.shape[0], 1), F32)
    for j in range(L // LANES):
        cs = _dot(x_ref[:, j * LANES:(j + 1) * LANES], upper, HI) + carry
        o_ref[:, j * LANES:(j + 1) * LANES] = cs
        carry = cs[:, LANES - 1:LANES]


def _cumsum_rows(x, seg):
    R, M = x.shape
    return pl.pallas_call(
        _cumsum_body,
        out_shape=jax.ShapeDtypeStruct((R, M), F32),
        grid=(M // seg,),
        in_specs=[pl.BlockSpec((R, seg), lambda b: (0, b))],
        out_specs=pl.BlockSpec((R, seg), lambda b: (0, b)),
        compiler_params=_cparams(1),
        name="cumsum_rows",
    )(x)


def _fox_body(q_ref, k_ref, v_ref, *rest, t, hg, dh, scale):
    nbg = q_ref.shape[0]
    c_refs = rest[:nbg]
    o_ref, vt_ref, cb_ref, acc_ref, m_ref = rest[nbg:]
    i = pl.program_id(2)
    L = k_ref.shape[1]
    log2e = math.log2(math.e)
    units = [(s, g) for s in range(nbg) for g in range(hg)]

    @pl.when(i == 0)
    def _():
        for u, (s, g) in enumerate(units):
            for jj in range(L // t):
                vt_ref[u, 0:dh, jj * t:(jj + 1) * t] = v_ref[s, jj * t:(jj + 1) * t, g * dh:(g + 1) * dh].T.astype(BF16)
            vt_ref[u, dh:, :] = jnp.ones((vt_ref.shape[1] - dh, L), BF16)
            for jj in range(L // LANES):
                row = c_refs[s][g, :, jj * LANES:(jj + 1) * LANES] * log2e
                cb_ref[u, jj * LANES:(jj + 1) * LANES, :] = jnp.broadcast_to(row, (LANES, LANES)).T

    q0 = pl.multiple_of(i * t, t)
    qs = [q_ref[s, :, g * dh:(g + 1) * dh].astype(BF16) for s, g in units]
    r2 = [c_refs[s][g, :, pl.ds(q0, t)][:, t - 1:t] * log2e for s, g in units]
    m_ref[...] = jnp.full_like(m_ref, NEG)
    acc_ref[...] = jnp.zeros_like(acc_ref)

    def step(j, masked):
        k0 = pl.multiple_of(j * t, t)
        qk = [_dot_nt(k_ref[s, pl.ds(k0, t), g * dh:(g + 1) * dh].astype(BF16), qs[u])
              for u, (s, g) in enumerate(units)]
        ps, alphas = [], []
        for u in range(len(units)):
            bias = r2[u] - cb_ref[u, pl.ds(k0, t), :]
            st = qk[u] * (scale * log2e) + jnp.concatenate([bias] * (t // LANES), axis=1)
            if masked:
                key = lax.broadcasted_iota(jnp.int32, (t, t), 0)
                qry = lax.broadcasted_iota(jnp.int32, (t, t), 1)
                st = jnp.where(key <= qry, st, NEG)
            m_prev = m_ref[u]
            m_new = jnp.maximum(m_prev, jnp.max(st, axis=0, keepdims=True))
            ps.append(jnp.exp2(st - m_new).astype(BF16))
            alphas.append(jnp.exp2(m_prev - m_new))
            m_ref[u] = m_new
        for u in range(len(units)):
            acc_ref[u] = acc_ref[u] * alphas[u] + _dot(vt_ref[u, :, pl.ds(k0, t)], ps[u])

    def loop_body(j, carry):
        step(j, False)
        return carry

    lax.fori_loop(0, i, loop_body, 0)
    step(i, True)
    for u, (s, g) in enumerate(units):
        acc = acc_ref[u]
        o_ref[s, :, g * dh:(g + 1) * dh] = (acc[0:dh] / acc[dh:dh + 1]).T


def _fox_prompt(q, k, v, c_rows, B, L, H):
    M, W = q.shape
    Dh = W // H
    t = min(L, 512)
    nq = L // t
    hg = 2 if H % 2 == 0 else 1
    nbg = 2 if B % 2 == 0 else 1
    nu = nbg * hg
    kernel = functools.partial(_fox_body, t=t, hg=hg, dh=Dh, scale=Dh ** -0.5)

    def c_map(s):
        return lambda p, h, i: ((p * nbg + s) * (H // hg) + h, 0, 0)

    o = pl.pallas_call(
        kernel,
        out_shape=jax.ShapeDtypeStruct((B, L, W), F32),
        grid=(B // nbg, H // hg, nq),
        in_specs=[
            pl.BlockSpec((nbg, t, hg * Dh), lambda p, h, i: (p, i, h)),
            pl.BlockSpec((nbg, L, hg * Dh), lambda p, h, i: (p, 0, h)),
            pl.BlockSpec((nbg, L, hg * Dh), lambda p, h, i: (p, 0, h)),
        ] + [pl.BlockSpec((hg, 1, L), c_map(s)) for s in range(nbg)],
        out_specs=pl.BlockSpec((nbg, t, hg * Dh), lambda p, h, i: (p, i, h)),
        scratch_shapes=[pltpu.VMEM((nu, Dh + 16, L), BF16), pltpu.VMEM((nu, L, LANES), F32),
                        pltpu.VMEM((nu, Dh + 16, t), F32), pltpu.VMEM((nu, 1, t), F32)],
        compiler_params=_cparams(3),
        name="fox_prompt",
    )(q.reshape(B, L, W), k.reshape(B, L, W), v.reshape(B, L, W), *([c_rows] * nbg))
    return o.reshape(M, W)


def _split2(x):
    hi = x.astype(BF16)
    return hi, (x - hi.astype(F32)).astype(BF16)


def _split3(x):
    h1 = x.astype(BF16)
    r1 = x - h1.astype(F32)
    h2 = r1.astype(BF16)
    return h1, h2, (r1 - h2.astype(F32)).astype(BF16)


def _dot3(a, b):
    (ah, al), (bh, bl) = a, b
    return _dot(ah, bh) + (_dot(ah, bl) + _dot(al, bh))


def _unit_lower_inverses(a_list, n):
    ri = lax.broadcasted_iota(jnp.int32, (n, n), 0)
    ci = lax.broadcasted_iota(jnp.int32, (n, n), 1)
    eye = (ri == ci).astype(F32)
    nm = len(a_list)
    diag16 = (ri >> 4) == (ci >> 4)
    pw = [jnp.where(diag16, a, 0.0) for a in a_list]
    t = [eye - p for p in pw]
    for _ in range(3):
        pw16 = [p.astype(BF16) for p in pw]
        pw = [_dot(p, p) for p in pw16]
        pn16 = [p.astype(BF16) for p in pw]
        t = [t[m] + _dot(t[m].astype(BF16), pn16[m]) for m in range(nm)]
    size = 16
    while size < GDN_CHUNK:
        sh = size.bit_length() - 1
        off = ((ri >> (sh + 1)) == (ci >> (sh + 1))) & ((ri >> sh) == (ci >> sh) + 1)
        t16 = [x.astype(BF16) for x in t]
        mid = [_dot(jnp.where(off, a_list[m], 0.0).astype(BF16), t16[m]) for m in range(nm)]
        t = [t[m] - _dot(t16[m], mid[m].astype(BF16)) for m in range(nm)]
        size *= 2
    ts = [_split2(x) for x in t]
    res = [(eye - t[m]) - _dot3(_split2(a_list[m]), ts[m]) for m in range(nm)]
    return [t[m] + _dot(ts[m][0], res[m].astype(BF16)) for m in range(nm)]


def _gdn_body(qkv_ref, z_ref, gc_ref, *rest, nh, dk, dv, nseq):
    gr_refs = rest[:nseq]
    ng_ref, o_ref, s_out_ref, s_ref, m16_ref = rest[nseq:]
    j = pl.program_id(1)
    T = qkv_ref.shape[1]
    C = GDN_CHUNK
    nc = T // C

    ri = lax.broadcasted_iota(jnp.int32, (T, T), 0)
    ci = lax.broadcasted_iota(jnp.int32, (T, T), 1)
    csh = C.bit_length() - 1
    same = (ri >> csh) == (ci >> csh)
    tril = same & (ri >= ci)
    strict = same & (ri > ci)

    @pl.when((pl.program_id(0) == 0) & (j == 0))
    def _():
        m16_ref[0] = tril.astype(BF16)
        m16_ref[1] = (same & (ri <= ci)).astype(BF16)

    @pl.when(j == 0)
    def _():
        s_ref[...] = jnp.zeros_like(s_ref)

    rowi = lax.broadcasted_iota(jnp.int32, (T, 1), 0)
    units = [(s, h) for s in range(nseq) for h in range(nh)]
    nu = len(units)
    gcols, grows = [], []
    for s in range(nseq):
        gcols.append(sum(_dot(m16_ref[0], part) for part in _split3(gc_ref[s])))
        grows.append(sum(_dot(part, m16_ref[1]) for part in _split3(gr_refs[s][...])))

    q, k, beta, gcc, gcr, gam, kb, k16 = [], [], [], [], [], [], [], []
    for u, (s, h) in enumerate(units):
        q.append(qkv_ref[s, :, h * dk:(h + 1) * dk])
        k.append(qkv_ref[s, :, nh * dk + h * dk:nh * dk + (h + 1) * dk])
        beta.append(gc_ref[s, :, 2 * nh + h:2 * nh + h + 1])
        gcc.append(gcols[s][:, nh + h:nh + h + 1])
        gcr.append(grows[s][nh + h:nh + h + 1, :])
        gam.append(jnp.exp(jnp.where(tril, gcc[u] - gcr[u], NEG)))
        kb.append(k[u] * beta[u])
        k16.append(k[u].astype(BF16))
    a = [jnp.where(strict, _dot_nt(kb[u].astype(BF16), k16[u]) * gam[u], 0.0) for u in range(nu)]
    tinv = _unit_lower_inverses(a, T)
    x = []
    for u, (s, h) in enumerate(units):
        v = qkv_ref[s, :, 2 * nh * dk + h * dv:2 * nh * dk + (h + 1) * dv]
        rhs = jnp.concatenate([v * beta[u], kb[u] * jnp.exp(gcc[u])], axis=-1)
        x.append(_dot3(_split2(tinv[u]), _split2(rhs)))
    attn, qg, kd, egl = [], [], [], []
    for u in range(nu):
        attn.append((_dot_nt(q[u].astype(BF16), k16[u]) * gam[u]).astype(BF16))
        qg.append((q[u] * jnp.exp(gcc[u])).astype(BF16))
        gl = gcr[u][:, C - 1:C]
        for cidx in range(1, nc):
            gl = jnp.where(rowi >= cidx * C, gcr[u][:, (cidx + 1) * C - 1:(cidx + 1) * C], gl)
        kd.append((k[u] * jnp.exp(gl - gcc[u])).astype(BF16))
        egl.append(jnp.exp(gl))
    S = [s_ref[s, h] for s, h in units]
    outs = [[] for _ in units]
    for cidx in range(nc):
        sl = slice(cidx * C, (cidx + 1) * C)
        S16 = [st.astype(BF16) for st in S]
        vn16 = [(x[u][sl, :dv] - _dot(x[u][sl, dv:].astype(BF16), S16[u])).astype(BF16) for u in range(nu)]
        for u in range(nu):
            outs[u].append(_dot(qg[u][sl], S16[u]) + _dot(attn[u][sl, cidx * C:(cidx + 1) * C], vn16[u]))
        S = [S[u] * egl[u][cidx * C:cidx * C + 1] + _dot_tn(kd[u][sl], vn16[u]) for u in range(nu)]
    for u, (s, h) in enumerate(units):
        s_ref[s, h] = S[u]
        o = _rms_rows(jnp.concatenate(outs[u], axis=0), ng_ref[...])
        o_ref[s, :, h * dv:(h + 1) * dv] = o * jax.nn.silu(z_ref[s, :, h * dv:(h + 1) * dv])

    @pl.when(j == pl.num_programs(1) - 1)
    def _():
        s_out_ref[...] = s_ref[...]


def _gdn_prompt(qkv, z, gates_col, gates_row, norm_g, B, L, nh, dk, dv):
    M, CD = qkv.shape
    T = 2 * GDN_CHUNK
    nb = L // T
    nseq = 4 if B % 4 == 0 else (2 if B % 2 == 0 else 1)
    kernel = functools.partial(_gdn_body, nh=nh, dk=dk, dv=dv, nseq=nseq)

    def row_map(s):
        return lambda p, j: (0, (p * nseq + s) * nb + j)

    per_seq = lambda p, j: (p, j, 0)
    go, S = pl.pallas_call(
        kernel,
        out_shape=[jax.ShapeDtypeStruct((B, L, nh * dv), F32), jax.ShapeDtypeStruct((B, nh, dk, dv), F32)],
        grid=(B // nseq, nb),
        in_specs=[
            pl.BlockSpec((nseq, T, CD), per_seq),
            pl.BlockSpec((nseq, T, nh * dv), per_seq),
            pl.BlockSpec((nseq, T, LANES), per_seq),
        ] + [pl.BlockSpec((16, T), row_map(s)) for s in range(nseq)] + [
            pl.BlockSpec((1, dv), lambda p, j: (0, 0)),
        ],
        out_specs=[
            pl.BlockSpec((nseq, T, nh * dv), per_seq),
            pl.BlockSpec((nseq, nh, dk, dv), lambda p, j: (p, 0, 0, 0)),
        ],
        scratch_shapes=[pltpu.VMEM((nseq, nh, dk, dv), F32), pltpu.VMEM((2, T, T), BF16)],
        compiler_params=_cparams(2),
        name="gdn_prompt",
    )(qkv.reshape(B, L, CD), z.reshape(B, L, nh * dv), gates_col.reshape(B, L, LANES),
      *([gates_row] * nseq), norm_g.reshape(1, dv))
    return go.reshape(M, nh * dv), S


def _rope(x, cos2, sin2, nh, dk):
    outs = []
    for h in range(nh):
        xh = x[:, h * dk:(h + 1) * dk]
        outs.append(xh * cos2 + pltpu.roll(xh, dk // 2, 1) * sin2)
    return outs


def _rope_tables(pos, freqs):
    ang = pos * freqs
    cos, sin = jnp.cos(ang), jnp.sin(ang)
    return jnp.concatenate([cos, cos], axis=-1), jnp.concatenate([-sin, sin], axis=-1)


def _head_layernorm(o):
    mu = jnp.mean(o, axis=-1, keepdims=True)
    oc = o - mu
    return oc * lax.rsqrt(jnp.mean(oc * oc, axis=-1, keepdims=True) + EPS)


def _log_gamma(h):
    return math.log(1.0 - 2.0 ** (-5.0 - h))


def _rope_table_body(fr_ref, cos_o, sin_o):
    C = cos_o.shape[0]
    rowi = lax.broadcasted_iota(jnp.int32, (C, 1), 0)
    pos = (pl.program_id(0) * C + rowi).astype(F32)
    cos_o[...], sin_o[...] = _rope_tables(pos, fr_ref[...])


def _rope_table(L, freqs):
    C = min(L, 512)
    dk = 2 * freqs.shape[1]
    return pl.pallas_call(
        _rope_table_body,
        out_shape=[jax.ShapeDtypeStruct((L, dk), F32)] * 2,
        grid=(L // C,),
        in_specs=[pl.BlockSpec(freqs.shape, lambda c: (0, 0))],
        out_specs=[pl.BlockSpec((C, dk), lambda c: (c, 0))] * 2,
        compiler_params=_cparams(1),
        name="rope_table",
    )(freqs)


def _ret_body(q_ref, k_ref, v_ref, sg_ref, gn_ref, o_ref, s_out_ref, s_ref, dmat_ref, *, nh, dk, dv):
    b = pl.program_id(0)
    j = pl.program_id(1)
    nseq, C = q_ref.shape[0], q_ref.shape[1]
    rowf = lax.broadcasted_iota(jnp.int32, (C, 1), 0).astype(F32)

    @pl.when((b == 0) & (j == 0))
    def _():
        ri = lax.broadcasted_iota(jnp.int32, (C, C), 0)
        ci = lax.broadcasted_iota(jnp.int32, (C, C), 1)
        diff = (ri - ci).astype(F32)
        for h in range(nh):
            dmat_ref[h] = jnp.exp(jnp.where(ri >= ci, diff * _log_gamma(h), NEG))

    @pl.when(j == 0)
    def _():
        s_ref[...] = jnp.zeros_like(s_ref)

    units = [(s, h) for s in range(nseq) for h in range(nh)]
    nu = range(len(units))
    q16 = [q_ref[s, :, h * dk:(h + 1) * dk] for s, h in units]
    k16 = [k_ref[s, :, h * dk:(h + 1) * dk] for s, h in units]
    v = [v_ref[s, :, h * dv:(h + 1) * dv] for s, h in units]
    S = [s_ref[s, h] for s, h in units]
    qk = [_dot_nt(q16[u], k16[u]) for u in nu]
    cross = [_dot(q16[u], S[u].astype(BF16)) for u in nu]
    for u, (s, h) in enumerate(units):
        vd = (v[u] * jnp.exp(_log_gamma(h) * (C - 1.0 - rowf))).astype(BF16)
        s_ref[s, h] = S[u] * math.exp(_log_gamma(h) * C) + _dot_tn(k16[u], vd)
    inner = [_dot((qk[u] * dmat_ref[h]).astype(BF16), v[u].astype(BF16)) for u, (s, h) in enumerate(units)]
    for u, (s, h) in enumerate(units):
        o = inner[u] + cross[u] * jnp.exp(_log_gamma(h) * (rowf + 1.0))
        o = _head_layernorm(o) * gn_ref[:, h * dv:(h + 1) * dv]
        o_ref[s, :, h * dv:(h + 1) * dv] = sg_ref[s, :, h * dv:(h + 1) * dv] * o

    @pl.when(j == pl.num_programs(1) - 1)
    def _():
        s_out_ref[...] = s_ref[...]


def _ret_prompt(q16, k16, v, sgate, gnorm, B, L, nh, dk, dv):
    M = q16.shape[0]
    C = 128
    nb = L // C
    nseq = 4 if B % 4 == 0 else (2 if B % 2 == 0 else 1)
    kernel = functools.partial(_ret_body, nh=nh, dk=dk, dv=dv)
    tok = lambda p, j: (p, j, 0)
    y, S = pl.pallas_call(
        kernel,
        out_shape=[jax.ShapeDtypeStruct((B, L, nh * dv), F32), jax.ShapeDtypeStruct((B, nh, dk, dv), F32)],
        grid=(B // nseq, nb),
        in_specs=[
            pl.BlockSpec((nseq, C, nh * dk), tok),
            pl.BlockSpec((nseq, C, nh * dk), tok),
            pl.BlockSpec((nseq, C, nh * dv), tok),
            pl.BlockSpec((nseq, C, nh * dv), tok),
            pl.BlockSpec((1, nh * dv), lambda p, j: (0, 0)),
        ],
        out_specs=[
            pl.BlockSpec((nseq, C, nh * dv), tok),
            pl.BlockSpec((nseq, nh, dk, dv), lambda p, j: (p, 0, 0, 0)),
        ],
        scratch_shapes=[pltpu.VMEM((nseq, nh, dk, dv), F32), pltpu.VMEM((nh, C, C), F32)],
        compiler_params=_cparams(2),
        name="ret_prompt",
    )(q16.reshape(B, L, nh * dk), k16.reshape(B, L, nh * dk), v.reshape(B, L, nh * dv),
      sgate.reshape(B, L, nh * dv), gnorm.reshape(1, nh * dv))
    return y.reshape(M, nh * dv), S


def _page_sums_body(x_ref, o_ref, op_ref):
    _, nh, plen = x_ref.shape
    n = nh * plen
    wide = o_ref.shape[1]

    @pl.when(pl.program_id(0) == 0)
    def _():
        kk = lax.broadcasted_iota(jnp.int32, (plen, wide), 0)
        cc = lax.broadcasted_iota(jnp.int32, (plen, wide), 1)
        key_c = cc >> (nh.bit_length() - 1)
        later_or_total = (cc >= n) | (kk > key_c)
        for h in range(nh):
            op_ref[h] = (((cc & (nh - 1)) == h) & later_or_total).astype(BF16)

    acc = None
    for h in range(nh):
        for part in _split2(x_ref[:, h, :]):
            d = _dot(part, op_ref[h])
            acc = d if acc is None else acc + d
    o_ref[...] = acc


def _page_sums(lf_pages):
    P, nh, plen = lf_pages.shape
    wide = nh * plen + LANES
    tp = 512 if P % 512 == 0 else P
    return pl.pallas_call(
        _page_sums_body,
        out_shape=jax.ShapeDtypeStruct((P, wide), F32),
        grid=(P // tp,),
        in_specs=[pl.BlockSpec((tp, nh, plen), lambda p: (p, 0, 0))],
        out_specs=pl.BlockSpec((tp, wide), lambda p: (p, 0)),
        scratch_shapes=[pltpu.VMEM((nh, plen, wide), BF16)],
        compiler_params=_cparams(1),
        name="page_sums",
    )(lf_pages)


def _paged_body(pt_ref, *refs, G, nh, scale):
    k_refs = refs[:G]
    v_refs = refs[G:2 * G]
    sums_ref, q_ref, kn_ref, vn_ref, lfn_ref, o_ref, m_ref, l_ref, acc_ref, carry_ref = refs[2 * G:]
    b_idx = pl.program_id(0)
    s_idx = pl.program_id(1)
    ns = pl.num_programs(1)
    n = k_refs[0].shape[0]
    dh = q_ref.shape[1]

    @pl.when(s_idx == 0)
    def _():
        m_ref[...] = jnp.full_like(m_ref, NEG)
        l_ref[...] = jnp.zeros_like(l_ref)
        acc_ref[...] = jnp.zeros_like(acc_ref)
        carry_ref[...] = jnp.zeros_like(carry_ref)

    q = q_ref[...]
    q8 = jnp.concatenate([q, jnp.zeros((8 - nh, dh), F32)], axis=0).astype(BF16)
    lfn = jnp.concatenate([lfn_ref[...], jnp.zeros((8 - nh, 1), F32)], axis=0)
    hrow = lax.broadcasted_iota(jnp.int32, (8, n), 0)
    hcol = lax.broadcasted_iota(jnp.int32, (8, n), 1) & (nh - 1)
    match = hrow == hcol
    carry = carry_ref[...]
    s_parts = [None] * G
    for i in reversed(range(G)):
        page_id = pt_ref[b_idx, (ns - 1 - s_idx) * G + i]
        wt = sums_ref[pl.ds(page_id, 1), :]
        bias = wt[:, :n] + carry
        carry = carry + jnp.concatenate([wt[:, n:]] * (n // LANES), axis=1)
        s = _dot_nt(q8, k_refs[i][...].astype(BF16)) * scale + bias + lfn
        s_parts[i] = jnp.where(match, s, NEG)
    carry_ref[...] = carry
    m_run, l_run, acc_run = m_ref[...], l_ref[...], acc_ref[...]
    half = max(G // 2, 1)
    for grp in (range(half, G), range(0, half)) if G > 1 else (range(G),):
        sg = [s_parts[i] for i in grp]
        m_new = jnp.maximum(m_run, jnp.max(functools.reduce(jnp.maximum, sg), axis=-1, keepdims=True))
        alpha = jnp.exp(m_run - m_new)
        pg = [jnp.exp(s - m_new) for s in sg]
        l_run = alpha * l_run + jnp.sum(functools.reduce(jnp.add, pg), axis=-1, keepdims=True)
        pv = functools.reduce(jnp.add, [_dot(p.astype(BF16), v_refs[i][...].astype(BF16)) for p, i in zip(pg, grp)])
        acc_run = alpha * acc_run + pv
        m_run = m_new
    m_ref[...], l_ref[...], acc_ref[...] = m_run, l_run, acc_run

    @pl.when(s_idx == pl.num_programs(1) - 1)
    def _():
        qf = q.astype(BF16).astype(F32)
        kn = kn_ref[...].astype(BF16).astype(F32)
        s_new = jnp.sum(qf * kn, axis=-1, keepdims=True) * scale
        m_prev = m_ref[0:nh, :]
        m_new = jnp.maximum(m_prev, s_new)
        alpha = jnp.exp(m_prev - m_new)
        p_new = jnp.exp(s_new - m_new)
        l_fin = alpha * l_ref[0:nh, :] + p_new
        vn = vn_ref[...].astype(BF16).astype(F32)
        num = alpha * acc_ref[0:nh, :] + p_new.astype(BF16).astype(F32) * vn
        o_ref[...] = num / l_fin


def _fox_sample(q, k_new, v_new, lf_new, k_pages, v_pages, page_sums, page_table, nh):
    Bn, _, dh = q.shape
    n_pages = page_table.shape[1]
    n_pool, n, _ = k_pages.shape
    G = 16 if n_pages % 16 == 0 else (8 if n_pages % 8 == 0 else 1)
    ns = n_pages // G

    def page_map(i):
        return lambda b, s, pt: (pt[b, (ns - 1 - s) * G + i], 0, 0)

    per_b = lambda b, s, pt: (b, 0, 0)
    grid_spec = pltpu.PrefetchScalarGridSpec(
        num_scalar_prefetch=1,
        grid=(Bn, ns),
        in_specs=[pl.BlockSpec((None, n, dh), page_map(i)) for i in range(G)]
        + [pl.BlockSpec((None, n, dh), page_map(i)) for i in range(G)]
        + [pl.BlockSpec(page_sums.shape, lambda b, s, pt: (0, 0)),
           pl.BlockSpec((None, nh, dh), per_b), pl.BlockSpec((None, nh, dh), per_b),
           pl.BlockSpec((None, nh, dh), per_b), pl.BlockSpec((None, nh, 1), per_b)],
        out_specs=pl.BlockSpec((None, nh, dh), per_b),
        scratch_shapes=[pltpu.VMEM((8, 1), F32), pltpu.VMEM((8, 1), F32), pltpu.VMEM((8, dh), F32),
                        pltpu.VMEM((1, n), F32)],
    )
    return pl.pallas_call(
        functools.partial(_paged_body, G=G, nh=nh, scale=dh ** -0.5),
        out_shape=jax.ShapeDtypeStruct((Bn, nh, dh), F32),
        grid_spec=grid_spec,
        compiler_params=_cparams(2),
        name="fox_sample",
    )(page_table, *([k_pages] * G), *([v_pages] * G), page_sums, q, k_new, v_new, lf_new)


def _pad8(row):
    return jnp.concatenate([row, jnp.zeros((8 - row.shape[0], row.shape[1]), row.dtype)], axis=0)


def _gdn_sample_body(cin_ref, prev_ref, z_ref, gt_ref, cw_ref, ng_ref, s0_ref, o_ref, cnew_ref, s_out_ref, *, nh, dk, dv):
    nb = cin_ref.shape[0]
    w = cw_ref[...]
    kw = w.shape[0]
    convs, gts = [], []
    for b in range(nb):
        u = cin_ref[b]
        prev = prev_ref[b]
        conv = u * w[kw - 1:kw]
        for i in range(kw - 1):
            conv = conv + prev[i:i + 1] * w[i:i + 1]
        cnew_ref[b, 0:kw - 2, :] = prev[1:kw - 1]
        cnew_ref[b, kw - 2:kw - 1, :] = u
        convs.append(jax.nn.silu(conv))
        gts.append(gt_ref[b])
    units = [(b, h) for b in range(nb) for h in range(nh)]
    nu = range(len(units))
    q = [_l2_rows(convs[b][:, h * dk:(h + 1) * dk]) * dk ** -0.5 for b, h in units]
    k = [_l2_rows(convs[b][:, nh * dk + h * dk:nh * dk + (h + 1) * dk]) for b, h in units]
    v = [convs[b][:, 2 * nh * dk + h * dv:2 * nh * dk + (h + 1) * dv] for b, h in units]
    beta = [gts[b][:, 2 * nh + h:2 * nh + h + 1] for b, h in units]
    eg = [jnp.exp(gts[b][:, nh + h:nh + h + 1]) for b, h in units]
    S = [s0_ref[b, h] for b, h in units]
    r = [_dot(_pad8(jnp.concatenate([k[u] * beta[u] * eg[u], q[u] * eg[u]], axis=0)).astype(BF16), S[u].astype(BF16))
         for u in nu]
    vn16 = [(v[u] * beta[u] - r[u][0:1]).astype(BF16) for u in nu]
    for u, (b, h) in enumerate(units):
        s_out_ref[b, h] = S[u] * eg[u] + _dot_tn(_pad8(k[u]).astype(BF16), _pad8(vn16[u]))
    for u, (b, h) in enumerate(units):
        qk = jnp.sum(q[u].astype(BF16).astype(F32) * k[u].astype(BF16).astype(F32), axis=-1, keepdims=True)
        o = r[u][1:2] + qk.astype(BF16).astype(F32) * vn16[u].astype(F32)
        o = _rms_rows(o, ng_ref[...])
        o_ref[b, :, h * dv:(h + 1) * dv] = o * jax.nn.silu(z_ref[b, :, h * dv:(h + 1) * dv])


def _gdn_sample(conv_in, conv_prev, z, gates_col, conv_w, norm_g, S0, nh, dk, dv):
    Bn, _, CD = conv_in.shape
    kw = conv_w.shape[0]
    nb = 4 if Bn % 4 == 0 else 1
    per_b3 = lambda b: (b, 0, 0)
    per_b4 = lambda b: (b, 0, 0, 0)
    kernel = functools.partial(_gdn_sample_body, nh=nh, dk=dk, dv=dv)
    return pl.pallas_call(
        kernel,
        out_shape=[jax.ShapeDtypeStruct((Bn, 1, nh * dv), F32), jax.ShapeDtypeStruct((Bn, kw - 1, CD), F32),
                   jax.ShapeDtypeStruct((Bn, nh, dk, dv), F32)],
        grid=(Bn // nb,),
        in_specs=[
            pl.BlockSpec((nb, 1, CD), per_b3),
            pl.BlockSpec((nb, kw - 1, CD), per_b3),
            pl.BlockSpec((nb, 1, nh * dv), per_b3),
            pl.BlockSpec((nb, 1, LANES), per_b3),
            pl.BlockSpec(conv_w.shape, lambda b: (0, 0)),
            pl.BlockSpec((1, dv), lambda b: (0, 0)),
            pl.BlockSpec((nb, nh, dk, dv), per_b4),
        ],
        out_specs=[
            pl.BlockSpec((nb, 1, nh * dv), per_b3),
            pl.BlockSpec((nb, kw - 1, CD), per_b3),
            pl.BlockSpec((nb, nh, dk, dv), per_b4),
        ],
        compiler_params=_cparams(1),
        name="gdn_sample",
    )(conv_in, conv_prev, z, gates_col, conv_w, norm_g.reshape(1, dv), S0)


def _ret_sample_body(q_ref, k_ref, v_ref, gate_ref, fr_ref, gn_ref, s0_ref, o_ref, s_out_ref, *, nh, dk, dv, pos):
    nb = q_ref.shape[0]
    cos2, sin2 = _rope_tables(jnp.full((1, 1), pos, F32), fr_ref[...])
    units = [(b, h) for b in range(nb) for h in range(nh)]
    qs, ks = [], []
    for b in range(nb):
        qs += _rope(q_ref[b], cos2, sin2, nh, dk)
        ks += [kh * dk ** -0.5 for kh in _rope(k_ref[b], cos2, sin2, nh, dk)]
    v16 = [v_ref[b, :, h * dv:(h + 1) * dv].astype(BF16) for b, h in units]
    S = [s0_ref[b, h] for b, h in units]
    cross = [_dot(_pad8(qs[u] * math.exp(_log_gamma(h))).astype(BF16), S[u].astype(BF16))[0:1]
             for u, (b, h) in enumerate(units)]
    for u, (b, h) in enumerate(units):
        s_out_ref[b, h] = S[u] * math.exp(_log_gamma(h)) + _dot_tn(_pad8(ks[u]).astype(BF16), _pad8(v16[u]))
    for u, (b, h) in enumerate(units):
        qk = jnp.sum(qs[u].astype(BF16).astype(F32) * ks[u].astype(BF16).astype(F32), axis=-1, keepdims=True)
        inner = qk.astype(BF16).astype(F32) * v16[u].astype(F32)
        o = _head_layernorm(inner + cross[u]) * gn_ref[:, h * dv:(h + 1) * dv]
        o_ref[b, :, h * dv:(h + 1) * dv] = jax.nn.silu(gate_ref[b, :, h * dv:(h + 1) * dv]) * o


def _ret_sample(q, k, v, gate, freqs, gnorm, S0, pos, nh, dk, dv):
    Bn = q.shape[0]
    nb = 2 if Bn % 2 == 0 else 1
    per_b3 = lambda b: (b, 0, 0)
    per_b4 = lambda b: (b, 0, 0, 0)
    kernel = functools.partial(_ret_sample_body, nh=nh, dk=dk, dv=dv, pos=float(pos))
    return pl.pallas_call(
        kernel,
        out_shape=[jax.ShapeDtypeStruct((Bn, 1, nh * dv), F32), jax.ShapeDtypeStruct((Bn, nh, dk, dv), F32)],
        grid=(Bn // nb,),
        in_specs=[
            pl.BlockSpec((nb, 1, nh * dk), per_b3),
            pl.BlockSpec((nb, 1, nh * dk), per_b3),
            pl.BlockSpec((nb, 1, nh * dv), per_b3),
            pl.BlockSpec((nb, 1, nh * dv), per_b3),
            pl.BlockSpec((1, dk // 2), lambda b: (0, 0)),
            pl.BlockSpec((1, nh * dv), lambda b: (0, 0)),
            pl.BlockSpec((nb, nh, dk, dv), per_b4),
        ],
        out_specs=[
            pl.BlockSpec((nb, 1, nh * dv), per_b3),
            pl.BlockSpec((nb, nh, dk, dv), per_b4),
        ],
        compiler_params=_cparams(1),
        name="ret_sample",
    )(q, k, v, gate, freqs, gnorm.reshape(1, nh * dv), S0)


def kernel(x_prompt, x_sample, cache_fox_k, cache_fox_v, cache_fox_logf, page_table, state_gdn_conv, state_gdn_S, state_ret_S, norm_g, final_norm_g, ffn_w_gu, ffn_w_down, ab_w_in, ab_w_out, fox_b_f, gdn_conv_w, gdn_A_log, gdn_dt_bias, gdn_norm_g, c_w_in, c_w_out, ret_norm_g):
    B, L, D = x_prompt.shape
    Bn, Ls, _ = x_sample.shape
    assert Ls == 1, "the sample group decodes one token per sequence"
    depth = norm_g.shape[0]
    _, n_pool, page, fh, fd = cache_fox_k.shape
    fw = fh * fd
    _, _, gh, gdk, gdv = state_gdn_S.shape
    conv_dim = gdn_conv_w.shape[2]
    kw = gdn_conv_w.shape[1]
    _, _, rh, rdk, rdv = state_ret_S.shape
    n_pages = page_table.shape[1]
    assert fh == gh and 3 * fh <= 16 and fh & (fh - 1) == 0

    xp = x_prompt.reshape(B * L, D)
    xs = x_sample.reshape(Bn, D)
    w_gu16 = _to_bf16(ffn_w_gu)
    w_down16 = _to_bf16(ffn_w_down)
    freqs = (ROPE_BASE ** (-jnp.arange(rdk // 2, dtype=F32) / (rdk // 2))).reshape(1, rdk // 2)
    cos2, sin2 = _rope_table(L, freqs)

    fkp, fvp, flp, fks, fvs, fls = [], [], [], [], [], []
    gcp, gsp, gcs, gss = [], [], [], []
    rsp, rss = [], []
    for li in range(depth):
        xp = _ffn(xp, norm_g[li, 0], w_gu16, w_down16, li, 0)
        xs = _ffn(xs, norm_g[li, 0], w_gu16, w_down16, li, 0)
        j = li // 2
        if li % 2 == 0:
            o0 = 3 * fw
            zw = gh * gdv
            c0 = o0 + fh
            widths = (conv_dim, fw, fw, fw, zw, LANES)
            w_cat = _regroup_bf16(ab_w_in[j].T, [(c0, conv_dim), (0, o0), (c0 + conv_dim, zw), (o0, fh),
                                                 (c0 + conv_dim + zw, 2 * gh)], sum(widths))
            w_out16 = ab_w_out[j].astype(BF16)

            gate_p = _gate_params(fox_b_f[j], gdn_dt_bias[j], gdn_A_log[j])
            gqkv, q, k, v, z, gates, k3, v3, tails = _ab_proj_prompt(
                xp, norm_g[li, 1], w_cat, gdn_conv_w[j], gate_p, B, L, widths, fh, fd, gh, gdk)
            gates_row = gates[:, :16].T
            c_rows = _cumsum_rows(gates_row, L)[:fh].reshape(fh, B, L).transpose(1, 0, 2).reshape(B * fh, 1, L)
            fo = _fox_prompt(q, k, v, c_rows, B, L, fh)
            go, S_p = _gdn_prompt(gqkv, z, gates, gates_row, gdn_norm_g[j], B, L, gh, gdk, gdv)
            pre_p = ([fo, go], w_out16)
            fkp.append(k3.reshape(B, L, fh, fd))
            fvp.append(v3.reshape(B, L, fh, fd))
            flp.append(gates[:, :fh].reshape(B, L, fh))
            gcp.append(tails[:, 8 - (kw - 1):])
            gsp.append(S_p)

            cin, q, k, v, z, small = _rms_proj(xs, norm_g[li, 1], w_cat, widths)
            gates = _gates(small, gate_p, fh)
            sums = _page_sums(jnp.transpose(cache_fox_logf[j], (0, 2, 1)))
            fo = _fox_sample(q.reshape(Bn, fh, fd), k.reshape(Bn, fh, fd), v.reshape(Bn, fh, fd),
                             gates[:, :fh].reshape(Bn, fh, 1),
                             cache_fox_k[j].reshape(n_pool, page * fh, fd), cache_fox_v[j].reshape(n_pool, page * fh, fd),
                             sums, page_table, fh)
            go, conv_s, S_s = _gdn_sample(cin.reshape(Bn, 1, conv_dim), state_gdn_conv[j], z.reshape(Bn, 1, gh * gdv),
                                          gates.reshape(Bn, 1, LANES), gdn_conv_w[j], gdn_norm_g[j], state_gdn_S[j],
                                          gh, gdk, gdv)
            pre_s = ([fo.reshape(Bn, fw), go.reshape(Bn, gh * gdv)], w_out16)
            fks.append(k.reshape(Bn, 1, fh, fd))
            fvs.append(v.reshape(Bn, 1, fh, fd))
            fls.append(gates[:, :fh].reshape(Bn, 1, fh))
            gcs.append(conv_s)
            gss.append(S_s)
        else:
            w_in16 = _to_bf16(c_w_in[j])
            qk_w, v_w = rh * rdk, rh * rdv
            widths = (qk_w, qk_w, v_w, v_w)
            w_out16 = c_w_out[j].astype(BF16)

            q16, k16, v, sgate = _c_proj_prompt(xp, norm_g[li, 1], w_in16, cos2, sin2, L, rh, rdk, v_w)
            y, R_p = _ret_prompt(q16, k16, v, sgate, ret_norm_g[j], B, L, rh, rdk, rdv)
            pre_p = ([y], w_out16)
            rsp.append(R_p)

            q, k, v, gate = _rms_proj(xs, norm_g[li, 1], w_in16, widths)
            y, R_s = _ret_sample(q.reshape(Bn, 1, qk_w), k.reshape(Bn, 1, qk_w), v.reshape(Bn, 1, v_w),
                                 gate.reshape(Bn, 1, v_w), freqs, ret_norm_g[j], state_ret_S[j],
                                 n_pages * page, rh, rdk, rdv)
            pre_s = ([y.reshape(Bn, v_w)], w_out16)
            rss.append(R_s)
        final_g = final_norm_g if li == depth - 1 else None
        xp = _ffn(xp, norm_g[li, 2], w_gu16, w_down16, li, 1, final_g, pre_p)
        xs = _ffn(xs, norm_g[li, 2], w_gu16, w_down16, li, 1, final_g, pre_s)
    y_prompt = xp.reshape(B, L, D)
    y_sample = xs.reshape(Bn, 1, D)
    return (y_prompt, y_sample,
            jnp.stack(fkp), jnp.stack(fvp), jnp.stack(flp),
            jnp.stack(fks), jnp.stack(fvs), jnp.stack(fls),
            jnp.stack(gcp), jnp.stack(gsp), jnp.stack(gcs), jnp.stack(gss),
            jnp.stack(rsp), jnp.stack(rss))
```

```python
import functools
import math

import jax
import jax.numpy as jnp
from jax import lax
from jax.experimental import pallas as pl
from jax.experimental.pallas import tpu as pltpu

F32 = jnp.float32
BF16 = jnp.bfloat16
EPS = 1e-6
ROPE_BASE = 10000.0
NEG = -1e30
LANES = 128
GDN_CHUNK = 64
VMEM_LIMIT = 56 * 1024 * 1024
HI = lax.Precision.HIGHEST


def _cparams(n_axes):
    return pltpu.CompilerParams(dimension_semantics=("arbitrary",) * n_axes,
                                vmem_limit_bytes=VMEM_LIMIT)


def _dot(a, b, precision=None):
    return jnp.dot(a, b, preferred_element_type=F32, precision=precision)


def _dot_nt(a, b, precision=None):
    return lax.dot_general(a, b, (((1,), (1,)), ((), ())), preferred_element_type=F32, precision=precision)


def _dot_tn(a, b, precision=None):
    return lax.dot_general(a, b, (((0,), (0,)), ((), ())), preferred_element_type=F32, precision=precision)


def _rms_rows(x, g):
    return x * lax.rsqrt(jnp.mean(x * x, axis=-1, keepdims=True) + EPS) * g


MXU_DEPTH = 256


def _ffn_body(*refs, F, chunk, final_norm, n_pre, n_cast):
    x_ref, g_ref, wgu_ref, wd_ref = refs[:4]
    a_refs = refs[4:4 + n_pre]
    rest = refs[4 + n_pre:]
    if n_cast:
        for src, dst in zip(rest[len(rest) - 2 * n_cast - 1:len(rest) - n_cast - 1], rest[len(rest) - n_cast:]):
            dst[...] = src[...].astype(dst.dtype)
        rest = rest[:len(rest) - 2 * n_cast - 1] + (rest[len(rest) - n_cast - 1],)
    x = x_ref[...]
    if n_pre:
        wo_ref, rest = rest[0], rest[1:]
        row = 0
        for a_ref in a_refs:
            n = a_ref.shape[1]
            x = x + _dot(a_ref[...].astype(BF16), wo_ref[row:row + n, :])
            row += n
    if final_norm:
        gf_ref, rest = rest[0], rest[1:]
    o_ref = rest[0]
    xn = _rms_rows(x, g_ref[...]).astype(BF16)
    acc = None
    for c0 in range(0, F, chunk):
        c1 = min(F, c0 + chunk)
        a = _dot(xn, wgu_ref[:, c0:c1])
        b = _dot(xn, wgu_ref[:, F + c0:F + c1])
        h = (jax.nn.silu(a) * b).astype(BF16)
        d = _dot(h, wd_ref[c0:c1, :])
        acc = d if acc is None else acc + d
    y = x + 0.5 * acc
    o_ref[...] = _rms_rows(y, gf_ref[...]) if final_norm else y


def _ffn(x, g, w_gu, w_down, li, j, final_g=None, pre=None, cast=()):
    M, D = x.shape
    F = w_down.shape[2]
    tm = min(M, 512)
    final_norm = final_g is not None
    const = lambda m: (0, 0)
    in_specs = [pl.BlockSpec((tm, D), lambda m: (m, 0)), pl.BlockSpec((1, D), const),
                pl.BlockSpec((None, None, D, 2 * F), lambda m: (li, j, 0, 0)),
                pl.BlockSpec((None, None, F, D), lambda m: (li, j, 0, 0))]
    args = [x, g.reshape(1, D), w_gu, w_down]
    acts, w_out = pre if pre is not None else ((), None)
    if acts:
        assert sum(a.shape[1] for a in acts) == w_out.shape[0]
        in_specs += [pl.BlockSpec((tm, a.shape[1]), lambda m: (m, 0)) for a in acts]
        in_specs.append(pl.BlockSpec(w_out.shape, const))
        args += [*acts, w_out]
    if final_norm:
        in_specs.append(pl.BlockSpec((1, D), const))
        args.append(final_g.reshape(1, D))
    out_shape = [jax.ShapeDtypeStruct((M, D), F32)]
    out_specs = [pl.BlockSpec((tm, D), lambda m: (m, 0))]
    steps = M // tm
    for w in cast:
        cols = w.shape[-1]
        rows = w.size // cols
        assert rows % (16 * steps) == 0, "a weight slab per grid step must hold whole bfloat16 tiles"
        in_specs.append(pl.BlockSpec((rows // steps, cols), lambda m: (m, 0)))
        args.append(w.reshape(rows, cols))
        out_shape.append(jax.ShapeDtypeStruct((rows, cols), BF16))
        out_specs.append(pl.BlockSpec((rows // steps, cols), lambda m: (m, 0)))
    outs = pl.pallas_call(
        functools.partial(_ffn_body, F=F, chunk=2 * MXU_DEPTH, final_norm=final_norm, n_pre=len(acts),
                          n_cast=len(cast)),
        out_shape=out_shape,
        grid=(steps,),
        in_specs=in_specs,
        out_specs=out_specs,
        compiler_params=_cparams(1),
        name="ffn",
    )(*args)
    return (outs[0], *[o.reshape(w.shape) for o, w in zip(outs[1:], cast)]) if cast else outs[0]


PROJ_CHUNK = 512


def _project(xn, w_ref, col, o_ref, o3_ref=None):
    n = o_ref.shape[1]
    for s0 in range(0, n, PROJ_CHUNK):
        s1 = min(n, s0 + PROJ_CHUNK)
        val = _dot(xn, w_ref[:, col + s0:col + s1])
        o_ref[:, s0:s1] = val
        if o3_ref is not None:
            dh = o3_ref.shape[2]
            nhh = (s1 - s0) // dh
            o3_ref[:, s0 // dh:s0 // dh + nhh, :] = val.reshape(val.shape[0], nhh, dh)
    return col + n


def _proj_body(x_ref, g_ref, w_ref, *o_refs):
    xn = _rms_rows(x_ref[...], g_ref[...]).astype(BF16)
    col = 0
    for o_ref in o_refs:
        col = _project(xn, w_ref, col, o_ref)


def _rms_proj(x, g, w, widths):
    M, D = x.shape
    tm = min(M, 256)
    assert sum(widths) == w.shape[1] and all(n % LANES == 0 for n in widths)
    return pl.pallas_call(
        _proj_body,
        out_shape=[jax.ShapeDtypeStruct((M, n), F32) for n in widths],
        grid=(M // tm,),
        in_specs=[pl.BlockSpec((tm, D), lambda m: (m, 0)), pl.BlockSpec((1, D), lambda m: (0, 0)),
                  pl.BlockSpec(w.shape, lambda m: (0, 0))],
        out_specs=[pl.BlockSpec((tm, n), lambda m: (m, 0)) for n in widths],
        compiler_params=_cparams(1),
        name="rms_proj",
    )(x, g.reshape(1, D), w)


def _l2_rows(x):
    return x * lax.rsqrt(jnp.sum(x * x, axis=-1, keepdims=True) + EPS)


def _ab_proj_body(x_ref, g_ref, w_ref, cw_ref, gp_ref, qkv_o, q_o, k_o, v_o, z_o, gates_o, k3_o, v3_o, tail_o,
                  win_ref, *, tiles_per_seq, nh, dk):
    m = pl.program_id(0)
    tm = x_ref.shape[0]
    cd = qkv_o.shape[1]
    xn = _rms_rows(x_ref[...], g_ref[...]).astype(BF16)

    @pl.when(lax.rem(m, tiles_per_seq) == 0)
    def _():
        win_ref[0:8, :] = jnp.zeros((8, cd), F32)

    kw = cw_ref.shape[0]
    step = nh * dk

    def conv_product(s0):
        win_ref[8:8 + tm, s0:s0 + step] = _dot(xn, w_ref[:, s0:s0 + step])
        tail_o[:, s0:s0 + step] = win_ref[tm:tm + 8, s0:s0 + step]

    def conv_finish(s0):
        s1 = s0 + step
        conv = win_ref[8 - kw + 1:8 - kw + 1 + tm, s0:s1] * cw_ref[0:1, s0:s1]
        for i in range(1, kw):
            conv = conv + win_ref[8 - kw + 1 + i:8 - kw + 1 + i + tm, s0:s1] * cw_ref[i:i + 1, s0:s1]
        win_ref[0:8, s0:s1] = win_ref[tm:tm + 8, s0:s1]
        conv = jax.nn.silu(conv)
        if s0 >= 2 * step:
            qkv_o[:, s0:s1] = conv
        else:
            for h in range(nh):
                xh = _l2_rows(conv[:, h * dk:(h + 1) * dk])
                qkv_o[:, s0 + h * dk:s0 + (h + 1) * dk] = xh * dk ** -0.5 if s0 == 0 else xh

    assert cd == 3 * step
    conv_product(0)
    conv_product(step)
    col = _project(xn, w_ref, cd, q_o)
    conv_finish(0)
    conv_product(2 * step)
    col = _project(xn, w_ref, col, k_o, k3_o)
    conv_finish(step)
    col = _project(xn, w_ref, col, v_o, v3_o)
    col = _project(xn, w_ref, col, z_o)
    conv_finish(2 * step)
    gates_o[...] = _gate_values(_dot(xn, w_ref[:, col:col + gates_o.shape[1]]), gp_ref, nh)


def _ab_proj_prompt(x, g, w, conv_w, gate_p, B, L, widths, fh, fd, gh, gdk):
    M, D = x.shape
    tm = min(L, 512)
    cd = widths[0]
    assert sum(widths) == w.shape[1] and L % tm == 0
    row = lambda m: (m, 0)
    const = lambda m: (0, 0)
    out_shape = [jax.ShapeDtypeStruct((M, n), F32) for n in widths]
    out_specs = [pl.BlockSpec((tm, n), row) for n in widths]
    for _ in range(2):
        out_shape.append(jax.ShapeDtypeStruct((M, fh, fd), F32))
        out_specs.append(pl.BlockSpec((tm, fh, fd), lambda m: (m, 0, 0)))
    out_shape.append(jax.ShapeDtypeStruct((B, 8, cd), F32))
    out_specs.append(pl.BlockSpec((None, 8, cd), lambda m: (m // (L // tm), 0, 0)))
    return pl.pallas_call(
        functools.partial(_ab_proj_body, tiles_per_seq=L // tm, nh=gh, dk=gdk),
        out_shape=out_shape,
        grid=(M // tm,),
        in_specs=[pl.BlockSpec((tm, D), row), pl.BlockSpec((1, D), const), pl.BlockSpec(w.shape, const),
                  pl.BlockSpec(conv_w.shape, const), pl.BlockSpec(gate_p.shape, const)],
        out_specs=out_specs,
        scratch_shapes=[pltpu.VMEM((8 + tm, cd), F32)],
        compiler_params=_cparams(1),
        name="ab_proj_prompt",
    )(x, g.reshape(1, D), w, conv_w, gate_p)


def _c_proj_body(x_ref, g_ref, w_ref, cos_ref, sin_ref, q_o, k_o, v_o, sg_o, *, nh, dk):
    xn = _rms_rows(x_ref[...], g_ref[...]).astype(BF16)
    cos2, sin2 = cos_ref[...], sin_ref[...]
    hpc = PROJ_CHUNK // dk
    for idx, o_ref in enumerate((q_o, k_o)):
        for h0 in range(0, nh, hpc):
            c0 = idx * nh * dk + h0 * dk
            val = _dot(xn, w_ref[:, c0:c0 + hpc * dk])
            for hh in range(hpc):
                xh = val[:, hh * dk:(hh + 1) * dk]
                xh = xh * cos2 + pltpu.roll(xh, dk // 2, 1) * sin2
                if idx == 1:
                    xh = xh * dk ** -0.5
                o_ref[:, (h0 + hh) * dk:(h0 + hh + 1) * dk] = xh.astype(BF16)
    col = _project(xn, w_ref, 2 * nh * dk, v_o)
    n = sg_o.shape[1]
    for s0 in range(0, n, PROJ_CHUNK):
        s1 = min(n, s0 + PROJ_CHUNK)
        sg_o[:, s0:s1] = jax.nn.silu(_dot(xn, w_ref[:, col + s0:col + s1]))


def _c_proj_prompt(x, g, w, cos2, sin2, L, nh, dk, v_w):
    M, D = x.shape
    tm = min(L, 512)
    qk_w = nh * dk
    row = lambda m: (m, 0)
    const = lambda m: (0, 0)
    pos = lambda m: (lax.rem(m, L // tm), 0)
    return pl.pallas_call(
        functools.partial(_c_proj_body, nh=nh, dk=dk),
        out_shape=[jax.ShapeDtypeStruct((M, qk_w), BF16), jax.ShapeDtypeStruct((M, qk_w), BF16),
                   jax.ShapeDtypeStruct((M, v_w), F32), jax.ShapeDtypeStruct((M, v_w), F32)],
        grid=(M // tm,),
        in_specs=[pl.BlockSpec((tm, D), row), pl.BlockSpec((1, D), const), pl.BlockSpec(w.shape, const),
                  pl.BlockSpec((tm, dk), pos), pl.BlockSpec((tm, dk), pos)],
        out_specs=[pl.BlockSpec((tm, qk_w), row), pl.BlockSpec((tm, qk_w), row),
                   pl.BlockSpec((tm, v_w), row), pl.BlockSpec((tm, v_w), row)],
        compiler_params=_cparams(1),
        name="c_proj_prompt",
    )(x, g.reshape(1, D), w, cos2, sin2)


def _cast_body(x_ref, o_ref):
    o_ref[...] = x_ref[...].astype(o_ref.dtype)


def _to_bf16(w, lead=None):
    cols = w.shape[-1]
    if lead is not None:
        rows = w.shape[2]
        tr = 512 if rows % 512 == 0 else rows
        a, b = lead
        in_spec = pl.BlockSpec((None, None, tr, cols), lambda r: (a, b, r, 0))
        src, out_shape = w, (1, 1, rows, cols)
    else:
        rows = w.size // cols
        tr = 512 if rows % 512 == 0 else rows
        in_spec = pl.BlockSpec((tr, cols), lambda r: (r, 0))
        src, out_shape = w.reshape(rows, cols), w.shape
    out = pl.pallas_call(
        _cast_body,
        out_shape=jax.ShapeDtypeStruct((rows, cols), BF16),
        grid=(rows // tr,),
        in_specs=[in_spec],
        out_specs=pl.BlockSpec((tr, cols), lambda r: (r, 0)),
        compiler_params=_cparams(1),
        name="to_bf16",
    )(src)
    return out.reshape(out_shape)


def _regroup_body(wt_ref, o_ref, *, pieces):
    N, D = wt_ref.shape
    col = 0
    narrow = []
    for start, width in pieces:
        if width % LANES == 0:
            assert not narrow and col % LANES == 0
            for r0 in range(0, width, PROJ_CHUNK):
                r1 = min(width, r0 + PROJ_CHUNK)
                o_ref[:, col + r0:col + r1] = wt_ref[start + r0:start + r1, :].T.astype(o_ref.dtype)
        else:
            w0 = min(start, N - LANES)
            narrow.append(wt_ref[w0:w0 + LANES, :].T[:, start - w0:start - w0 + width])
        col += width
    if narrow:
        first = col - sum(p.shape[1] for p in narrow)
        assert first % LANES == 0 and o_ref.shape[1] - first == LANES
        pad = jnp.zeros((D, o_ref.shape[1] - col), F32)
        o_ref[:, first:] = jnp.concatenate(narrow + [pad], axis=1).astype(o_ref.dtype)


def _regroup_bf16(wt, pieces, n_out):
    N, D = wt.shape
    return pl.pallas_call(
        functools.partial(_regroup_body, pieces=tuple(pieces)),
        out_shape=jax.ShapeDtypeStruct((D, n_out), BF16),
        grid=(1,),
        in_specs=[pl.BlockSpec((N, D), lambda r: (0, 0))],
        out_specs=pl.BlockSpec((D, n_out), lambda r: (0, 0)),
        compiler_params=_cparams(1),
        name="regroup_bf16",
    )(wt)


def _gate_values(s, p_ref, nh):
    s = s + p_ref[0:1, :]
    lane = lax.broadcasted_iota(jnp.int32, s.shape, 1)
    logf = jax.nn.log_sigmoid(s)
    g = -jnp.exp(p_ref[1:2, :]) * jax.nn.softplus(s)
    beta = jax.nn.sigmoid(s)
    return jnp.where(lane < nh, logf, jnp.where(lane < 2 * nh, g, jnp.where(lane < 3 * nh, beta, 0.0)))


def _gate_params(b_f, dt_bias, a_log):
    nh = b_f.shape[0]
    pad = jnp.zeros((LANES - 3 * nh,), F32)
    p = jnp.zeros((8, LANES), F32)
    p = p.at[0].set(jnp.concatenate([b_f, dt_bias, jnp.zeros((nh,), F32), pad]))
    return p.at[1].set(jnp.concatenate([jnp.zeros((nh,), F32), a_log, jnp.zeros((nh,), F32), pad]))


def _gates_body(s_ref, p_ref, o_ref, *, nh):
    o_ref[...] = _gate_values(s_ref[...], p_ref, nh)


def _gates(small, gate_p, nh):
    M = small.shape[0]
    tm = min(M, 2048)
    return pl.pallas_call(
        functools.partial(_gates_body, nh=nh),
        out_shape=jax.ShapeDtypeStruct((M, LANES), F32),
        grid=(M // tm,),
        in_specs=[pl.BlockSpec((tm, LANES), lambda m: (m, 0)), pl.BlockSpec((8, LANES), lambda m: (0, 0))],
        out_specs=pl.BlockSpec((tm, LANES), lambda m: (m, 0)),
        compiler_params=_cparams(1),
        name="gates",
    )(small, gate_p)


def _cumsum_body(x_ref, o_ref):
    L = x_ref.shape[1]
    r = lax.broadcasted_iota(jnp.int32, (LANES, LANES), 0)
    c = lax.broadcasted_iota(jnp.int32, (LANES, LANES), 1)
    upper = (r <= c).astype(F32)
    carry = jnp.zeros((x_ref.shape[0], 1), F32)
    for j in range(L // LANES):
        cs = _dot(x_ref[:, j * LANES:(j + 1) * LANES], upper, HI) + carry
        o_ref[:, j * LANES:(j + 1) * LANES] = cs
        carry = cs[:, LANES - 1:LANES]


def _cumsum_rows(x, seg):
    R, M = x.shape
    return pl.pallas_call(
        _cumsum_body,
        out_shape=jax.ShapeDtypeStruct((R, M), F32),
        grid=(M // seg,),
        in_specs=[pl.BlockSpec((R, seg), lambda b: (0, b))],
        out_specs=pl.BlockSpec((R, seg), lambda b: (0, b)),
        compiler_params=_cparams(1),
        name="cumsum_rows",
    )(x)


def _fox_body(q_ref, k_ref, v_ref, *rest, t, hg, dh, scale):
    nbg = q_ref.shape[0]
    c_refs = rest[:nbg]
    o_ref, vt_ref, cb_ref, acc_ref, m_ref = rest[nbg:]
    i = pl.program_id(2)
    L = k_ref.shape[1]
    log2e = math.log2(math.e)
    units = [(s, g) for s in range(nbg) for g in range(hg)]

    @pl.when(i == 0)
    def _():
        for u, (s, g) in enumerate(units):
            for jj in range(L // t):
                vt_ref[u, 0:dh, jj * t:(jj + 1) * t] = v_ref[s, jj * t:(jj + 1) * t, g * dh:(g + 1) * dh].T.astype(BF16)
            vt_ref[u, dh:, :] = jnp.ones((vt_ref.shape[1] - dh, L), BF16)
            for jj in range(L // LANES):
                row = c_refs[s][g, :, jj * LANES:(jj + 1) * LANES] * log2e
                cb_ref[u, jj * LANES:(jj + 1) * LANES, :] = jnp.broadcast_to(row, (LANES, LANES)).T

    q0 = pl.multiple_of(i * t, t)
    qs = [q_ref[s, :, g * dh:(g + 1) * dh].astype(BF16) for s, g in units]
    r2 = [c_refs[s][g, :, pl.ds(q0, t)][:, t - 1:t] * log2e for s, g in units]
    m_ref[...] = jnp.full_like(m_ref, NEG)
    acc_ref[...] = jnp.zeros_like(acc_ref)

    def step(j, masked):
        k0 = pl.multiple_of(j * t, t)
        qk = [_dot_nt(k_ref[s, pl.ds(k0, t), g * dh:(g + 1) * dh].astype(BF16), qs[u])
              for u, (s, g) in enumerate(units)]
        ps, alphas = [], []
        for u in range(len(units)):
            bias = r2[u] - cb_ref[u, pl.ds(k0, t), :]
            st = qk[u] * (scale * log2e) + jnp.concatenate([bias] * (t // LANES), axis=1)
            if masked:
                key = lax.broadcasted_iota(jnp.int32, (t, t), 0)
                qry = lax.broadcasted_iota(jnp.int32, (t, t), 1)
                st = jnp.where(key <= qry, st, NEG)
            m_prev = m_ref[u]
            m_new = jnp.maximum(m_prev, jnp.max(st, axis=0, keepdims=True))
            ps.append(jnp.exp2(st - m_new).astype(BF16))
            alphas.append(jnp.exp2(m_prev - m_new))
            m_ref[u] = m_new
        for u in range(len(units)):
            acc_ref[u] = acc_ref[u] * alphas[u] + _dot(vt_ref[u, :, pl.ds(k0, t)], ps[u])

    def loop_body(j, carry):
        step(j, False)
        return carry

    lax.fori_loop(0, i, loop_body, 0)
    step(i, True)
    for u, (s, g) in enumerate(units):
        acc = acc_ref[u]
        o_ref[s, :, g * dh:(g + 1) * dh] = (acc[0:dh] / acc[dh:dh + 1]).T


def _fox_prompt(q, k, v, c_rows, B, L, H):
    M, W = q.shape
    Dh = W // H
    t = min(L, 512)
    nq = L // t
    hg = 2 if H % 2 == 0 else 1
    nbg = 2 if B % 2 == 0 else 1
    nu = nbg * hg
    kernel = functools.partial(_fox_body, t=t, hg=hg, dh=Dh, scale=Dh ** -0.5)

    def c_map(s):
        return lambda p, h, i: ((p * nbg + s) * (H // hg) + h, 0, 0)

    o = pl.pallas_call(
        kernel,
        out_shape=jax.ShapeDtypeStruct((B, L, W), F32),
        grid=(B // nbg, H // hg, nq),
        in_specs=[
            pl.BlockSpec((nbg, t, hg * Dh), lambda p, h, i: (p, i, h)),
            pl.BlockSpec((nbg, L, hg * Dh), lambda p, h, i: (p, 0, h)),
            pl.BlockSpec((nbg, L, hg * Dh), lambda p, h, i: (p, 0, h)),
        ] + [pl.BlockSpec((hg, 1, L), c_map(s)) for s in range(nbg)],
        out_specs=pl.BlockSpec((nbg, t, hg * Dh), lambda p, h, i: (p, i, h)),
        scratch_shapes=[pltpu.VMEM((nu, Dh + 16, L), BF16), pltpu.VMEM((nu, L, LANES), F32),
                        pltpu.VMEM((nu, Dh + 16, t), F32), pltpu.VMEM((nu, 1, t), F32)],
        compiler_params=_cparams(3),
        name="fox_prompt",
    )(q.reshape(B, L, W), k.reshape(B, L, W), v.reshape(B, L, W), *([c_rows] * nbg))
    return o.reshape(M, W)


def _split2(x):
    hi = x.astype(BF16)
    return hi, (x - hi.astype(F32)).astype(BF16)


def _split3(x):
    h1 = x.astype(BF16)
    r1 = x - h1.astype(F32)
    h2 = r1.astype(BF16)
    return h1, h2, (r1 - h2.astype(F32)).astype(BF16)


def _dot3(a, b):
    (ah, al), (bh, bl) = a, b
    return _dot(ah, bh) + (_dot(ah, bl) + _dot(al, bh))


def _unit_lower_inverses(a_list, n):
    ri = lax.broadcasted_iota(jnp.int32, (n, n), 0)
    ci = lax.broadcasted_iota(jnp.int32, (n, n), 1)
    eye = (ri == ci).astype(F32)
    nm = len(a_list)
    diag16 = (ri >> 4) == (ci >> 4)
    pw = [jnp.where(diag16, a, 0.0) for a in a_list]
    t = [eye - p for p in pw]
    for _ in range(3):
        pw16 = [p.astype(BF16) for p in pw]
        pw = [_dot(p, p) for p in pw16]
        pn16 = [p.astype(BF16) for p in pw]
        t = [t[m] + _dot(t[m].astype(BF16), pn16[m]) for m in range(nm)]
    size = 16
    while size < GDN_CHUNK:
        sh = size.bit_length() - 1
        off = ((ri >> (sh + 1)) == (ci >> (sh + 1))) & ((ri >> sh) == (ci >> sh) + 1)
        t16 = [x.astype(BF16) for x in t]
        mid = [_dot(jnp.where(off, a_list[m], 0.0).astype(BF16), t16[m]) for m in range(nm)]
        t = [t[m] - _dot(t16[m], mid[m].astype(BF16)) for m in range(nm)]
        size *= 2
    ts = [_split2(x) for x in t]
    res = [(eye - t[m]) - _dot3(_split2(a_list[m]), ts[m]) for m in range(nm)]
    return [t[m] + _dot(ts[m][0], res[m].astype(BF16)) for m in range(nm)]


def _gdn_body(qkv_ref, z_ref, gc_ref, *rest, nh, dk, dv, nseq):
    gr_refs = rest[:nseq]
    ng_ref, o_ref, s_out_ref, s_ref, m16_ref = rest[nseq:]
    j = pl.program_id(1)
    T = qkv_ref.shape[1]
    C = GDN_CHUNK
    nc = T // C

    ri = lax.broadcasted_iota(jnp.int32, (T, T), 0)
    ci = lax.broadcasted_iota(jnp.int32, (T, T), 1)
    csh = C.bit_length() - 1
    same = (ri >> csh) == (ci >> csh)
    tril = same & (ri >= ci)
    strict = same & (ri > ci)

    @pl.when((pl.program_id(0) == 0) & (j == 0))
    def _():
        m16_ref[0] = tril.astype(BF16)
        m16_ref[1] = (same & (ri <= ci)).astype(BF16)

    @pl.when(j == 0)
    def _():
        s_ref[...] = jnp.zeros_like(s_ref)

    rowi = lax.broadcasted_iota(jnp.int32, (T, 1), 0)
    units = [(s, h) for s in range(nseq) for h in range(nh)]
    nu = len(units)
    gcols, grows = [], []
    for s in range(nseq):
        gcols.append(sum(_dot(m16_ref[0], part) for part in _split3(gc_ref[s])))
        grows.append(sum(_dot(part, m16_ref[1]) for part in _split3(gr_refs[s][...])))

    q, k, beta, gcc, gcr, gam, kb, k16 = [], [], [], [], [], [], [], []
    for u, (s, h) in enumerate(units):
        q.append(qkv_ref[s, :, h * dk:(h + 1) * dk])
        k.append(qkv_ref[s, :, nh * dk + h * dk:nh * dk + (h + 1) * dk])
        beta.append(gc_ref[s, :, 2 * nh + h:2 * nh + h + 1])
        gcc.append(gcols[s][:, nh + h:nh + h + 1])
        gcr.append(grows[s][nh + h:nh + h + 1, :])
        gam.append(jnp.exp(jnp.where(tril, gcc[u] - gcr[u], NEG)))
        kb.append(k[u] * beta[u])
        k16.append(k[u].astype(BF16))
    a = [jnp.where(strict, _dot_nt(kb[u].astype(BF16), k16[u]) * gam[u], 0.0) for u in range(nu)]
    tinv = _unit_lower_inverses(a, T)
    x = []
    for u, (s, h) in enumerate(units):
        v = qkv_ref[s, :, 2 * nh * dk + h * dv:2 * nh * dk + (h + 1) * dv]
        rhs = jnp.concatenate([v * beta[u], kb[u] * jnp.exp(gcc[u])], axis=-1)
        x.append(_dot3(_split2(tinv[u]), _split2(rhs)))
    attn, qg, kd, egl = [], [], [], []
    for u in range(nu):
        attn.append((_dot_nt(q[u].astype(BF16), k16[u]) * gam[u]).astype(BF16))
        qg.append((q[u] * jnp.exp(gcc[u])).astype(BF16))
        gl = gcr[u][:, C - 1:C]
        for cidx in range(1, nc):
            gl = jnp.where(rowi >= cidx * C, gcr[u][:, (cidx + 1) * C - 1:(cidx + 1) * C], gl)
        kd.append((k[u] * jnp.exp(gl - gcc[u])).astype(BF16))
        egl.append(jnp.exp(gl))
    S = [s_ref[s, h] for s, h in units]
    outs = [[] for _ in units]
    for cidx in range(nc):
        sl = slice(cidx * C, (cidx + 1) * C)
        S16 = [st.astype(BF16) for st in S]
        vn16 = [(x[u][sl, :dv] - _dot(x[u][sl, dv:].astype(BF16), S16[u])).astype(BF16) for u in range(nu)]
        for u in range(nu):
            outs[u].append(_dot(qg[u][sl], S16[u]) + _dot(attn[u][sl, cidx * C:(cidx + 1) * C], vn16[u]))
        S = [S[u] * egl[u][cidx * C:cidx * C + 1] + _dot_tn(kd[u][sl], vn16[u]) for u in range(nu)]
    for u, (s, h) in enumerate(units):
        s_ref[s, h] = S[u]
        o = _rms_rows(jnp.concatenate(outs[u], axis=0), ng_ref[...])
        o_ref[s, :, h * dv:(h + 1) * dv] = o * jax.nn.silu(z_ref[s, :, h * dv:(h + 1) * dv])

    @pl.when(j == pl.num_programs(1) - 1)
    def _():
        s_out_ref[...] = s_ref[...]


def _gdn_prompt(qkv, z, gates_col, gates_row, norm_g, B, L, nh, dk, dv):
    M, CD = qkv.shape
    T = 2 * GDN_CHUNK
    nb = L // T
    nseq = 4 if B % 4 == 0 else (2 if B % 2 == 0 else 1)
    kernel = functools.partial(_gdn_body, nh=nh, dk=dk, dv=dv, nseq=nseq)

    def row_map(s):
        return lambda p, j: (0, (p * nseq + s) * nb + j)

    per_seq = lambda p, j: (p, j, 0)
    go, S = pl.pallas_call(
        kernel,
        out_shape=[jax.ShapeDtypeStruct((B, L, nh * dv), F32), jax.ShapeDtypeStruct((B, nh, dk, dv), F32)],
        grid=(B // nseq, nb),
        in_specs=[
            pl.BlockSpec((nseq, T, CD), per_seq),
            pl.BlockSpec((nseq, T, nh * dv), per_seq),
            pl.BlockSpec((nseq, T, LANES), per_seq),
        ] + [pl.BlockSpec((16, T), row_map(s)) for s in range(nseq)] + [
            pl.BlockSpec((1, dv), lambda p, j: (0, 0)),
        ],
        out_specs=[
            pl.BlockSpec((nseq, T, nh * dv), per_seq),
            pl.BlockSpec((nseq, nh, dk, dv), lambda p, j: (p, 0, 0, 0)),
        ],
        scratch_shapes=[pltpu.VMEM((nseq, nh, dk, dv), F32), pltpu.VMEM((2, T, T), BF16)],
        compiler_params=_cparams(2),
        name="gdn_prompt",
    )(qkv.reshape(B, L, CD), z.reshape(B, L, nh * dv), gates_col.reshape(B, L, LANES),
      *([gates_row] * nseq), norm_g.reshape(1, dv))
    return go.reshape(M, nh * dv), S


def _rope(x, cos2, sin2, nh, dk):
    outs = []
    for h in range(nh):
        xh = x[:, h * dk:(h + 1) * dk]
        outs.append(xh * cos2 + pltpu.roll(xh, dk // 2, 1) * sin2)
    return outs


def _rope_tables(pos, freqs):
    ang = pos * freqs
    cos, sin = jnp.cos(ang), jnp.sin(ang)
    return jnp.concatenate([cos, cos], axis=-1), jnp.concatenate([-sin, sin], axis=-1)


def _head_layernorm(o):
    mu = jnp.mean(o, axis=-1, keepdims=True)
    oc = o - mu
    return oc * lax.rsqrt(jnp.mean(oc * oc, axis=-1, keepdims=True) + EPS)


def _log_gamma(h):
    return math.log(1.0 - 2.0 ** (-5.0 - h))


def _rope_table_body(fr_ref, cos_o, sin_o):
    C = cos_o.shape[0]
    rowi = lax.broadcasted_iota(jnp.int32, (C, 1), 0)
    pos = (pl.program_id(0) * C + rowi).astype(F32)
    cos_o[...], sin_o[...] = _rope_tables(pos, fr_ref[...])


def _rope_table(L, freqs):
    C = min(L, 512)
    dk = 2 * freqs.shape[1]
    return pl.pallas_call(
        _rope_table_body,
        out_shape=[jax.ShapeDtypeStruct((L, dk), F32)] * 2,
        grid=(L // C,),
        in_specs=[pl.BlockSpec(freqs.shape, lambda c: (0, 0))],
        out_specs=[pl.BlockSpec((C, dk), lambda c: (c, 0))] * 2,
        compiler_params=_cparams(1),
        name="rope_table",
    )(freqs)


def _ret_body(q_ref, k_ref, v_ref, sg_ref, gn_ref, o_ref, s_out_ref, s_ref, dmat_ref, *, nh, dk, dv):
    b = pl.program_id(0)
    j = pl.program_id(1)
    nseq, C = q_ref.shape[0], q_ref.shape[1]
    rowf = lax.broadcasted_iota(jnp.int32, (C, 1), 0).astype(F32)

    @pl.when((b == 0) & (j == 0))
    def _():
        ri = lax.broadcasted_iota(jnp.int32, (C, C), 0)
        ci = lax.broadcasted_iota(jnp.int32, (C, C), 1)
        diff = (ri - ci).astype(F32)
        for h in range(nh):
            dmat_ref[h] = jnp.exp(jnp.where(ri >= ci, diff * _log_gamma(h), NEG))

    @pl.when(j == 0)
    def _():
        s_ref[...] = jnp.zeros_like(s_ref)

    units = [(s, h) for s in range(nseq) for h in range(nh)]
    nu = range(len(units))
    q16 = [q_ref[s, :, h * dk:(h + 1) * dk] for s, h in units]
    k16 = [k_ref[s, :, h * dk:(h + 1) * dk] for s, h in units]
    v = [v_ref[s, :, h * dv:(h + 1) * dv] for s, h in units]
    S = [s_ref[s, h] for s, h in units]
    qk = [_dot_nt(q16[u], k16[u]) for u in nu]
    cross = [_dot(q16[u], S[u].astype(BF16)) for u in nu]
    for u, (s, h) in enumerate(units):
        vd = (v[u] * jnp.exp(_log_gamma(h) * (C - 1.0 - rowf))).astype(BF16)
        s_ref[s, h] = S[u] * math.exp(_log_gamma(h) * C) + _dot_tn(k16[u], vd)
    inner = [_dot((qk[u] * dmat_ref[h]).astype(BF16), v[u].astype(BF16)) for u, (s, h) in enumerate(units)]
    for u, (s, h) in enumerate(units):
        o = inner[u] + cross[u] * jnp.exp(_log_gamma(h) * (rowf + 1.0))
        o = _head_layernorm(o) * gn_ref[:, h * dv:(h + 1) * dv]
        o_ref[s, :, h * dv:(h + 1) * dv] = sg_ref[s, :, h * dv:(h + 1) * dv] * o

    @pl.when(j == pl.num_programs(1) - 1)
    def _():
        s_out_ref[...] = s_ref[...]


def _ret_prompt(q16, k16, v, sgate, gnorm, B, L, nh, dk, dv):
    M = q16.shape[0]
    C = 128
    nb = L // C
    nseq = 4 if B % 4 == 0 else (2 if B % 2 == 0 else 1)
    kernel = functools.partial(_ret_body, nh=nh, dk=dk, dv=dv)
    tok = lambda p, j: (p, j, 0)
    y, S = pl.pallas_call(
        kernel,
        out_shape=[jax.ShapeDtypeStruct((B, L, nh * dv), F32), jax.ShapeDtypeStruct((B, nh, dk, dv), F32)],
        grid=(B // nseq, nb),
        in_specs=[
            pl.BlockSpec((nseq, C, nh * dk), tok),
            pl.BlockSpec((nseq, C, nh * dk), tok),
            pl.BlockSpec((nseq, C, nh * dv), tok),
            pl.BlockSpec((nseq, C, nh * dv), tok),
            pl.BlockSpec((1, nh * dv), lambda p, j: (0, 0)),
        ],
        out_specs=[
            pl.BlockSpec((nseq, C, nh * dv), tok),
            pl.BlockSpec((nseq, nh, dk, dv), lambda p, j: (p, 0, 0, 0)),
        ],
        scratch_shapes=[pltpu.VMEM((nseq, nh, dk, dv), F32), pltpu.VMEM((nh, C, C), F32)],
        compiler_params=_cparams(2),
        name="ret_prompt",
    )(q16.reshape(B, L, nh * dk), k16.reshape(B, L, nh * dk), v.reshape(B, L, nh * dv),
      sgate.reshape(B, L, nh * dv), gnorm.reshape(1, nh * dv))
    return y.reshape(M, nh * dv), S


def _page_sums_body(x_ref, o_ref, op_ref):
    _, nh, plen = x_ref.shape
    n = nh * plen
    wide = o_ref.shape[1]

    @pl.when(pl.program_id(0) == 0)
    def _():
        kk = lax.broadcasted_iota(jnp.int32, (plen, wide), 0)
        cc = lax.broadcasted_iota(jnp.int32, (plen, wide), 1)
        key_c = cc >> (nh.bit_length() - 1)
        later_or_total = (cc >= n) | (kk > key_c)
        for h in range(nh):
            op_ref[h] = (((cc & (nh - 1)) == h) & later_or_total).astype(BF16)

    acc = None
    for h in range(nh):
        for part in _split2(x_ref[:, h, :]):
            d = _dot(part, op_ref[h])
            acc = d if acc is None else acc + d
    o_ref[...] = acc


def _page_sums(lf_pages):
    P, nh, plen = lf_pages.shape
    wide = nh * plen + LANES
    tp = 512 if P % 512 == 0 else P
    return pl.pallas_call(
        _page_sums_body,
        out_shape=jax.ShapeDtypeStruct((P, wide), F32),
        grid=(P // tp,),
        in_specs=[pl.BlockSpec((tp, nh, plen), lambda p: (p, 0, 0))],
        out_specs=pl.BlockSpec((tp, wide), lambda p: (p, 0)),
        scratch_shapes=[pltpu.VMEM((nh, plen, wide), BF16)],
        compiler_params=_cparams(1),
        name="page_sums",
    )(lf_pages)


def _paged_body(pt_ref, *refs, G, nh, scale):
    k_refs = refs[:G]
    v_refs = refs[G:2 * G]
    sums_ref, q_ref, kn_ref, vn_ref, lfn_ref, o_ref, m_ref, l_ref, acc_ref, carry_ref = refs[2 * G:]
    b_idx = pl.program_id(0)
    s_idx = pl.program_id(1)
    ns = pl.num_programs(1)
    n = k_refs[0].shape[0]
    dh = q_ref.shape[1]

    @pl.when(s_idx == 0)
    def _():
        m_ref[...] = jnp.full_like(m_ref, NEG)
        l_ref[...] = jnp.zeros_like(l_ref)
        acc_ref[...] = jnp.zeros_like(acc_ref)
        carry_ref[...] = jnp.zeros_like(carry_ref)

    q = q_ref[...]
    q8 = jnp.concatenate([q, jnp.zeros((8 - nh, dh), F32)], axis=0).astype(BF16)
    lfn = jnp.concatenate([lfn_ref[...], jnp.zeros((8 - nh, 1), F32)], axis=0)
    hrow = lax.broadcasted_iota(jnp.int32, (8, n), 0)
    hcol = lax.broadcasted_iota(jnp.int32, (8, n), 1) & (nh - 1)
    match = hrow == hcol
    carry = carry_ref[...]
    s_parts = [None] * G
    for i in reversed(range(G)):
        page_id = pt_ref[b_idx, (ns - 1 - s_idx) * G + i]
        wt = sums_ref[pl.ds(page_id, 1), :]
        bias = wt[:, :n] + carry
        carry = carry + jnp.concatenate([wt[:, n:]] * (n // LANES), axis=1)
        s = _dot_nt(q8, k_refs[i][...].astype(BF16)) * scale + bias + lfn
        s_parts[i] = jnp.where(match, s, NEG)
    carry_ref[...] = carry
    m_run, l_run, acc_run = m_ref[...], l_ref[...], acc_ref[...]
    half = max(G // 2, 1)
    for grp in (range(half, G), range(0, half)) if G > 1 else (range(G),):
        sg = [s_parts[i] for i in grp]
        m_new = jnp.maximum(m_run, jnp.max(functools.reduce(jnp.maximum, sg), axis=-1, keepdims=True))
        alpha = jnp.exp(m_run - m_new)
        pg = [jnp.exp(s - m_new) for s in sg]
        l_run = alpha * l_run + jnp.sum(functools.reduce(jnp.add, pg), axis=-1, keepdims=True)
        pv = functools.reduce(jnp.add, [_dot(p.astype(BF16), v_refs[i][...].astype(BF16)) for p, i in zip(pg, grp)])
        acc_run = alpha * acc_run + pv
        m_run = m_new
    m_ref[...], l_ref[...], acc_ref[...] = m_run, l_run, acc_run

    @pl.when(s_idx == pl.num_programs(1) - 1)
    def _():
        qf = q.astype(BF16).astype(F32)
        kn = kn_ref[...].astype(BF16).astype(F32)
        s_new = jnp.sum(qf * kn, axis=-1, keepdims=True) * scale
        m_prev = m_ref[0:nh, :]
        m_new = jnp.maximum(m_prev, s_new)
        alpha = jnp.exp(m_prev - m_new)
        p_new = jnp.exp(s_new - m_new)
        l_fin = alpha * l_ref[0:nh, :] + p_new
        vn = vn_ref[...].astype(BF16).astype(F32)
        num = alpha * acc_ref[0:nh, :] + p_new.astype(BF16).astype(F32) * vn
        o_ref[...] = num / l_fin


def _fox_sample(q, k_new, v_new, lf_new, k_pages, v_pages, page_sums, page_table, nh):
    Bn, _, dh = q.shape
    n_pages = page_table.shape[1]
    n_pool, n, _ = k_pages.shape
    G = 16 if n_pages % 16 == 0 else (8 if n_pages % 8 == 0 else 1)
    ns = n_pages // G

    def page_map(i):
        return lambda b, s, pt: (pt[b, (ns - 1 - s) * G + i], 0, 0)

    per_b = lambda b, s, pt: (b, 0, 0)
    grid_spec = pltpu.PrefetchScalarGridSpec(
        num_scalar_prefetch=1,
        grid=(Bn, ns),
        in_specs=[pl.BlockSpec((None, n, dh), page_map(i)) for i in range(G)]
        + [pl.BlockSpec((None, n, dh), page_map(i)) for i in range(G)]
        + [pl.BlockSpec(page_sums.shape, lambda b, s, pt: (0, 0)),
           pl.BlockSpec((None, nh, dh), per_b), pl.BlockSpec((None, nh, dh), per_b),
           pl.BlockSpec((None, nh, dh), per_b), pl.BlockSpec((None, nh, 1), per_b)],
        out_specs=pl.BlockSpec((None, nh, dh), per_b),
        scratch_shapes=[pltpu.VMEM((8, 1), F32), pltpu.VMEM((8, 1), F32), pltpu.VMEM((8, dh), F32),
                        pltpu.VMEM((1, n), F32)],
    )
    return pl.pallas_call(
        functools.partial(_paged_body, G=G, nh=nh, scale=dh ** -0.5),
        out_shape=jax.ShapeDtypeStruct((Bn, nh, dh), F32),
        grid_spec=grid_spec,
        compiler_params=_cparams(2),
        name="fox_sample",
    )(page_table, *([k_pages] * G), *([v_pages] * G), page_sums, q, k_new, v_new, lf_new)


def _pad8(row):
    return jnp.concatenate([row, jnp.zeros((8 - row.shape[0], row.shape[1]), row.dtype)], axis=0)


def _gdn_sample_body(cin_ref, prev_ref, z_ref, gt_ref, cw_ref, ng_ref, s0_ref, o_ref, cnew_ref, s_out_ref, *, nh, dk, dv):
    nb = cin_ref.shape[0]
    w = cw_ref[...]
    kw = w.shape[0]
    convs, gts = [], []
    for b in range(nb):
        u = cin_ref[b]
        prev = prev_ref[b]
        conv = u * w[kw - 1:kw]
        for i in range(kw - 1):
            conv = conv + prev[i:i + 1] * w[i:i + 1]
        cnew_ref[b, 0:kw - 2, :] = prev[1:kw - 1]
        cnew_ref[b, kw - 2:kw - 1, :] = u
        convs.append(jax.nn.silu(conv))
        gts.append(gt_ref[b])
    units = [(b, h) for b in range(nb) for h in range(nh)]
    nu = range(len(units))
    q = [_l2_rows(convs[b][:, h * dk:(h + 1) * dk]) * dk ** -0.5 for b, h in units]
    k = [_l2_rows(convs[b][:, nh * dk + h * dk:nh * dk + (h + 1) * dk]) for b, h in units]
    v = [convs[b][:, 2 * nh * dk + h * dv:2 * nh * dk + (h + 1) * dv] for b, h in units]
    beta = [gts[b][:, 2 * nh + h:2 * nh + h + 1] for b, h in units]
    eg = [jnp.exp(gts[b][:, nh + h:nh + h + 1]) for b, h in units]
    S = [s0_ref[b, h] for b, h in units]
    r = [_dot(_pad8(jnp.concatenate([k[u] * beta[u] * eg[u], q[u] * eg[u]], axis=0)).astype(BF16), S[u].astype(BF16))
         for u in nu]
    vn16 = [(v[u] * beta[u] - r[u][0:1]).astype(BF16) for u in nu]
    for u, (b, h) in enumerate(units):
        s_out_ref[b, h] = S[u] * eg[u] + _dot_tn(_pad8(k[u]).astype(BF16), _pad8(vn16[u]))
    for u, (b, h) in enumerate(units):
        qk = jnp.sum(q[u].astype(BF16).astype(F32) * k[u].astype(BF16).astype(F32), axis=-1, keepdims=True)
        o = r[u][1:2] + qk.astype(BF16).astype(F32) * vn16[u].astype(F32)
        o = _rms_rows(o, ng_ref[...])
        o_ref[b, :, h * dv:(h + 1) * dv] = o * jax.nn.silu(z_ref[b, :, h * dv:(h + 1) * dv])


def _gdn_sample(conv_in, conv_prev, z, gates_col, conv_w, norm_g, S0, nh, dk, dv):
    Bn, _, CD = conv_in.shape
    kw = conv_w.shape[0]
    nb = 4 if Bn % 4 == 0 else 1
    per_b3 = lambda b: (b, 0, 0)
    per_b4 = lambda b: (b, 0, 0, 0)
    kernel = functools.partial(_gdn_sample_body, nh=nh, dk=dk, dv=dv)
    return pl.pallas_call(
        kernel,
        out_shape=[jax.ShapeDtypeStruct((Bn, 1, nh * dv), F32), jax.ShapeDtypeStruct((Bn, kw - 1, CD), F32),
                   jax.ShapeDtypeStruct((Bn, nh, dk, dv), F32)],
        grid=(Bn // nb,),
        in_specs=[
            pl.BlockSpec((nb, 1, CD), per_b3),
            pl.BlockSpec((nb, kw - 1, CD), per_b3),
            pl.BlockSpec((nb, 1, nh * dv), per_b3),
            pl.BlockSpec((nb, 1, LANES), per_b3),
            pl.BlockSpec(conv_w.shape, lambda b: (0, 0)),
            pl.BlockSpec((1, dv), lambda b: (0, 0)),
            pl.BlockSpec((nb, nh, dk, dv), per_b4),
        ],
        out_specs=[
            pl.BlockSpec((nb, 1, nh * dv), per_b3),
            pl.BlockSpec((nb, kw - 1, CD), per_b3),
            pl.BlockSpec((nb, nh, dk, dv), per_b4),
        ],
        compiler_params=_cparams(1),
        name="gdn_sample",
    )(conv_in, conv_prev, z, gates_col, conv_w, norm_g.reshape(1, dv), S0)


def _ret_sample_body(q_ref, k_ref, v_ref, gate_ref, fr_ref, gn_ref, s0_ref, o_ref, s_out_ref, *, nh, dk, dv, pos):
    nb = q_ref.shape[0]
    cos2, sin2 = _rope_tables(jnp.full((1, 1), pos, F32), fr_ref[...])
    units = [(b, h) for b in range(nb) for h in range(nh)]
    qs, ks = [], []
    for b in range(nb):
        qs += _rope(q_ref[b], cos2, sin2, nh, dk)
        ks += [kh * dk ** -0.5 for kh in _rope(k_ref[b], cos2, sin2, nh, dk)]
    v16 = [v_ref[b, :, h * dv:(h + 1) * dv].astype(BF16) for b, h in units]
    S = [s0_ref[b, h] for b, h in units]
    cross = [_dot(_pad8(qs[u] * math.exp(_log_gamma(h))).astype(BF16), S[u].astype(BF16))[0:1]
             for u, (b, h) in enumerate(units)]
    for u, (b, h) in enumerate(units):
        s_out_ref[b, h] = S[u] * math.exp(_log_gamma(h)) + _dot_tn(_pad8(ks[u]).astype(BF16), _pad8(v16[u]))
    for u, (b, h) in enumerate(units):
        qk = jnp.sum(qs[u].astype(BF16).astype(F32) * ks[u].astype(BF16).astype(F32), axis=-1, keepdims=True)
        inner = qk.astype(BF16).astype(F32) * v16[u].astype(F32)
        o = _head_layernorm(inner + cross[u]) * gn_ref[:, h * dv:(h + 1) * dv]
        o_ref[b, :, h * dv:(h + 1) * dv] = jax.nn.silu(gate_ref[b, :, h * dv:(h + 1) * dv]) * o


def _ret_sample(q, k, v, gate, freqs, gnorm, S0, pos, nh, dk, dv):
    Bn = q.shape[0]
    nb = 2 if Bn % 2 == 0 else 1
    per_b3 = lambda b: (b, 0, 0)
    per_b4 = lambda b: (b, 0, 0, 0)
    kernel = functools.partial(_ret_sample_body, nh=nh, dk=dk, dv=dv, pos=float(pos))
    return pl.pallas_call(
        kernel,
        out_shape=[jax.ShapeDtypeStruct((Bn, 1, nh * dv), F32), jax.ShapeDtypeStruct((Bn, nh, dk, dv), F32)],
        grid=(Bn // nb,),
        in_specs=[
            pl.BlockSpec((nb, 1, nh * dk), per_b3),
            pl.BlockSpec((nb, 1, nh * dk), per_b3),
            pl.BlockSpec((nb, 1, nh * dv), per_b3),
            pl.BlockSpec((nb, 1, nh * dv), per_b3),
            pl.BlockSpec((1, dk // 2), lambda b: (0, 0)),
            pl.BlockSpec((1, nh * dv), lambda b: (0, 0)),
            pl.BlockSpec((nb, nh, dk, dv), per_b4),
        ],
        out_specs=[
            pl.BlockSpec((nb, 1, nh * dv), per_b3),
            pl.BlockSpec((nb, nh, dk, dv), per_b4),
        ],
        compiler_params=_cparams(1),
        name="ret_sample",
    )(q, k, v, gate, freqs, gnorm.reshape(1, nh * dv), S0)


def kernel(x_prompt, x_sample, cache_fox_k, cache_fox_v, cache_fox_logf, page_table, state_gdn_conv, state_gdn_S, state_ret_S, norm_g, final_norm_g, ffn_w_gu, ffn_w_down, ab_w_in, ab_w_out, fox_b_f, gdn_conv_w, gdn_A_log, gdn_dt_bias, gdn_norm_g, c_w_in, c_w_out, ret_norm_g):
    B, L, D = x_prompt.shape
    Bn, Ls, _ = x_sample.shape
    assert Ls == 1, "the sample group decodes one token per sequence"
    depth = norm_g.shape[0]
    _, n_pool, page, fh, fd = cache_fox_k.shape
    fw = fh * fd
    _, _, gh, gdk, gdv = state_gdn_S.shape
    conv_dim = gdn_conv_w.shape[2]
    kw = gdn_conv_w.shape[1]
    _, _, rh, rdk, rdv = state_ret_S.shape
    n_pages = page_table.shape[1]
    assert fh == gh and 3 * fh <= 16 and fh & (fh - 1) == 0

    xp = x_prompt.reshape(B * L, D)
    xs = x_sample.reshape(Bn, D)
    w_gu00 = _to_bf16(ffn_w_gu, (0, 0))
    w_down00 = _to_bf16(ffn_w_down, (0, 0))
    freqs = (ROPE_BASE ** (-jnp.arange(rdk // 2, dtype=F32) / (rdk // 2))).reshape(1, rdk // 2)
    cos2, sin2 = _rope_table(L, freqs)

    fkp, fvp, flp, fks, fvs, fls = [], [], [], [], [], []
    gcp, gsp, gcs, gss = [], [], [], []
    rsp, rss = [], []
    for li in range(depth):
        if li == 0:
            xs = _ffn(xs, norm_g[li, 0], w_gu00, w_down00, 0, 0)
            xp, w_gu16, w_down16 = _ffn(xp, norm_g[li, 0], w_gu00, w_down00, 0, 0, cast=(ffn_w_gu, ffn_w_down))
        else:
            xp = _ffn(xp, norm_g[li, 0], w_gu16, w_down16, li, 0)
            xs = _ffn(xs, norm_g[li, 0], w_gu16, w_down16, li, 0)
        j = li // 2
        if li % 2 == 0:
            o0 = 3 * fw
            zw = gh * gdv
            c0 = o0 + fh
            widths = (conv_dim, fw, fw, fw, zw, LANES)
            w_cat = _regroup_bf16(ab_w_in[j].T, [(c0, conv_dim), (0, o0), (c0 + conv_dim, zw), (o0, fh),
                                                 (c0 + conv_dim + zw, 2 * gh)], sum(widths))
            w_out16 = ab_w_out[j].astype(BF16)

            gate_p = _gate_params(fox_b_f[j], gdn_dt_bias[j], gdn_A_log[j])
            gqkv, q, k, v, z, gates, k3, v3, tails = _ab_proj_prompt(
                xp, norm_g[li, 1], w_cat, gdn_conv_w[j], gate_p, B, L, widths, fh, fd, gh, gdk)
            gates_row = gates[:, :16].T
            c_rows = _cumsum_rows(gates_row, L)[:fh].reshape(fh, B, L).transpose(1, 0, 2).reshape(B * fh, 1, L)
            fo = _fox_prompt(q, k, v, c_rows, B, L, fh)
            go, S_p = _gdn_prompt(gqkv, z, gates, gates_row, gdn_norm_g[j], B, L, gh, gdk, gdv)
            pre_p = ([fo, go], w_out16)
            fkp.append(k3.reshape(B, L, fh, fd))
            fvp.append(v3.reshape(B, L, fh, fd))
            flp.append(gates[:, :fh].reshape(B, L, fh))
            gcp.append(tails[:, 8 - (kw - 1):])
            gsp.append(S_p)

            cin, q, k, v, z, small = _rms_proj(xs, norm_g[li, 1], w_cat, widths)
            gates = _gates(small, gate_p, fh)
            sums = _page_sums(jnp.transpose(cache_fox_logf[j], (0, 2, 1)))
            fo = _fox_sample(q.reshape(Bn, fh, fd), k.reshape(Bn, fh, fd), v.reshape(Bn, fh, fd),
                             gates[:, :fh].reshape(Bn, fh, 1),
                             cache_fox_k[j].reshape(n_pool, page * fh, fd), cache_fox_v[j].reshape(n_pool, page * fh, fd),
                             sums, page_table, fh)
            go, conv_s, S_s = _gdn_sample(cin.reshape(Bn, 1, conv_dim), state_gdn_conv[j], z.reshape(Bn, 1, gh * gdv),
                                          gates.reshape(Bn, 1, LANES), gdn_conv_w[j], gdn_norm_g[j], state_gdn_S[j],
                                          gh, gdk, gdv)
            pre_s = ([fo.reshape(Bn, fw), go.reshape(Bn, gh * gdv)], w_out16)
            fks.append(k.reshape(Bn, 1, fh, fd))
            fvs.append(v.reshape(Bn, 1, fh, fd))
            fls.append(gates[:, :fh].reshape(Bn, 1, fh))
            gcs.append(conv_s)
            gss.append(S_s)
        else:
            w_in16 = _to_bf16(c_w_in[j])
            qk_w, v_w = rh * rdk, rh * rdv
            widths = (qk_w, qk_w, v_w, v_w)
            w_out16 = c_w_out[j].astype(BF16)

            q16, k16, v, sgate = _c_proj_prompt(xp, norm_g[li, 1], w_in16, cos2, sin2, L, rh, rdk, v_w)
            y, R_p = _ret_prompt(q16, k16, v, sgate, ret_norm_g[j], B, L, rh, rdk, rdv)
            pre_p = ([y], w_out16)
            rsp.append(R_p)

            q, k, v, gate = _rms_proj(xs, norm_g[li, 1], w_in16, widths)
            y, R_s = _ret_sample(q.reshape(Bn, 1, qk_w), k.reshape(Bn, 1, qk_w), v.reshape(Bn, 1, v_w),
                                 gate.reshape(Bn, 1, v_w), freqs, ret_norm_g[j], state_ret_S[j],
                                 n_pages * page, rh, rdk, rdv)
            pre_s = ([y.reshape(Bn, v_w)], w_out16)
            rss.append(R_s)
        final_g = final_norm_g if li == depth - 1 else None
        xp = _ffn(xp, norm_g[li, 2], w_gu16, w_down16, li, 1, final_g, pre_p)
        xs = _ffn(xs, norm_g[li, 2], w_gu16, w_down16, li, 1, final_g, pre_s)
    y_prompt = xp.reshape(B, L, D)
    y_sample = xs.reshape(Bn, 1, D)
    return (y_prompt, y_sample,
            jnp.stack(fkp), jnp.stack(fvp), jnp.stack(flp),
            jnp.stack(fks), jnp.stack(fvs), jnp.stack(fls),
            jnp.stack(gcp), jnp.stack(gsp), jnp.stack(gcs), jnp.stack(gss),
            jnp.stack(rsp), jnp.stack(rss))
```

```python
import functools
import math

import jax
import jax.numpy as jnp
from jax import lax
from jax.experimental import pallas as pl
from jax.experimental.pallas import tpu as pltpu

F32 = jnp.float32
BF16 = jnp.bfloat16
EPS = 1e-6
ROPE_BASE = 10000.0
NEG = -1e30
LANES = 128
GDN_CHUNK = 64
VMEM_LIMIT = 56 * 1024 * 1024
HI = lax.Precision.HIGHEST


def _cparams(n_axes):
    return pltpu.CompilerParams(dimension_semantics=("arbitrary",) * n_axes,
                                vmem_limit_bytes=VMEM_LIMIT)


def _dot(a, b, precision=None):
    return jnp.dot(a, b, preferred_element_type=F32, precision=precision)


def _dot_nt(a, b, precision=None):
    return lax.dot_general(a, b, (((1,), (1,)), ((), ())), preferred_element_type=F32, precision=precision)


def _dot_tn(a, b, precision=None):
    return lax.dot_general(a, b, (((0,), (0,)), ((), ())), preferred_element_type=F32, precision=precision)


def _rms_rows(x, g):
    return x * lax.rsqrt(jnp.mean(x * x, axis=-1, keepdims=True) + EPS) * g


MXU_DEPTH = 256


def _ffn_body(*refs, F, chunk, final_norm, n_pre, n_cast):
    x_ref, g_ref, wgu_ref, wd_ref = refs[:4]
    a_refs = refs[4:4 + n_pre]
    rest = refs[4 + n_pre:]
    if n_cast:
        for src, dst in zip(rest[len(rest) - 2 * n_cast - 1:len(rest) - n_cast - 1], rest[len(rest) - n_cast:]):
            dst[...] = src[...].astype(dst.dtype)
        rest = rest[:len(rest) - 2 * n_cast - 1] + (rest[len(rest) - n_cast - 1],)
    x = x_ref[...]
    if n_pre:
        wo_ref, rest = rest[0], rest[1:]
        row = 0
        for a_ref in a_refs:
            n = a_ref.shape[1]
            x = x + _dot(a_ref[...].astype(BF16), wo_ref[row:row + n, :])
            row += n
    if final_norm:
        gf_ref, rest = rest[0], rest[1:]
    o_ref = rest[0]
    xn = _rms_rows(x, g_ref[...]).astype(BF16)
    acc = None
    for c0 in range(0, F, chunk):
        c1 = min(F, c0 + chunk)
        a = _dot(xn, wgu_ref[:, c0:c1])
        b = _dot(xn, wgu_ref[:, F + c0:F + c1])
        h = (jax.nn.silu(a) * b).astype(BF16)
        d = _dot(h, wd_ref[c0:c1, :])
        acc = d if acc is None else acc + d
    y = x + 0.5 * acc
    o_ref[...] = _rms_rows(y, gf_ref[...]) if final_norm else y


def _ffn(x, g, w_gu, w_down, li, j, final_g=None, pre=None, cast=()):
    M, D = x.shape
    F = w_down.shape[2]
    tm = min(M, 512)
    final_norm = final_g is not None
    const = lambda m: (0, 0)
    in_specs = [pl.BlockSpec((tm, D), lambda m: (m, 0)), pl.BlockSpec((1, D), const),
                pl.BlockSpec((None, None, D, 2 * F), lambda m: (li, j, 0, 0)),
                pl.BlockSpec((None, None, F, D), lambda m: (li, j, 0, 0))]
    args = [x, g.reshape(1, D), w_gu, w_down]
    acts, w_out = pre if pre is not None else ((), None)
    if acts:
        assert sum(a.shape[1] for a in acts) == w_out.shape[0]
        in_specs += [pl.BlockSpec((tm, a.shape[1]), lambda m: (m, 0)) for a in acts]
        in_specs.append(pl.BlockSpec(w_out.shape, const))
        args += [*acts, w_out]
    if final_norm:
        in_specs.append(pl.BlockSpec((1, D), const))
        args.append(final_g.reshape(1, D))
    out_shape = [jax.ShapeDtypeStruct((M, D), F32)]
    out_specs = [pl.BlockSpec((tm, D), lambda m: (m, 0))]
    steps = M // tm
    for w in cast:
        cols = w.shape[-1]
        rows = w.size // cols
        assert rows % (16 * steps) == 0, "a weight slab per grid step must hold whole bfloat16 tiles"
        in_specs.append(pl.BlockSpec((rows // steps, cols), lambda m: (m, 0)))
        args.append(w.reshape(rows, cols))
        out_shape.append(jax.ShapeDtypeStruct((rows, cols), BF16))
        out_specs.append(pl.BlockSpec((rows // steps, cols), lambda m: (m, 0)))
    outs = pl.pallas_call(
        functools.partial(_ffn_body, F=F, chunk=2 * MXU_DEPTH, final_norm=final_norm, n_pre=len(acts),
                          n_cast=len(cast)),
        out_shape=out_shape,
        grid=(steps,),
        in_specs=in_specs,
        out_specs=out_specs,
        compiler_params=_cparams(1),
        name="ffn",
    )(*args)
    return (outs[0], *[o.reshape(w.shape) for o, w in zip(outs[1:], cast)]) if cast else outs[0]


PROJ_CHUNK = 512


def _project(xn, w_ref, col, o_ref, o3_ref=None):
    n = o_ref.shape[1]
    for s0 in range(0, n, PROJ_CHUNK):
        s1 = min(n, s0 + PROJ_CHUNK)
        val = _dot(xn, w_ref[:, col + s0:col + s1])
        o_ref[:, s0:s1] = val
        if o3_ref is not None:
            dh = o3_ref.shape[2]
            nhh = (s1 - s0) // dh
            o3_ref[:, s0 // dh:s0 // dh + nhh, :] = val.reshape(val.shape[0], nhh, dh)
    return col + n


def _proj_body(x_ref, g_ref, w_ref, *o_refs):
    xn = _rms_rows(x_ref[...], g_ref[...]).astype(BF16)
    col = 0
    for o_ref in o_refs:
        col = _project(xn, w_ref, col, o_ref)


def _rms_proj(x, g, w, widths):
    M, D = x.shape
    tm = min(M, 256)
    assert sum(widths) == w.shape[1] and all(n % LANES == 0 for n in widths)
    return pl.pallas_call(
        _proj_body,
        out_shape=[jax.ShapeDtypeStruct((M, n), F32) for n in widths],
        grid=(M // tm,),
        in_specs=[pl.BlockSpec((tm, D), lambda m: (m, 0)), pl.BlockSpec((1, D), lambda m: (0, 0)),
                  pl.BlockSpec(w.shape, lambda m: (0, 0))],
        out_specs=[pl.BlockSpec((tm, n), lambda m: (m, 0)) for n in widths],
        compiler_params=_cparams(1),
        name="rms_proj",
    )(x, g.reshape(1, D), w)


def _l2_rows(x):
    return x * lax.rsqrt(jnp.sum(x * x, axis=-1, keepdims=True) + EPS)


def _ab_proj_body(x_ref, g_ref, w_ref, cw_ref, gp_ref, qkv_o, q_o, k_o, v_o, z_o, gates_o, k3_o, v3_o, tail_o,
                  win_ref, *, tiles_per_seq, nh, dk):
    m = pl.program_id(0)
    tm = x_ref.shape[0]
    cd = qkv_o.shape[1]
    xn = _rms_rows(x_ref[...], g_ref[...]).astype(BF16)

    @pl.when(lax.rem(m, tiles_per_seq) == 0)
    def _():
        win_ref[0:8, :] = jnp.zeros((8, cd), F32)

    kw = cw_ref.shape[0]
    step = nh * dk

    def conv_product(s0):
        win_ref[8:8 + tm, s0:s0 + step] = _dot(xn, w_ref[:, s0:s0 + step])
        tail_o[:, s0:s0 + step] = win_ref[tm:tm + 8, s0:s0 + step]

    def conv_finish(s0):
        s1 = s0 + step
        conv = win_ref[8 - kw + 1:8 - kw + 1 + tm, s0:s1] * cw_ref[0:1, s0:s1]
        for i in range(1, kw):
            conv = conv + win_ref[8 - kw + 1 + i:8 - kw + 1 + i + tm, s0:s1] * cw_ref[i:i + 1, s0:s1]
        win_ref[0:8, s0:s1] = win_ref[tm:tm + 8, s0:s1]
        conv = jax.nn.silu(conv)
        if s0 >= 2 * step:
            qkv_o[:, s0:s1] = conv
        else:
            for h in range(nh):
                xh = _l2_rows(conv[:, h * dk:(h + 1) * dk])
                qkv_o[:, s0 + h * dk:s0 + (h + 1) * dk] = xh * dk ** -0.5 if s0 == 0 else xh

    assert cd == 3 * step
    conv_product(0)
    conv_product(step)
    col = _project(xn, w_ref, cd, q_o)
    conv_finish(0)
    conv_product(2 * step)
    col = _project(xn, w_ref, col, k_o, k3_o)
    conv_finish(step)
    col = _project(xn, w_ref, col, v_o, v3_o)
    col = _project(xn, w_ref, col, z_o)
    conv_finish(2 * step)
    gates_o[...] = _gate_values(_dot(xn, w_ref[:, col:col + gates_o.shape[1]]), gp_ref, nh)


def _ab_proj_prompt(x, g, w, conv_w, gate_p, B, L, widths, fh, fd, gh, gdk):
    M, D = x.shape
    tm = min(L, 512)
    cd = widths[0]
    assert sum(widths) == w.shape[1] and L % tm == 0
    row = lambda m: (m, 0)
    const = lambda m: (0, 0)
    out_shape = [jax.ShapeDtypeStruct((M, n), F32) for n in widths]
    out_specs = [pl.BlockSpec((tm, n), row) for n in widths]
    for _ in range(2):
        out_shape.append(jax.ShapeDtypeStruct((M, fh, fd), F32))
        out_specs.append(pl.BlockSpec((tm, fh, fd), lambda m: (m, 0, 0)))
    out_shape.append(jax.ShapeDtypeStruct((B, 8, cd), F32))
    out_specs.append(pl.BlockSpec((None, 8, cd), lambda m: (m // (L // tm), 0, 0)))
    return pl.pallas_call(
        functools.partial(_ab_proj_body, tiles_per_seq=L // tm, nh=gh, dk=gdk),
        out_shape=out_shape,
        grid=(M // tm,),
        in_specs=[pl.BlockSpec((tm, D), row), pl.BlockSpec((1, D), const), pl.BlockSpec(w.shape, const),
                  pl.BlockSpec(conv_w.shape, const), pl.BlockSpec(gate_p.shape, const)],
        out_specs=out_specs,
        scratch_shapes=[pltpu.VMEM((8 + tm, cd), F32)],
        compiler_params=_cparams(1),
        name="ab_proj_prompt",
    )(x, g.reshape(1, D), w, conv_w, gate_p)


def _c_proj_body(x_ref, g_ref, w_ref, cos_ref, sin_ref, q_o, k_o, v_o, sg_o, *, nh, dk):
    xn = _rms_rows(x_ref[...], g_ref[...]).astype(BF16)
    cos2, sin2 = cos_ref[...], sin_ref[...]
    hpc = PROJ_CHUNK // dk
    for idx, o_ref in enumerate((q_o, k_o)):
        for h0 in range(0, nh, hpc):
            c0 = idx * nh * dk + h0 * dk
            val = _dot(xn, w_ref[:, c0:c0 + hpc * dk])
            for hh in range(hpc):
                xh = val[:, hh * dk:(hh + 1) * dk]
                xh = xh * cos2 + pltpu.roll(xh, dk // 2, 1) * sin2
                if idx == 1:
                    xh = xh * dk ** -0.5
                o_ref[:, (h0 + hh) * dk:(h0 + hh + 1) * dk] = xh.astype(BF16)
    col = _project(xn, w_ref, 2 * nh * dk, v_o)
    n = sg_o.shape[1]
    for s0 in range(0, n, PROJ_CHUNK):
        s1 = min(n, s0 + PROJ_CHUNK)
        sg_o[:, s0:s1] = jax.nn.silu(_dot(xn, w_ref[:, col + s0:col + s1]))


def _c_proj_prompt(x, g, w, cos2, sin2, L, nh, dk, v_w):
    M, D = x.shape
    tm = min(L, 512)
    qk_w = nh * dk
    row = lambda m: (m, 0)
    const = lambda m: (0, 0)
    pos = lambda m: (lax.rem(m, L // tm), 0)
    return pl.pallas_call(
        functools.partial(_c_proj_body, nh=nh, dk=dk),
        out_shape=[jax.ShapeDtypeStruct((M, qk_w), BF16), jax.ShapeDtypeStruct((M, qk_w), BF16),
                   jax.ShapeDtypeStruct((M, v_w), F32), jax.ShapeDtypeStruct((M, v_w), F32)],
        grid=(M // tm,),
        in_specs=[pl.BlockSpec((tm, D), row), pl.BlockSpec((1, D), const), pl.BlockSpec(w.shape, const),
                  pl.BlockSpec((tm, dk), pos), pl.BlockSpec((tm, dk), pos)],
        out_specs=[pl.BlockSpec((tm, qk_w), row), pl.BlockSpec((tm, qk_w), row),
                   pl.BlockSpec((tm, v_w), row), pl.BlockSpec((tm, v_w), row)],
        compiler_params=_cparams(1),
        name="c_proj_prompt",
    )(x, g.reshape(1, D), w, cos2, sin2)


def _cast_body(x_ref, o_ref):
    o_ref[...] = x_ref[...].astype(o_ref.dtype)


def _to_bf16(w, lead=None):
    cols = w.shape[-1]
    if lead is not None:
        rows = w.shape[2]
        tr = 512 if rows % 512 == 0 else rows
        a, b = lead
        in_spec = pl.BlockSpec((None, None, tr, cols), lambda r: (a, b, r, 0))
        src, out_shape = w, (1, 1, rows, cols)
    else:
        rows = w.size // cols
        tr = 512 if rows % 512 == 0 else rows
        in_spec = pl.BlockSpec((tr, cols), lambda r: (r, 0))
        src, out_shape = w.reshape(rows, cols), w.shape
    out = pl.pallas_call(
        _cast_body,
        out_shape=jax.ShapeDtypeStruct((rows, cols), BF16),
        grid=(rows // tr,),
        in_specs=[in_spec],
        out_specs=pl.BlockSpec((tr, cols), lambda r: (r, 0)),
        compiler_params=_cparams(1),
        name="to_bf16",
    )(src)
    return out.reshape(out_shape)


def _regroup_body(wt_ref, o_ref, *, pieces):
    N, D = wt_ref.shape
    col = 0
    narrow = []
    for start, width in pieces:
        if width % LANES == 0:
            assert not narrow and col % LANES == 0
            for r0 in range(0, width, PROJ_CHUNK):
                r1 = min(width, r0 + PROJ_CHUNK)
                o_ref[:, col + r0:col + r1] = wt_ref[start + r0:start + r1, :].T.astype(o_ref.dtype)
        else:
            w0 = min(start, N - LANES)
            narrow.append(wt_ref[w0:w0 + LANES, :].T[:, start - w0:start - w0 + width])
        col += width
    if narrow:
        first = col - sum(p.shape[1] for p in narrow)
        assert first % LANES == 0 and o_ref.shape[1] - first == LANES
        pad = jnp.zeros((D, o_ref.shape[1] - col), F32)
        o_ref[:, first:] = jnp.concatenate(narrow + [pad], axis=1).astype(o_ref.dtype)


def _regroup_bf16(wt, pieces, n_out):
    N, D = wt.shape
    return pl.pallas_call(
        functools.partial(_regroup_body, pieces=tuple(pieces)),
        out_shape=jax.ShapeDtypeStruct((D, n_out), BF16),
        grid=(1,),
        in_specs=[pl.BlockSpec((N, D), lambda r: (0, 0))],
        out_specs=pl.BlockSpec((D, n_out), lambda r: (0, 0)),
        compiler_params=_cparams(1),
        name="regroup_bf16",
    )(wt)


def _gate_values(s, p_ref, nh):
    s = s + p_ref[0:1, :]
    lane = lax.broadcasted_iota(jnp.int32, s.shape, 1)
    logf = jax.nn.log_sigmoid(s)
    g = -jnp.exp(p_ref[1:2, :]) * jax.nn.softplus(s)
    beta = jax.nn.sigmoid(s)
    return jnp.where(lane < nh, logf, jnp.where(lane < 2 * nh, g, jnp.where(lane < 3 * nh, beta, 0.0)))


def _gate_params(b_f, dt_bias, a_log):
    nh = b_f.shape[0]
    pad = jnp.zeros((LANES - 3 * nh,), F32)
    p = jnp.zeros((8, LANES), F32)
    p = p.at[0].set(jnp.concatenate([b_f, dt_bias, jnp.zeros((nh,), F32), pad]))
    return p.at[1].set(jnp.concatenate([jnp.zeros((nh,), F32), a_log, jnp.zeros((nh,), F32), pad]))


def _gates_body(s_ref, p_ref, o_ref, *, nh):
    o_ref[...] = _gate_values(s_ref[...], p_ref, nh)


def _gates(small, gate_p, nh):
    M = small.shape[0]
    tm = min(M, 2048)
    return pl.pallas_call(
        functools.partial(_gates_body, nh=nh),
        out_shape=jax.ShapeDtypeStruct((M, LANES), F32),
        grid=(M // tm,),
        in_specs=[pl.BlockSpec((tm, LANES), lambda m: (m, 0)), pl.BlockSpec((8, LANES), lambda m: (0, 0))],
        out_specs=pl.BlockSpec((tm, LANES), lambda m: (m, 0)),
        compiler_params=_cparams(1),
        name="gates",
    )(small, gate_p)


def _cumsum_body(x_ref, o_ref):
    L = x_ref.shape[1]
    r = lax.broadcasted_iota(jnp.int32, (LANES, LANES), 0)
    c = lax.broadcasted_iota(jnp.int32, (LANES, LANES), 1)
    upper = (r <= c).astype(F32)
    carry = jnp.zeros((x_ref.shape[0], 1), F32)
    for j in range(L // LANES):
        cs = _dot(x_ref[:, j * LANES:(j + 1) * LANES], upper, HI) + carry
        o_ref[:, j * LANES:(j + 1) * LANES] = cs
        carry = cs[:, LANES - 1:LANES]


def _cumsum_rows(x, seg):
    R, M = x.shape
    return pl.pallas_call(
        _cumsum_body,
        out_shape=jax.ShapeDtypeStruct((R, M), F32),
        grid=(M // seg,),
        in_specs=[pl.BlockSpec((R, seg), lambda b: (0, b))],
        out_specs=pl.BlockSpec((R, seg), lambda b: (0, b)),
        compiler_params=_cparams(1),
        name="cumsum_rows",
    )(x)


def _fox_body(q_ref, k_ref, v_ref, *rest, t, hg, dh, scale):
    nbg = q_ref.shape[0]
    c_refs = rest[:nbg]
    o_ref, vt_ref, cb_ref, acc_ref, m_ref = rest[nbg:]
    i = pl.program_id(2)
    L = k_ref.shape[1]
    log2e = math.log2(math.e)
    units = [(s, g) for s in range(nbg) for g in range(hg)]

    @pl.when(i == 0)
    def _():
        for u, (s, g) in enumerate(units):
            for jj in range(L // t):
                vt_ref[u, 0:dh, jj * t:(jj + 1) * t] = v_ref[s, jj * t:(jj + 1) * t, g * dh:(g + 1) * dh].T.astype(BF16)
            vt_ref[u, dh:, :] = jnp.ones((vt_ref.shape[1] - dh, L), BF16)
            for jj in range(L // LANES):
                row = c_refs[s][g, :, jj * LANES:(jj + 1) * LANES] * log2e
                cb_ref[u, jj * LANES:(jj + 1) * LANES, :] = jnp.broadcast_to(row, (LANES, LANES)).T

    q0 = pl.multiple_of(i * t, t)
    qs = [q_ref[s, :, g * dh:(g + 1) * dh].astype(BF16) for s, g in units]
    r2 = [c_refs[s][g, :, pl.ds(q0, t)][:, t - 1:t] * log2e for s, g in units]
    m_ref[...] = jnp.full_like(m_ref, NEG)
    acc_ref[...] = jnp.zeros_like(acc_ref)

    def step(j, masked):
        k0 = pl.multiple_of(j * t, t)
        qk = [_dot_nt(k_ref[s, pl.ds(k0, t), g * dh:(g + 1) * dh].astype(BF16), qs[u])
              for u, (s, g) in enumerate(units)]
        ps, alphas = [], []
        for u in range(len(units)):
            bias = r2[u] - cb_ref[u, pl.ds(k0, t), :]
            st = qk[u] * (scale * log2e) + jnp.concatenate([bias] * (t // LANES), axis=1)
            if masked:
                key = lax.broadcasted_iota(jnp.int32, (t, t), 0)
                qry = lax.broadcasted_iota(jnp.int32, (t, t), 1)
                st = jnp.where(key <= qry, st, NEG)
            m_prev = m_ref[u]
            m_new = jnp.maximum(m_prev, jnp.max(st, axis=0, keepdims=True))
            ps.append(jnp.exp2(st - m_new).astype(BF16))
            alphas.append(jnp.exp2(m_prev - m_new))
            m_ref[u] = m_new
        for u in range(len(units)):
            acc_ref[u] = acc_ref[u] * alphas[u] + _dot(vt_ref[u, :, pl.ds(k0, t)], ps[u])

    def loop_body(j, carry):
        step(j, False)
        return carry

    lax.fori_loop(0, i, loop_body, 0)
    step(i, True)
    for u, (s, g) in enumerate(units):
        acc = acc_ref[u]
        o_ref[s, :, g * dh:(g + 1) * dh] = (acc[0:dh] / acc[dh:dh + 1]).T


def _fox_prompt(q, k, v, c_rows, B, L, H):
    M, W = q.shape
    Dh = W // H
    t = min(L, 512)
    nq = L // t
    hg = 2 if H % 2 == 0 else 1
    nbg = 2 if B % 2 == 0 else 1
    nu = nbg * hg
    kernel = functools.partial(_fox_body, t=t, hg=hg, dh=Dh, scale=Dh ** -0.5)

    def c_map(s):
        return lambda p, h, i: ((p * nbg + s) * (H // hg) + h, 0, 0)

    o = pl.pallas_call(
        kernel,
        out_shape=jax.ShapeDtypeStruct((B, L, W), F32),
        grid=(B // nbg, H // hg, nq),
        in_specs=[
            pl.BlockSpec((nbg, t, hg * Dh), lambda p, h, i: (p, i, h)),
            pl.BlockSpec((nbg, L, hg * Dh), lambda p, h, i: (p, 0, h)),
            pl.BlockSpec((nbg, L, hg * Dh), lambda p, h, i: (p, 0, h)),
        ] + [pl.BlockSpec((hg, 1, L), c_map(s)) for s in range(nbg)],
        out_specs=pl.BlockSpec((nbg, t, hg * Dh), lambda p, h, i: (p, i, h)),
        scratch_shapes=[pltpu.VMEM((nu, Dh + 16, L), BF16), pltpu.VMEM((nu, L, LANES), F32),
                        pltpu.VMEM((nu, Dh + 16, t), F32), pltpu.VMEM((nu, 1, t), F32)],
        compiler_params=_cparams(3),
        name="fox_prompt",
    )(q.reshape(B, L, W), k.reshape(B, L, W), v.reshape(B, L, W), *([c_rows] * nbg))
    return o.reshape(M, W)


def _split2(x):
    hi = x.astype(BF16)
    return hi, (x - hi.astype(F32)).astype(BF16)


def _split3(x):
    h1 = x.astype(BF16)
    r1 = x - h1.astype(F32)
    h2 = r1.astype(BF16)
    return h1, h2, (r1 - h2.astype(F32)).astype(BF16)


def _dot3(a, b):
    (ah, al), (bh, bl) = a, b
    return _dot(ah, bh) + (_dot(ah, bl) + _dot(al, bh))


def _unit_lower_inverses(a_list, n):
    ri = lax.broadcasted_iota(jnp.int32, (n, n), 0)
    ci = lax.broadcasted_iota(jnp.int32, (n, n), 1)
    eye = (ri == ci).astype(F32)
    nm = len(a_list)
    diag16 = (ri >> 4) == (ci >> 4)
    pw = [jnp.where(diag16, a, 0.0) for a in a_list]
    t = [eye - p for p in pw]
    for _ in range(3):
        pw16 = [p.astype(BF16) for p in pw]
        pw = [_dot(p, p) for p in pw16]
        pn16 = [p.astype(BF16) for p in pw]
        t = [t[m] + _dot(t[m].astype(BF16), pn16[m]) for m in range(nm)]
    size = 16
    while size < GDN_CHUNK:
        sh = size.bit_length() - 1
        off = ((ri >> (sh + 1)) == (ci >> (sh + 1))) & ((ri >> sh) == (ci >> sh) + 1)
        t16 = [x.astype(BF16) for x in t]
        mid = [_dot(jnp.where(off, a_list[m], 0.0).astype(BF16), t16[m]) for m in range(nm)]
        t = [t[m] - _dot(t16[m], mid[m].astype(BF16)) for m in range(nm)]
        size *= 2
    ts = [_split2(x) for x in t]
    res = [(eye - t[m]) - _dot3(_split2(a_list[m]), ts[m]) for m in range(nm)]
    return [t[m] + _dot(ts[m][0], res[m].astype(BF16)) for m in range(nm)]


def _gdn_body(qkv_ref, z_ref, gc_ref, *rest, nh, dk, dv, nseq):
    gr_refs = rest[:nseq]
    ng_ref, o_ref, s_out_ref, s_ref, m16_ref = rest[nseq:]
    j = pl.program_id(1)
    T = qkv_ref.shape[1]
    C = GDN_CHUNK
    nc = T // C

    ri = lax.broadcasted_iota(jnp.int32, (T, T), 0)
    ci = lax.broadcasted_iota(jnp.int32, (T, T), 1)
    csh = C.bit_length() - 1
    same = (ri >> csh) == (ci >> csh)
    tril = same & (ri >= ci)
    strict = same & (ri > ci)

    @pl.when((pl.program_id(0) == 0) & (j == 0))
    def _():
        m16_ref[0] = tril.astype(BF16)
        m16_ref[1] = (same & (ri <= ci)).astype(BF16)

    @pl.when(j == 0)
    def _():
        s_ref[...] = jnp.zeros_like(s_ref)

    rowi = lax.broadcasted_iota(jnp.int32, (T, 1), 0)
    units = [(s, h) for s in range(nseq) for h in range(nh)]
    nu = len(units)
    gcols, grows = [], []
    for s in range(nseq):
        gcols.append(sum(_dot(m16_ref[0], part) for part in _split3(gc_ref[s])))
        grows.append(sum(_dot(part, m16_ref[1]) for part in _split3(gr_refs[s][...])))

    q, k, beta, gcc, gcr, gam, kb, k16 = [], [], [], [], [], [], [], []
    for u, (s, h) in enumerate(units):
        q.append(qkv_ref[s, :, h * dk:(h + 1) * dk])
        k.append(qkv_ref[s, :, nh * dk + h * dk:nh * dk + (h + 1) * dk])
        beta.append(gc_ref[s, :, 2 * nh + h:2 * nh + h + 1])
        gcc.append(gcols[s][:, nh + h:nh + h + 1])
        gcr.append(grows[s][nh + h:nh + h + 1, :])
        gam.append(jnp.exp(jnp.where(tril, gcc[u] - gcr[u], NEG)))
        kb.append(k[u] * beta[u])
        k16.append(k[u].astype(BF16))
    a = [jnp.where(strict, _dot_nt(kb[u].astype(BF16), k16[u]) * gam[u], 0.0) for u in range(nu)]
    tinv = _unit_lower_inverses(a, T)
    x = []
    for u, (s, h) in enumerate(units):
        v = qkv_ref[s, :, 2 * nh * dk + h * dv:2 * nh * dk + (h + 1) * dv]
        rhs = jnp.concatenate([v * beta[u], kb[u] * jnp.exp(gcc[u])], axis=-1)
        x.append(_dot3(_split2(tinv[u]), _split2(rhs)))
    attn, qg, kd, egl = [], [], [], []
    for u in range(nu):
        attn.append((_dot_nt(q[u].astype(BF16), k16[u]) * gam[u]).astype(BF16))
        qg.append((q[u] * jnp.exp(gcc[u])).astype(BF16))
        gl = gcr[u][:, C - 1:C]
        for cidx in range(1, nc):
            gl = jnp.where(rowi >= cidx * C, gcr[u][:, (cidx + 1) * C - 1:(cidx + 1) * C], gl)
        kd.append((k[u] * jnp.exp(gl - gcc[u])).astype(BF16))
        egl.append(jnp.exp(gl))
    S = [s_ref[s, h] for s, h in units]
    outs = [[] for _ in units]
    for cidx in range(nc):
        sl = slice(cidx * C, (cidx + 1) * C)
        S16 = [st.astype(BF16) for st in S]
        vn16 = [(x[u][sl, :dv] - _dot(x[u][sl, dv:].astype(BF16), S16[u])).astype(BF16) for u in range(nu)]
        for u in range(nu):
            outs[u].append(_dot(qg[u][sl], S16[u]) + _dot(attn[u][sl, cidx * C:(cidx + 1) * C], vn16[u]))
        S = [S[u] * egl[u][cidx * C:cidx * C + 1] + _dot_tn(kd[u][sl], vn16[u]) for u in range(nu)]
    for u, (s, h) in enumerate(units):
        s_ref[s, h] = S[u]
        o = _rms_rows(jnp.concatenate(outs[u], axis=0), ng_ref[...])
        o_ref[s, :, h * dv:(h + 1) * dv] = o * jax.nn.silu(z_ref[s, :, h * dv:(h + 1) * dv])

    @pl.when(j == pl.num_programs(1) - 1)
    def _():
        s_out_ref[...] = s_ref[...]


def _gdn_prompt(qkv, z, gates_col, gates_row, norm_g, B, L, nh, dk, dv):
    M, CD = qkv.shape
    T = 2 * GDN_CHUNK
    nb = L // T
    nseq = 4 if B % 4 == 0 else (2 if B % 2 == 0 else 1)
    kernel = functools.partial(_gdn_body, nh=nh, dk=dk, dv=dv, nseq=nseq)

    def row_map(s):
        return lambda p, j: (0, (p * nseq + s) * nb + j)

    per_seq = lambda p, j: (p, j, 0)
    go, S = pl.pallas_call(
        kernel,
        out_shape=[jax.ShapeDtypeStruct((B, L, nh * dv), F32), jax.ShapeDtypeStruct((B, nh, dk, dv), F32)],
        grid=(B // nseq, nb),
        in_specs=[
            pl.BlockSpec((nseq, T, CD), per_seq),
            pl.BlockSpec((nseq, T, nh * dv), per_seq),
            pl.BlockSpec((nseq, T, LANES), per_seq),
        ] + [pl.BlockSpec((16, T), row_map(s)) for s in range(nseq)] + [
            pl.BlockSpec((1, dv), lambda p, j: (0, 0)),
        ],
        out_specs=[
            pl.BlockSpec((nseq, T, nh * dv), per_seq),
            pl.BlockSpec((nseq, nh, dk, dv), lambda p, j: (p, 0, 0, 0)),
        ],
        scratch_shapes=[pltpu.VMEM((nseq, nh, dk, dv), F32), pltpu.VMEM((2, T, T), BF16)],
        compiler_params=_cparams(2),
        name="gdn_prompt",
    )(qkv.reshape(B, L, CD), z.reshape(B, L, nh * dv), gates_col.reshape(B, L, LANES),
      *([gates_row] * nseq), norm_g.reshape(1, dv))
    return go.reshape(M, nh * dv), S


def _rope(x, cos2, sin2, nh, dk):
    outs = []
    for h in range(nh):
        xh = x[:, h * dk:(h + 1) * dk]
        outs.append(xh * cos2 + pltpu.roll(xh, dk // 2, 1) * sin2)
    return outs


def _rope_tables(pos, freqs):
    ang = pos * freqs
    cos, sin = jnp.cos(ang), jnp.sin(ang)
    return jnp.concatenate([cos, cos], axis=-1), jnp.concatenate([-sin, sin], axis=-1)


def _head_layernorm(o):
    mu = jnp.mean(o, axis=-1, keepdims=True)
    oc = o - mu
    return oc * lax.rsqrt(jnp.mean(oc * oc, axis=-1, keepdims=True) + EPS)


def _log_gamma(h):
    return math.log(1.0 - 2.0 ** (-5.0 - h))


def _rope_table_body(fr_ref, cos_o, sin_o):
    C = cos_o.shape[0]
    rowi = lax.broadcasted_iota(jnp.int32, (C, 1), 0)
    pos = (pl.program_id(0) * C + rowi).astype(F32)
    cos_o[...], sin_o[...] = _rope_tables(pos, fr_ref[...])


def _rope_table(L, freqs):
    C = min(L, 512)
    dk = 2 * freqs.shape[1]
    return pl.pallas_call(
        _rope_table_body,
        out_shape=[jax.ShapeDtypeStruct((L, dk), F32)] * 2,
        grid=(L // C,),
        in_specs=[pl.BlockSpec(freqs.shape, lambda c: (0, 0))],
        out_specs=[pl.BlockSpec((C, dk), lambda c: (c, 0))] * 2,
        compiler_params=_cparams(1),
        name="rope_table",
    )(freqs)


def _ret_body(q_ref, k_ref, v_ref, sg_ref, gn_ref, o_ref, s_out_ref, s_ref, dmat_ref, *, nh, dk, dv):
    b = pl.program_id(0)
    j = pl.program_id(1)
    nseq, C = q_ref.shape[0], q_ref.shape[1]
    rowf = lax.broadcasted_iota(jnp.int32, (C, 1), 0).astype(F32)

    @pl.when((b == 0) & (j == 0))
    def _():
        ri = lax.broadcasted_iota(jnp.int32, (C, C), 0)
        ci = lax.broadcasted_iota(jnp.int32, (C, C), 1)
        diff = (ri - ci).astype(F32)
        for h in range(nh):
            dmat_ref[h] = jnp.exp(jnp.where(ri >= ci, diff * _log_gamma(h), NEG))

    @pl.when(j == 0)
    def _():
        s_ref[...] = jnp.zeros_like(s_ref)

    units = [(s, h) for s in range(nseq) for h in range(nh)]
    nu = range(len(units))
    q16 = [q_ref[s, :, h * dk:(h + 1) * dk] for s, h in units]
    k16 = [k_ref[s, :, h * dk:(h + 1) * dk] for s, h in units]
    v = [v_ref[s, :, h * dv:(h + 1) * dv] for s, h in units]
    S = [s_ref[s, h] for s, h in units]
    qk = [_dot_nt(q16[u], k16[u]) for u in nu]
    cross = [_dot(q16[u], S[u].astype(BF16)) for u in nu]
    for u, (s, h) in enumerate(units):
        vd = (v[u] * jnp.exp(_log_gamma(h) * (C - 1.0 - rowf))).astype(BF16)
        s_ref[s, h] = S[u] * math.exp(_log_gamma(h) * C) + _dot_tn(k16[u], vd)
    inner = [_dot((qk[u] * dmat_ref[h]).astype(BF16), v[u].astype(BF16)) for u, (s, h) in enumerate(units)]
    for u, (s, h) in enumerate(units):
        o = inner[u] + cross[u] * jnp.exp(_log_gamma(h) * (rowf + 1.0))
        o = _head_layernorm(o) * gn_ref[:, h * dv:(h + 1) * dv]
        o_ref[s, :, h * dv:(h + 1) * dv] = sg_ref[s, :, h * dv:(h + 1) * dv] * o

    @pl.when(j == pl.num_programs(1) - 1)
    def _():
        s_out_ref[...] = s_ref[...]


def _ret_prompt(q16, k16, v, sgate, gnorm, B, L, nh, dk, dv):
    M = q16.shape[0]
    C = 128
    nb = L // C
    nseq = 4 if B % 4 == 0 else (2 if B % 2 == 0 else 1)
    kernel = functools.partial(_ret_body, nh=nh, dk=dk, dv=dv)
    tok = lambda p, j: (p, j, 0)
    y, S = pl.pallas_call(
        kernel,
        out_shape=[jax.ShapeDtypeStruct((B, L, nh * dv), F32), jax.ShapeDtypeStruct((B, nh, dk, dv), F32)],
        grid=(B // nseq, nb),
        in_specs=[
            pl.BlockSpec((nseq, C, nh * dk), tok),
            pl.BlockSpec((nseq, C, nh * dk), tok),
            pl.BlockSpec((nseq, C, nh * dv), tok),
            pl.BlockSpec((nseq, C, nh * dv), tok),
            pl.BlockSpec((1, nh * dv), lambda p, j: (0, 0)),
        ],
        out_specs=[
            pl.BlockSpec((nseq, C, nh * dv), tok),
            pl.BlockSpec((nseq, nh, dk, dv), lambda p, j: (p, 0, 0, 0)),
        ],
        scratch_shapes=[pltpu.VMEM((nseq, nh, dk, dv), F32), pltpu.VMEM((nh, C, C), F32)],
        compiler_params=_cparams(2),
        name="ret_prompt",
    )(q16.reshape(B, L, nh * dk), k16.reshape(B, L, nh * dk), v.reshape(B, L, nh * dv),
      sgate.reshape(B, L, nh * dv), gnorm.reshape(1, nh * dv))
    return y.reshape(M, nh * dv), S


def _page_sums_body(x_ref, o_ref, op_ref):
    _, nh, plen = x_ref.shape
    n = nh * plen
    wide = o_ref.shape[1]

    @pl.when(pl.program_id(0) == 0)
    def _():
        kk = lax.broadcasted_iota(jnp.int32, (plen, wide), 0)
        cc = lax.broadcasted_iota(jnp.int32, (plen, wide), 1)
        key_c = cc >> (nh.bit_length() - 1)
        later_or_total = (cc >= n) | (kk > key_c)
        for h in range(nh):
            op_ref[h] = (((cc & (nh - 1)) == h) & later_or_total).astype(BF16)

    acc = None
    for h in range(nh):
        for part in _split2(x_ref[:, h, :]):
            d = _dot(part, op_ref[h])
            acc = d if acc is None else acc + d
    o_ref[...] = acc


def _page_sums(lf_pages):
    P, nh, plen = lf_pages.shape
    wide = nh * plen + LANES
    tp = 512 if P % 512 == 0 else P
    return pl.pallas_call(
        _page_sums_body,
        out_shape=jax.ShapeDtypeStruct((P, wide), F32),
        grid=(P // tp,),
        in_specs=[pl.BlockSpec((tp, nh, plen), lambda p: (p, 0, 0))],
        out_specs=pl.BlockSpec((tp, wide), lambda p: (p, 0)),
        scratch_shapes=[pltpu.VMEM((nh, plen, wide), BF16)],
        compiler_params=_cparams(1),
        name="page_sums",
    )(lf_pages)


def _paged_body(pt_ref, *refs, G, nh, scale):
    k_refs = refs[:G]
    v_refs = refs[G:2 * G]
    sums_ref, q_ref, kn_ref, vn_ref, lfn_ref, o_ref, m_ref, l_ref, acc_ref, carry_ref = refs[2 * G:]
    b_idx = pl.program_id(0)
    s_idx = pl.program_id(1)
    ns = pl.num_programs(1)
    n = k_refs[0].shape[0]
    dh = q_ref.shape[1]

    @pl.when(s_idx == 0)
    def _():
        m_ref[...] = jnp.full_like(m_ref, NEG)
        l_ref[...] = jnp.zeros_like(l_ref)
        acc_ref[...] = jnp.zeros_like(acc_ref)
        carry_ref[...] = jnp.zeros_like(carry_ref)

    q = q_ref[...]
    q8 = jnp.concatenate([q, jnp.zeros((8 - nh, dh), F32)], axis=0).astype(BF16)
    lfn = jnp.concatenate([lfn_ref[...], jnp.zeros((8 - nh, 1), F32)], axis=0)
    hrow = lax.broadcasted_iota(jnp.int32, (8, n), 0)
    hcol = lax.broadcasted_iota(jnp.int32, (8, n), 1) & (nh - 1)
    match = hrow == hcol
    carry = carry_ref[...]
    s_parts = [None] * G
    for i in reversed(range(G)):
        page_id = pt_ref[b_idx, (ns - 1 - s_idx) * G + i]
        wt = sums_ref[pl.ds(page_id, 1), :]
        bias = wt[:, :n] + carry
        carry = carry + jnp.concatenate([wt[:, n:]] * (n // LANES), axis=1)
        s = _dot_nt(q8, k_refs[i][...].astype(BF16)) * scale + bias + lfn
        s_parts[i] = jnp.where(match, s, NEG)
    carry_ref[...] = carry
    m_run, l_run, acc_run = m_ref[...], l_ref[...], acc_ref[...]
    half = max(G // 2, 1)
    for grp in (range(half, G), range(0, half)) if G > 1 else (range(G),):
        sg = [s_parts[i] for i in grp]
        m_new = jnp.maximum(m_run, jnp.max(functools.reduce(jnp.maximum, sg), axis=-1, keepdims=True))
        alpha = jnp.exp(m_run - m_new)
        pg = [jnp.exp(s - m_new) for s in sg]
        l_run = alpha * l_run + jnp.sum(functools.reduce(jnp.add, pg), axis=-1, keepdims=True)
        pv = functools.reduce(jnp.add, [_dot(p.astype(BF16), v_refs[i][...].astype(BF16)) for p, i in zip(pg, grp)])
        acc_run = alpha * acc_run + pv
        m_run = m_new
    m_ref[...], l_ref[...], acc_ref[...] = m_run, l_run, acc_run

    @pl.when(s_idx == pl.num_programs(1) - 1)
    def _():
        qf = q.astype(BF16).astype(F32)
        kn = kn_ref[...].astype(BF16).astype(F32)
        s_new = jnp.sum(qf * kn, axis=-1, keepdims=True) * scale
        m_prev = m_ref[0:nh, :]
        m_new = jnp.maximum(m_prev, s_new)
        alpha = jnp.exp(m_prev - m_new)
        p_new = jnp.exp(s_new - m_new)
        l_fin = alpha * l_ref[0:nh, :] + p_new
        vn = vn_ref[...].astype(BF16).astype(F32)
        num = alpha * acc_ref[0:nh, :] + p_new.astype(BF16).astype(F32) * vn
        o_ref[...] = num / l_fin


def _fox_sample(q, k_new, v_new, lf_new, k_pages, v_pages, page_sums, page_table, nh):
    Bn, _, dh = q.shape
    n_pages = page_table.shape[1]
    n_pool, n, _ = k_pages.shape
    G = 16 if n_pages % 16 == 0 else (8 if n_pages % 8 == 0 else 1)
    ns = n_pages // G

    def page_map(i):
        return lambda b, s, pt: (pt[b, (ns - 1 - s) * G + i], 0, 0)

    per_b = lambda b, s, pt: (b, 0, 0)
    grid_spec = pltpu.PrefetchScalarGridSpec(
        num_scalar_prefetch=1,
        grid=(Bn, ns),
        in_specs=[pl.BlockSpec((None, n, dh), page_map(i)) for i in range(G)]
        + [pl.BlockSpec((None, n, dh), page_map(i)) for i in range(G)]
        + [pl.BlockSpec(page_sums.shape, lambda b, s, pt: (0, 0)),
           pl.BlockSpec((None, nh, dh), per_b), pl.BlockSpec((None, nh, dh), per_b),
           pl.BlockSpec((None, nh, dh), per_b), pl.BlockSpec((None, nh, 1), per_b)],
        out_specs=pl.BlockSpec((None, nh, dh), per_b),
        scratch_shapes=[pltpu.VMEM((8, 1), F32), pltpu.VMEM((8, 1), F32), pltpu.VMEM((8, dh), F32),
                        pltpu.VMEM((1, n), F32)],
    )
    return pl.pallas_call(
        functools.partial(_paged_body, G=G, nh=nh, scale=dh ** -0.5),
        out_shape=jax.ShapeDtypeStruct((Bn, nh, dh), F32),
        grid_spec=grid_spec,
        compiler_params=_cparams(2),
        name="fox_sample",
    )(page_table, *([k_pages] * G), *([v_pages] * G), page_sums, q, k_new, v_new, lf_new)


def _pad8(row):
    return jnp.concatenate([row, jnp.zeros((8 - row.shape[0], row.shape[1]), row.dtype)], axis=0)


def _gdn_sample_body(cin_ref, prev_ref, z_ref, gt_ref, cw_ref, ng_ref, s0_ref, o_ref, cnew_ref, s_out_ref, *, nh, dk, dv):
    nb = cin_ref.shape[0]
    w = cw_ref[...]
    kw = w.shape[0]
    convs, gts = [], []
    for b in range(nb):
        u = cin_ref[b]
        prev = prev_ref[b]
        conv = u * w[kw - 1:kw]
        for i in range(kw - 1):
            conv = conv + prev[i:i + 1] * w[i:i + 1]
        cnew_ref[b, 0:kw - 2, :] = prev[1:kw - 1]
        cnew_ref[b, kw - 2:kw - 1, :] = u
        convs.append(jax.nn.silu(conv))
        gts.append(gt_ref[b])
    units = [(b, h) for b in range(nb) for h in range(nh)]
    nu = range(len(units))
    q = [_l2_rows(convs[b][:, h * dk:(h + 1) * dk]) * dk ** -0.5 for b, h in units]
    k = [_l2_rows(convs[b][:, nh * dk + h * dk:nh * dk + (h + 1) * dk]) for b, h in units]
    v = [convs[b][:, 2 * nh * dk + h * dv:2 * nh * dk + (h + 1) * dv] for b, h in units]
    beta = [gts[b][:, 2 * nh + h:2 * nh + h + 1] for b, h in units]
    eg = [jnp.exp(gts[b][:, nh + h:nh + h + 1]) for b, h in units]
    S = [s0_ref[b, h] for b, h in units]
    r = [_dot(_pad8(jnp.concatenate([k[u] * beta[u] * eg[u], q[u] * eg[u]], axis=0)).astype(BF16), S[u].astype(BF16))
         for u in nu]
    vn16 = [(v[u] * beta[u] - r[u][0:1]).astype(BF16) for u in nu]
    for u, (b, h) in enumerate(units):
        s_out_ref[b, h] = S[u] * eg[u] + _dot_tn(_pad8(k[u]).astype(BF16), _pad8(vn16[u]))
    for u, (b, h) in enumerate(units):
        qk = jnp.sum(q[u].astype(BF16).astype(F32) * k[u].astype(BF16).astype(F32), axis=-1, keepdims=True)
        o = r[u][1:2] + qk.astype(BF16).astype(F32) * vn16[u].astype(F32)
        o = _rms_rows(o, ng_ref[...])
        o_ref[b, :, h * dv:(h + 1) * dv] = o * jax.nn.silu(z_ref[b, :, h * dv:(h + 1) * dv])


def _gdn_sample(conv_in, conv_prev, z, gates_col, conv_w, norm_g, S0, nh, dk, dv):
    Bn, _, CD = conv_in.shape
    kw = conv_w.shape[0]
    nb = 4 if Bn % 4 == 0 else 1
    per_b3 = lambda b: (b, 0, 0)
    per_b4 = lambda b: (b, 0, 0, 0)
    kernel = functools.partial(_gdn_sample_body, nh=nh, dk=dk, dv=dv)
    return pl.pallas_call(
        kernel,
        out_shape=[jax.ShapeDtypeStruct((Bn, 1, nh * dv), F32), jax.ShapeDtypeStruct((Bn, kw - 1, CD), F32),
                   jax.ShapeDtypeStruct((Bn, nh, dk, dv), F32)],
        grid=(Bn // nb,),
        in_specs=[
            pl.BlockSpec((nb, 1, CD), per_b3),
            pl.BlockSpec((nb, kw - 1, CD), per_b3),
            pl.BlockSpec((nb, 1, nh * dv), per_b3),
            pl.BlockSpec((nb, 1, LANES), per_b3),
            pl.BlockSpec(conv_w.shape, lambda b: (0, 0)),
            pl.BlockSpec((1, dv), lambda b: (0, 0)),
            pl.BlockSpec((nb, nh, dk, dv), per_b4),
        ],
        out_specs=[
            pl.BlockSpec((nb, 1, nh * dv), per_b3),
            pl.BlockSpec((nb, kw - 1, CD), per_b3),
            pl.BlockSpec((nb, nh, dk, dv), per_b4),
        ],
        compiler_params=_cparams(1),
        name="gdn_sample",
    )(conv_in, conv_prev, z, gates_col, conv_w, norm_g.reshape(1, dv), S0)


def _ret_sample_body(q_ref, k_ref, v_ref, gate_ref, fr_ref, gn_ref, s0_ref, o_ref, s_out_ref, *, nh, dk, dv, pos):
    nb = q_ref.shape[0]
    cos2, sin2 = _rope_tables(jnp.full((1, 1), pos, F32), fr_ref[...])
    units = [(b, h) for b in range(nb) for h in range(nh)]
    qs, ks = [], []
    for b in range(nb):
        qs += _rope(q_ref[b], cos2, sin2, nh, dk)
        ks += [kh * dk ** -0.5 for kh in _rope(k_ref[b], cos2, sin2, nh, dk)]
    v16 = [v_ref[b, :, h * dv:(h + 1) * dv].astype(BF16) for b, h in units]
    S = [s0_ref[b, h] for b, h in units]
    cross = [_dot(_pad8(qs[u] * math.exp(_log_gamma(h))).astype(BF16), S[u].astype(BF16))[0:1]
             for u, (b, h) in enumerate(units)]
    for u, (b, h) in enumerate(units):
        s_out_ref[b, h] = S[u] * math.exp(_log_gamma(h)) + _dot_tn(_pad8(ks[u]).astype(BF16), _pad8(v16[u]))
    for u, (b, h) in enumerate(units):
        qk = jnp.sum(qs[u].astype(BF16).astype(F32) * ks[u].astype(BF16).astype(F32), axis=-1, keepdims=True)
        inner = qk.astype(BF16).astype(F32) * v16[u].astype(F32)
        o = _head_layernorm(inner + cross[u]) * gn_ref[:, h * dv:(h + 1) * dv]
        o_ref[b, :, h * dv:(h + 1) * dv] = jax.nn.silu(gate_ref[b, :, h * dv:(h + 1) * dv]) * o


def _ret_sample(q, k, v, gate, freqs, gnorm, S0, pos, nh, dk, dv):
    Bn = q.shape[0]
    nb = 2 if Bn % 2 == 0 else 1
    per_b3 = lambda b: (b, 0, 0)
    per_b4 = lambda b: (b, 0, 0, 0)
    kernel = functools.partial(_ret_sample_body, nh=nh, dk=dk, dv=dv, pos=float(pos))
    return pl.pallas_call(
        kernel,
        out_shape=[jax.ShapeDtypeStruct((Bn, 1, nh * dv), F32), jax.ShapeDtypeStruct((Bn, nh, dk, dv), F32)],
        grid=(Bn // nb,),
        in_specs=[
            pl.BlockSpec((nb, 1, nh * dk), per_b3),
            pl.BlockSpec((nb, 1, nh * dk), per_b3),
            pl.BlockSpec((nb, 1, nh * dv), per_b3),
            pl.BlockSpec((nb, 1, nh * dv), per_b3),
            pl.BlockSpec((1, dk // 2), lambda b: (0, 0)),
            pl.BlockSpec((1, nh * dv), lambda b: (0, 0)),
            pl.BlockSpec((nb, nh, dk, dv), per_b4),
        ],
        out_specs=[
            pl.BlockSpec((nb, 1, nh * dv), per_b3),
            pl.BlockSpec((nb, nh, dk, dv), per_b4),
        ],
        compiler_params=_cparams(1),
        name="ret_sample",
    )(q, k, v, gate, freqs, gnorm.reshape(1, nh * dv), S0)


def kernel(x_prompt, x_sample, cache_fox_k, cache_fox_v, cache_fox_logf, page_table, state_gdn_conv, state_gdn_S, state_ret_S, norm_g, final_norm_g, ffn_w_gu, ffn_w_down, ab_w_in, ab_w_out, fox_b_f, gdn_conv_w, gdn_A_log, gdn_dt_bias, gdn_norm_g, c_w_in, c_w_out, ret_norm_g):
    B, L, D = x_prompt.shape
    Bn, Ls, _ = x_sample.shape
    assert Ls == 1, "the sample group decodes one token per sequence"
    depth = norm_g.shape[0]
    _, n_pool, page, fh, fd = cache_fox_k.shape
    fw = fh * fd
    _, _, gh, gdk, gdv = state_gdn_S.shape
    conv_dim = gdn_conv_w.shape[2]
    kw = gdn_conv_w.shape[1]
    _, _, rh, rdk, rdv = state_ret_S.shape
    n_pages = page_table.shape[1]
    assert fh == gh and 3 * fh <= 16 and fh & (fh - 1) == 0

    xp = x_prompt.reshape(B * L, D)
    xs = x_sample.reshape(Bn, D)
    w_gu00 = _to_bf16(ffn_w_gu, (0, 0))
    w_down00 = _to_bf16(ffn_w_down, (0, 0))
    freqs = (ROPE_BASE ** (-jnp.arange(rdk // 2, dtype=F32) / (rdk // 2))).reshape(1, rdk // 2)
    cos2, sin2 = _rope_table(L, freqs)

    fkp, fvp, flp, fks, fvs, fls = [], [], [], [], [], []
    gcp, gsp, gcs, gss = [], [], [], []
    rsp, rss = [], []
    for li in range(depth):
        if li == 0:
            xs = _ffn(xs, norm_g[li, 0], w_gu00, w_down00, 0, 0)
            xp, w_gu16, w_down16, c_w_in16 = _ffn(xp, norm_g[li, 0], w_gu00, w_down00, 0, 0,
                                                  cast=(ffn_w_gu, ffn_w_down, c_w_in))
        else:
            xp = _ffn(xp, norm_g[li, 0], w_gu16, w_down16, li, 0)
            xs = _ffn(xs, norm_g[li, 0], w_gu16, w_down16, li, 0)
        j = li // 2
        if li % 2 == 0:
            o0 = 3 * fw
            zw = gh * gdv
            c0 = o0 + fh
            widths = (conv_dim, fw, fw, fw, zw, LANES)
            w_cat = _regroup_bf16(ab_w_in[j].T, [(c0, conv_dim), (0, o0), (c0 + conv_dim, zw), (o0, fh),
                                                 (c0 + conv_dim + zw, 2 * gh)], sum(widths))
            w_out16 = ab_w_out[j].astype(BF16)

            gate_p = _gate_params(fox_b_f[j], gdn_dt_bias[j], gdn_A_log[j])
            gqkv, q, k, v, z, gates, k3, v3, tails = _ab_proj_prompt(
                xp, norm_g[li, 1], w_cat, gdn_conv_w[j], gate_p, B, L, widths, fh, fd, gh, gdk)
            gates_row = gates[:, :16].T
            c_rows = _cumsum_rows(gates_row, L)[:fh].reshape(fh, B, L).transpose(1, 0, 2).reshape(B * fh, 1, L)
            fo = _fox_prompt(q, k, v, c_rows, B, L, fh)
            go, S_p = _gdn_prompt(gqkv, z, gates, gates_row, gdn_norm_g[j], B, L, gh, gdk, gdv)
            pre_p = ([fo, go], w_out16)
            fkp.append(k3.reshape(B, L, fh, fd))
            fvp.append(v3.reshape(B, L, fh, fd))
            flp.append(gates[:, :fh].reshape(B, L, fh))
            gcp.append(tails[:, 8 - (kw - 1):])
            gsp.append(S_p)

            cin, q, k, v, z, small = _rms_proj(xs, norm_g[li, 1], w_cat, widths)
            gates = _gates(small, gate_p, fh)
            sums = _page_sums(jnp.transpose(cache_fox_logf[j], (0, 2, 1)))
            fo = _fox_sample(q.reshape(Bn, fh, fd), k.reshape(Bn, fh, fd), v.reshape(Bn, fh, fd),
                             gates[:, :fh].reshape(Bn, fh, 1),
                             cache_fox_k[j].reshape(n_pool, page * fh, fd), cache_fox_v[j].reshape(n_pool, page * fh, fd),
                             sums, page_table, fh)
            go, conv_s, S_s = _gdn_sample(cin.reshape(Bn, 1, conv_dim), state_gdn_conv[j], z.reshape(Bn, 1, gh * gdv),
                                          gates.reshape(Bn, 1, LANES), gdn_conv_w[j], gdn_norm_g[j], state_gdn_S[j],
                                          gh, gdk, gdv)
            pre_s = ([fo.reshape(Bn, fw), go.reshape(Bn, gh * gdv)], w_out16)
            fks.append(k.reshape(Bn, 1, fh, fd))
            fvs.append(v.reshape(Bn, 1, fh, fd))
            fls.append(gates[:, :fh].reshape(Bn, 1, fh))
            gcs.append(conv_s)
            gss.append(S_s)
        else:
            w_in16 = c_w_in16[j]
            qk_w, v_w = rh * rdk, rh * rdv
            widths = (qk_w, qk_w, v_w, v_w)
            w_out16 = c_w_out[j].astype(BF16)

            q16, k16, v, sgate = _c_proj_prompt(xp, norm_g[li, 1], w_in16, cos2, sin2, L, rh, rdk, v_w)
            y, R_p = _ret_prompt(q16, k16, v, sgate, ret_norm_g[j], B, L, rh, rdk, rdv)
            pre_p = ([y], w_out16)
            rsp.append(R_p)

            q, k, v, gate = _rms_proj(xs, norm_g[li, 1], w_in16, widths)
            y, R_s = _ret_sample(q.reshape(Bn, 1, qk_w), k.reshape(Bn, 1, qk_w), v.reshape(Bn, 1, v_w),
                                 gate.reshape(Bn, 1, v_w), freqs, ret_norm_g[j], state_ret_S[j],
                                 n_pages * page, rh, rdk, rdv)
            pre_s = ([y.reshape(Bn, v_w)], w_out16)
            rss.append(R_s)
        final_g = final_norm_g if li == depth - 1 else None
        xp = _ffn(xp, norm_g[li, 2], w_gu16, w_down16, li, 1, final_g, pre_p)
        xs = _ffn(xs, norm_g[li, 2], w_gu16, w_down16, li, 1, final_g, pre_s)
    y_prompt = xp.reshape(B, L, D)
    y_sample = xs.reshape(Bn, 1, D)
    return (y_prompt, y_sample,
            jnp.stack(fkp), jnp.stack(fvp), jnp.stack(flp),
            jnp.stack(fks), jnp.stack(fvs), jnp.stack(fls),
            jnp.stack(gcp), jnp.stack(gsp), jnp.stack(gcs), jnp.stack(gss),
            jnp.stack(rsp), jnp.stack(rss))
```
